```python
import math
import jax, jax.numpy as jnp
from jax import lax
import numpy as np

D_MODEL = 4096
BATCH = 16
SEQ = 2048
DEPTH = 2

N_MIXERS = 2
N_A_LAYERS = (DEPTH + 1) // 2
N_B_LAYERS = DEPTH // 2
HEAD_DIM = 128
ATT_WIDTH = D_MODEL
N_HEADS = ATT_WIDTH // HEAD_DIM
Q_BLOCK = 128
SSM_WIDTH = D_MODEL
GROUP = 16
N_GROUPS = SSM_WIDTH // GROUP
STATE = 64
SCAN_CHUNK = 128
RMS_EPS = 1e-6
DT_MIN = 1e-3
DT_MAX = 1e-1

kernel_name = "hybrid_stickbreak_s5_decoder"


def _rmsnorm(x, g):
    xf = x.astype(jnp.float32)
    r = lax.rsqrt(jnp.mean(xf * xf, axis=-1, keepdims=True) + RMS_EPS)
    return (xf * r).astype(x.dtype) * g


def _stick_breaking_attention(q, k, v):
    S = q.shape[2]
    scale = 1.0 / math.sqrt(q.shape[-1])
    outs = []
    for blk in range(S // Q_BLOCK):
        q0 = blk * Q_BLOCK
        kl = q0 + Q_BLOCK
        qb = q[:, :, q0:kl]
        kb = k[:, :, :kl]
        vb = v[:, :, :kl]
        z = jnp.einsum('bhqd,bhkd->bhqk', qb, kb).astype(jnp.float32) * scale
        qpos = q0 + jnp.arange(Q_BLOCK)[:, None]
        kpos = jnp.arange(kl)[None, :]
        mask = kpos < qpos
        log_1m = jnp.where(mask, jax.nn.log_sigmoid(-z), 0.0)
        suffix = lax.cumsum(log_1m, axis=3, reverse=True) - log_1m
        w = jnp.where(mask, jnp.exp(jax.nn.log_sigmoid(z) + suffix), 0.0)
        outs.append(jnp.einsum('bhqk,bhkd->bhqd', w.astype(vb.dtype), vb))
    return jnp.concatenate(outs, axis=2)


def _attention_layer(x, norm_g, w_in, q_g, k_g, w_out):
    B, S, _ = x.shape
    h = _rmsnorm(x, norm_g)
    proj = h @ w_in
    q, k, v, gate = jnp.split(proj, 4, axis=-1)
    heads = lambda t: t.reshape(B, S, N_HEADS, HEAD_DIM)
    q = _rmsnorm(heads(q), q_g).transpose(0, 2, 1, 3)
    k = _rmsnorm(heads(k), k_g).transpose(0, 2, 1, 3)
    v = heads(v).transpose(0, 2, 1, 3)
    o = _stick_breaking_attention(q, k, v)
    o = o.transpose(0, 2, 1, 3).reshape(B, S, ATT_WIDTH).astype(x.dtype)
    return x + (o * jax.nn.silu(gate)) @ w_out


def _complex_affine_combine(e1, e2):
    a1r, a1i, b1r, b1i = e1
    a2r, a2i, b2r, b2i = e2
    ar = a2r * a1r - a2i * a1i
    ai = a2r * a1i + a2i * a1r
    br = a2r * b1r - a2i * b1i + b2r
    bi = a2r * b1i + a2i * b1r + b2i
    return ar, ai, br, bi


def _s5_scan(u, A_re, A_im, log_dt, B_re, B_im, C_re, C_im, D):
    Bsz, S, _ = u.shape
    f32 = jnp.float32
    A_re, A_im = A_re.astype(f32), A_im.astype(f32)
    B_re, B_im = B_re.astype(f32), B_im.astype(f32)
    C_re, C_im = C_re.astype(f32), C_im.astype(f32)
    dt = jnp.exp(log_dt.astype(f32))[:, None]
    mag = jnp.exp(A_re * dt)
    Ab_re = mag * jnp.cos(A_im * dt)
    Ab_im = mag * jnp.sin(A_im * dt)
    den = A_re * A_re + A_im * A_im
    f_re = ((Ab_re - 1.0) * A_re + Ab_im * A_im) / den
    f_im = (Ab_im * A_re - (Ab_re - 1.0) * A_im) / den
    Bb_re = f_re[..., None] * B_re - f_im[..., None] * B_im
    Bb_im = f_re[..., None] * B_im + f_im[..., None] * B_re
    Dg = D.astype(f32).reshape(N_GROUPS, GROUP)

    nc = S // SCAN_CHUNK
    uc = u.astype(f32).reshape(Bsz, nc, SCAN_CHUNK, N_GROUPS, GROUP).transpose(1, 2, 0, 3, 4)
    a_re_b = jnp.broadcast_to(Ab_re, (SCAN_CHUNK, Bsz, N_GROUPS, STATE))
    a_im_b = jnp.broadcast_to(Ab_im, (SCAN_CHUNK, Bsz, N_GROUPS, STATE))

    def chunk_step(carry, u_c):
        hp_re, hp_im = carry
        bu_re = jnp.einsum('cbgi,gpi->cbgp', u_c, Bb_re)
        bu_im = jnp.einsum('cbgi,gpi->cbgp', u_c, Bb_im)
        ac_re, ac_im, h_re, h_im = lax.associative_scan(
            _complex_affine_combine, (a_re_b, a_im_b, bu_re, bu_im), axis=0)
        h_re, h_im = (h_re + ac_re * hp_re[None] - ac_im * hp_im[None],
                      h_im + ac_re * hp_im[None] + ac_im * hp_re[None])
        y = (jnp.einsum('cbgp,gip->cbgi', h_re, C_re)
             - jnp.einsum('cbgp,gip->cbgi', h_im, C_im)
             + Dg * u_c)
        return (h_re[-1], h_im[-1]), y

    h0 = jnp.zeros((Bsz, N_GROUPS, STATE), f32)
    _, ys = lax.scan(chunk_step, (h0, h0), uc)
    return ys.transpose(2, 0, 1, 3, 4).reshape(Bsz, S, SSM_WIDTH)


def _ssm_layer(x, norm_g, w_in, A_re, A_im, log_dt, B_re, B_im, C_re, C_im, D, glu_w, glu_b, w_out):
    h = _rmsnorm(x, norm_g)
    proj = h @ w_in
    u, gate = jnp.split(proj, 2, axis=-1)
    y = _s5_scan(u, A_re, A_im, log_dt, B_re, B_im, C_re, C_im, D).astype(x.dtype)
    y = jax.nn.gelu(y)
    y = y * jax.nn.sigmoid(y @ glu_w + glu_b)
    return x + (y * jax.nn.silu(gate)) @ w_out


def _fwd_setup_inputs(seed: int = 0) -> dict:
    key = jax.random.key(seed)
    ks = jax.random.split(key, 20)
    n = jax.random.normal
    f32 = jnp.float32
    x = n(ks[0], (BATCH, SEQ, D_MODEL), f32)
    norm_g = 1.0 + 0.02 * n(ks[1], (DEPTH, D_MODEL), f32)
    attn_w_in = n(ks[2], (N_A_LAYERS, D_MODEL, 4 * ATT_WIDTH), f32) * D_MODEL ** -0.5
    attn_q_g = 1.0 + 0.02 * n(ks[3], (N_A_LAYERS, HEAD_DIM), f32)
    attn_k_g = 1.0 + 0.02 * n(ks[4], (N_A_LAYERS, HEAD_DIM), f32)
    attn_w_out = n(ks[5], (N_A_LAYERS, ATT_WIDTH, D_MODEL), f32) * ATT_WIDTH ** -0.5
    ssm_w_in = n(ks[6], (N_B_LAYERS, D_MODEL, 2 * SSM_WIDTH), f32) * D_MODEL ** -0.5
    ssm_A_re = -0.5 + 0.01 * n(ks[7], (N_B_LAYERS, N_GROUPS, STATE), f32)
    ssm_A_im = (math.pi * jnp.arange(STATE, dtype=f32)
                + 0.01 * n(ks[8], (N_B_LAYERS, N_GROUPS, STATE), f32))
    ssm_log_dt = jax.random.uniform(ks[9], (N_B_LAYERS, N_GROUPS), f32,
                                    math.log(DT_MIN), math.log(DT_MAX))
    ssm_B_re = n(ks[10], (N_B_LAYERS, N_GROUPS, STATE, GROUP), f32) * (2 * GROUP) ** -0.5
    ssm_B_im = n(ks[11], (N_B_LAYERS, N_GROUPS, STATE, GROUP), f32) * (2 * GROUP) ** -0.5
    ssm_C_re = n(ks[12], (N_B_LAYERS, N_GROUPS, GROUP, STATE), f32) * STATE ** -0.5
    ssm_C_im = n(ks[13], (N_B_LAYERS, N_GROUPS, GROUP, STATE), f32) * STATE ** -0.5
    ssm_D = n(ks[14], (N_B_LAYERS, SSM_WIDTH), f32)
    ssm_glu_w = n(ks[15], (N_B_LAYERS, SSM_WIDTH, SSM_WIDTH), f32) * SSM_WIDTH ** -0.5
    ssm_glu_b = 0.01 * n(ks[16], (N_B_LAYERS, SSM_WIDTH), f32)
    ssm_w_out = n(ks[17], (N_B_LAYERS, SSM_WIDTH, D_MODEL), f32) * SSM_WIDTH ** -0.5
    return {"x": x, "norm_g": norm_g, "attn_w_in": attn_w_in, "attn_q_g": attn_q_g,
            "attn_k_g": attn_k_g, "attn_w_out": attn_w_out, "ssm_w_in": ssm_w_in,
            "ssm_A_re": ssm_A_re, "ssm_A_im": ssm_A_im, "ssm_log_dt": ssm_log_dt,
            "ssm_B_re": ssm_B_re, "ssm_B_im": ssm_B_im, "ssm_C_re": ssm_C_re,
            "ssm_C_im": ssm_C_im, "ssm_D": ssm_D, "ssm_glu_w": ssm_glu_w,
            "ssm_glu_b": ssm_glu_b, "ssm_w_out": ssm_w_out}


def _fwd_reference(x, norm_g, attn_w_in, attn_q_g, attn_k_g, attn_w_out, ssm_w_in, ssm_A_re, ssm_A_im,
              ssm_log_dt, ssm_B_re, ssm_B_im, ssm_C_re, ssm_C_im, ssm_D, ssm_glu_w, ssm_glu_b,
              ssm_w_out):
    for i in range(DEPTH):
        j = i // N_MIXERS
        if i % N_MIXERS == 0:
            x = _attention_layer(x, norm_g[i], attn_w_in[j], attn_q_g[j], attn_k_g[j], attn_w_out[j])
        else:
            x = _ssm_layer(x, norm_g[i], ssm_w_in[j], ssm_A_re[j], ssm_A_im[j], ssm_log_dt[j],
                           ssm_B_re[j], ssm_B_im[j], ssm_C_re[j], ssm_C_im[j], ssm_D[j],
                           ssm_glu_w[j], ssm_glu_b[j], ssm_w_out[j])
    return x


import jax as _jax
import jax.numpy as _jnp

TWIN_FORMAT = 'train_step'
FWD_PARAMS = ['x', 'norm_g', 'attn_w_in', 'attn_q_g', 'attn_k_g', 'attn_w_out', 'ssm_w_in', 'ssm_A_re', 'ssm_A_im', 'ssm_log_dt', 'ssm_B_re', 'ssm_B_im', 'ssm_C_re', 'ssm_C_im', 'ssm_D', 'ssm_glu_w', 'ssm_glu_b', 'ssm_w_out']
TWIN_WEIGHTS = ['norm_g', 'attn_w_in', 'attn_q_g', 'attn_k_g', 'attn_w_out', 'ssm_w_in', 'ssm_A_re', 'ssm_A_im', 'ssm_log_dt', 'ssm_B_re', 'ssm_B_im', 'ssm_C_re', 'ssm_C_im', 'ssm_D', 'ssm_glu_w', 'ssm_glu_b', 'ssm_w_out']
TWIN_DIFF_INPUT = 'x'
TWIN_INPUTS = ['x', 'norm_g', 'attn_w_in', 'attn_q_g', 'attn_k_g', 'attn_w_out', 'ssm_w_in', 'ssm_A_re', 'ssm_A_im', 'ssm_log_dt', 'ssm_B_re', 'ssm_B_im', 'ssm_C_re', 'ssm_C_im', 'ssm_D', 'ssm_glu_w', 'ssm_glu_b', 'ssm_w_out', 'loss_target', 'm_norm_g', 'm_attn_w_in', 'm_attn_q_g', 'm_attn_k_g', 'm_attn_w_out', 'm_ssm_w_in', 'm_ssm_A_re', 'm_ssm_A_im', 'm_ssm_log_dt', 'm_ssm_B_re', 'm_ssm_B_im', 'm_ssm_C_re', 'm_ssm_C_im', 'm_ssm_D', 'm_ssm_glu_w', 'm_ssm_glu_b', 'm_ssm_w_out', 'v_norm_g', 'v_attn_w_in', 'v_attn_q_g', 'v_attn_k_g', 'v_attn_w_out', 'v_ssm_w_in', 'v_ssm_A_re', 'v_ssm_A_im', 'v_ssm_log_dt', 'v_ssm_B_re', 'v_ssm_B_im', 'v_ssm_C_re', 'v_ssm_C_im', 'v_ssm_D', 'v_ssm_glu_w', 'v_ssm_glu_b', 'v_ssm_w_out']
TWIN_OUTPUTS = ['loss', 'grad_x', 'grad_norm_g', 'grad_attn_w_in', 'grad_attn_q_g', 'grad_attn_k_g', 'grad_attn_w_out', 'grad_ssm_w_in', 'grad_ssm_A_re', 'grad_ssm_A_im', 'grad_ssm_log_dt', 'grad_ssm_B_re', 'grad_ssm_B_im', 'grad_ssm_C_re', 'grad_ssm_C_im', 'grad_ssm_D', 'grad_ssm_glu_w', 'grad_ssm_glu_b', 'grad_ssm_w_out', 'delta_norm_g', 'delta_attn_w_in', 'delta_attn_q_g', 'delta_attn_k_g', 'delta_attn_w_out', 'delta_ssm_w_in', 'delta_ssm_A_re', 'delta_ssm_A_im', 'delta_ssm_log_dt', 'delta_ssm_B_re', 'delta_ssm_B_im', 'delta_ssm_C_re', 'delta_ssm_C_im', 'delta_ssm_D', 'delta_ssm_glu_w', 'delta_ssm_glu_b', 'delta_ssm_w_out', 'new_m_norm_g', 'new_m_attn_w_in', 'new_m_attn_q_g', 'new_m_attn_k_g', 'new_m_attn_w_out', 'new_m_ssm_w_in', 'new_m_ssm_A_re', 'new_m_ssm_A_im', 'new_m_ssm_log_dt', 'new_m_ssm_B_re', 'new_m_ssm_B_im', 'new_m_ssm_C_re', 'new_m_ssm_C_im', 'new_m_ssm_D', 'new_m_ssm_glu_w', 'new_m_ssm_glu_b', 'new_m_ssm_w_out', 'new_v_norm_g', 'new_v_attn_w_in', 'new_v_attn_q_g', 'new_v_attn_k_g', 'new_v_attn_w_out', 'new_v_ssm_w_in', 'new_v_ssm_A_re', 'new_v_ssm_A_im', 'new_v_ssm_log_dt', 'new_v_ssm_B_re', 'new_v_ssm_B_im', 'new_v_ssm_C_re', 'new_v_ssm_C_im', 'new_v_ssm_D', 'new_v_ssm_glu_w', 'new_v_ssm_glu_b', 'new_v_ssm_w_out']
TWIN_LEAF_KINDS = {'loss': 'loss', 'grad_x': 'grad_x', 'grad_norm_g': 'grad_w', 'grad_attn_w_in': 'grad_w', 'grad_attn_q_g': 'grad_w', 'grad_attn_k_g': 'grad_w', 'grad_attn_w_out': 'grad_w', 'grad_ssm_w_in': 'grad_w', 'grad_ssm_A_re': 'grad_w', 'grad_ssm_A_im': 'grad_w', 'grad_ssm_log_dt': 'grad_w', 'grad_ssm_B_re': 'grad_w', 'grad_ssm_B_im': 'grad_w', 'grad_ssm_C_re': 'grad_w', 'grad_ssm_C_im': 'grad_w', 'grad_ssm_D': 'grad_w', 'grad_ssm_glu_w': 'grad_w', 'grad_ssm_glu_b': 'grad_w', 'grad_ssm_w_out': 'grad_w', 'delta_norm_g': 'delta_w', 'delta_attn_w_in': 'delta_w', 'delta_attn_q_g': 'delta_w', 'delta_attn_k_g': 'delta_w', 'delta_attn_w_out': 'delta_w', 'delta_ssm_w_in': 'delta_w', 'delta_ssm_A_re': 'delta_w', 'delta_ssm_A_im': 'delta_w', 'delta_ssm_log_dt': 'delta_w', 'delta_ssm_B_re': 'delta_w', 'delta_ssm_B_im': 'delta_w', 'delta_ssm_C_re': 'delta_w', 'delta_ssm_C_im': 'delta_w', 'delta_ssm_D': 'delta_w', 'delta_ssm_glu_w': 'delta_w', 'delta_ssm_glu_b': 'delta_w', 'delta_ssm_w_out': 'delta_w', 'new_m_norm_g': 'new_m', 'new_m_attn_w_in': 'new_m', 'new_m_attn_q_g': 'new_m', 'new_m_attn_k_g': 'new_m', 'new_m_attn_w_out': 'new_m', 'new_m_ssm_w_in': 'new_m', 'new_m_ssm_A_re': 'new_m', 'new_m_ssm_A_im': 'new_m', 'new_m_ssm_log_dt': 'new_m', 'new_m_ssm_B_re': 'new_m', 'new_m_ssm_B_im': 'new_m', 'new_m_ssm_C_re': 'new_m', 'new_m_ssm_C_im': 'new_m', 'new_m_ssm_D': 'new_m', 'new_m_ssm_glu_w': 'new_m', 'new_m_ssm_glu_b': 'new_m', 'new_m_ssm_w_out': 'new_m', 'new_v_norm_g': 'new_v', 'new_v_attn_w_in': 'new_v', 'new_v_attn_q_g': 'new_v', 'new_v_attn_k_g': 'new_v', 'new_v_attn_w_out': 'new_v', 'new_v_ssm_w_in': 'new_v', 'new_v_ssm_A_re': 'new_v', 'new_v_ssm_A_im': 'new_v', 'new_v_ssm_log_dt': 'new_v', 'new_v_ssm_B_re': 'new_v', 'new_v_ssm_B_im': 'new_v', 'new_v_ssm_C_re': 'new_v', 'new_v_ssm_C_im': 'new_v', 'new_v_ssm_D': 'new_v', 'new_v_ssm_glu_w': 'new_v', 'new_v_ssm_glu_b': 'new_v', 'new_v_ssm_w_out': 'new_v'}


def _forward(args):
    return _fwd_reference(*[args[k] for k in FWD_PARAMS])


def _output_shape():
    def fwd():
        inp = _fwd_setup_inputs(0)
        return _fwd_reference(*[inp[k] for k in FWD_PARAMS])
    out = _jax.eval_shape(fwd)
    return out.shape, out.dtype

N_MICROBATCH = 1
ADAM_LR = 0.001
ADAM_B1 = 0.9
ADAM_B2 = 0.999
ADAM_EPS = 1e-08
ADAM_WD = 0.01
ADAM_STEP = 10
PER_EXAMPLE_BATCH_AXIS = {'x': 0, 'loss_target': 0}
SHARED_INPUTS = []
_WEIGHT_DTYPES = {'norm_g': _jnp.float32, 'attn_w_in': _jnp.float32, 'attn_q_g': _jnp.float32, 'attn_k_g': _jnp.float32, 'attn_w_out': _jnp.float32, 'ssm_w_in': _jnp.float32, 'ssm_A_re': _jnp.float32, 'ssm_A_im': _jnp.float32, 'ssm_log_dt': _jnp.float32, 'ssm_B_re': _jnp.float32, 'ssm_B_im': _jnp.float32, 'ssm_C_re': _jnp.float32, 'ssm_C_im': _jnp.float32, 'ssm_D': _jnp.float32, 'ssm_glu_w': _jnp.float32, 'ssm_glu_b': _jnp.float32, 'ssm_w_out': _jnp.float32}
MOMENT_SCALE = {'norm_g': 1.914189e+00, 'attn_w_in': 3.068136e-02, 'attn_q_g': 5.695070e+00, 'attn_k_g': 5.694600e+00, 'attn_w_out': 3.069879e-02, 'ssm_w_in': 2.758477e-02, 'ssm_A_re': 1.834455e-03, 'ssm_A_im': 1.326875e-03, 'ssm_log_dt': 1.071950e+00, 'ssm_B_re': 1.147805e-03, 'ssm_B_im': 1.166710e-03, 'ssm_C_re': 1.631094e-03, 'ssm_C_im': 1.653269e-03, 'ssm_D': 4.361604e-01, 'ssm_glu_w': 8.760239e-02, 'ssm_glu_b': 2.706320e-01, 'ssm_w_out': 1.966137e-02}


def _to_microbatches(a, axis):
    t = _jnp.moveaxis(a, axis, 0)
    t = t.reshape((N_MICROBATCH, t.shape[0] // N_MICROBATCH) + t.shape[1:])
    return _jnp.moveaxis(t, 1, axis + 1)


def setup_inputs(seed: int = 0) -> dict:
    inp = _fwd_setup_inputs(seed)
    key = _jax.random.fold_in(_jax.random.key(seed), 7919)
    shape, _ = _output_shape()
    out = dict(inp)
    out["loss_target"] = _jax.random.normal(_jax.random.fold_in(key, 0), shape, _jnp.float32)
    for i, name in enumerate(TWIN_WEIGHTS):
        w = inp[name].astype(_jnp.float32)
        if MOMENT_SCALE is None:
            s = _jnp.sqrt(_jnp.mean(_jnp.square(w)) + 1e-30)
        else:
            s = MOMENT_SCALE[name]
        km, kv = _jax.random.split(_jax.random.fold_in(key, i + 1))
        out[name] = w
        out["m_" + name] = s * _jax.random.normal(km, w.shape, _jnp.float32)
        out["v_" + name] = (s * s) * _jax.random.uniform(kv, w.shape, _jnp.float32, 0.5, 1.5)
    if N_MICROBATCH > 1:
        for name, axis in PER_EXAMPLE_BATCH_AXIS.items():
            out[name] = _to_microbatches(out[name], axis)
    return {'x': out['x'], 'norm_g': out['norm_g'], 'attn_w_in': out['attn_w_in'], 'attn_q_g': out['attn_q_g'], 'attn_k_g': out['attn_k_g'], 'attn_w_out': out['attn_w_out'], 'ssm_w_in': out['ssm_w_in'], 'ssm_A_re': out['ssm_A_re'], 'ssm_A_im': out['ssm_A_im'], 'ssm_log_dt': out['ssm_log_dt'], 'ssm_B_re': out['ssm_B_re'], 'ssm_B_im': out['ssm_B_im'], 'ssm_C_re': out['ssm_C_re'], 'ssm_C_im': out['ssm_C_im'], 'ssm_D': out['ssm_D'], 'ssm_glu_w': out['ssm_glu_w'], 'ssm_glu_b': out['ssm_glu_b'], 'ssm_w_out': out['ssm_w_out'], 'loss_target': out['loss_target'], 'm_norm_g': out['m_norm_g'], 'm_attn_w_in': out['m_attn_w_in'], 'm_attn_q_g': out['m_attn_q_g'], 'm_attn_k_g': out['m_attn_k_g'], 'm_attn_w_out': out['m_attn_w_out'], 'm_ssm_w_in': out['m_ssm_w_in'], 'm_ssm_A_re': out['m_ssm_A_re'], 'm_ssm_A_im': out['m_ssm_A_im'], 'm_ssm_log_dt': out['m_ssm_log_dt'], 'm_ssm_B_re': out['m_ssm_B_re'], 'm_ssm_B_im': out['m_ssm_B_im'], 'm_ssm_C_re': out['m_ssm_C_re'], 'm_ssm_C_im': out['m_ssm_C_im'], 'm_ssm_D': out['m_ssm_D'], 'm_ssm_glu_w': out['m_ssm_glu_w'], 'm_ssm_glu_b': out['m_ssm_glu_b'], 'm_ssm_w_out': out['m_ssm_w_out'], 'v_norm_g': out['v_norm_g'], 'v_attn_w_in': out['v_attn_w_in'], 'v_attn_q_g': out['v_attn_q_g'], 'v_attn_k_g': out['v_attn_k_g'], 'v_attn_w_out': out['v_attn_w_out'], 'v_ssm_w_in': out['v_ssm_w_in'], 'v_ssm_A_re': out['v_ssm_A_re'], 'v_ssm_A_im': out['v_ssm_A_im'], 'v_ssm_log_dt': out['v_ssm_log_dt'], 'v_ssm_B_re': out['v_ssm_B_re'], 'v_ssm_B_im': out['v_ssm_B_im'], 'v_ssm_C_re': out['v_ssm_C_re'], 'v_ssm_C_im': out['v_ssm_C_im'], 'v_ssm_D': out['v_ssm_D'], 'v_ssm_glu_w': out['v_ssm_glu_w'], 'v_ssm_glu_b': out['v_ssm_glu_b'], 'v_ssm_w_out': out['v_ssm_w_out']}


def _loss(weights, diff, rest, loss_target):
    with _jax.named_scope("forward"):
        args = {**rest, TWIN_DIFF_INPUT: diff, **{k: w.astype(_WEIGHT_DTYPES[k]) for k, w in weights.items()}}
        y = _forward(args)
    with _jax.named_scope("loss_head"):
        err = _jnp.square(y.astype(_jnp.float32) - loss_target)
        return 0.5 * _jnp.sum(_jnp.mean(err, axis=-1)) if err.ndim else 0.5 * err


def _adamw(w, g, m, v):
    m = ADAM_B1 * m + (1.0 - ADAM_B1) * g
    v = ADAM_B2 * v + (1.0 - ADAM_B2) * _jnp.square(g)
    m_hat = m / (1.0 - ADAM_B1 ** ADAM_STEP)
    v_hat = v / (1.0 - ADAM_B2 ** ADAM_STEP)
    delta = -ADAM_LR * (m_hat / (_jnp.sqrt(v_hat) + ADAM_EPS) + ADAM_WD * w)
    return delta, m, v


def reference(x, norm_g, attn_w_in, attn_q_g, attn_k_g, attn_w_out, ssm_w_in, ssm_A_re, ssm_A_im, ssm_log_dt, ssm_B_re, ssm_B_im, ssm_C_re, ssm_C_im, ssm_D, ssm_glu_w, ssm_glu_b, ssm_w_out, loss_target, m_norm_g, m_attn_w_in, m_attn_q_g, m_attn_k_g, m_attn_w_out, m_ssm_w_in, m_ssm_A_re, m_ssm_A_im, m_ssm_log_dt, m_ssm_B_re, m_ssm_B_im, m_ssm_C_re, m_ssm_C_im, m_ssm_D, m_ssm_glu_w, m_ssm_glu_b, m_ssm_w_out, v_norm_g, v_attn_w_in, v_attn_q_g, v_attn_k_g, v_attn_w_out, v_ssm_w_in, v_ssm_A_re, v_ssm_A_im, v_ssm_log_dt, v_ssm_B_re, v_ssm_B_im, v_ssm_C_re, v_ssm_C_im, v_ssm_D, v_ssm_glu_w, v_ssm_glu_b, v_ssm_w_out):
    given = dict(x=x, norm_g=norm_g, attn_w_in=attn_w_in, attn_q_g=attn_q_g, attn_k_g=attn_k_g, attn_w_out=attn_w_out, ssm_w_in=ssm_w_in, ssm_A_re=ssm_A_re, ssm_A_im=ssm_A_im, ssm_log_dt=ssm_log_dt, ssm_B_re=ssm_B_re, ssm_B_im=ssm_B_im, ssm_C_re=ssm_C_re, ssm_C_im=ssm_C_im, ssm_D=ssm_D, ssm_glu_w=ssm_glu_w, ssm_glu_b=ssm_glu_b, ssm_w_out=ssm_w_out, loss_target=loss_target, m_norm_g=m_norm_g, m_attn_w_in=m_attn_w_in, m_attn_q_g=m_attn_q_g, m_attn_k_g=m_attn_k_g, m_attn_w_out=m_attn_w_out, m_ssm_w_in=m_ssm_w_in, m_ssm_A_re=m_ssm_A_re, m_ssm_A_im=m_ssm_A_im, m_ssm_log_dt=m_ssm_log_dt, m_ssm_B_re=m_ssm_B_re, m_ssm_B_im=m_ssm_B_im, m_ssm_C_re=m_ssm_C_re, m_ssm_C_im=m_ssm_C_im, m_ssm_D=m_ssm_D, m_ssm_glu_w=m_ssm_glu_w, m_ssm_glu_b=m_ssm_glu_b, m_ssm_w_out=m_ssm_w_out, v_norm_g=v_norm_g, v_attn_w_in=v_attn_w_in, v_attn_q_g=v_attn_q_g, v_attn_k_g=v_attn_k_g, v_attn_w_out=v_attn_w_out, v_ssm_w_in=v_ssm_w_in, v_ssm_A_re=v_ssm_A_re, v_ssm_A_im=v_ssm_A_im, v_ssm_log_dt=v_ssm_log_dt, v_ssm_B_re=v_ssm_B_re, v_ssm_B_im=v_ssm_B_im, v_ssm_C_re=v_ssm_C_re, v_ssm_C_im=v_ssm_C_im, v_ssm_D=v_ssm_D, v_ssm_glu_w=v_ssm_glu_w, v_ssm_glu_b=v_ssm_glu_b, v_ssm_w_out=v_ssm_w_out)
    weights = {n: given[n] for n in TWIN_WEIGHTS}
    shared = {n: given[n] for n in SHARED_INPUTS}
    per_example = {n: given[n] for n in ['x']}
    grad_fn = _jax.value_and_grad(_loss, argnums=(0, 1))

    def one_microbatch(ex, loss_target):
        ex = dict(ex)
        diff = ex.pop(TWIN_DIFF_INPUT)
        return grad_fn(weights, diff, {**shared, **ex}, loss_target)

    if N_MICROBATCH == 1:
        loss, (grad_w, grad_x) = one_microbatch(per_example, given["loss_target"])
    else:
        def body(carry, xs):
            loss_sum, grad_sum = carry
            l_k, (gw_k, gx_k) = one_microbatch(xs[0], xs[1])
            with _jax.named_scope("update"):
                return (loss_sum + l_k, _jax.tree.map(_jnp.add, grad_sum, gw_k)), gx_k

        init = (_jnp.zeros((), _jnp.float32), _jax.tree.map(_jnp.zeros_like, weights))
        (loss, grad_w), grad_x = _jax.lax.scan(body, init, (per_example, given["loss_target"]))
    with _jax.named_scope("update"):
        delta_w, new_m, new_v = {}, {}, {}
        for n in TWIN_WEIGHTS:
            delta_w[n], new_m[n], new_v[n] = _adamw(weights[n], grad_w[n], given["m_" + n], given["v_" + n])
    return (loss, grad_x, *[grad_w[n] for n in TWIN_WEIGHTS], *[delta_w[n] for n in TWIN_WEIGHTS],
            *[new_m[n] for n in TWIN_WEIGHTS], *[new_v[n] for n in TWIN_WEIGHTS])
```

```python
import functools
import math

import jax
import jax.numpy as jnp
from jax import lax
from jax.experimental import pallas as pl
from jax.experimental.pallas import tpu as pltpu

F32 = jnp.float32
BF16 = jnp.bfloat16

N_DEV = 8
HEAD_DIM = 128
GROUP = 16
STATE = 64
GROUPS_PER_BLOCK = 8
SUBLANES = 8
RMS_EPS = 1e-6
ADAM_LR, ADAM_B1, ADAM_B2, ADAM_EPS, ADAM_WD, ADAM_STEP = 0.001, 0.9, 0.999, 1e-08, 0.01, 10
VMEM_LIMIT = 56 * 1024 * 1024
GELU_C = math.sqrt(2.0 / math.pi)
PACK_W = 128


def _params(sem, **kw):
    return pltpu.CompilerParams(dimension_semantics=sem, vmem_limit_bytes=VMEM_LIMIT, **kw)


def _tile(n, t):
    t = min(n, t)
    assert n % t == 0, (n, t)
    return t


def _ones_where(cond):
    return jnp.where(cond, 1.0, 0.0).astype(BF16)


def _sigmoid(x):
    return 1.0 / (1.0 + jnp.exp(-x))


def _log_sigmoid(z):
    return jnp.minimum(z, 0.0) - jnp.log(1.0 + jnp.exp(-jnp.abs(z)))


def _dot(a, b, dims):
    return lax.dot_general(a, b, (dims, ((), ())), preferred_element_type=F32)


NN = ((1,), (0,))
NT = ((1,), (1,))
TN = ((0,), (0,))


def _exchange(x, scatter, name):
    shape = x.shape[1:] if scatter else x.shape

    def body(x_ref, out_ref, send_sems, recv_sems, local_sem):
        ix, iy, ic = lax.axis_index("x"), lax.axis_index("y"), lax.axis_index("c")
        me = 4 * ix + 2 * iy + ic

        def peer(k):
            kx, ky, kc = (k >> 2) & 1, (k >> 1) & 1, k & 1
            px, py, pc = ix ^ kx, iy ^ ky, ic ^ kc
            return (px, py, pc), 4 * px + 2 * py + pc

        mine = pltpu.make_async_copy(x_ref.at[me] if scatter else x_ref, out_ref.at[me], local_sem)
        mine.start()
        copies = []
        for k in range(1, N_DEV):
            pid, pidx = peer(k)
            cp = pltpu.make_async_remote_copy(
                src_ref=x_ref.at[pidx] if scatter else x_ref,
                dst_ref=out_ref.at[me],
                send_sem=send_sems.at[k - 1], recv_sem=recv_sems.at[k - 1],
                device_id=pid, device_id_type=pl.DeviceIdType.MESH)
            cp.start()
            copies.append(cp)
        for k in range(1, N_DEV):
            pid, pidx = peer(k)
            pltpu.make_async_remote_copy(
                src_ref=x_ref.at[pidx] if scatter else x_ref,
                dst_ref=out_ref.at[pidx],
                send_sem=send_sems.at[k - 1], recv_sem=recv_sems.at[k - 1],
                device_id=pid, device_id_type=pl.DeviceIdType.MESH).wait_recv()
        for cp in copies:
            cp.wait_send()
        mine.wait()

    return pl.pallas_call(
        body, name=name,
        out_shape=jax.ShapeDtypeStruct((N_DEV,) + tuple(shape), x.dtype),
        in_specs=[pl.BlockSpec(memory_space=pl.ANY)],
        out_specs=pl.BlockSpec(memory_space=pl.ANY),
        scratch_shapes=[pltpu.SemaphoreType.DMA((N_DEV - 1,)), pltpu.SemaphoreType.DMA((N_DEV - 1,)),
                        pltpu.SemaphoreType.DMA],
    )(x)


def _mm_nn(a, b, out_dtype, name, bias=None, residual=None, tm=1024, tn=1024, tk=512):
    M, K = a.shape
    J, K2, Nj = b.shape
    assert K == K2
    tm, tn, tk = _tile(M, tm), _tile(Nj, tn), _tile(K, tk)
    nb, nk = Nj // tn, K // tk

    def body(*refs):
        a_ref, b_ref = refs[0], refs[1]
        i = 2
        bias_ref = res_ref = None
        if bias is not None:
            bias_ref = refs[i]; i += 1
        if residual is not None:
            res_ref = refs[i]; i += 1
        o_ref, acc = refs[i], refs[i + 1]
        k = pl.program_id(3)

        @pl.when(k == 0)
        def _():
            acc[...] = jnp.zeros_like(acc)

        acc[...] += _dot(a_ref[...], b_ref[...], NN)

        @pl.when(k == nk - 1)
        def _():
            r = acc[...]
            if bias_ref is not None:
                r = r + bias_ref[...]
            if res_ref is not None:
                r = r + res_ref[...]
            o_ref[...] = r.astype(out_dtype)

    in_specs = [pl.BlockSpec((tm, tk), lambda j, m, n, k: (m, k)),
                pl.BlockSpec((None, tk, tn), lambda j, m, n, k: (j, k, n))]
    args = [a, b]
    if bias is not None:
        in_specs.append(pl.BlockSpec((1, tn), lambda j, m, n, k: (0, j * nb + n)))
        args.append(bias)
    if residual is not None:
        in_specs.append(pl.BlockSpec((tm, tn), lambda j, m, n, k: (m, j * nb + n)))
        args.append(residual)
    return pl.pallas_call(
        body, name=name,
        out_shape=jax.ShapeDtypeStruct((M, J * Nj), out_dtype),
        grid=(J, M // tm, nb, nk),
        in_specs=in_specs,
        out_specs=pl.BlockSpec((tm, tn), lambda j, m, n, k: (m, j * nb + n)),
        scratch_shapes=[pltpu.VMEM((tm, tn), F32)],
        compiler_params=_params(("parallel", "parallel", "parallel", "arbitrary")),
    )(*args)


def _mm_nt(a, b, out_dtype, name, tm=1024, tp=1024, tq=512):
    M, Q = a.shape
    J, P, Qj = b.shape
    assert Q == J * Qj
    tm, tp, tq = _tile(M, tm), _tile(P, tp), _tile(Qj, tq)
    nq = Qj // tq

    def body(a_ref, b_ref, o_ref, acc):
        j, q = pl.program_id(2), pl.program_id(3)

        @pl.when((j == 0) & (q == 0))
        def _():
            acc[...] = jnp.zeros_like(acc)

        acc[...] += _dot(a_ref[...], b_ref[...], NT)

        @pl.when((j == J - 1) & (q == nq - 1))
        def _():
            o_ref[...] = acc[...].astype(out_dtype)

    return pl.pallas_call(
        body, name=name,
        out_shape=jax.ShapeDtypeStruct((M, P), out_dtype),
        grid=(M // tm, P // tp, J, nq),
        in_specs=[pl.BlockSpec((tm, tq), lambda m, p, j, q: (m, j * nq + q)),
                  pl.BlockSpec((None, tp, tq), lambda m, p, j, q: (j, p, q))],
        out_specs=pl.BlockSpec((tm, tp), lambda m, p, j, q: (m, p)),
        scratch_shapes=[pltpu.VMEM((tm, tp), F32)],
        compiler_params=_params(("parallel", "parallel", "arbitrary", "arbitrary")),
    )(a, b)


def _mm_tn(a, b, J, out_dtype, name, tm=1024, tn=1024, tr=512):
    R, M = a.shape
    R2, N = b.shape
    assert R == R2 and N % J == 0
    Nj = N // J
    tm, tn, tr = _tile(M, tm), _tile(Nj, tn), _tile(R, tr)
    nb, nr = Nj // tn, R // tr

    def body(a_ref, b_ref, o_ref, acc):
        r = pl.program_id(3)

        @pl.when(r == 0)
        def _():
            acc[...] = jnp.zeros_like(acc)

        acc[...] += _dot(a_ref[...], b_ref[...], TN)

        @pl.when(r == nr - 1)
        def _():
            o_ref[...] = acc[...].astype(out_dtype)

    return pl.pallas_call(
        body, name=name,
        out_shape=jax.ShapeDtypeStruct((J, M, Nj), out_dtype),
        grid=(J, M // tm, nb, nr),
        in_specs=[pl.BlockSpec((tr, tm), lambda j, m, n, r: (r, m)),
                  pl.BlockSpec((tr, tn), lambda j, m, n, r: (r, j * nb + n))],
        out_specs=pl.BlockSpec((None, tm, tn), lambda j, m, n, r: (j, m, n)),
        scratch_shapes=[pltpu.VMEM((tm, tn), F32)],
        compiler_params=_params(("parallel", "parallel", "parallel", "arbitrary")),
    )(a, b)


def _ew(fn, ins, vecs, out_dtypes, n_acc, name, tr=256, tc=1024):
    T, C = ins[0].shape
    tr, tc = _tile(T, tr), _tile(C, tc)
    n_in, n_vec, n_out = len(ins), len(vecs), len(out_dtypes)

    def body(*refs):
        in_refs = refs[:n_in + n_vec]
        out_refs = refs[n_in + n_vec:n_in + n_vec + n_out]
        acc_refs = refs[n_in + n_vec + n_out:]
        res = fn(*[r[...] for r in in_refs])
        for o_ref, v in zip(out_refs, res[:n_out]):
            o_ref[...] = v.astype(o_ref.dtype)
        if n_acc:
            r = pl.program_id(1)

            @pl.when(r == 0)
            def _():
                for a_ref in acc_refs:
                    a_ref[...] = jnp.zeros_like(a_ref)

            for a_ref, v in zip(acc_refs, res[n_out:]):
                a_ref[...] += v

    blk = pl.BlockSpec((tr, tc), lambda c, r: (r, c))
    vec = pl.BlockSpec((1, tc), lambda c, r: (0, c))
    out = pl.pallas_call(
        body, name=name,
        out_shape=tuple([jax.ShapeDtypeStruct((T, C), d) for d in out_dtypes]
                        + [jax.ShapeDtypeStruct((1, C), F32)] * n_acc),
        grid=(C // tc, T // tr),
        in_specs=[blk] * n_in + [vec] * n_vec,
        out_specs=tuple([blk] * n_out + [vec] * n_acc),
        compiler_params=_params(("parallel", "arbitrary")),
    )(*ins, *vecs)
    return out


def _cast_bf16(w, name):
    return _ew(lambda a: (a,), [w], [], [BF16], 0, name)[0]


def _rmsnorm_fwd(x, g, name, tr=128):
    T, D = x.shape
    tr = _tile(T, tr)

    def body(x_ref, g_ref, h_ref):
        xv = x_ref[...]
        r = lax.rsqrt(jnp.mean(xv * xv, axis=-1, keepdims=True) + RMS_EPS)
        h_ref[...] = ((xv * r) * g_ref[...]).astype(BF16)

    return pl.pallas_call(
        body, name=name,
        out_shape=jax.ShapeDtypeStruct((T, D), BF16),
        grid=(T // tr,),
        in_specs=[pl.BlockSpec((tr, D), lambda i: (i, 0)), pl.BlockSpec((1, D), lambda i: (0, 0))],
        out_specs=pl.BlockSpec((tr, D), lambda i: (i, 0)),
        compiler_params=_params(("parallel",)),
    )(x, g)


def _rmsnorm_bwd(x, dh, dres, g, name, tr=128):
    T, D = x.shape
    tr = _tile(T, tr)

    def body(x_ref, dh_ref, dres_ref, g_ref, dx_ref, dxb_ref, dg_ref):
        xv = x_ref[...]
        r = lax.rsqrt(jnp.mean(xv * xv, axis=-1, keepdims=True) + RMS_EPS)
        xn = xv * r
        dhv = dh_ref[...].astype(F32)
        dxn = dhv * g_ref[...]
        dx = dres_ref[...] + r * (dxn - xn * jnp.mean(dxn * xn, axis=-1, keepdims=True))
        dx_ref[...] = dx
        dxb_ref[...] = dx.astype(BF16)

        @pl.when(pl.program_id(0) == 0)
        def _():
            dg_ref[...] = jnp.zeros_like(dg_ref)

        dg_ref[...] += jnp.sum(dhv * xn, axis=0, keepdims=True)

    blk = pl.BlockSpec((tr, D), lambda i: (i, 0))
    vec = pl.BlockSpec((1, D), lambda i: (0, 0))
    return pl.pallas_call(
        body, name=name,
        out_shape=(jax.ShapeDtypeStruct((T, D), F32), jax.ShapeDtypeStruct((T, D), BF16),
                   jax.ShapeDtypeStruct((1, D), F32)),
        grid=(T // tr,),
        in_specs=[blk, blk, blk, vec],
        out_specs=(blk, blk, vec),
        compiler_params=_params(("arbitrary",)),
    )(x, dh, dres, g)


def _loss_head(y, target, name, tr=128):
    T, D = y.shape
    tr = _tile(T, tr)
    n = T // tr

    def body(y_ref, t_ref, dy_ref, dyb_ref, loss_ref, acc):
        i = pl.program_id(0)

        @pl.when(i == 0)
        def _():
            acc[...] = jnp.zeros_like(acc)

        err = y_ref[...] - t_ref[...]
        dy = err * (1.0 / D)
        dy_ref[...] = dy
        dyb_ref[...] = dy.astype(BF16)
        acc[...] += jnp.sum(err * err, axis=0, keepdims=True)

        @pl.when(i == n - 1)
        def _():
            loss_ref[...] = jnp.sum(acc[...], axis=1, keepdims=True) * (0.5 / D)

    blk = pl.BlockSpec((tr, D), lambda i: (i, 0))
    return pl.pallas_call(
        body, name=name,
        out_shape=(jax.ShapeDtypeStruct((T, D), F32), jax.ShapeDtypeStruct((T, D), BF16),
                   jax.ShapeDtypeStruct((1, 1), F32)),
        grid=(n,),
        in_specs=[blk, blk],
        out_specs=(blk, blk, pl.BlockSpec((1, 1), lambda i: (0, 0))),
        scratch_shapes=[pltpu.VMEM((1, D), F32)],
        compiler_params=_params(("arbitrary",)),
    )(y, target)


def _gelu(x):
    t = jnp.tanh(GELU_C * (x + 0.044715 * (x * x * x)))
    return x * (0.5 * (1.0 + t)), t


def _adamw(parts, w, m, v, name, tr=256, tc=1024):
    n, R, C = parts.shape
    tr, tc = _tile(R, tr), _tile(C, tc)
    c1 = 1.0 - ADAM_B1 ** ADAM_STEP
    c2 = 1.0 - ADAM_B2 ** ADAM_STEP

    def body(p_ref, w_ref, m_ref, v_ref, g_out, d_out, m_out, v_out):
        g = p_ref[0].astype(F32)
        for k in range(1, n):
            g = g + p_ref[k].astype(F32)
        mn = ADAM_B1 * m_ref[...] + (1.0 - ADAM_B1) * g
        vn = ADAM_B2 * v_ref[...] + (1.0 - ADAM_B2) * (g * g)
        m_hat = mn / c1
        v_hat = vn / c2
        g_out[...] = g
        d_out[...] = -ADAM_LR * (m_hat / (jnp.sqrt(v_hat) + ADAM_EPS) + ADAM_WD * w_ref[...])
        m_out[...] = mn
        v_out[...] = vn

    blk = pl.BlockSpec((tr, tc), lambda r, c: (r, c))
    return pl.pallas_call(
        body, name=name,
        out_shape=tuple([jax.ShapeDtypeStruct((R, C), F32)] * 4),
        grid=(R // tr, C // tc),
        in_specs=[pl.BlockSpec((n, tr, tc), lambda r, c: (0, r, c)), blk, blk, blk],
        out_specs=(blk, blk, blk, blk),
        compiler_params=_params(("parallel", "parallel")),
    )(parts, w, m, v)


def _sum_parts(parts, name):
    n, R, C = parts.shape

    def body(p_ref, o_ref):
        g = p_ref[0].astype(F32)
        for k in range(1, n):
            g = g + p_ref[k].astype(F32)
        o_ref[...] = g

    return pl.pallas_call(
        body, name=name,
        out_shape=jax.ShapeDtypeStruct((R, C), F32),
        compiler_params=pltpu.CompilerParams(vmem_limit_bytes=VMEM_LIMIT),
    )(parts)


def _head_norm(x_ref, g_ref):
    xv = x_ref[...].astype(F32)
    r = lax.rsqrt(jnp.mean(xv * xv, axis=-1, keepdims=True) + RMS_EPS)
    return xv * r, r


def _split_bf16(x):
    hi = x.astype(BF16)
    lo = (x - hi.astype(F32)).astype(BF16)
    return hi, lo


def _attn_specs(S, H):
    def spec(part):
        return pl.BlockSpec((S, HEAD_DIM), lambda b, h: (b, part * H + h))
    return spec


def _attn_fwd(proj, q_g, k_g, B, S, H, name):
    TQ = _tile(S, 256)
    nq = S // TQ
    scale = 1.0 / math.sqrt(HEAD_DIM)

    def body(q_ref, k_ref, v_ref, gate_ref, qg_ref, kg_ref, og_ref, o_ref, qn_s, kn_s):
        qh, _ = _head_norm(q_ref, qg_ref)
        qn_s[...] = (qh * qg_ref[...]).astype(BF16)
        kh, _ = _head_norm(k_ref, kg_ref)
        kn_s[...] = (kh * kg_ref[...]).astype(BF16)
        row = lax.broadcasted_iota(jnp.int32, (TQ, TQ), 0)
        col = lax.broadcasted_iota(jnp.int32, (TQ, TQ), 1)
        later = _ones_where(row > col)
        causal = col < row

        def q_block(qi, _):
            q0 = pl.multiple_of(qi * TQ, TQ)
            qb = qn_s[pl.ds(q0, TQ), :]

            def k_block(k0, carry, acc, diag):
                z = _dot(qb, kn_s[pl.ds(k0, TQ), :], NT) * scale
                ls = _log_sigmoid(z)
                l1m = ls - z
                if diag:
                    l1m = jnp.where(causal, l1m, 0.0)
                hi, lo = _split_bf16(l1m)
                suffix = _dot(hi, later, NN) + _dot(lo, later, NN) + carry
                w = jnp.exp(ls + suffix)
                if diag:
                    w = jnp.where(causal, w, 0.0)
                acc = acc + _dot(w.astype(BF16), v_ref[pl.ds(k0, TQ), :], NN)
                return carry + jnp.sum(l1m, axis=1, keepdims=True), acc

            carry, acc = k_block(q0, jnp.zeros((TQ, 1), F32), jnp.zeros((TQ, HEAD_DIM), F32), True)

            def k_loop(i, c):
                return k_block(pl.multiple_of((qi - 1 - i) * TQ, TQ), c[0], c[1], False)

            carry, acc = lax.fori_loop(0, qi, k_loop, (carry, acc))
            o_ref[pl.ds(q0, TQ), :] = acc.astype(BF16)
            gate = gate_ref[pl.ds(q0, TQ), :].astype(F32)
            og_ref[pl.ds(q0, TQ), :] = (acc * (gate * _sigmoid(gate))).astype(BF16)
            return 0

        lax.fori_loop(0, nq, q_block, 0)

    spec = _attn_specs(S, H)
    vec = pl.BlockSpec((1, HEAD_DIM), lambda b, h: (0, 0))
    out = pl.BlockSpec((S, HEAD_DIM), lambda b, h: (b, h))
    return pl.pallas_call(
        body, name=name,
        out_shape=(jax.ShapeDtypeStruct((B * S, H * HEAD_DIM), BF16),) * 2,
        grid=(B, H),
        in_specs=[spec(0), spec(1), spec(2), spec(3), vec, vec],
        out_specs=(out, out),
        scratch_shapes=[pltpu.VMEM((S, HEAD_DIM), BF16)] * 2,
        compiler_params=_params(("parallel", "parallel")),
    )(proj, proj, proj, proj, q_g, k_g)


def _attn_bwd(proj, o, dog, q_g, k_g, B, S, H, name):
    TQ = _tile(S, 256)
    nq = S // TQ
    scale = 1.0 / math.sqrt(HEAD_DIM)

    def body(q_ref, k_ref, v_ref, gate_ref, o_ref, dog_ref, qg_ref, kg_ref,
             dq_ref, dk_ref, dv_ref, dgate_ref, dqg_ref, dkg_ref,
             qn_s, kn_s, do_s, dqn_s, dkn_s, dv_s, w_s, sg_s):
        first = (pl.program_id(0) == 0) & (pl.program_id(1) == 0)

        @pl.when(first)
        def _():
            dqg_ref[...] = jnp.zeros_like(dqg_ref)
            dkg_ref[...] = jnp.zeros_like(dkg_ref)

        qn_s[...] = (_head_norm(q_ref, qg_ref)[0] * qg_ref[...]).astype(BF16)
        kn_s[...] = (_head_norm(k_ref, kg_ref)[0] * kg_ref[...]).astype(BF16)
        gate = gate_ref[...].astype(F32)
        sg = _sigmoid(gate)
        dog_v = dog_ref[...].astype(F32)
        do_s[...] = (dog_v * (gate * sg)).astype(BF16)
        dgate_ref[...] = (dog_v * o_ref[...].astype(F32) * (sg * (1.0 + gate * (1.0 - sg)))).astype(BF16)
        dkn_s[...] = jnp.zeros_like(dkn_s)
        dv_s[...] = jnp.zeros_like(dv_s)

        row = lax.broadcasted_iota(jnp.int32, (TQ, TQ), 0)
        col = lax.broadcasted_iota(jnp.int32, (TQ, TQ), 1)
        later = _ones_where(row > col)
        earlier = _ones_where(row < col)
        causal = col < row

        def q_block(qi, _):
            q0 = pl.multiple_of(qi * TQ, TQ)
            qb = qn_s[pl.ds(q0, TQ), :]
            dob = do_s[pl.ds(q0, TQ), :]

            def weights(ki, carry, diag):
                k0 = pl.multiple_of(ki * TQ, TQ)
                z = _dot(qb, kn_s[pl.ds(k0, TQ), :], NT) * scale
                ls = _log_sigmoid(z)
                l1m = ls - z
                if diag:
                    l1m = jnp.where(causal, l1m, 0.0)
                hi, lo = _split_bf16(l1m)
                suffix = _dot(hi, later, NN) + _dot(lo, later, NN) + carry
                w = jnp.exp(ls + suffix)
                if diag:
                    w = jnp.where(causal, w, 0.0)
                w_s[ki] = w
                sg_s[ki] = jnp.exp(ls)
                return carry + jnp.sum(l1m, axis=1, keepdims=True)

            carry = weights(qi, jnp.zeros((TQ, 1), F32), True)
            lax.fori_loop(0, qi, lambda i, c: weights(qi - 1 - i, c, False), carry)

            def grads(ki, carry, dq, diag):
                k0 = pl.multiple_of(ki * TQ, TQ)
                w = w_s[ki]
                sgz = sg_s[ki]
                da = _dot(dob, v_ref[pl.ds(k0, TQ), :], NT) * w
                hi, lo = _split_bf16(da)
                prefix = _dot(hi, earlier, NN) + _dot(lo, earlier, NN) + carry
                dz = da * (1.0 - sgz) - sgz * prefix
                if diag:
                    dz = jnp.where(causal, dz, 0.0)
                dzb = (dz * scale).astype(BF16)
                dq = dq + _dot(dzb, kn_s[pl.ds(k0, TQ), :], NN)
                dkn_s[pl.ds(k0, TQ), :] += _dot(dzb, qb, TN)
                dv_s[pl.ds(k0, TQ), :] += _dot(w.astype(BF16), dob, TN)
                return carry + jnp.sum(da, axis=1, keepdims=True), dq

            carry, dq = lax.fori_loop(0, qi, lambda i, c: grads(i, c[0], c[1], False),
                                      (jnp.zeros((TQ, 1), F32), jnp.zeros((TQ, HEAD_DIM), F32)))
            carry, dq = grads(qi, carry, dq, True)
            dqn_s[pl.ds(q0, TQ), :] = dq
            return 0

        lax.fori_loop(0, nq, q_block, 0)

        def norm_bwd(xh, r, g_ref, dn, dx_ref, dg_ref):
            dg_ref[...] += jnp.sum(dn * xh, axis=0, keepdims=True)
            dxh = dn * g_ref[...]
            dx_ref[...] = (r * (dxh - xh * jnp.mean(dxh * xh, axis=-1, keepdims=True))).astype(BF16)

        qh, rq = _head_norm(q_ref, qg_ref)
        norm_bwd(qh, rq, qg_ref, dqn_s[...], dq_ref, dqg_ref)
        kh, rk = _head_norm(k_ref, kg_ref)
        norm_bwd(kh, rk, kg_ref, dkn_s[...], dk_ref, dkg_ref)
        dv_ref[...] = dv_s[...].astype(BF16)

    spec = _attn_specs(S, H)
    vec = pl.BlockSpec((1, HEAD_DIM), lambda b, h: (0, 0))
    blk = pl.BlockSpec((S, HEAD_DIM), lambda b, h: (b, h))
    big = jax.ShapeDtypeStruct((B * S, H * HEAD_DIM), BF16)
    small = jax.ShapeDtypeStruct((1, HEAD_DIM), F32)
    return pl.pallas_call(
        body, name=name,
        out_shape=(big, big, big, big, small, small),
        grid=(B, H),
        in_specs=[spec(0), spec(1), spec(2), spec(3), blk, blk, vec, vec],
        out_specs=(blk, blk, blk, blk, vec, vec),
        scratch_shapes=[pltpu.VMEM((S, HEAD_DIM), BF16)] * 3 + [pltpu.VMEM((S, HEAD_DIM), F32)] * 3
        + [pltpu.VMEM((nq, TQ, TQ), F32)] * 2,
        compiler_params=_params(("arbitrary", "arbitrary")),
    )(proj, proj, proj, proj, o, dog, q_g, k_g)


HALF = GROUPS_PER_BLOCK * STATE


def _cmul(ar, ai, br, bi):
    return ar * br - ai * bi, ar * bi + ai * br


def _cpow(ar, ai, n):
    rr = ri = None
    while n:
        if n & 1:
            rr, ri = (ar, ai) if rr is None else _cmul(rr, ri, ar, ai)
        n >>= 1
        if n:
            ar, ai = _cmul(ar, ai, ar, ai)
    return rr, ri


def _segment_carry(er, ei, lr, li, seg_len, segs_per_seq, reverse):
    Lr, Li = _cpow(lr, li, seg_len)
    pos = lax.broadcasted_iota(jnp.int32, er.shape, 0) % segs_per_seq
    outr = jnp.zeros_like(er)
    outi = jnp.zeros_like(ei)
    pr = pi = None
    for d in range(1, segs_per_seq):
        shift = (SUBLANES - d) if reverse else d
        sr = pltpu.roll(er, shift, 0)
        si = pltpu.roll(ei, shift, 0)
        ok = (pos + d < segs_per_seq) if reverse else (pos >= d)
        sr = jnp.where(ok, sr, 0.0)
        si = jnp.where(ok, si, 0.0)
        if pr is not None:
            sr, si = _cmul(sr, si, pr, pi)
        outr = outr + sr
        outi = outi + si
        pr, pi = (Lr, Li) if pr is None else _cmul(pr, pi, Lr, Li)
    return outr, outi


def _s5_sizes(T, B):
    assert SUBLANES % B == 0
    segs_per_seq = SUBLANES // B
    n_steps = T // SUBLANES
    cj = _tile(n_steps, 64)
    return segs_per_seq, n_steps, cj


def _s5_fwd(u_p, wb, wc, lam, dvec, B, name):
    T, C = u_p.shape
    nb = C // 128
    segs_per_seq, n_steps, cj = _s5_sizes(T, B)
    n_chunks = n_steps // cj
    rows = cj * SUBLANES

    def body(u_ref, wb_ref, wc_ref, lam_ref, d_ref, y_ref, hin_ref, bu_s, h_s):
        lr = jnp.broadcast_to(lam_ref[:, :HALF], (SUBLANES, HALF))
        li = jnp.broadcast_to(lam_ref[:, HALF:], (SUBLANES, HALF))

        def scan_chunk(c, hr, hi, store):
            r0 = pl.multiple_of(c * rows, rows)
            bu_s[...] = _dot(u_ref[pl.ds(r0, rows), :], wb_ref[...], NN)

            def step(j, carry):
                hr, hi = carry
                o = pl.multiple_of(j * SUBLANES, SUBLANES)
                nr = lr * hr - li * hi + bu_s[pl.ds(o, SUBLANES), :HALF]
                ni = lr * hi + li * hr + bu_s[pl.ds(o, SUBLANES), HALF:]
                if store:
                    h_s[pl.ds(o, SUBLANES), :HALF] = nr
                    h_s[pl.ds(o, SUBLANES), HALF:] = ni
                return nr, ni

            hr, hi = lax.fori_loop(0, cj, step, (hr, hi))
            if store:
                uv = u_ref[pl.ds(r0, rows), :].astype(F32)
                y_ref[pl.ds(r0, rows), :] = _dot(h_s[...].astype(BF16), wc_ref[...], NN) + d_ref[...] * uv
            return hr, hi

        zero = jnp.zeros((SUBLANES, HALF), F32)
        er, ei = lax.fori_loop(0, n_chunks, lambda c, h: scan_chunk(c, h[0], h[1], False), (zero, zero))
        h0r, h0i = _segment_carry(er, ei, lr, li, n_steps, segs_per_seq, False)
        hin_ref[:, :HALF] = h0r
        hin_ref[:, HALF:] = h0i
        lax.fori_loop(0, n_chunks, lambda c, h: scan_chunk(c, h[0], h[1], True), (h0r, h0i))

    return pl.pallas_call(
        body, name=name,
        out_shape=(jax.ShapeDtypeStruct((T, C), F32), jax.ShapeDtypeStruct((nb, SUBLANES, 2 * HALF), F32)),
        grid=(nb,),
        in_specs=[pl.BlockSpec((T, 128), lambda g: (0, g)),
                  pl.BlockSpec((None, 128, 2 * HALF), lambda g: (g, 0, 0)),
                  pl.BlockSpec((None, 2 * HALF, 128), lambda g: (g, 0, 0)),
                  pl.BlockSpec((None, 1, 2 * HALF), lambda g: (g, 0, 0)),
                  pl.BlockSpec((1, 128), lambda g: (0, g))],
        out_specs=(pl.BlockSpec((T, 128), lambda g: (0, g)),
                   pl.BlockSpec((None, SUBLANES, 2 * HALF), lambda g: (g, 0, 0))),
        scratch_shapes=[pltpu.VMEM((rows, 2 * HALF), F32)] * 2,
        compiler_params=_params(("parallel",)),
    )(u_p, wb, wc, lam, dvec)


def _s5_bwd(u_p, dy_p, wb, wbt, wc, wct, lam, dvec, h_in, B, name):
    T, C = u_p.shape
    nb = C // 128
    segs_per_seq, n_steps, cj = _s5_sizes(T, B)
    n_chunks = n_steps // cj
    rows = cj * SUBLANES

    def body(u_ref, dy_ref, wb_ref, wbt_ref, wc_ref, wct_ref, lam_ref, d_ref, hin_ref,
             du_ref, dwb_ref, dwc_ref, dlam_ref, dd_ref, h_all, x_s, g_s):
        lr = jnp.broadcast_to(lam_ref[:, :HALF], (SUBLANES, HALF))
        li = jnp.broadcast_to(lam_ref[:, HALF:], (SUBLANES, HALF))
        zero = jnp.zeros((SUBLANES, HALF), F32)

        h_all[pl.ds(0, SUBLANES), :] = hin_ref[...]

        def fwd_chunk(c, carry):
            r0 = pl.multiple_of(c * rows, rows)
            x_s[...] = _dot(u_ref[pl.ds(r0, rows), :], wb_ref[...], NN)

            def step(j, carry):
                hr, hi = carry
                o = pl.multiple_of(j * SUBLANES, SUBLANES)
                nr = lr * hr - li * hi + x_s[pl.ds(o, SUBLANES), :HALF]
                ni = lr * hi + li * hr + x_s[pl.ds(o, SUBLANES), HALF:]
                late = pl.multiple_of(r0 + o + SUBLANES, SUBLANES)
                h_all[pl.ds(late, SUBLANES), :HALF] = nr
                h_all[pl.ds(late, SUBLANES), HALF:] = ni
                return nr, ni

            return lax.fori_loop(0, cj, step, carry)

        lax.fori_loop(0, n_chunks, fwd_chunk, (hin_ref[:, :HALF], hin_ref[:, HALF:]))

        def bwd_chunk(i, carry, store):
            c = n_chunks - 1 - i
            r0 = pl.multiple_of(c * rows, rows)
            dyv = dy_ref[pl.ds(r0, rows), :]
            x_s[...] = _dot(dyv, wct_ref[...], NN)

            def step(jj, carry):
                ar, ai, accr, acci = carry
                j = cj - 1 - jj
                o = pl.multiple_of(j * SUBLANES, SUBLANES)
                nr = lr * ar + li * ai + x_s[pl.ds(o, SUBLANES), :HALF]
                ni = lr * ai - li * ar + x_s[pl.ds(o, SUBLANES), HALF:]
                if store:
                    g_s[pl.ds(o, SUBLANES), :HALF] = nr
                    g_s[pl.ds(o, SUBLANES), HALF:] = ni
                    prev = pl.multiple_of(r0 + o, SUBLANES)
                    pr = h_all[pl.ds(prev, SUBLANES), :HALF]
                    pi = h_all[pl.ds(prev, SUBLANES), HALF:]
                    accr = accr + nr * pr + ni * pi
                    acci = acci + ni * pr - nr * pi
                return nr, ni, accr, acci

            carry = lax.fori_loop(0, cj, step, carry)
            if store:
                gb = g_s[...].astype(BF16)
                uv = u_ref[pl.ds(r0, rows), :]
                dyf = dyv.astype(F32)
                du_ref[pl.ds(r0, rows), :] = (_dot(gb, wbt_ref[...], NN) + d_ref[...] * dyf).astype(BF16)
                dwb_ref[...] += _dot(uv, gb, TN)
                hb = h_all[pl.ds(pl.multiple_of(r0 + SUBLANES, SUBLANES), rows), :].astype(BF16)
                dwc_ref[...] += _dot(hb, dyv, TN)
                dd_ref[...] += jnp.sum(dyf * uv.astype(F32), axis=0, keepdims=True)
            return carry

        er, ei, _, _ = lax.fori_loop(0, n_chunks, lambda i, c: bwd_chunk(i, c, False), (zero, zero, zero, zero))
        a0r, a0i = _segment_carry(er, ei, lr, -li, n_steps, segs_per_seq, True)
        dwb_ref[...] = jnp.zeros_like(dwb_ref)
        dwc_ref[...] = jnp.zeros_like(dwc_ref)
        dd_ref[...] = jnp.zeros_like(dd_ref)
        _, _, accr, acci = lax.fori_loop(0, n_chunks, lambda i, c: bwd_chunk(i, c, True), (a0r, a0i, zero, zero))
        dlam_ref[:, :HALF] = jnp.sum(accr, axis=0, keepdims=True)
        dlam_ref[:, HALF:] = jnp.sum(acci, axis=0, keepdims=True)

    col = pl.BlockSpec((T, 128), lambda g: (0, g))
    vec = pl.BlockSpec((1, 128), lambda g: (0, g))

    def per_block(*shape):
        return pl.BlockSpec((None,) + shape, lambda g: (g, 0, 0))

    return pl.pallas_call(
        body, name=name,
        out_shape=(jax.ShapeDtypeStruct((T, C), BF16),
                   jax.ShapeDtypeStruct((nb, 128, 2 * HALF), F32),
                   jax.ShapeDtypeStruct((nb, 2 * HALF, 128), F32),
                   jax.ShapeDtypeStruct((nb, 1, 2 * HALF), F32),
                   jax.ShapeDtypeStruct((1, C), F32)),
        grid=(nb,),
        in_specs=[col, col, per_block(128, 2 * HALF), per_block(2 * HALF, 128), per_block(2 * HALF, 128),
                  per_block(128, 2 * HALF), per_block(1, 2 * HALF), vec, per_block(SUBLANES, 2 * HALF)],
        out_specs=(col, per_block(128, 2 * HALF), per_block(2 * HALF, 128), per_block(1, 2 * HALF), vec),
        scratch_shapes=[pltpu.VMEM((T + SUBLANES, 2 * HALF), F32),
                        pltpu.VMEM((rows, 2 * HALF), F32), pltpu.VMEM((rows, 2 * HALF), F32)],
        compiler_params=_params(("parallel",)),
    )(u_p, dy_p, wb, wbt, wc, wct, lam, dvec, h_in)


def _discretize(a_re, a_im, log_dt, b_re, b_im):
    dt = jnp.exp(log_dt)[:, None]
    mag = jnp.exp(a_re * dt)
    lam_re = mag * jnp.cos(a_im * dt)
    lam_im = mag * jnp.sin(a_im * dt)
    den = a_re * a_re + a_im * a_im
    f_re = ((lam_re - 1.0) * a_re + lam_im * a_im) / den
    f_im = (lam_im * a_re - (lam_re - 1.0) * a_im) / den
    bb_re = f_re[..., None] * b_re - f_im[..., None] * b_im
    bb_im = f_re[..., None] * b_im + f_im[..., None] * b_re
    return lam_re, lam_im, bb_re, bb_im


def _block_diag_in(bb_re, bb_im):
    eye = jnp.eye(GROUPS_PER_BLOCK, dtype=F32)

    def one(bb):
        t = bb.reshape(-1, GROUPS_PER_BLOCK, STATE, GROUP)
        return jnp.einsum('gapi,ab->gaibp', t, eye).reshape(-1, 128, HALF)

    return jnp.concatenate([one(bb_re), one(bb_im)], axis=-1)


def _block_diag_in_grad(dwb):
    eye = jnp.eye(GROUPS_PER_BLOCK, dtype=F32)

    def one(d):
        t = d.reshape(-1, GROUPS_PER_BLOCK, GROUP, GROUPS_PER_BLOCK, STATE)
        return jnp.einsum('gaibp,ab->gapi', t, eye).reshape(-1, STATE, GROUP)

    return one(dwb[..., :HALF]), one(dwb[..., HALF:])


def _block_diag_out(c_re, c_im):
    eye = jnp.eye(GROUPS_PER_BLOCK, dtype=F32)

    def one(cc):
        t = cc.reshape(-1, GROUPS_PER_BLOCK, GROUP, STATE)
        return jnp.einsum('gaip,ab->gbpai', t, eye).reshape(-1, HALF, 128)

    return jnp.concatenate([one(c_re), -one(c_im)], axis=1)


def _block_diag_out_grad(dwc):
    eye = jnp.eye(GROUPS_PER_BLOCK, dtype=F32)

    def one(d):
        t = d.reshape(-1, GROUPS_PER_BLOCK, STATE, GROUPS_PER_BLOCK, GROUP)
        return jnp.einsum('gbpai,ab->gaip', t, eye).reshape(-1, GROUP, STATE)

    return one(dwc[:, :HALF]), -one(dwc[:, HALF:])


def _pack(parts):
    return jnp.concatenate([p.reshape(-1, PACK_W) for p in parts], axis=0)


def _unpack(buf, shapes):
    lead = buf.shape[:-2]
    out, r = [], 0
    for s in shapes:
        n = math.prod(s) // PACK_W
        out.append(buf[..., r:r + n, :].reshape(lead + tuple(s)))
        r += n
    return out


def _permute_rows(a, n_steps):
    T, C = a.shape
    return a.reshape(SUBLANES, n_steps, C).transpose(1, 0, 2).reshape(T, C)


def _unpermute_rows(a, n_steps):
    T, C = a.shape
    return a.reshape(n_steps, SUBLANES, C).transpose(1, 0, 2).reshape(T, C)


def kernel(x, norm_g, attn_w_in, attn_q_g, attn_k_g, attn_w_out, ssm_w_in, ssm_A_re, ssm_A_im, ssm_log_dt, ssm_B_re, ssm_B_im, ssm_C_re, ssm_C_im, ssm_D, ssm_glu_w, ssm_glu_b, ssm_w_out, loss_target, m_norm_g, m_attn_w_in, m_attn_q_g, m_attn_k_g, m_attn_w_out, m_ssm_w_in, m_ssm_A_re, m_ssm_A_im, m_ssm_log_dt, m_ssm_B_re, m_ssm_B_im, m_ssm_C_re, m_ssm_C_im, m_ssm_D, m_ssm_glu_w, m_ssm_glu_b, m_ssm_w_out, v_norm_g, v_attn_w_in, v_attn_q_g, v_attn_k_g, v_attn_w_out, v_ssm_w_in, v_ssm_A_re, v_ssm_A_im, v_ssm_log_dt, v_ssm_B_re, v_ssm_B_im, v_ssm_C_re, v_ssm_C_im, v_ssm_D, v_ssm_glu_w, v_ssm_glu_b, v_ssm_w_out):
    B, S, D = x.shape
    T = B * S
    H = D // HEAD_DIM
    G_loc = ssm_A_re.shape[1]
    G = G_loc * N_DEV
    n_steps = T // SUBLANES
    xf = x.reshape(T, D)
    target = loss_target.reshape(T, D)

    w_in0 = _exchange(_cast_bf16(attn_w_in[0], "cast_attn_w_in"), False, "gather_attn_w_in")
    w_out0 = _exchange(_cast_bf16(attn_w_out[0], "cast_attn_w_out"), False, "gather_attn_w_out")
    w_in1 = _exchange(_cast_bf16(ssm_w_in[0], "cast_ssm_w_in"), False, "gather_ssm_w_in")
    w_glu = _exchange(_cast_bf16(ssm_glu_w[0], "cast_ssm_glu_w"), False, "gather_ssm_glu_w")
    w_out1 = _exchange(_cast_bf16(ssm_w_out[0], "cast_ssm_w_out"), False, "gather_ssm_w_out")
    w_out0 = w_out0.reshape(1, D, D)
    w_glu = w_glu.reshape(1, D, D)
    w_out1 = w_out1.reshape(1, D, D)

    small_shapes = [(G_loc, STATE), (G_loc, STATE), (G_loc, STATE, GROUP), (G_loc, STATE, GROUP),
                    (G_loc, GROUP, STATE), (G_loc, GROUP, STATE), (G_loc * GROUP,), (G_loc * GROUP,)]
    disc_in = (ssm_A_re[0], ssm_A_im[0], ssm_log_dt[0], ssm_B_re[0], ssm_B_im[0])
    (lam_re, lam_im, bb_re, bb_im), disc_vjp = jax.vjp(_discretize, *disc_in)
    small = _pack([lam_re, lam_im, bb_re, bb_im, ssm_C_re[0], ssm_C_im[0], ssm_D[0], ssm_glu_b[0]])
    small_all = _exchange(small, False, "gather_small")
    lam_re_a, lam_im_a, bb_re_a, bb_im_a, c_re_a, c_im_a, d_a, glu_b_a = [
        t.reshape((G,) + t.shape[2:]) if t.ndim > 2 else t.reshape(-1)
        for t in _unpack(small_all, small_shapes)]
    wb = _block_diag_in(bb_re_a, bb_im_a)
    wc = _block_diag_out(c_re_a, c_im_a)
    wb_b, wc_b = wb.astype(BF16), wc.astype(BF16)
    wbt_b, wct_b = wb_b.transpose(0, 2, 1), wc_b.transpose(0, 2, 1)
    lam = jnp.concatenate([lam_re_a.reshape(-1, 1, HALF), lam_im_a.reshape(-1, 1, HALF)], axis=-1)
    d_row = d_a.reshape(1, D)
    glu_b_row = glu_b_a.reshape(1, D)
    g0, g1 = norm_g[0:1], norm_g[1:2]
    q_g, k_g = attn_q_g, attn_k_g

    h0 = _rmsnorm_fwd(xf, g0, "norm0")
    proj0 = _mm_nn(h0, w_in0, BF16, "attn_in")
    og, o = _attn_fwd(proj0, q_g, k_g, B, S, H, "attn_fwd")
    x1 = _mm_nn(og, w_out0, F32, "attn_out", residual=xf)

    h1 = _rmsnorm_fwd(x1, g1, "norm1")
    proj1 = _mm_nn(h1, w_in1, BF16, "ssm_in")
    u_p = _permute_rows(proj1[:, :D], n_steps)
    gate1 = proj1[:, D:]
    y_p, h_in = _s5_fwd(u_p, wb_b, wc_b, lam, d_row, B, "s5_fwd")
    y_ssm = _unpermute_rows(y_p, n_steps)
    (yg,) = _ew(lambda a: (_gelu(a)[0],), [y_ssm], [], [BF16], 0, "gelu")
    z = _mm_nn(yg, w_glu, F32, "glu_in", bias=glu_b_row)

    def glu_fwd(y, zz, gt):
        gt = gt.astype(F32)
        return (_gelu(y)[0] * _sigmoid(zz) * (gt * _sigmoid(gt)),)

    (y3,) = _ew(glu_fwd, [y_ssm, z, gate1], [], [BF16], 0, "glu_gate")
    out = _mm_nn(y3, w_out1, F32, "ssm_out", residual=x1)

    dout, dout_b, loss_part = _loss_head(out, target, "loss")
    loss = lax.psum(loss_part[0, 0], ("x", "y", "c"))

    dy3 = _mm_nt(dout_b, w_out1, F32, "ssm_out_dx")
    p_w_out1 = _mm_tn(y3, dout_b, 1, BF16, "ssm_out_dw").reshape(N_DEV, D // N_DEV, D)

    def glu_bwd(d3, y, zz, gt):
        gt = gt.astype(F32)
        sg = _sigmoid(gt)
        sz = _sigmoid(zz)
        ygv, _ = _gelu(y)
        dy2 = d3 * (gt * sg)
        dgate = d3 * (ygv * sz) * (sg * (1.0 + gt * (1.0 - sg)))
        dz = dy2 * ygv * (sz * (1.0 - sz))
        return dz, dgate, dy2 * sz, jnp.sum(dz, axis=0, keepdims=True)

    dz_b, dgate1, dyg_a, dglu_b = _ew(glu_bwd, [dy3, y_ssm, z, gate1], [], [BF16, BF16, F32], 1, "glu_gate_bwd")
    dyg_b = _mm_nt(dz_b, w_glu, F32, "glu_in_dx")
    p_w_glu = _mm_tn(yg, dz_b, 1, BF16, "glu_in_dw").reshape(N_DEV, D // N_DEV, D)

    def gelu_bwd(da, db, y):
        _, t = _gelu(y)
        dg = 0.5 * (1.0 + t) + 0.5 * y * (1.0 - t * t) * (GELU_C * (1.0 + 3.0 * 0.044715 * (y * y)))
        return ((da + db) * dg,)

    (dy_ssm,) = _ew(gelu_bwd, [dyg_a, dyg_b, y_ssm], [], [BF16], 0, "gelu_bwd")
    dy_p = _permute_rows(dy_ssm, n_steps)
    du_p, dwb, dwc, dlam, dd = _s5_bwd(u_p, dy_p, wb_b, wbt_b, wc_b, wct_b, lam, d_row, h_in, B, "s5_bwd")
    du = _unpermute_rows(du_p, n_steps)
    dproj1 = jnp.concatenate([du, dgate1], axis=1)
    dh1 = _mm_nt(dproj1, w_in1, F32, "ssm_in_dx")
    p_w_in1 = _mm_tn(h1, dproj1, N_DEV, BF16, "ssm_in_dw")
    dx1, dx1_b, dg1 = _rmsnorm_bwd(x1, dh1, dout, g1, "norm1_bwd")

    dog = _mm_nt(dx1_b, w_out0, BF16, "attn_out_dx")
    p_w_out0 = _mm_tn(og, dx1_b, 1, BF16, "attn_out_dw").reshape(N_DEV, D // N_DEV, D)
    dq, dk, dv, dgate0, dqg, dkg = _attn_bwd(proj0, o, dog, q_g, k_g, B, S, H, "attn_bwd")
    dproj0 = jnp.concatenate([dq, dk, dv, dgate0], axis=1)
    dh0 = _mm_nt(dproj0, w_in0, F32, "attn_in_dx")
    p_w_in0 = _mm_tn(h0, dproj0, N_DEV, BF16, "attn_in_dw")
    dx, _, dg0 = _rmsnorm_bwd(xf, dh0, dx1, g0, "norm0_bwd")

    def update(parts, w, m, v, name):
        recv = _exchange(parts, True, "scatter_" + name)
        return _adamw(recv, w[0], m[0], v[0], "adamw_" + name)

    r_attn_w_in = update(p_w_in0, attn_w_in, m_attn_w_in, v_attn_w_in, "attn_w_in")
    r_attn_w_out = update(p_w_out0, attn_w_out, m_attn_w_out, v_attn_w_out, "attn_w_out")
    r_ssm_w_in = update(p_w_in1, ssm_w_in, m_ssm_w_in, v_ssm_w_in, "ssm_w_in")
    r_ssm_glu_w = update(p_w_glu, ssm_glu_w, m_ssm_glu_w, v_ssm_glu_w, "ssm_glu_w")
    r_ssm_w_out = update(p_w_out1, ssm_w_out, m_ssm_w_out, v_ssm_w_out, "ssm_w_out")

    dbb_re, dbb_im = _block_diag_in_grad(dwb)
    dc_re, dc_im = _block_diag_out_grad(dwc)
    dlam_re = dlam[:, 0, :HALF].reshape(G, STATE)
    dlam_im = dlam[:, 0, HALF:].reshape(G, STATE)
    by_owner = [t.reshape((N_DEV, -1)) for t in (dlam_re, dlam_im, dbb_re, dbb_im, dc_re, dc_im, dd, dglu_b)]
    small_parts = jnp.concatenate([t.reshape(N_DEV, -1, PACK_W) for t in by_owner], axis=1)
    small_sum = _sum_parts(_exchange(small_parts, True, "scatter_small"), "sum_small")
    s_lam_re, s_lam_im, s_bb_re, s_bb_im, s_c_re, s_c_im, s_d, s_glu_b = _unpack(small_sum, small_shapes)
    g_a_re, g_a_im, g_log_dt, g_b_re, g_b_im = disc_vjp((s_lam_re, s_lam_im, s_bb_re, s_bb_im))

    local_names = ["ssm_A_re", "ssm_A_im", "ssm_log_dt", "ssm_B_re", "ssm_B_im", "ssm_C_re", "ssm_C_im",
                   "ssm_D", "ssm_glu_b"]
    local_g = [g_a_re, g_a_im, g_log_dt, g_b_re, g_b_im, s_c_re, s_c_im, s_d, s_glu_b]
    local_w = [ssm_A_re, ssm_A_im, ssm_log_dt, ssm_B_re, ssm_B_im, ssm_C_re, ssm_C_im, ssm_D, ssm_glu_b]
    local_m = [m_ssm_A_re, m_ssm_A_im, m_ssm_log_dt, m_ssm_B_re, m_ssm_B_im, m_ssm_C_re, m_ssm_C_im,
               m_ssm_D, m_ssm_glu_b]
    local_v = [v_ssm_A_re, v_ssm_A_im, v_ssm_log_dt, v_ssm_B_re, v_ssm_B_im, v_ssm_C_re, v_ssm_C_im,
               v_ssm_D, v_ssm_glu_b]
    r_local = _adamw_small(local_g, local_w, local_m, local_v, None, "adamw_small")

    rep_g = [jnp.concatenate([dg0, dg1], axis=0), dqg, dkg]
    rep_w = [norm_g, attn_q_g, attn_k_g]
    rep_m = [m_norm_g, m_attn_q_g, m_attn_k_g]
    rep_v = [v_norm_g, v_attn_q_g, v_attn_k_g]
    r_rep = _adamw_small(rep_g, rep_w, rep_m, rep_v, "gather_rep", "adamw_rep")

    res = {"attn_w_in": r_attn_w_in, "attn_w_out": r_attn_w_out, "ssm_w_in": r_ssm_w_in,
           "ssm_glu_w": r_ssm_glu_w, "ssm_w_out": r_ssm_w_out}
    ref_w = {"attn_w_in": attn_w_in, "attn_w_out": attn_w_out, "ssm_w_in": ssm_w_in,
             "ssm_glu_w": ssm_glu_w, "ssm_w_out": ssm_w_out}
    for name, r, w in zip(local_names, r_local, local_w):
        res[name], ref_w[name] = r, w
    for name, r, w in zip(["norm_g", "attn_q_g", "attn_k_g"], r_rep, rep_w):
        res[name], ref_w[name] = r, w
    order = ["norm_g", "attn_w_in", "attn_q_g", "attn_k_g", "attn_w_out", "ssm_w_in", "ssm_A_re", "ssm_A_im",
             "ssm_log_dt", "ssm_B_re", "ssm_B_im", "ssm_C_re", "ssm_C_im", "ssm_D", "ssm_glu_w", "ssm_glu_b",
             "ssm_w_out"]
    outs = [loss, dx.reshape(B, S, D)]
    for kind in range(4):
        outs += [res[n][kind].reshape(ref_w[n].shape) for n in order]
    return tuple(outs)


def _adamw_small(grads, ws, ms, vs, gather_name, name):
    sizes = [math.prod(w.shape) for w in ws]
    total = sum(sizes)
    rows = -(-total // (PACK_W * 8)) * 8
    if rows > 256:
        rows = -(-rows // 256) * 256

    def pack(ts, fill):
        flat = jnp.concatenate([t.reshape(-1).astype(F32) for t in ts])
        flat = jnp.concatenate([flat, jnp.full((rows * PACK_W - total,), fill, F32)])
        return flat.reshape(rows, PACK_W)

    g = pack(grads, 0.0)
    parts = _exchange(g, False, gather_name) if gather_name else g[None]
    res = _adamw(parts, pack(ws, 0.0), pack(ms, 0.0), pack(vs, 1.0), name)
    outs = []
    off = 0
    flats = [r.reshape(-1) for r in res]
    for n in sizes:
        outs.append(tuple(f[off:off + n] for f in flats))
        off += n
    return outs
```

```python
import functools
import math

import jax
import jax.numpy as jnp
from jax import lax
from jax.experimental import pallas as pl
from jax.experimental.pallas import tpu as pltpu

F32 = jnp.float32
BF16 = jnp.bfloat16

N_DEV = 8
HEAD_DIM = 128
GROUP = 16
STATE = 64
GROUPS_PER_BLOCK = 8
SUBLANES = 8
RMS_EPS = 1e-6
ADAM_LR, ADAM_B1, ADAM_B2, ADAM_EPS, ADAM_WD, ADAM_STEP = 0.001, 0.9, 0.999, 1e-08, 0.01, 10
VMEM_LIMIT = 56 * 1024 * 1024
GELU_C = math.sqrt(2.0 / math.pi)
PACK_W = 128


def _params(sem, **kw):
    return pltpu.CompilerParams(dimension_semantics=sem, vmem_limit_bytes=VMEM_LIMIT, **kw)


def _tile(n, t):
    t = min(n, t)
    assert n % t == 0, (n, t)
    return t


def _ones_where(cond):
    return jnp.where(cond, 1.0, 0.0).astype(BF16)


def _sigmoid(x):
    return 1.0 / (1.0 + jnp.exp(-x))


def _log_sigmoid(z):
    return jnp.minimum(z, 0.0) - jnp.log(1.0 + jnp.exp(-jnp.abs(z)))


def _dot(a, b, dims):
    return lax.dot_general(a, b, (dims, ((), ())), preferred_element_type=F32)


NN = ((1,), (0,))
NT = ((1,), (1,))
TN = ((0,), (0,))


def _exchange(x, scatter, name):
    shape = x.shape[1:] if scatter else x.shape

    def body(x_ref, out_ref, send_sems, recv_sems, local_sem):
        ix, iy, ic = lax.axis_index("x"), lax.axis_index("y"), lax.axis_index("c")
        me = 4 * ix + 2 * iy + ic

        def peer(k):
            kx, ky, kc = (k >> 2) & 1, (k >> 1) & 1, k & 1
            px, py, pc = ix ^ kx, iy ^ ky, ic ^ kc
            return (px, py, pc), 4 * px + 2 * py + pc

        mine = pltpu.make_async_copy(x_ref.at[me] if scatter else x_ref, out_ref.at[me], local_sem)
        mine.start()
        copies = []
        for k in range(1, N_DEV):
            pid, pidx = peer(k)
            cp = pltpu.make_async_remote_copy(
                src_ref=x_ref.at[pidx] if scatter else x_ref,
                dst_ref=out_ref.at[me],
                send_sem=send_sems.at[k - 1], recv_sem=recv_sems.at[k - 1],
                device_id=pid, device_id_type=pl.DeviceIdType.MESH)
            cp.start()
            copies.append(cp)
        for k in range(1, N_DEV):
            pid, pidx = peer(k)
            pltpu.make_async_remote_copy(
                src_ref=x_ref.at[pidx] if scatter else x_ref,
                dst_ref=out_ref.at[pidx],
                send_sem=send_sems.at[k - 1], recv_sem=recv_sems.at[k - 1],
                device_id=pid, device_id_type=pl.DeviceIdType.MESH).wait_recv()
        for cp in copies:
            cp.wait_send()
        mine.wait()

    return pl.pallas_call(
        body, name=name,
        out_shape=jax.ShapeDtypeStruct((N_DEV,) + tuple(shape), x.dtype),
        in_specs=[pl.BlockSpec(memory_space=pl.ANY)],
        out_specs=pl.BlockSpec(memory_space=pl.ANY),
        scratch_shapes=[pltpu.SemaphoreType.DMA((N_DEV - 1,)), pltpu.SemaphoreType.DMA((N_DEV - 1,)),
                        pltpu.SemaphoreType.DMA],
    )(x)


def _accumulate(acc, part, step, n_steps, finish):
    if n_steps == 1:
        finish(part)
        return

    @pl.when(step == 0)
    def _():
        acc[...] = part

    @pl.when((step > 0) & (step < n_steps - 1))
    def _():
        acc[...] += part

    @pl.when(step == n_steps - 1)
    def _():
        finish(acc[...] + part)


def _mm_nn(a, b, out_dtype, name, bias=None, residual=None, tm=1024, tn=1024, tk=2048):
    M, K = a.shape
    J, K2, Nj = b.shape
    assert K == K2
    tm, tn, tk = _tile(M, tm), _tile(Nj, tn), _tile(K, tk)
    nb, nk = Nj // tn, K // tk

    def body(*refs):
        a_ref, b_ref = refs[0], refs[1]
        i = 2
        bias_ref = res_ref = None
        if bias is not None:
            bias_ref = refs[i]; i += 1
        if residual is not None:
            res_ref = refs[i]; i += 1
        o_ref, acc = refs[i], refs[i + 1]

        def finish(r):
            if bias_ref is not None:
                r = r + bias_ref[...]
            if res_ref is not None:
                r = r + res_ref[...]
            o_ref[...] = r.astype(out_dtype)

        _accumulate(acc, _dot(a_ref[...], b_ref[...], NN), pl.program_id(3), nk, finish)

    in_specs = [pl.BlockSpec((tm, tk), lambda j, m, n, k: (m, k)),
                pl.BlockSpec((None, tk, tn), lambda j, m, n, k: (j, k, n))]
    args = [a, b]
    if bias is not None:
        in_specs.append(pl.BlockSpec((1, tn), lambda j, m, n, k: (0, j * nb + n)))
        args.append(bias)
    if residual is not None:
        in_specs.append(pl.BlockSpec((tm, tn), lambda j, m, n, k: (m, j * nb + n)))
        args.append(residual)
    return pl.pallas_call(
        body, name=name,
        out_shape=jax.ShapeDtypeStruct((M, J * Nj), out_dtype),
        grid=(J, M // tm, nb, nk),
        in_specs=in_specs,
        out_specs=pl.BlockSpec((tm, tn), lambda j, m, n, k: (m, j * nb + n)),
        scratch_shapes=[pltpu.VMEM((tm, tn), F32)],
        compiler_params=_params(("parallel", "parallel", "parallel", "arbitrary")),
    )(*args)


def _mm_nt(a, b, out_dtype, name, tm=1024, tp=1024, tq=2048):
    M, Q = a.shape
    J, P, Qj = b.shape
    assert Q == J * Qj
    tm, tp, tq = _tile(M, tm), _tile(P, tp), _tile(Qj, tq)
    nq = Qj // tq

    def body(a_ref, b_ref, o_ref, acc):
        def finish(r):
            o_ref[...] = r.astype(out_dtype)

        _accumulate(acc, _dot(a_ref[...], b_ref[...], NT), pl.program_id(2) * nq + pl.program_id(3), J * nq, finish)

    return pl.pallas_call(
        body, name=name,
        out_shape=jax.ShapeDtypeStruct((M, P), out_dtype),
        grid=(M // tm, P // tp, J, nq),
        in_specs=[pl.BlockSpec((tm, tq), lambda m, p, j, q: (m, j * nq + q)),
                  pl.BlockSpec((None, tp, tq), lambda m, p, j, q: (j, p, q))],
        out_specs=pl.BlockSpec((tm, tp), lambda m, p, j, q: (m, p)),
        scratch_shapes=[pltpu.VMEM((tm, tp), F32)],
        compiler_params=_params(("parallel", "parallel", "arbitrary", "arbitrary")),
    )(a, b)


def _mm_tn(a, b, J, out_dtype, name, tm=1024, tn=1024, tr=2048):
    R, M = a.shape
    R2, N = b.shape
    assert R == R2 and N % J == 0
    Nj = N // J
    tm, tn, tr = _tile(M, tm), _tile(Nj, tn), _tile(R, tr)
    nb, nr = Nj // tn, R // tr

    def body(a_ref, b_ref, o_ref, acc):
        def finish(r):
            o_ref[...] = r.astype(out_dtype)

        _accumulate(acc, _dot(a_ref[...], b_ref[...], TN), pl.program_id(3), nr, finish)

    return pl.pallas_call(
        body, name=name,
        out_shape=jax.ShapeDtypeStruct((J, M, Nj), out_dtype),
        grid=(J, M // tm, nb, nr),
        in_specs=[pl.BlockSpec((tr, tm), lambda j, m, n, r: (r, m)),
                  pl.BlockSpec((tr, tn), lambda j, m, n, r: (r, j * nb + n))],
        out_specs=pl.BlockSpec((None, tm, tn), lambda j, m, n, r: (j, m, n)),
        scratch_shapes=[pltpu.VMEM((tm, tn), F32)],
        compiler_params=_params(("parallel", "parallel", "parallel", "arbitrary")),
    )(a, b)


def _ew(fn, ins, vecs, out_dtypes, n_acc, name, tr=256, tc=1024):
    T, C = ins[0].shape
    tr, tc = _tile(T, tr), _tile(C, tc)
    n_in, n_vec, n_out = len(ins), len(vecs), len(out_dtypes)

    def body(*refs):
        in_refs = refs[:n_in + n_vec]
        out_refs = refs[n_in + n_vec:n_in + n_vec + n_out]
        acc_refs = refs[n_in + n_vec + n_out:]
        res = fn(*[r[...] for r in in_refs])
        for o_ref, v in zip(out_refs, res[:n_out]):
            o_ref[...] = v.astype(o_ref.dtype)
        if n_acc:
            r = pl.program_id(1)

            @pl.when(r == 0)
            def _():
                for a_ref in acc_refs:
                    a_ref[...] = jnp.zeros_like(a_ref)

            for a_ref, v in zip(acc_refs, res[n_out:]):
                a_ref[...] += v

    blk = pl.BlockSpec((tr, tc), lambda c, r: (r, c))
    vec = pl.BlockSpec((1, tc), lambda c, r: (0, c))
    out = pl.pallas_call(
        body, name=name,
        out_shape=tuple([jax.ShapeDtypeStruct((T, C), d) for d in out_dtypes]
                        + [jax.ShapeDtypeStruct((1, C), F32)] * n_acc),
        grid=(C // tc, T // tr),
        in_specs=[blk] * n_in + [vec] * n_vec,
        out_specs=tuple([blk] * n_out + [vec] * n_acc),
        compiler_params=_params(("parallel", "arbitrary")),
    )(*ins, *vecs)
    return out


def _cast_bf16(w, name):
    return _ew(lambda a: (a,), [w], [], [BF16], 0, name)[0]


def _rmsnorm_fwd(x, g, name, tr=128):
    T, D = x.shape
    tr = _tile(T, tr)

    def body(x_ref, g_ref, h_ref):
        xv = x_ref[...]
        r = lax.rsqrt(jnp.mean(xv * xv, axis=-1, keepdims=True) + RMS_EPS)
        h_ref[...] = ((xv * r) * g_ref[...]).astype(BF16)

    return pl.pallas_call(
        body, name=name,
        out_shape=jax.ShapeDtypeStruct((T, D), BF16),
        grid=(T // tr,),
        in_specs=[pl.BlockSpec((tr, D), lambda i: (i, 0)), pl.BlockSpec((1, D), lambda i: (0, 0))],
        out_specs=pl.BlockSpec((tr, D), lambda i: (i, 0)),
        compiler_params=_params(("parallel",)),
    )(x, g)


def _rmsnorm_bwd(x, dh, dres, g, name, tr=128):
    T, D = x.shape
    tr = _tile(T, tr)

    def body(x_ref, dh_ref, dres_ref, g_ref, dx_ref, dxb_ref, dg_ref):
        xv = x_ref[...]
        r = lax.rsqrt(jnp.mean(xv * xv, axis=-1, keepdims=True) + RMS_EPS)
        xn = xv * r
        dhv = dh_ref[...].astype(F32)
        dxn = dhv * g_ref[...]
        dx = dres_ref[...] + r * (dxn - xn * jnp.mean(dxn * xn, axis=-1, keepdims=True))
        dx_ref[...] = dx
        dxb_ref[...] = dx.astype(BF16)

        @pl.when(pl.program_id(0) == 0)
        def _():
            dg_ref[...] = jnp.zeros_like(dg_ref)

        dg_ref[...] += jnp.sum(dhv * xn, axis=0, keepdims=True)

    blk = pl.BlockSpec((tr, D), lambda i: (i, 0))
    vec = pl.BlockSpec((1, D), lambda i: (0, 0))
    return pl.pallas_call(
        body, name=name,
        out_shape=(jax.ShapeDtypeStruct((T, D), F32), jax.ShapeDtypeStruct((T, D), BF16),
                   jax.ShapeDtypeStruct((1, D), F32)),
        grid=(T // tr,),
        in_specs=[blk, blk, blk, vec],
        out_specs=(blk, blk, vec),
        compiler_params=_params(("arbitrary",)),
    )(x, dh, dres, g)


def _loss_head(y, target, name, tr=128):
    T, D = y.shape
    tr = _tile(T, tr)
    n = T // tr

    def body(y_ref, t_ref, dy_ref, dyb_ref, loss_ref, acc):
        i = pl.program_id(0)

        @pl.when(i == 0)
        def _():
            acc[...] = jnp.zeros_like(acc)

        err = y_ref[...] - t_ref[...]
        dy = err * (1.0 / D)
        dy_ref[...] = dy
        dyb_ref[...] = dy.astype(BF16)
        acc[...] += jnp.sum(err * err, axis=0, keepdims=True)

        @pl.when(i == n - 1)
        def _():
            loss_ref[...] = jnp.sum(acc[...], axis=1, keepdims=True) * (0.5 / D)

    blk = pl.BlockSpec((tr, D), lambda i: (i, 0))
    return pl.pallas_call(
        body, name=name,
        out_shape=(jax.ShapeDtypeStruct((T, D), F32), jax.ShapeDtypeStruct((T, D), BF16),
                   jax.ShapeDtypeStruct((1, 1), F32)),
        grid=(n,),
        in_specs=[blk, blk],
        out_specs=(blk, blk, pl.BlockSpec((1, 1), lambda i: (0, 0))),
        scratch_shapes=[pltpu.VMEM((1, D), F32)],
        compiler_params=_params(("arbitrary",)),
    )(y, target)


def _gelu(x):
    t = jnp.tanh(GELU_C * (x + 0.044715 * (x * x * x)))
    return x * (0.5 * (1.0 + t)), t


def _adamw(parts, w, m, v, name, tr=256, tc=1024):
    n, R, C = parts.shape
    tr, tc = _tile(R, tr), _tile(C, tc)
    c1 = 1.0 - ADAM_B1 ** ADAM_STEP
    c2 = 1.0 - ADAM_B2 ** ADAM_STEP

    def body(p_ref, w_ref, m_ref, v_ref, g_out, d_out, m_out, v_out):
        g = p_ref[0].astype(F32)
        for k in range(1, n):
            g = g + p_ref[k].astype(F32)
        mn = ADAM_B1 * m_ref[...] + (1.0 - ADAM_B1) * g
        vn = ADAM_B2 * v_ref[...] + (1.0 - ADAM_B2) * (g * g)
        m_hat = mn / c1
        v_hat = vn / c2
        g_out[...] = g
        d_out[...] = -ADAM_LR * (m_hat / (jnp.sqrt(v_hat) + ADAM_EPS) + ADAM_WD * w_ref[...])
        m_out[...] = mn
        v_out[...] = vn

    blk = pl.BlockSpec((tr, tc), lambda r, c: (r, c))
    return pl.pallas_call(
        body, name=name,
        out_shape=tuple([jax.ShapeDtypeStruct((R, C), F32)] * 4),
        grid=(R // tr, C // tc),
        in_specs=[pl.BlockSpec((n, tr, tc), lambda r, c: (0, r, c)), blk, blk, blk],
        out_specs=(blk, blk, blk, blk),
        compiler_params=_params(("parallel", "parallel")),
    )(parts, w, m, v)


def _sum_parts(parts, name):
    n, R, C = parts.shape

    def body(p_ref, o_ref):
        g = p_ref[0].astype(F32)
        for k in range(1, n):
            g = g + p_ref[k].astype(F32)
        o_ref[...] = g

    return pl.pallas_call(
        body, name=name,
        out_shape=jax.ShapeDtypeStruct((R, C), F32),
        compiler_params=pltpu.CompilerParams(vmem_limit_bytes=VMEM_LIMIT),
    )(parts)


def _head_norm(xv):
    xv = xv.astype(F32)
    r = lax.rsqrt(jnp.mean(xv * xv, axis=-1, keepdims=True) + RMS_EPS)
    return xv * r, r


def _split_bf16(x):
    hi = x.astype(BF16)
    lo = (x - hi.astype(F32)).astype(BF16)
    return hi, lo


FWD_HEADS = 4
BWD_HEADS = 2


def _attn_specs(S, H, HP):
    def spec(part):
        return pl.BlockSpec((S, HP * HEAD_DIM), lambda b, h: (b, part * (H // HP) + h))
    return spec


def _lanes(hh):
    return slice(hh * HEAD_DIM, (hh + 1) * HEAD_DIM)


def _attn_fwd(proj, q_g, k_g, B, S, H, name):
    TQ = _tile(S, 256)
    nq = S // TQ
    scale = 1.0 / math.sqrt(HEAD_DIM)
    HP = FWD_HEADS
    ATT_W = HP * HEAD_DIM
    heads = range(HP)

    def body(q_ref, k_ref, v_ref, gate_ref, qg_ref, kg_ref, og_ref, o_ref, qn_s, kn_s):
        for hh in heads:
            qn_s[:, _lanes(hh)] = (_head_norm(q_ref[:, _lanes(hh)])[0] * qg_ref[...]).astype(BF16)
            kn_s[:, _lanes(hh)] = (_head_norm(k_ref[:, _lanes(hh)])[0] * kg_ref[...]).astype(BF16)
        row = lax.broadcasted_iota(jnp.int32, (TQ, TQ), 0)
        col = lax.broadcasted_iota(jnp.int32, (TQ, TQ), 1)
        later = _ones_where(row > col)
        causal = col < row

        def q_block(qi, _):
            q0 = pl.multiple_of(qi * TQ, TQ)

            def both(ki, state, diag):
                k0 = pl.multiple_of(ki * TQ, TQ)
                z = [_dot(qn_s[pl.ds(q0, TQ), _lanes(hh)], kn_s[pl.ds(k0, TQ), _lanes(hh)], NT) * scale
                     for hh in heads]
                ls = [_log_sigmoid(zz) for zz in z]
                l1m = [a - zz for a, zz in zip(ls, z)]
                if diag:
                    l1m = [jnp.where(causal, a, 0.0) for a in l1m]
                parts = [_split_bf16(a) for a in l1m]
                suffix = [_dot(hi, later, NN) + _dot(lo, later, NN) + state[hh][0]
                          for hh, (hi, lo) in zip(heads, parts)]
                w = [jnp.exp(a + sfx) for a, sfx in zip(ls, suffix)]
                if diag:
                    w = [jnp.where(causal, a, 0.0) for a in w]
                acc = [state[hh][1] + _dot(w[hh].astype(BF16), v_ref[pl.ds(k0, TQ), _lanes(hh)], NN)
                       for hh in heads]
                return tuple((state[hh][0] + jnp.sum(l1m[hh], axis=1, keepdims=True), acc[hh]) for hh in heads)

            zero = (jnp.zeros((TQ, 1), F32), jnp.zeros((TQ, HEAD_DIM), F32))
            state = both(qi, (zero,) * HP, True)
            state = lax.fori_loop(0, qi, lambda i, st: both(qi - 1 - i, st, False), state)
            for hh in heads:
                acc = state[hh][1]
                o_ref[pl.ds(q0, TQ), _lanes(hh)] = acc.astype(BF16)
                gate = gate_ref[pl.ds(q0, TQ), _lanes(hh)].astype(F32)
                og_ref[pl.ds(q0, TQ), _lanes(hh)] = (acc * (gate * _sigmoid(gate))).astype(BF16)
            return 0

        lax.fori_loop(0, nq, q_block, 0)

    spec = _attn_specs(S, H, HP)
    vec = pl.BlockSpec((1, HEAD_DIM), lambda b, h: (0, 0))
    out = pl.BlockSpec((S, ATT_W), lambda b, h: (b, h))
    return pl.pallas_call(
        body, name=name,
        out_shape=(jax.ShapeDtypeStruct((B * S, H * HEAD_DIM), BF16),) * 2,
        grid=(B, H // HP),
        in_specs=[spec(0), spec(1), spec(2), spec(3), vec, vec],
        out_specs=(out, out),
        scratch_shapes=[pltpu.VMEM((S, ATT_W), BF16)] * 2,
        compiler_params=_params(("parallel", "parallel")),
    )(proj, proj, proj, proj, q_g, k_g)


def _attn_bwd(proj, o, dog, q_g, k_g, B, S, H, name):
    TQ = _tile(S, 256)
    nq = S // TQ
    scale = 1.0 / math.sqrt(HEAD_DIM)
    HP = BWD_HEADS
    ATT_W = HP * HEAD_DIM
    heads = range(HP)

    def body(q_ref, k_ref, v_ref, gate_ref, o_ref, dog_ref, qg_ref, kg_ref,
             dq_ref, dk_ref, dv_ref, dgate_ref, dqg_ref, dkg_ref,
             qn_s, kn_s, do_s, dqn_s, dkn_s, dv_s, w_s, sg_s):
        first = (pl.program_id(0) == 0) & (pl.program_id(1) == 0)

        @pl.when(first)
        def _():
            dqg_ref[...] = jnp.zeros_like(dqg_ref)
            dkg_ref[...] = jnp.zeros_like(dkg_ref)

        for hh in heads:
            qn_s[:, _lanes(hh)] = (_head_norm(q_ref[:, _lanes(hh)])[0] * qg_ref[...]).astype(BF16)
            kn_s[:, _lanes(hh)] = (_head_norm(k_ref[:, _lanes(hh)])[0] * kg_ref[...]).astype(BF16)
        gate = gate_ref[...].astype(F32)
        sg = _sigmoid(gate)
        dog_v = dog_ref[...].astype(F32)
        do_s[...] = (dog_v * (gate * sg)).astype(BF16)
        dgate_ref[...] = (dog_v * o_ref[...].astype(F32) * (sg * (1.0 + gate * (1.0 - sg)))).astype(BF16)
        dkn_s[...] = jnp.zeros_like(dkn_s)
        dv_s[...] = jnp.zeros_like(dv_s)

        row = lax.broadcasted_iota(jnp.int32, (TQ, TQ), 0)
        col = lax.broadcasted_iota(jnp.int32, (TQ, TQ), 1)
        later = _ones_where(row > col)
        earlier = _ones_where(row < col)
        causal = col < row

        def q_block(qi, _):
            q0 = pl.multiple_of(qi * TQ, TQ)

            def weights_both(ki, carries, diag):
                k0 = pl.multiple_of(ki * TQ, TQ)
                z = [_dot(qn_s[pl.ds(q0, TQ), _lanes(hh)], kn_s[pl.ds(k0, TQ), _lanes(hh)], NT) * scale
                     for hh in heads]
                ls = [_log_sigmoid(zz) for zz in z]
                l1m = [a - zz for a, zz in zip(ls, z)]
                if diag:
                    l1m = [jnp.where(causal, a, 0.0) for a in l1m]
                parts = [_split_bf16(a) for a in l1m]
                suffix = [_dot(hi, later, NN) + _dot(lo, later, NN) + carries[hh]
                          for hh, (hi, lo) in zip(heads, parts)]
                for hh in heads:
                    w = jnp.exp(ls[hh] + suffix[hh])
                    if diag:
                        w = jnp.where(causal, w, 0.0)
                    w_s[hh, ki] = w
                    sg_s[hh, ki] = jnp.exp(ls[hh])
                return tuple(carries[hh] + jnp.sum(l1m[hh], axis=1, keepdims=True) for hh in heads)

            carries = weights_both(qi, (jnp.zeros((TQ, 1), F32),) * HP, True)
            lax.fori_loop(0, qi, lambda i, c: weights_both(qi - 1 - i, c, False), carries)

            def grads_both(ki, state, diag):
                k0 = pl.multiple_of(ki * TQ, TQ)
                qb = [qn_s[pl.ds(q0, TQ), _lanes(hh)] for hh in heads]
                dob = [do_s[pl.ds(q0, TQ), _lanes(hh)] for hh in heads]
                w = [w_s[hh, ki] for hh in heads]
                da = [_dot(dob[hh], v_ref[pl.ds(k0, TQ), _lanes(hh)], NT) * w[hh] for hh in heads]
                for hh in heads:
                    dv_s[pl.ds(k0, TQ), _lanes(hh)] += _dot(w[hh].astype(BF16), dob[hh], TN)
                parts = [_split_bf16(a) for a in da]
                prefix = [_dot(hi, earlier, NN) + _dot(lo, earlier, NN) + state[hh][0]
                          for hh, (hi, lo) in zip(heads, parts)]
                dzb = []
                for hh in heads:
                    sgz = sg_s[hh, ki]
                    dz = da[hh] * (1.0 - sgz) - sgz * prefix[hh]
                    if diag:
                        dz = jnp.where(causal, dz, 0.0)
                    dzb.append((dz * scale).astype(BF16))
                dq = [state[hh][1] + _dot(dzb[hh], kn_s[pl.ds(k0, TQ), _lanes(hh)], NN) for hh in heads]
                for hh in heads:
                    dkn_s[pl.ds(k0, TQ), _lanes(hh)] += _dot(dzb[hh], qb[hh], TN)
                return tuple((state[hh][0] + jnp.sum(da[hh], axis=1, keepdims=True), dq[hh]) for hh in heads)

            zero = (jnp.zeros((TQ, 1), F32), jnp.zeros((TQ, HEAD_DIM), F32))
            state = lax.fori_loop(0, qi, lambda i, st: grads_both(i, st, False), (zero,) * HP)
            state = grads_both(qi, state, True)
            for hh in heads:
                dqn_s[pl.ds(q0, TQ), _lanes(hh)] = state[hh][1]
            return 0

        lax.fori_loop(0, nq, q_block, 0)

        def norm_bwd(x_ref, g_ref, dn_s, dx_ref, dg_ref):
            for hh in heads:
                xh, r = _head_norm(x_ref[:, _lanes(hh)])
                dn = dn_s[:, _lanes(hh)]
                dg_ref[...] += jnp.sum(dn * xh, axis=0, keepdims=True)
                dxh = dn * g_ref[...]
                dx_ref[:, _lanes(hh)] = (r * (dxh - xh * jnp.mean(dxh * xh, axis=-1, keepdims=True))).astype(BF16)

        norm_bwd(q_ref, qg_ref, dqn_s, dq_ref, dqg_ref)
        norm_bwd(k_ref, kg_ref, dkn_s, dk_ref, dkg_ref)
        dv_ref[...] = dv_s[...].astype(BF16)

    spec = _attn_specs(S, H, HP)
    vec = pl.BlockSpec((1, HEAD_DIM), lambda b, h: (0, 0))
    blk = pl.BlockSpec((S, ATT_W), lambda b, h: (b, h))
    big = jax.ShapeDtypeStruct((B * S, H * HEAD_DIM), BF16)
    small = jax.ShapeDtypeStruct((1, HEAD_DIM), F32)
    return pl.pallas_call(
        body, name=name,
        out_shape=(big, big, big, big, small, small),
        grid=(B, H // HP),
        in_specs=[spec(0), spec(1), spec(2), spec(3), blk, blk, vec, vec],
        out_specs=(blk, blk, blk, blk, vec, vec),
        scratch_shapes=[pltpu.VMEM((S, ATT_W), BF16)] * 3 + [pltpu.VMEM((S, ATT_W), F32)] * 3
        + [pltpu.VMEM((HP, nq, TQ, TQ), F32)] * 2,
        compiler_params=_params(("arbitrary", "arbitrary")),
    )(proj, proj, proj, proj, o, dog, q_g, k_g)


HALF = GROUPS_PER_BLOCK * STATE


def _cmul(ar, ai, br, bi):
    return ar * br - ai * bi, ar * bi + ai * br


def _cpow(ar, ai, n):
    rr = ri = None
    while n:
        if n & 1:
            rr, ri = (ar, ai) if rr is None else _cmul(rr, ri, ar, ai)
        n >>= 1
        if n:
            ar, ai = _cmul(ar, ai, ar, ai)
    return rr, ri


def _segment_carry(er, ei, lr, li, seg_len, segs_per_seq, reverse):
    Lr, Li = _cpow(lr, li, seg_len)
    pos = lax.broadcasted_iota(jnp.int32, er.shape, 0) % segs_per_seq
    outr = jnp.zeros_like(er)
    outi = jnp.zeros_like(ei)
    pr = pi = None
    for d in range(1, segs_per_seq):
        shift = (SUBLANES - d) if reverse else d
        sr = pltpu.roll(er, shift, 0)
        si = pltpu.roll(ei, shift, 0)
        ok = (pos + d < segs_per_seq) if reverse else (pos >= d)
        sr = jnp.where(ok, sr, 0.0)
        si = jnp.where(ok, si, 0.0)
        if pr is not None:
            sr, si = _cmul(sr, si, pr, pi)
        outr = outr + sr
        outi = outi + si
        pr, pi = (Lr, Li) if pr is None else _cmul(pr, pi, Lr, Li)
    return outr, outi


def _s5_sizes(T, B):
    assert SUBLANES % B == 0
    segs_per_seq = SUBLANES // B
    n_steps = T // SUBLANES
    cj = _tile(n_steps, 64)
    return segs_per_seq, n_steps, cj


def _s5_fwd(u_p, wb, wc, lam, dvec, B, name):
    T, C = u_p.shape
    nb = C // 128
    segs_per_seq, n_steps, cj = _s5_sizes(T, B)
    n_chunks = n_steps // cj
    rows = cj * SUBLANES

    def body(u_ref, wb_ref, wc_ref, lam_ref, d_ref, y_ref, hin_ref, bu_s, h_s):
        lr = jnp.broadcast_to(lam_ref[:, :HALF], (SUBLANES, HALF))
        li = jnp.broadcast_to(lam_ref[:, HALF:], (SUBLANES, HALF))

        def scan_chunk(c, hr, hi, store):
            r0 = pl.multiple_of(c * rows, rows)
            bu_s[...] = _dot(u_ref[pl.ds(r0, rows), :], wb_ref[...], NN)

            def step(j, carry):
                hr, hi = carry
                o = pl.multiple_of(j * SUBLANES, SUBLANES)
                nr = lr * hr - li * hi + bu_s[pl.ds(o, SUBLANES), :HALF]
                ni = lr * hi + li * hr + bu_s[pl.ds(o, SUBLANES), HALF:]
                if store:
                    h_s[pl.ds(o, SUBLANES), :HALF] = nr
                    h_s[pl.ds(o, SUBLANES), HALF:] = ni
                return nr, ni

            hr, hi = lax.fori_loop(0, cj, step, (hr, hi))
            if store:
                uv = u_ref[pl.ds(r0, rows), :].astype(F32)
                y_ref[pl.ds(r0, rows), :] = _dot(h_s[...].astype(BF16), wc_ref[...], NN) + d_ref[...] * uv
            return hr, hi

        zero = jnp.zeros((SUBLANES, HALF), F32)
        er, ei = lax.fori_loop(0, n_chunks, lambda c, h: scan_chunk(c, h[0], h[1], False), (zero, zero))
        h0r, h0i = _segment_carry(er, ei, lr, li, n_steps, segs_per_seq, False)
        hin_ref[:, :HALF] = h0r
        hin_ref[:, HALF:] = h0i
        lax.fori_loop(0, n_chunks, lambda c, h: scan_chunk(c, h[0], h[1], True), (h0r, h0i))

    return pl.pallas_call(
        body, name=name,
        out_shape=(jax.ShapeDtypeStruct((T, C), F32), jax.ShapeDtypeStruct((nb, SUBLANES, 2 * HALF), F32)),
        grid=(nb,),
        in_specs=[pl.BlockSpec((T, 128), lambda g: (0, g)),
                  pl.BlockSpec((None, 128, 2 * HALF), lambda g: (g, 0, 0)),
                  pl.BlockSpec((None, 2 * HALF, 128), lambda g: (g, 0, 0)),
                  pl.BlockSpec((None, 1, 2 * HALF), lambda g: (g, 0, 0)),
                  pl.BlockSpec((1, 128), lambda g: (0, g))],
        out_specs=(pl.BlockSpec((T, 128), lambda g: (0, g)),
                   pl.BlockSpec((None, SUBLANES, 2 * HALF), lambda g: (g, 0, 0))),
        scratch_shapes=[pltpu.VMEM((rows, 2 * HALF), F32)] * 2,
        compiler_params=_params(("parallel",)),
    )(u_p, wb, wc, lam, dvec)


def _s5_bwd(u_p, dy_p, wb, wbt, wc, wct, lam, dvec, h_in, B, name):
    T, C = u_p.shape
    nb = C // 128
    segs_per_seq, n_steps, cj = _s5_sizes(T, B)
    n_chunks = n_steps // cj
    rows = cj * SUBLANES

    def body(u_ref, dy_ref, wb_ref, wbt_ref, wc_ref, wct_ref, lam_ref, d_ref, hin_ref,
             du_ref, dwb_ref, dwc_ref, dlam_ref, dd_ref, h_all, x_s, g_s):
        lr = jnp.broadcast_to(lam_ref[:, :HALF], (SUBLANES, HALF))
        li = jnp.broadcast_to(lam_ref[:, HALF:], (SUBLANES, HALF))
        zero = jnp.zeros((SUBLANES, HALF), F32)

        h_all[pl.ds(0, SUBLANES), :] = hin_ref[...]

        def fwd_chunk(c, carry):
            r0 = pl.multiple_of(c * rows, rows)
            x_s[...] = _dot(u_ref[pl.ds(r0, rows), :], wb_ref[...], NN)

            def step(j, carry):
                hr, hi = carry
                o = pl.multiple_of(j * SUBLANES, SUBLANES)
                nr = lr * hr - li * hi + x_s[pl.ds(o, SUBLANES), :HALF]
                ni = lr * hi + li * hr + x_s[pl.ds(o, SUBLANES), HALF:]
                late = pl.multiple_of(r0 + o + SUBLANES, SUBLANES)
                h_all[pl.ds(late, SUBLANES), :HALF] = nr
                h_all[pl.ds(late, SUBLANES), HALF:] = ni
                return nr, ni

            return lax.fori_loop(0, cj, step, carry)

        lax.fori_loop(0, n_chunks, fwd_chunk, (hin_ref[:, :HALF], hin_ref[:, HALF:]))

        def bwd_chunk(i, carry, store):
            c = n_chunks - 1 - i
            r0 = pl.multiple_of(c * rows, rows)
            dyv = dy_ref[pl.ds(r0, rows), :]
            x_s[...] = _dot(dyv, wct_ref[...], NN)

            def step(jj, carry):
                ar, ai, accr, acci = carry
                j = cj - 1 - jj
                o = pl.multiple_of(j * SUBLANES, SUBLANES)
                nr = lr * ar + li * ai + x_s[pl.ds(o, SUBLANES), :HALF]
                ni = lr * ai - li * ar + x_s[pl.ds(o, SUBLANES), HALF:]
                if store:
                    g_s[pl.ds(o, SUBLANES), :HALF] = nr
                    g_s[pl.ds(o, SUBLANES), HALF:] = ni
                    prev = pl.multiple_of(r0 + o, SUBLANES)
                    pr = h_all[pl.ds(prev, SUBLANES), :HALF]
                    pi = h_all[pl.ds(prev, SUBLANES), HALF:]
                    accr = accr + nr * pr + ni * pi
                    acci = acci + ni * pr - nr * pi
                return nr, ni, accr, acci

            carry = lax.fori_loop(0, cj, step, carry)
            if store:
                gb = g_s[...].astype(BF16)
                uv = u_ref[pl.ds(r0, rows), :]
                dyf = dyv.astype(F32)
                du_ref[pl.ds(r0, rows), :] = (_dot(gb, wbt_ref[...], NN) + d_ref[...] * dyf).astype(BF16)
                dwb_ref[...] += _dot(uv, gb, TN)
                hb = h_all[pl.ds(pl.multiple_of(r0 + SUBLANES, SUBLANES), rows), :].astype(BF16)
                dwc_ref[...] += _dot(hb, dyv, TN)
                dd_ref[...] += jnp.sum(dyf * uv.astype(F32), axis=0, keepdims=True)
            return carry

        er, ei, _, _ = lax.fori_loop(0, n_chunks, lambda i, c: bwd_chunk(i, c, False), (zero, zero, zero, zero))
        a0r, a0i = _segment_carry(er, ei, lr, -li, n_steps, segs_per_seq, True)
        dwb_ref[...] = jnp.zeros_like(dwb_ref)
        dwc_ref[...] = jnp.zeros_like(dwc_ref)
        dd_ref[...] = jnp.zeros_like(dd_ref)
        _, _, accr, acci = lax.fori_loop(0, n_chunks, lambda i, c: bwd_chunk(i, c, True), (a0r, a0i, zero, zero))
        dlam_ref[:, :HALF] = jnp.sum(accr, axis=0, keepdims=True)
        dlam_ref[:, HALF:] = jnp.sum(acci, axis=0, keepdims=True)

    col = pl.BlockSpec((T, 128), lambda g: (0, g))
    vec = pl.BlockSpec((1, 128), lambda g: (0, g))

    def per_block(*shape):
        return pl.BlockSpec((None,) + shape, lambda g: (g, 0, 0))

    return pl.pallas_call(
        body, name=name,
        out_shape=(jax.ShapeDtypeStruct((T, C), BF16),
                   jax.ShapeDtypeStruct((nb, 128, 2 * HALF), F32),
                   jax.ShapeDtypeStruct((nb, 2 * HALF, 128), F32),
                   jax.ShapeDtypeStruct((nb, 1, 2 * HALF), F32),
                   jax.ShapeDtypeStruct((1, C), F32)),
        grid=(nb,),
        in_specs=[col, col, per_block(128, 2 * HALF), per_block(2 * HALF, 128), per_block(2 * HALF, 128),
                  per_block(128, 2 * HALF), per_block(1, 2 * HALF), vec, per_block(SUBLANES, 2 * HALF)],
        out_specs=(col, per_block(128, 2 * HALF), per_block(2 * HALF, 128), per_block(1, 2 * HALF), vec),
        scratch_shapes=[pltpu.VMEM((T + SUBLANES, 2 * HALF), F32),
                        pltpu.VMEM((rows, 2 * HALF), F32), pltpu.VMEM((rows, 2 * HALF), F32)],
        compiler_params=_params(("parallel",)),
    )(u_p, dy_p, wb, wbt, wc, wct, lam, dvec, h_in)


def _discretize(a_re, a_im, log_dt, b_re, b_im):
    dt = jnp.exp(log_dt)[:, None]
    mag = jnp.exp(a_re * dt)
    lam_re = mag * jnp.cos(a_im * dt)
    lam_im = mag * jnp.sin(a_im * dt)
    den = a_re * a_re + a_im * a_im
    f_re = ((lam_re - 1.0) * a_re + lam_im * a_im) / den
    f_im = (lam_im * a_re - (lam_re - 1.0) * a_im) / den
    bb_re = f_re[..., None] * b_re - f_im[..., None] * b_im
    bb_im = f_re[..., None] * b_im + f_im[..., None] * b_re
    return lam_re, lam_im, bb_re, bb_im


def _block_diag_in(bb_re, bb_im):
    eye = jnp.eye(GROUPS_PER_BLOCK, dtype=F32)

    def one(bb):
        t = bb.reshape(-1, GROUPS_PER_BLOCK, STATE, GROUP)
        return jnp.einsum('gapi,ab->gaibp', t, eye).reshape(-1, 128, HALF)

    return jnp.concatenate([one(bb_re), one(bb_im)], axis=-1)


def _block_diag_in_grad(dwb):
    eye = jnp.eye(GROUPS_PER_BLOCK, dtype=F32)

    def one(d):
        t = d.reshape(-1, GROUPS_PER_BLOCK, GROUP, GROUPS_PER_BLOCK, STATE)
        return jnp.einsum('gaibp,ab->gapi', t, eye).reshape(-1, STATE, GROUP)

    return one(dwb[..., :HALF]), one(dwb[..., HALF:])


def _block_diag_out(c_re, c_im):
    eye = jnp.eye(GROUPS_PER_BLOCK, dtype=F32)

    def one(cc):
        t = cc.reshape(-1, GROUPS_PER_BLOCK, GROUP, STATE)
        return jnp.einsum('gaip,ab->gbpai', t, eye).reshape(-1, HALF, 128)

    return jnp.concatenate([one(c_re), -one(c_im)], axis=1)


def _block_diag_out_grad(dwc):
    eye = jnp.eye(GROUPS_PER_BLOCK, dtype=F32)

    def one(d):
        t = d.reshape(-1, GROUPS_PER_BLOCK, STATE, GROUPS_PER_BLOCK, GROUP)
        return jnp.einsum('gbpai,ab->gaip', t, eye).reshape(-1, GROUP, STATE)

    return one(dwc[:, :HALF]), -one(dwc[:, HALF:])


def _pack(parts):
    return jnp.concatenate([p.reshape(-1, PACK_W) for p in parts], axis=0)


def _unpack(buf, shapes):
    lead = buf.shape[:-2]
    out, r = [], 0
    for s in shapes:
        n = math.prod(s) // PACK_W
        out.append(buf[..., r:r + n, :].reshape(lead + tuple(s)))
        r += n
    return out


def _permute_rows(a, n_steps):
    T, C = a.shape
    return a.reshape(SUBLANES, n_steps, C).transpose(1, 0, 2).reshape(T, C)


def _unpermute_rows(a, n_steps):
    T, C = a.shape
    return a.reshape(n_steps, SUBLANES, C).transpose(1, 0, 2).reshape(T, C)


def kernel(x, norm_g, attn_w_in, attn_q_g, attn_k_g, attn_w_out, ssm_w_in, ssm_A_re, ssm_A_im, ssm_log_dt, ssm_B_re, ssm_B_im, ssm_C_re, ssm_C_im, ssm_D, ssm_glu_w, ssm_glu_b, ssm_w_out, loss_target, m_norm_g, m_attn_w_in, m_attn_q_g, m_attn_k_g, m_attn_w_out, m_ssm_w_in, m_ssm_A_re, m_ssm_A_im, m_ssm_log_dt, m_ssm_B_re, m_ssm_B_im, m_ssm_C_re, m_ssm_C_im, m_ssm_D, m_ssm_glu_w, m_ssm_glu_b, m_ssm_w_out, v_norm_g, v_attn_w_in, v_attn_q_g, v_attn_k_g, v_attn_w_out, v_ssm_w_in, v_ssm_A_re, v_ssm_A_im, v_ssm_log_dt, v_ssm_B_re, v_ssm_B_im, v_ssm_C_re, v_ssm_C_im, v_ssm_D, v_ssm_glu_w, v_ssm_glu_b, v_ssm_w_out):
    B, S, D = x.shape
    T = B * S
    H = D // HEAD_DIM
    G_loc = ssm_A_re.shape[1]
    G = G_loc * N_DEV
    n_steps = T // SUBLANES
    xf = x.reshape(T, D)
    target = loss_target.reshape(T, D)

    w_in0 = _exchange(_cast_bf16(attn_w_in[0], "cast_attn_w_in"), False, "gather_attn_w_in")
    w_out0 = _exchange(_cast_bf16(attn_w_out[0], "cast_attn_w_out"), False, "gather_attn_w_out")
    w_in1 = _exchange(_cast_bf16(ssm_w_in[0], "cast_ssm_w_in"), False, "gather_ssm_w_in")
    w_glu = _exchange(_cast_bf16(ssm_glu_w[0], "cast_ssm_glu_w"), False, "gather_ssm_glu_w")
    w_out1 = _exchange(_cast_bf16(ssm_w_out[0], "cast_ssm_w_out"), False, "gather_ssm_w_out")
    w_out0 = w_out0.reshape(1, D, D)
    w_glu = w_glu.reshape(1, D, D)
    w_out1 = w_out1.reshape(1, D, D)

    small_shapes = [(G_loc, STATE), (G_loc, STATE), (G_loc, STATE, GROUP), (G_loc, STATE, GROUP),
                    (G_loc, GROUP, STATE), (G_loc, GROUP, STATE), (G_loc * GROUP,), (G_loc * GROUP,)]
    disc_in = (ssm_A_re[0], ssm_A_im[0], ssm_log_dt[0], ssm_B_re[0], ssm_B_im[0])
    (lam_re, lam_im, bb_re, bb_im), disc_vjp = jax.vjp(_discretize, *disc_in)
    small = _pack([lam_re, lam_im, bb_re, bb_im, ssm_C_re[0], ssm_C_im[0], ssm_D[0], ssm_glu_b[0]])
    small_all = _exchange(small, False, "gather_small")
    lam_re_a, lam_im_a, bb_re_a, bb_im_a, c_re_a, c_im_a, d_a, glu_b_a = [
        t.reshape((G,) + t.shape[2:]) if t.ndim > 2 else t.reshape(-1)
        for t in _unpack(small_all, small_shapes)]
    wb = _block_diag_in(bb_re_a, bb_im_a)
    wc = _block_diag_out(c_re_a, c_im_a)
    wb_b, wc_b = wb.astype(BF16), wc.astype(BF16)
    wbt_b, wct_b = wb_b.transpose(0, 2, 1), wc_b.transpose(0, 2, 1)
    lam = jnp.concatenate([lam_re_a.reshape(-1, 1, HALF), lam_im_a.reshape(-1, 1, HALF)], axis=-1)
    d_row = d_a.reshape(1, D)
    glu_b_row = glu_b_a.reshape(1, D)
    g0, g1 = norm_g[0:1], norm_g[1:2]
    q_g, k_g = attn_q_g, attn_k_g

    h0 = _rmsnorm_fwd(xf, g0, "norm0")
    proj0 = _mm_nn(h0, w_in0, BF16, "attn_in")
    og, o = _attn_fwd(proj0, q_g, k_g, B, S, H, "attn_fwd")
    x1 = _mm_nn(og, w_out0, F32, "attn_out", residual=xf)

    h1 = _rmsnorm_fwd(x1, g1, "norm1")
    proj1 = _mm_nn(h1, w_in1, BF16, "ssm_in")
    u_p = _permute_rows(proj1[:, :D], n_steps)
    gate1 = proj1[:, D:]
    y_p, h_in = _s5_fwd(u_p, wb_b, wc_b, lam, d_row, B, "s5_fwd")
    y_ssm = _unpermute_rows(y_p, n_steps)
    (yg,) = _ew(lambda a: (_gelu(a)[0],), [y_ssm], [], [BF16], 0, "gelu")
    z = _mm_nn(yg, w_glu, F32, "glu_in", bias=glu_b_row)

    def glu_fwd(y, zz, gt):
        gt = gt.astype(F32)
        return (_gelu(y)[0] * _sigmoid(zz) * (gt * _sigmoid(gt)),)

    (y3,) = _ew(glu_fwd, [y_ssm, z, gate1], [], [BF16], 0, "glu_gate")
    out = _mm_nn(y3, w_out1, F32, "ssm_out", residual=x1)

    dout, dout_b, loss_part = _loss_head(out, target, "loss")
    loss = lax.psum(loss_part[0, 0], ("x", "y", "c"))

    dy3 = _mm_nt(dout_b, w_out1, F32, "ssm_out_dx")
    p_w_out1 = _mm_tn(y3, dout_b, 1, BF16, "ssm_out_dw").reshape(N_DEV, D // N_DEV, D)

    def glu_bwd(d3, y, zz, gt):
        gt = gt.astype(F32)
        sg = _sigmoid(gt)
        sz = _sigmoid(zz)
        ygv, _ = _gelu(y)
        dy2 = d3 * (gt * sg)
        dgate = d3 * (ygv * sz) * (sg * (1.0 + gt * (1.0 - sg)))
        dz = dy2 * ygv * (sz * (1.0 - sz))
        return dz, dgate, dy2 * sz, jnp.sum(dz, axis=0, keepdims=True)

    dz_b, dgate1, dyg_a, dglu_b = _ew(glu_bwd, [dy3, y_ssm, z, gate1], [], [BF16, BF16, F32], 1, "glu_gate_bwd")
    dyg_b = _mm_nt(dz_b, w_glu, F32, "glu_in_dx")
    p_w_glu = _mm_tn(yg, dz_b, 1, BF16, "glu_in_dw").reshape(N_DEV, D // N_DEV, D)

    def gelu_bwd(da, db, y):
        _, t = _gelu(y)
        dg = 0.5 * (1.0 + t) + 0.5 * y * (1.0 - t * t) * (GELU_C * (1.0 + 3.0 * 0.044715 * (y * y)))
        return ((da + db) * dg,)

    (dy_ssm,) = _ew(gelu_bwd, [dyg_a, dyg_b, y_ssm], [], [BF16], 0, "gelu_bwd")
    dy_p = _permute_rows(dy_ssm, n_steps)
    du_p, dwb, dwc, dlam, dd = _s5_bwd(u_p, dy_p, wb_b, wbt_b, wc_b, wct_b, lam, d_row, h_in, B, "s5_bwd")
    du = _unpermute_rows(du_p, n_steps)
    dproj1 = jnp.concatenate([du, dgate1], axis=1)
    dh1 = _mm_nt(dproj1, w_in1, F32, "ssm_in_dx")
    p_w_in1 = _mm_tn(h1, dproj1, N_DEV, BF16, "ssm_in_dw")
    dx1, dx1_b, dg1 = _rmsnorm_bwd(x1, dh1, dout, g1, "norm1_bwd")

    dog = _mm_nt(dx1_b, w_out0, BF16, "attn_out_dx")
    p_w_out0 = _mm_tn(og, dx1_b, 1, BF16, "attn_out_dw").reshape(N_DEV, D // N_DEV, D)
    dq, dk, dv, dgate0, dqg, dkg = _attn_bwd(proj0, o, dog, q_g, k_g, B, S, H, "attn_bwd")
    dproj0 = jnp.concatenate([dq, dk, dv, dgate0], axis=1)
    dh0 = _mm_nt(dproj0, w_in0, F32, "attn_in_dx")
    p_w_in0 = _mm_tn(h0, dproj0, N_DEV, BF16, "attn_in_dw")
    dx, _, dg0 = _rmsnorm_bwd(xf, dh0, dx1, g0, "norm0_bwd")

    def update(parts, w, m, v, name):
        recv = _exchange(parts, True, "scatter_" + name)
        return _adamw(recv, w[0], m[0], v[0], "adamw_" + name)

    r_attn_w_in = update(p_w_in0, attn_w_in, m_attn_w_in, v_attn_w_in, "attn_w_in")
    r_attn_w_out = update(p_w_out0, attn_w_out, m_attn_w_out, v_attn_w_out, "attn_w_out")
    r_ssm_w_in = update(p_w_in1, ssm_w_in, m_ssm_w_in, v_ssm_w_in, "ssm_w_in")
    r_ssm_glu_w = update(p_w_glu, ssm_glu_w, m_ssm_glu_w, v_ssm_glu_w, "ssm_glu_w")
    r_ssm_w_out = update(p_w_out1, ssm_w_out, m_ssm_w_out, v_ssm_w_out, "ssm_w_out")

    dbb_re, dbb_im = _block_diag_in_grad(dwb)
    dc_re, dc_im = _block_diag_out_grad(dwc)
    dlam_re = dlam[:, 0, :HALF].reshape(G, STATE)
    dlam_im = dlam[:, 0, HALF:].reshape(G, STATE)
    by_owner = [t.reshape((N_DEV, -1)) for t in (dlam_re, dlam_im, dbb_re, dbb_im, dc_re, dc_im, dd, dglu_b)]
    small_parts = jnp.concatenate([t.reshape(N_DEV, -1, PACK_W) for t in by_owner], axis=1)
    small_sum = _sum_parts(_exchange(small_parts, True, "scatter_small"), "sum_small")
    s_lam_re, s_lam_im, s_bb_re, s_bb_im, s_c_re, s_c_im, s_d, s_glu_b = _unpack(small_sum, small_shapes)
    g_a_re, g_a_im, g_log_dt, g_b_re, g_b_im = disc_vjp((s_lam_re, s_lam_im, s_bb_re, s_bb_im))

    local_names = ["ssm_A_re", "ssm_A_im", "ssm_log_dt", "ssm_B_re", "ssm_B_im", "ssm_C_re", "ssm_C_im",
                   "ssm_D", "ssm_glu_b"]
    local_g = [g_a_re, g_a_im, g_log_dt, g_b_re, g_b_im, s_c_re, s_c_im, s_d, s_glu_b]
    local_w = [ssm_A_re, ssm_A_im, ssm_log_dt, ssm_B_re, ssm_B_im, ssm_C_re, ssm_C_im, ssm_D, ssm_glu_b]
    local_m = [m_ssm_A_re, m_ssm_A_im, m_ssm_log_dt, m_ssm_B_re, m_ssm_B_im, m_ssm_C_re, m_ssm_C_im,
               m_ssm_D, m_ssm_glu_b]
    local_v = [v_ssm_A_re, v_ssm_A_im, v_ssm_log_dt, v_ssm_B_re, v_ssm_B_im, v_ssm_C_re, v_ssm_C_im,
               v_ssm_D, v_ssm_glu_b]
    r_local = _adamw_small(local_g, local_w, local_m, local_v, None, "adamw_small")

    rep_g = [jnp.concatenate([dg0, dg1], axis=0), dqg, dkg]
    rep_w = [norm_g, attn_q_g, attn_k_g]
    rep_m = [m_norm_g, m_attn_q_g, m_attn_k_g]
    rep_v = [v_norm_g, v_attn_q_g, v_attn_k_g]
    r_rep = _adamw_small(rep_g, rep_w, rep_m, rep_v, "gather_rep", "adamw_rep")

    res = {"attn_w_in": r_attn_w_in, "attn_w_out": r_attn_w_out, "ssm_w_in": r_ssm_w_in,
           "ssm_glu_w": r_ssm_glu_w, "ssm_w_out": r_ssm_w_out}
    ref_w = {"attn_w_in": attn_w_in, "attn_w_out": attn_w_out, "ssm_w_in": ssm_w_in,
             "ssm_glu_w": ssm_glu_w, "ssm_w_out": ssm_w_out}
    for name, r, w in zip(local_names, r_local, local_w):
        res[name], ref_w[name] = r, w
    for name, r, w in zip(["norm_g", "attn_q_g", "attn_k_g"], r_rep, rep_w):
        res[name], ref_w[name] = r, w
    order = ["norm_g", "attn_w_in", "attn_q_g", "attn_k_g", "attn_w_out", "ssm_w_in", "ssm_A_re", "ssm_A_im",
             "ssm_log_dt", "ssm_B_re", "ssm_B_im", "ssm_C_re", "ssm_C_im", "ssm_D", "ssm_glu_w", "ssm_glu_b",
             "ssm_w_out"]
    outs = [loss, dx.reshape(B, S, D)]
    for kind in range(4):
        outs += [res[n][kind].reshape(ref_w[n].shape) for n in order]
    return tuple(outs)


def _adamw_small(grads, ws, ms, vs, gather_name, name):
    sizes = [math.prod(w.shape) for w in ws]
    total = sum(sizes)
    rows = -(-total // (PACK_W * 8)) * 8
    if rows > 256:
        rows = -(-rows // 256) * 256

    def pack(ts, fill):
        flat = jnp.concatenate([t.reshape(-1).astype(F32) for t in ts])
        flat = jnp.concatenate([flat, jnp.full((rows * PACK_W - total,), fill, F32)])
        return flat.reshape(rows, PACK_W)

    g = pack(grads, 0.0)
    parts = _exchange(g, False, gather_name) if gather_name else g[None]
    res = _adamw(parts, pack(ws, 0.0), pack(ms, 0.0), pack(vs, 1.0), name)
    outs = []
    off = 0
    flats = [r.reshape(-1) for r in res]
    for n in sizes:
        outs.append(tuple(f[off:off + n] for f in flats))
        off += n
    return outs
```

```python
import functools
import math

import jax
import jax.numpy as jnp
from jax import lax
from jax.experimental import pallas as pl
from jax.experimental.pallas import tpu as pltpu

F32 = jnp.float32
BF16 = jnp.bfloat16

N_DEV = 8
HEAD_DIM = 128
GROUP = 16
STATE = 64
GROUPS_PER_BLOCK = 8
SUBLANES = 8
RMS_EPS = 1e-6
ADAM_LR, ADAM_B1, ADAM_B2, ADAM_EPS, ADAM_WD, ADAM_STEP = 0.001, 0.9, 0.999, 1e-08, 0.01, 10
VMEM_LIMIT = 56 * 1024 * 1024
GELU_C = math.sqrt(2.0 / math.pi)
PACK_W = 128


def _params(sem, **kw):
    return pltpu.CompilerParams(dimension_semantics=sem, vmem_limit_bytes=VMEM_LIMIT, **kw)


def _tile(n, t):
    t = min(n, t)
    assert n % t == 0, (n, t)
    return t


def _ones_where(cond):
    return jnp.where(cond, 1.0, 0.0).astype(BF16)


def _sigmoid(x):
    return 1.0 / (1.0 + jnp.exp(-x))


def _log_sigmoid(z):
    return jnp.minimum(z, 0.0) - jnp.log(1.0 + jnp.exp(-jnp.abs(z)))


def _dot(a, b, dims):
    return lax.dot_general(a, b, (dims, ((), ())), preferred_element_type=F32)


NN = ((1,), (0,))
NT = ((1,), (1,))
TN = ((0,), (0,))


def _exchange(x, scatter, name):
    shape = x.shape[1:] if scatter else x.shape

    def body(x_ref, out_ref, send_sems, recv_sems, local_sem):
        ix, iy, ic = lax.axis_index("x"), lax.axis_index("y"), lax.axis_index("c")
        me = 4 * ix + 2 * iy + ic

        def peer(k):
            kx, ky, kc = (k >> 2) & 1, (k >> 1) & 1, k & 1
            px, py, pc = ix ^ kx, iy ^ ky, ic ^ kc
            return (px, py, pc), 4 * px + 2 * py + pc

        mine = pltpu.make_async_copy(x_ref.at[me] if scatter else x_ref, out_ref.at[me], local_sem)
        mine.start()
        copies = []
        for k in range(1, N_DEV):
            pid, pidx = peer(k)
            cp = pltpu.make_async_remote_copy(
                src_ref=x_ref.at[pidx] if scatter else x_ref,
                dst_ref=out_ref.at[me],
                send_sem=send_sems.at[k - 1], recv_sem=recv_sems.at[k - 1],
                device_id=pid, device_id_type=pl.DeviceIdType.MESH)
            cp.start()
            copies.append(cp)
        for k in range(1, N_DEV):
            pid, pidx = peer(k)
            pltpu.make_async_remote_copy(
                src_ref=x_ref.at[pidx] if scatter else x_ref,
                dst_ref=out_ref.at[pidx],
                send_sem=send_sems.at[k - 1], recv_sem=recv_sems.at[k - 1],
                device_id=pid, device_id_type=pl.DeviceIdType.MESH).wait_recv()
        for cp in copies:
            cp.wait_send()
        mine.wait()

    return pl.pallas_call(
        body, name=name,
        out_shape=jax.ShapeDtypeStruct((N_DEV,) + tuple(shape), x.dtype),
        in_specs=[pl.BlockSpec(memory_space=pl.ANY)],
        out_specs=pl.BlockSpec(memory_space=pl.ANY),
        scratch_shapes=[pltpu.SemaphoreType.DMA((N_DEV - 1,)), pltpu.SemaphoreType.DMA((N_DEV - 1,)),
                        pltpu.SemaphoreType.DMA],
    )(x)


_HBM = pl.BlockSpec(memory_space=pltpu.HBM)
_SEM = pl.BlockSpec(memory_space=pltpu.SEMAPHORE)
_EFFECT = pltpu.SideEffectType.DATAFLOW_SIDE_EFFECTING


def _peer(k):
    ix, iy, ic = lax.axis_index("x"), lax.axis_index("y"), lax.axis_index("c")
    px, py, pc = ix ^ ((k >> 2) & 1), iy ^ ((k >> 1) & 1), ic ^ (k & 1)
    return (px, py, pc), 4 * px + 2 * py + pc


def _my_index():
    return 4 * lax.axis_index("x") + 2 * lax.axis_index("y") + lax.axis_index("c")


def _exchange_start(x, scatter, name):
    shape = x.shape[1:] if scatter else x.shape

    def body(x_ref, land_ref, send_sems, recv_sems, x_thru, land_thru, token, local_sem):
        me = _my_index()
        for k in range(1, N_DEV):
            pid, pidx = _peer(k)
            pltpu.make_async_remote_copy(
                src_ref=x_ref.at[pidx] if scatter else x_ref, dst_ref=land_ref.at[me],
                send_sem=send_sems.at[k - 1], recv_sem=recv_sems.at[k - 1],
                device_id=pid, device_id_type=pl.DeviceIdType.MESH).start()
        mine = pltpu.make_async_copy(x_ref.at[me] if scatter else x_ref, land_ref.at[me], local_sem)
        mine.start()
        mine.wait()
        token[...] = jnp.zeros_like(token)

    land = lax.empty((N_DEV,) + tuple(shape), x.dtype)
    return pl.pallas_call(
        body, name=name,
        out_shape=(pltpu.SemaphoreType.DMA((N_DEV - 1,)), pltpu.SemaphoreType.DMA((N_DEV - 1,)),
                   pltpu.HBM(x.shape, x.dtype), pltpu.HBM(land.shape, land.dtype),
                   jax.ShapeDtypeStruct((8, 128), F32)),
        in_specs=(_HBM, _HBM),
        out_specs=(_SEM, _SEM, _HBM, _HBM, pl.BlockSpec(memory_space=pltpu.VMEM)),
        input_output_aliases={0: 2, 1: 3},
        scratch_shapes=[pltpu.SemaphoreType.DMA],
        compiler_params=pltpu.CompilerParams(has_side_effects=_EFFECT),
    )(pltpu.with_memory_space_constraint(x, pltpu.HBM), pltpu.with_memory_space_constraint(land, pltpu.HBM))


def _exchange_wait(started, after, scatter, name):
    send_sems, recv_sems, x_thru, land_thru, _ = started

    def body(x_ref, land_ref, send_sems, recv_sems, after_ref, x_dead, got_ref):
        for k in range(1, N_DEV):
            pid, pidx = _peer(k)
            cp = pltpu.make_async_remote_copy(
                src_ref=x_ref.at[pidx] if scatter else x_ref, dst_ref=land_ref.at[pidx],
                send_sem=send_sems.at[k - 1], recv_sem=recv_sems.at[k - 1],
                device_id=pid, device_id_type=pl.DeviceIdType.MESH)
            cp.wait_send()
            cp.wait_recv()

    return pl.pallas_call(
        body, name=name,
        out_shape=(pltpu.HBM(x_thru.shape, x_thru.dtype), pltpu.HBM(land_thru.shape, land_thru.dtype)),
        in_specs=(_HBM, _HBM, _SEM, _SEM, pl.BlockSpec(memory_space=pl.ANY)),
        out_specs=(_HBM, _HBM),
        input_output_aliases={0: 0, 1: 1},
        compiler_params=pltpu.CompilerParams(has_side_effects=_EFFECT),
    )(x_thru, land_thru, send_sems, recv_sems, after)[1]


def _after(x, *deps):
    return lax.optimization_barrier((x,) + tuple(deps))[0]


def _accumulate(acc, part, step, n_steps, finish):
    if n_steps == 1:
        finish(part)
        return

    @pl.when(step == 0)
    def _():
        acc[...] = part

    @pl.when((step > 0) & (step < n_steps - 1))
    def _():
        acc[...] += part

    @pl.when(step == n_steps - 1)
    def _():
        finish(acc[...] + part)


def _mm_nn(a, b, out_dtype, name, bias=None, residual=None, tm=1024, tn=1024, tk=2048):
    M, K = a.shape
    J, K2, Nj = b.shape
    assert K == K2
    tm, tn, tk = _tile(M, tm), _tile(Nj, tn), _tile(K, tk)
    nb, nk = Nj // tn, K // tk

    def body(*refs):
        a_ref, b_ref = refs[0], refs[1]
        i = 2
        bias_ref = res_ref = None
        if bias is not None:
            bias_ref = refs[i]; i += 1
        if residual is not None:
            res_ref = refs[i]; i += 1
        o_ref, acc = refs[i], refs[i + 1]

        def finish(r):
            if bias_ref is not None:
                r = r + bias_ref[...]
            if res_ref is not None:
                r = r + res_ref[...]
            o_ref[...] = r.astype(out_dtype)

        _accumulate(acc, _dot(a_ref[...], b_ref[...], NN), pl.program_id(3), nk, finish)

    in_specs = [pl.BlockSpec((tm, tk), lambda j, m, n, k: (m, k)),
                pl.BlockSpec((None, tk, tn), lambda j, m, n, k: (j, k, n))]
    args = [a, b]
    if bias is not None:
        in_specs.append(pl.BlockSpec((1, tn), lambda j, m, n, k: (0, j * nb + n)))
        args.append(bias)
    if residual is not None:
        in_specs.append(pl.BlockSpec((tm, tn), lambda j, m, n, k: (m, j * nb + n)))
        args.append(residual)
    return pl.pallas_call(
        body, name=name,
        out_shape=jax.ShapeDtypeStruct((M, J * Nj), out_dtype),
        grid=(J, M // tm, nb, nk),
        in_specs=in_specs,
        out_specs=pl.BlockSpec((tm, tn), lambda j, m, n, k: (m, j * nb + n)),
        scratch_shapes=[pltpu.VMEM((tm, tn), F32)],
        compiler_params=_params(("parallel", "parallel", "parallel", "arbitrary")),
    )(*args)


def _mm_nt(a, b, out_dtype, name, tm=1024, tp=1024, tq=2048):
    M, Q = a.shape
    J, P, Qj = b.shape
    assert Q == J * Qj
    tm, tp, tq = _tile(M, tm), _tile(P, tp), _tile(Qj, tq)
    nq = Qj // tq

    def body(a_ref, b_ref, o_ref, acc):
        def finish(r):
            o_ref[...] = r.astype(out_dtype)

        _accumulate(acc, _dot(a_ref[...], b_ref[...], NT), pl.program_id(2) * nq + pl.program_id(3), J * nq, finish)

    return pl.pallas_call(
        body, name=name,
        out_shape=jax.ShapeDtypeStruct((M, P), out_dtype),
        grid=(M // tm, P // tp, J, nq),
        in_specs=[pl.BlockSpec((tm, tq), lambda m, p, j, q: (m, j * nq + q)),
                  pl.BlockSpec((None, tp, tq), lambda m, p, j, q: (j, p, q))],
        out_specs=pl.BlockSpec((tm, tp), lambda m, p, j, q: (m, p)),
        scratch_shapes=[pltpu.VMEM((tm, tp), F32)],
        compiler_params=_params(("parallel", "parallel", "arbitrary", "arbitrary")),
    )(a, b)


def _mm_tn(a, b, J, out_dtype, name, tm=1024, tn=1024, tr=2048):
    R, M = a.shape
    R2, N = b.shape
    assert R == R2 and N % J == 0
    Nj = N // J
    tm, tn, tr = _tile(M, tm), _tile(Nj, tn), _tile(R, tr)
    nb, nr = Nj // tn, R // tr

    def body(a_ref, b_ref, o_ref, acc):
        def finish(r):
            o_ref[...] = r.astype(out_dtype)

        _accumulate(acc, _dot(a_ref[...], b_ref[...], TN), pl.program_id(3), nr, finish)

    return pl.pallas_call(
        body, name=name,
        out_shape=jax.ShapeDtypeStruct((J, M, Nj), out_dtype),
        grid=(J, M // tm, nb, nr),
        in_specs=[pl.BlockSpec((tr, tm), lambda j, m, n, r: (r, m)),
                  pl.BlockSpec((tr, tn), lambda j, m, n, r: (r, j * nb + n))],
        out_specs=pl.BlockSpec((None, tm, tn), lambda j, m, n, r: (j, m, n)),
        scratch_shapes=[pltpu.VMEM((tm, tn), F32)],
        compiler_params=_params(("parallel", "parallel", "parallel", "arbitrary")),
    )(a, b)


def _ew(fn, ins, vecs, out_dtypes, n_acc, name, tr=256, tc=1024):
    T, C = ins[0].shape
    tr, tc = _tile(T, tr), _tile(C, tc)
    n_in, n_vec, n_out = len(ins), len(vecs), len(out_dtypes)

    def body(*refs):
        in_refs = refs[:n_in + n_vec]
        out_refs = refs[n_in + n_vec:n_in + n_vec + n_out]
        acc_refs = refs[n_in + n_vec + n_out:]
        res = fn(*[r[...] for r in in_refs])
        for o_ref, v in zip(out_refs, res[:n_out]):
            o_ref[...] = v.astype(o_ref.dtype)
        if n_acc:
            r = pl.program_id(1)

            @pl.when(r == 0)
            def _():
                for a_ref in acc_refs:
                    a_ref[...] = jnp.zeros_like(a_ref)

            for a_ref, v in zip(acc_refs, res[n_out:]):
                a_ref[...] += v

    blk = pl.BlockSpec((tr, tc), lambda c, r: (r, c))
    vec = pl.BlockSpec((1, tc), lambda c, r: (0, c))
    out = pl.pallas_call(
        body, name=name,
        out_shape=tuple([jax.ShapeDtypeStruct((T, C), d) for d in out_dtypes]
                        + [jax.ShapeDtypeStruct((1, C), F32)] * n_acc),
        grid=(C // tc, T // tr),
        in_specs=[blk] * n_in + [vec] * n_vec,
        out_specs=tuple([blk] * n_out + [vec] * n_acc),
        compiler_params=_params(("parallel", "arbitrary")),
    )(*ins, *vecs)
    return out


def _cast_bf16(w, name):
    return _ew(lambda a: (a,), [w], [], [BF16], 0, name)[0]


def _rmsnorm_fwd(x, g, name, tr=128):
    T, D = x.shape
    tr = _tile(T, tr)

    def body(x_ref, g_ref, h_ref):
        xv = x_ref[...]
        r = lax.rsqrt(jnp.mean(xv * xv, axis=-1, keepdims=True) + RMS_EPS)
        h_ref[...] = ((xv * r) * g_ref[...]).astype(BF16)

    return pl.pallas_call(
        body, name=name,
        out_shape=jax.ShapeDtypeStruct((T, D), BF16),
        grid=(T // tr,),
        in_specs=[pl.BlockSpec((tr, D), lambda i: (i, 0)), pl.BlockSpec((1, D), lambda i: (0, 0))],
        out_specs=pl.BlockSpec((tr, D), lambda i: (i, 0)),
        compiler_params=_params(("parallel",)),
    )(x, g)


def _rmsnorm_bwd(x, dh, dres, g, name, tr=128):
    T, D = x.shape
    tr = _tile(T, tr)

    def body(x_ref, dh_ref, dres_ref, g_ref, dx_ref, dxb_ref, dg_ref):
        xv = x_ref[...]
        r = lax.rsqrt(jnp.mean(xv * xv, axis=-1, keepdims=True) + RMS_EPS)
        xn = xv * r
        dhv = dh_ref[...].astype(F32)
        dxn = dhv * g_ref[...]
        dx = dres_ref[...] + r * (dxn - xn * jnp.mean(dxn * xn, axis=-1, keepdims=True))
        dx_ref[...] = dx
        dxb_ref[...] = dx.astype(BF16)

        @pl.when(pl.program_id(0) == 0)
        def _():
            dg_ref[...] = jnp.zeros_like(dg_ref)

        dg_ref[...] += jnp.sum(dhv * xn, axis=0, keepdims=True)

    blk = pl.BlockSpec((tr, D), lambda i: (i, 0))
    vec = pl.BlockSpec((1, D), lambda i: (0, 0))
    return pl.pallas_call(
        body, name=name,
        out_shape=(jax.ShapeDtypeStruct((T, D), F32), jax.ShapeDtypeStruct((T, D), BF16),
                   jax.ShapeDtypeStruct((1, D), F32)),
        grid=(T // tr,),
        in_specs=[blk, blk, blk, vec],
        out_specs=(blk, blk, vec),
        compiler_params=_params(("arbitrary",)),
    )(x, dh, dres, g)


def _loss_head(y, target, name, tr=128):
    T, D = y.shape
    tr = _tile(T, tr)
    n = T // tr

    def body(y_ref, t_ref, dy_ref, dyb_ref, loss_ref, acc):
        i = pl.program_id(0)

        @pl.when(i == 0)
        def _():
            acc[...] = jnp.zeros_like(acc)

        err = y_ref[...] - t_ref[...]
        dy = err * (1.0 / D)
        dy_ref[...] = dy
        dyb_ref[...] = dy.astype(BF16)
        acc[...] += jnp.sum(err * err, axis=0, keepdims=True)

        @pl.when(i == n - 1)
        def _():
            loss_ref[...] = jnp.sum(acc[...], axis=1, keepdims=True) * (0.5 / D)

    blk = pl.BlockSpec((tr, D), lambda i: (i, 0))
    return pl.pallas_call(
        body, name=name,
        out_shape=(jax.ShapeDtypeStruct((T, D), F32), jax.ShapeDtypeStruct((T, D), BF16),
                   jax.ShapeDtypeStruct((1, 1), F32)),
        grid=(n,),
        in_specs=[blk, blk],
        out_specs=(blk, blk, pl.BlockSpec((1, 1), lambda i: (0, 0))),
        scratch_shapes=[pltpu.VMEM((1, D), F32)],
        compiler_params=_params(("arbitrary",)),
    )(y, target)


def _gelu(x):
    t = jnp.tanh(GELU_C * (x + 0.044715 * (x * x * x)))
    return x * (0.5 * (1.0 + t)), t


def _adamw(parts, w, m, v, name, tr=256, tc=1024):
    n, R, C = parts.shape
    tr, tc = _tile(R, tr), _tile(C, tc)
    c1 = 1.0 - ADAM_B1 ** ADAM_STEP
    c2 = 1.0 - ADAM_B2 ** ADAM_STEP

    def body(p_ref, w_ref, m_ref, v_ref, g_out, d_out, m_out, v_out):
        g = p_ref[0].astype(F32)
        for k in range(1, n):
            g = g + p_ref[k].astype(F32)
        mn = ADAM_B1 * m_ref[...] + (1.0 - ADAM_B1) * g
        vn = ADAM_B2 * v_ref[...] + (1.0 - ADAM_B2) * (g * g)
        m_hat = mn / c1
        v_hat = vn / c2
        g_out[...] = g
        d_out[...] = -ADAM_LR * (m_hat / (jnp.sqrt(v_hat) + ADAM_EPS) + ADAM_WD * w_ref[...])
        m_out[...] = mn
        v_out[...] = vn

    blk = pl.BlockSpec((tr, tc), lambda r, c: (r, c))
    return pl.pallas_call(
        body, name=name,
        out_shape=tuple([jax.ShapeDtypeStruct((R, C), F32)] * 4),
        grid=(R // tr, C // tc),
        in_specs=[pl.BlockSpec((n, tr, tc), lambda r, c: (0, r, c)), blk, blk, blk],
        out_specs=(blk, blk, blk, blk),
        compiler_params=_params(("parallel", "parallel")),
    )(parts, w, m, v)


def _sum_parts(parts, name):
    n, R, C = parts.shape

    def body(p_ref, o_ref):
        g = p_ref[0].astype(F32)
        for k in range(1, n):
            g = g + p_ref[k].astype(F32)
        o_ref[...] = g

    return pl.pallas_call(
        body, name=name,
        out_shape=jax.ShapeDtypeStruct((R, C), F32),
        compiler_params=pltpu.CompilerParams(vmem_limit_bytes=VMEM_LIMIT),
    )(parts)


def _head_norm(xv):
    xv = xv.astype(F32)
    r = lax.rsqrt(jnp.mean(xv * xv, axis=-1, keepdims=True) + RMS_EPS)
    return xv * r, r


def _split_bf16(x):
    hi = x.astype(BF16)
    lo = (x - hi.astype(F32)).astype(BF16)
    return hi, lo


FWD_HEADS = 4
BWD_HEADS = 2


def _attn_specs(S, H, HP):
    def spec(part):
        return pl.BlockSpec((S, HP * HEAD_DIM), lambda b, h: (b, part * (H // HP) + h))
    return spec


def _lanes(hh):
    return slice(hh * HEAD_DIM, (hh + 1) * HEAD_DIM)


def _attn_fwd(proj, q_g, k_g, B, S, H, name):
    TQ = _tile(S, 256)
    nq = S // TQ
    scale = 1.0 / math.sqrt(HEAD_DIM)
    HP = FWD_HEADS
    ATT_W = HP * HEAD_DIM
    heads = range(HP)

    def body(q_ref, k_ref, v_ref, gate_ref, qg_ref, kg_ref, og_ref, o_ref, qn_s, kn_s):
        for hh in heads:
            qn_s[:, _lanes(hh)] = (_head_norm(q_ref[:, _lanes(hh)])[0] * qg_ref[...]).astype(BF16)
            kn_s[:, _lanes(hh)] = (_head_norm(k_ref[:, _lanes(hh)])[0] * kg_ref[...]).astype(BF16)
        row = lax.broadcasted_iota(jnp.int32, (TQ, TQ), 0)
        col = lax.broadcasted_iota(jnp.int32, (TQ, TQ), 1)
        later = _ones_where(row > col)
        causal = col < row

        def q_block(qi, _):
            q0 = pl.multiple_of(qi * TQ, TQ)

            def both(ki, state, diag):
                k0 = pl.multiple_of(ki * TQ, TQ)
                z = [_dot(qn_s[pl.ds(q0, TQ), _lanes(hh)], kn_s[pl.ds(k0, TQ), _lanes(hh)], NT) * scale
                     for hh in heads]
                ls = [_log_sigmoid(zz) for zz in z]
                l1m = [a - zz for a, zz in zip(ls, z)]
                if diag:
                    l1m = [jnp.where(causal, a, 0.0) for a in l1m]
                parts = [_split_bf16(a) for a in l1m]
                suffix = [_dot(hi, later, NN) + _dot(lo, later, NN) + state[hh][0]
                          for hh, (hi, lo) in zip(heads, parts)]
                w = [jnp.exp(a + sfx) for a, sfx in zip(ls, suffix)]
                if diag:
                    w = [jnp.where(causal, a, 0.0) for a in w]
                acc = [state[hh][1] + _dot(w[hh].astype(BF16), v_ref[pl.ds(k0, TQ), _lanes(hh)], NN)
                       for hh in heads]
                return tuple((state[hh][0] + jnp.sum(l1m[hh], axis=1, keepdims=True), acc[hh]) for hh in heads)

            zero = (jnp.zeros((TQ, 1), F32), jnp.zeros((TQ, HEAD_DIM), F32))
            state = both(qi, (zero,) * HP, True)
            state = lax.fori_loop(0, qi, lambda i, st: both(qi - 1 - i, st, False), state)
            for hh in heads:
                acc = state[hh][1]
                o_ref[pl.ds(q0, TQ), _lanes(hh)] = acc.astype(BF16)
                gate = gate_ref[pl.ds(q0, TQ), _lanes(hh)].astype(F32)
                og_ref[pl.ds(q0, TQ), _lanes(hh)] = (acc * (gate * _sigmoid(gate))).astype(BF16)
            return 0

        lax.fori_loop(0, nq, q_block, 0)

    spec = _attn_specs(S, H, HP)
    vec = pl.BlockSpec((1, HEAD_DIM), lambda b, h: (0, 0))
    out = pl.BlockSpec((S, ATT_W), lambda b, h: (b, h))
    return pl.pallas_call(
        body, name=name,
        out_shape=(jax.ShapeDtypeStruct((B * S, H * HEAD_DIM), BF16),) * 2,
        grid=(B, H // HP),
        in_specs=[spec(0), spec(1), spec(2), spec(3), vec, vec],
        out_specs=(out, out),
        scratch_shapes=[pltpu.VMEM((S, ATT_W), BF16)] * 2,
        compiler_params=_params(("parallel", "parallel")),
    )(proj, proj, proj, proj, q_g, k_g)


def _attn_bwd(proj, o, dog, q_g, k_g, B, S, H, name):
    TQ = _tile(S, 256)
    nq = S // TQ
    scale = 1.0 / math.sqrt(HEAD_DIM)
    HP = BWD_HEADS
    ATT_W = HP * HEAD_DIM
    heads = range(HP)

    def body(q_ref, k_ref, v_ref, gate_ref, o_ref, dog_ref, qg_ref, kg_ref,
             dq_ref, dk_ref, dv_ref, dgate_ref, dqg_ref, dkg_ref,
             qn_s, kn_s, do_s, dqn_s, dkn_s, dv_s, w_s, sg_s):
        first = (pl.program_id(0) == 0) & (pl.program_id(1) == 0)

        @pl.when(first)
        def _():
            dqg_ref[...] = jnp.zeros_like(dqg_ref)
            dkg_ref[...] = jnp.zeros_like(dkg_ref)

        for hh in heads:
            qn_s[:, _lanes(hh)] = (_head_norm(q_ref[:, _lanes(hh)])[0] * qg_ref[...]).astype(BF16)
            kn_s[:, _lanes(hh)] = (_head_norm(k_ref[:, _lanes(hh)])[0] * kg_ref[...]).astype(BF16)
        gate = gate_ref[...].astype(F32)
        sg = _sigmoid(gate)
        dog_v = dog_ref[...].astype(F32)
        do_s[...] = (dog_v * (gate * sg)).astype(BF16)
        dgate_ref[...] = (dog_v * o_ref[...].astype(F32) * (sg * (1.0 + gate * (1.0 - sg)))).astype(BF16)
        dkn_s[...] = jnp.zeros_like(dkn_s)
        dv_s[...] = jnp.zeros_like(dv_s)

        row = lax.broadcasted_iota(jnp.int32, (TQ, TQ), 0)
        col = lax.broadcasted_iota(jnp.int32, (TQ, TQ), 1)
        later = _ones_where(row > col)
        earlier = _ones_where(row < col)
        causal = col < row

        def q_block(qi, _):
            q0 = pl.multiple_of(qi * TQ, TQ)

            def weights_both(ki, carries, diag):
                k0 = pl.multiple_of(ki * TQ, TQ)
                z = [_dot(qn_s[pl.ds(q0, TQ), _lanes(hh)], kn_s[pl.ds(k0, TQ), _lanes(hh)], NT) * scale
                     for hh in heads]
                ls = [_log_sigmoid(zz) for zz in z]
                l1m = [a - zz for a, zz in zip(ls, z)]
                if diag:
                    l1m = [jnp.where(causal, a, 0.0) for a in l1m]
                parts = [_split_bf16(a) for a in l1m]
                suffix = [_dot(hi, later, NN) + _dot(lo, later, NN) + carries[hh]
                          for hh, (hi, lo) in zip(heads, parts)]
                for hh in heads:
                    w = jnp.exp(ls[hh] + suffix[hh])
                    if diag:
                        w = jnp.where(causal, w, 0.0)
                    w_s[hh, ki] = w
                    sg_s[hh, ki] = jnp.exp(ls[hh])
                return tuple(carries[hh] + jnp.sum(l1m[hh], axis=1, keepdims=True) for hh in heads)

            carries = weights_both(qi, (jnp.zeros((TQ, 1), F32),) * HP, True)
            lax.fori_loop(0, qi, lambda i, c: weights_both(qi - 1 - i, c, False), carries)

            def grads_both(ki, state, diag):
                k0 = pl.multiple_of(ki * TQ, TQ)
                qb = [qn_s[pl.ds(q0, TQ), _lanes(hh)] for hh in heads]
                dob = [do_s[pl.ds(q0, TQ), _lanes(hh)] for hh in heads]
                w = [w_s[hh, ki] for hh in heads]
                da = [_dot(dob[hh], v_ref[pl.ds(k0, TQ), _lanes(hh)], NT) * w[hh] for hh in heads]
                for hh in heads:
                    dv_s[pl.ds(k0, TQ), _lanes(hh)] += _dot(w[hh].astype(BF16), dob[hh], TN)
                parts = [_split_bf16(a) for a in da]
                prefix = [_dot(hi, earlier, NN) + _dot(lo, earlier, NN) + state[hh][0]
                          for hh, (hi, lo) in zip(heads, parts)]
                dzb = []
                for hh in heads:
                    sgz = sg_s[hh, ki]
                    dz = da[hh] * (1.0 - sgz) - sgz * prefix[hh]
                    if diag:
                        dz = jnp.where(causal, dz, 0.0)
                    dzb.append((dz * scale).astype(BF16))
                dq = [state[hh][1] + _dot(dzb[hh], kn_s[pl.ds(k0, TQ), _lanes(hh)], NN) for hh in heads]
                for hh in heads:
                    dkn_s[pl.ds(k0, TQ), _lanes(hh)] += _dot(dzb[hh], qb[hh], TN)
                return tuple((state[hh][0] + jnp.sum(da[hh], axis=1, keepdims=True), dq[hh]) for hh in heads)

            zero = (jnp.zeros((TQ, 1), F32), jnp.zeros((TQ, HEAD_DIM), F32))
            state = lax.fori_loop(0, qi, lambda i, st: grads_both(i, st, False), (zero,) * HP)
            state = grads_both(qi, state, True)
            for hh in heads:
                dqn_s[pl.ds(q0, TQ), _lanes(hh)] = state[hh][1]
            return 0

        lax.fori_loop(0, nq, q_block, 0)

        def norm_bwd(x_ref, g_ref, dn_s, dx_ref, dg_ref):
            for hh in heads:
                xh, r = _head_norm(x_ref[:, _lanes(hh)])
                dn = dn_s[:, _lanes(hh)]
                dg_ref[...] += jnp.sum(dn * xh, axis=0, keepdims=True)
                dxh = dn * g_ref[...]
                dx_ref[:, _lanes(hh)] = (r * (dxh - xh * jnp.mean(dxh * xh, axis=-1, keepdims=True))).astype(BF16)

        norm_bwd(q_ref, qg_ref, dqn_s, dq_ref, dqg_ref)
        norm_bwd(k_ref, kg_ref, dkn_s, dk_ref, dkg_ref)
        dv_ref[...] = dv_s[...].astype(BF16)

    spec = _attn_specs(S, H, HP)
    vec = pl.BlockSpec((1, HEAD_DIM), lambda b, h: (0, 0))
    blk = pl.BlockSpec((S, ATT_W), lambda b, h: (b, h))
    big = jax.ShapeDtypeStruct((B * S, H * HEAD_DIM), BF16)
    small = jax.ShapeDtypeStruct((1, HEAD_DIM), F32)
    return pl.pallas_call(
        body, name=name,
        out_shape=(big, big, big, big, small, small),
        grid=(B, H // HP),
        in_specs=[spec(0), spec(1), spec(2), spec(3), blk, blk, vec, vec],
        out_specs=(blk, blk, blk, blk, vec, vec),
        scratch_shapes=[pltpu.VMEM((S, ATT_W), BF16)] * 3 + [pltpu.VMEM((S, ATT_W), F32)] * 3
        + [pltpu.VMEM((HP, nq, TQ, TQ), F32)] * 2,
        compiler_params=_params(("arbitrary", "arbitrary")),
    )(proj, proj, proj, proj, o, dog, q_g, k_g)


HALF = GROUPS_PER_BLOCK * STATE


def _cmul(ar, ai, br, bi):
    return ar * br - ai * bi, ar * bi + ai * br


def _cpow(ar, ai, n):
    rr = ri = None
    while n:
        if n & 1:
            rr, ri = (ar, ai) if rr is None else _cmul(rr, ri, ar, ai)
        n >>= 1
        if n:
            ar, ai = _cmul(ar, ai, ar, ai)
    return rr, ri


def _segment_carry(er, ei, lr, li, seg_len, segs_per_seq, reverse):
    Lr, Li = _cpow(lr, li, seg_len)
    pos = lax.broadcasted_iota(jnp.int32, er.shape, 0) % segs_per_seq
    outr = jnp.zeros_like(er)
    outi = jnp.zeros_like(ei)
    pr = pi = None
    for d in range(1, segs_per_seq):
        shift = (SUBLANES - d) if reverse else d
        sr = pltpu.roll(er, shift, 0)
        si = pltpu.roll(ei, shift, 0)
        ok = (pos + d < segs_per_seq) if reverse else (pos >= d)
        sr = jnp.where(ok, sr, 0.0)
        si = jnp.where(ok, si, 0.0)
        if pr is not None:
            sr, si = _cmul(sr, si, pr, pi)
        outr = outr + sr
        outi = outi + si
        pr, pi = (Lr, Li) if pr is None else _cmul(pr, pi, Lr, Li)
    return outr, outi


def _s5_sizes(T, B):
    assert SUBLANES % B == 0
    segs_per_seq = SUBLANES // B
    n_steps = T // SUBLANES
    cj = _tile(n_steps, 64)
    return segs_per_seq, n_steps, cj


def _s5_fwd(u_p, wb, wc, lam, dvec, B, name):
    T, C = u_p.shape
    nb = C // 128
    segs_per_seq, n_steps, cj = _s5_sizes(T, B)
    n_chunks = n_steps // cj
    rows = cj * SUBLANES

    def body(u_ref, wb_ref, wc_ref, lam_ref, d_ref, y_ref, hin_ref, bu_s, h_s):
        lr = jnp.broadcast_to(lam_ref[:, :HALF], (SUBLANES, HALF))
        li = jnp.broadcast_to(lam_ref[:, HALF:], (SUBLANES, HALF))

        def scan_chunk(c, hr, hi, store):
            r0 = pl.multiple_of(c * rows, rows)
            bu_s[...] = _dot(u_ref[pl.ds(r0, rows), :], wb_ref[...], NN)

            def step(j, carry):
                hr, hi = carry
                o = pl.multiple_of(j * SUBLANES, SUBLANES)
                nr = lr * hr - li * hi + bu_s[pl.ds(o, SUBLANES), :HALF]
                ni = lr * hi + li * hr + bu_s[pl.ds(o, SUBLANES), HALF:]
                if store:
                    h_s[pl.ds(o, SUBLANES), :HALF] = nr
                    h_s[pl.ds(o, SUBLANES), HALF:] = ni
                return nr, ni

            hr, hi = lax.fori_loop(0, cj, step, (hr, hi))
            if store:
                uv = u_ref[pl.ds(r0, rows), :].astype(F32)
                y_ref[pl.ds(r0, rows), :] = _dot(h_s[...].astype(BF16), wc_ref[...], NN) + d_ref[...] * uv
            return hr, hi

        zero = jnp.zeros((SUBLANES, HALF), F32)
        er, ei = lax.fori_loop(0, n_chunks, lambda c, h: scan_chunk(c, h[0], h[1], False), (zero, zero))
        h0r, h0i = _segment_carry(er, ei, lr, li, n_steps, segs_per_seq, False)
        hin_ref[:, :HALF] = h0r
        hin_ref[:, HALF:] = h0i
        lax.fori_loop(0, n_chunks, lambda c, h: scan_chunk(c, h[0], h[1], True), (h0r, h0i))

    return pl.pallas_call(
        body, name=name,
        out_shape=(jax.ShapeDtypeStruct((T, C), F32), jax.ShapeDtypeStruct((nb, SUBLANES, 2 * HALF), F32)),
        grid=(nb,),
        in_specs=[pl.BlockSpec((T, 128), lambda g: (0, g)),
                  pl.BlockSpec((None, 128, 2 * HALF), lambda g: (g, 0, 0)),
                  pl.BlockSpec((None, 2 * HALF, 128), lambda g: (g, 0, 0)),
                  pl.BlockSpec((None, 1, 2 * HALF), lambda g: (g, 0, 0)),
                  pl.BlockSpec((1, 128), lambda g: (0, g))],
        out_specs=(pl.BlockSpec((T, 128), lambda g: (0, g)),
                   pl.BlockSpec((None, SUBLANES, 2 * HALF), lambda g: (g, 0, 0))),
        scratch_shapes=[pltpu.VMEM((rows, 2 * HALF), F32)] * 2,
        compiler_params=_params(("parallel",)),
    )(u_p, wb, wc, lam, dvec)


def _s5_bwd(u_p, dy_p, wb, wbt, wc, wct, lam, dvec, h_in, B, name):
    T, C = u_p.shape
    nb = C // 128
    segs_per_seq, n_steps, cj = _s5_sizes(T, B)
    n_chunks = n_steps // cj
    rows = cj * SUBLANES

    def body(u_ref, dy_ref, wb_ref, wbt_ref, wc_ref, wct_ref, lam_ref, d_ref, hin_ref,
             du_ref, dwb_ref, dwc_ref, dlam_ref, dd_ref, h_all, x_s, g_s):
        lr = jnp.broadcast_to(lam_ref[:, :HALF], (SUBLANES, HALF))
        li = jnp.broadcast_to(lam_ref[:, HALF:], (SUBLANES, HALF))
        zero = jnp.zeros((SUBLANES, HALF), F32)

        h_all[pl.ds(0, SUBLANES), :] = hin_ref[...]

        def fwd_chunk(c, carry):
            r0 = pl.multiple_of(c * rows, rows)
            x_s[...] = _dot(u_ref[pl.ds(r0, rows), :], wb_ref[...], NN)

            def step(j, carry):
                hr, hi = carry
                o = pl.multiple_of(j * SUBLANES, SUBLANES)
                nr = lr * hr - li * hi + x_s[pl.ds(o, SUBLANES), :HALF]
                ni = lr * hi + li * hr + x_s[pl.ds(o, SUBLANES), HALF:]
                late = pl.multiple_of(r0 + o + SUBLANES, SUBLANES)
                h_all[pl.ds(late, SUBLANES), :HALF] = nr
                h_all[pl.ds(late, SUBLANES), HALF:] = ni
                return nr, ni

            return lax.fori_loop(0, cj, step, carry)

        lax.fori_loop(0, n_chunks, fwd_chunk, (hin_ref[:, :HALF], hin_ref[:, HALF:]))

        def bwd_chunk(i, carry, store):
            c = n_chunks - 1 - i
            r0 = pl.multiple_of(c * rows, rows)
            dyv = dy_ref[pl.ds(r0, rows), :]
            x_s[...] = _dot(dyv, wct_ref[...], NN)

            def step(jj, carry):
                ar, ai, accr, acci = carry
                j = cj - 1 - jj
                o = pl.multiple_of(j * SUBLANES, SUBLANES)
                nr = lr * ar + li * ai + x_s[pl.ds(o, SUBLANES), :HALF]
                ni = lr * ai - li * ar + x_s[pl.ds(o, SUBLANES), HALF:]
                if store:
                    g_s[pl.ds(o, SUBLANES), :HALF] = nr
                    g_s[pl.ds(o, SUBLANES), HALF:] = ni
                    prev = pl.multiple_of(r0 + o, SUBLANES)
                    pr = h_all[pl.ds(prev, SUBLANES), :HALF]
                    pi = h_all[pl.ds(prev, SUBLANES), HALF:]
                    accr = accr + nr * pr + ni * pi
                    acci = acci + ni * pr - nr * pi
                return nr, ni, accr, acci

            carry = lax.fori_loop(0, cj, step, carry)
            if store:
                gb = g_s[...].astype(BF16)
                uv = u_ref[pl.ds(r0, rows), :]
                dyf = dyv.astype(F32)
                du_ref[pl.ds(r0, rows), :] = (_dot(gb, wbt_ref[...], NN) + d_ref[...] * dyf).astype(BF16)
                dwb_ref[...] += _dot(uv, gb, TN)
                hb = h_all[pl.ds(pl.multiple_of(r0 + SUBLANES, SUBLANES), rows), :].astype(BF16)
                dwc_ref[...] += _dot(hb, dyv, TN)
                dd_ref[...] += jnp.sum(dyf * uv.astype(F32), axis=0, keepdims=True)
            return carry

        er, ei, _, _ = lax.fori_loop(0, n_chunks, lambda i, c: bwd_chunk(i, c, False), (zero, zero, zero, zero))
        a0r, a0i = _segment_carry(er, ei, lr, -li, n_steps, segs_per_seq, True)
        dwb_ref[...] = jnp.zeros_like(dwb_ref)
        dwc_ref[...] = jnp.zeros_like(dwc_ref)
        dd_ref[...] = jnp.zeros_like(dd_ref)
        _, _, accr, acci = lax.fori_loop(0, n_chunks, lambda i, c: bwd_chunk(i, c, True), (a0r, a0i, zero, zero))
        dlam_ref[:, :HALF] = jnp.sum(accr, axis=0, keepdims=True)
        dlam_ref[:, HALF:] = jnp.sum(acci, axis=0, keepdims=True)

    col = pl.BlockSpec((T, 128), lambda g: (0, g))
    vec = pl.BlockSpec((1, 128), lambda g: (0, g))

    def per_block(*shape):
        return pl.BlockSpec((None,) + shape, lambda g: (g, 0, 0))

    return pl.pallas_call(
        body, name=name,
        out_shape=(jax.ShapeDtypeStruct((T, C), BF16),
                   jax.ShapeDtypeStruct((nb, 128, 2 * HALF), F32),
                   jax.ShapeDtypeStruct((nb, 2 * HALF, 128), F32),
                   jax.ShapeDtypeStruct((nb, 1, 2 * HALF), F32),
                   jax.ShapeDtypeStruct((1, C), F32)),
        grid=(nb,),
        in_specs=[col, col, per_block(128, 2 * HALF), per_block(2 * HALF, 128), per_block(2 * HALF, 128),
                  per_block(128, 2 * HALF), per_block(1, 2 * HALF), vec, per_block(SUBLANES, 2 * HALF)],
        out_specs=(col, per_block(128, 2 * HALF), per_block(2 * HALF, 128), per_block(1, 2 * HALF), vec),
        scratch_shapes=[pltpu.VMEM((T + SUBLANES, 2 * HALF), F32),
                        pltpu.VMEM((rows, 2 * HALF), F32), pltpu.VMEM((rows, 2 * HALF), F32)],
        compiler_params=_params(("parallel",)),
    )(u_p, dy_p, wb, wbt, wc, wct, lam, dvec, h_in)


def _discretize(a_re, a_im, log_dt, b_re, b_im):
    dt = jnp.exp(log_dt)[:, None]
    mag = jnp.exp(a_re * dt)
    lam_re = mag * jnp.cos(a_im * dt)
    lam_im = mag * jnp.sin(a_im * dt)
    den = a_re * a_re + a_im * a_im
    f_re = ((lam_re - 1.0) * a_re + lam_im * a_im) / den
    f_im = (lam_im * a_re - (lam_re - 1.0) * a_im) / den
    bb_re = f_re[..., None] * b_re - f_im[..., None] * b_im
    bb_im = f_re[..., None] * b_im + f_im[..., None] * b_re
    return lam_re, lam_im, bb_re, bb_im


def _block_diag_in(bb_re, bb_im):
    eye = jnp.eye(GROUPS_PER_BLOCK, dtype=F32)

    def one(bb):
        t = bb.reshape(-1, GROUPS_PER_BLOCK, STATE, GROUP)
        return jnp.einsum('gapi,ab->gaibp', t, eye).reshape(-1, 128, HALF)

    return jnp.concatenate([one(bb_re), one(bb_im)], axis=-1)


def _block_diag_in_grad(dwb):
    eye = jnp.eye(GROUPS_PER_BLOCK, dtype=F32)

    def one(d):
        t = d.reshape(-1, GROUPS_PER_BLOCK, GROUP, GROUPS_PER_BLOCK, STATE)
        return jnp.einsum('gaibp,ab->gapi', t, eye).reshape(-1, STATE, GROUP)

    return one(dwb[..., :HALF]), one(dwb[..., HALF:])


def _block_diag_out(c_re, c_im):
    eye = jnp.eye(GROUPS_PER_BLOCK, dtype=F32)

    def one(cc):
        t = cc.reshape(-1, GROUPS_PER_BLOCK, GROUP, STATE)
        return jnp.einsum('gaip,ab->gbpai', t, eye).reshape(-1, HALF, 128)

    return jnp.concatenate([one(c_re), -one(c_im)], axis=1)


def _block_diag_out_grad(dwc):
    eye = jnp.eye(GROUPS_PER_BLOCK, dtype=F32)

    def one(d):
        t = d.reshape(-1, GROUPS_PER_BLOCK, STATE, GROUPS_PER_BLOCK, GROUP)
        return jnp.einsum('gbpai,ab->gaip', t, eye).reshape(-1, GROUP, STATE)

    return one(dwc[:, :HALF]), -one(dwc[:, HALF:])


def _pack(parts):
    return jnp.concatenate([p.reshape(-1, PACK_W) for p in parts], axis=0)


def _unpack(buf, shapes):
    lead = buf.shape[:-2]
    out, r = [], 0
    for s in shapes:
        n = math.prod(s) // PACK_W
        out.append(buf[..., r:r + n, :].reshape(lead + tuple(s)))
        r += n
    return out


def _permute_rows(a, n_steps):
    T, C = a.shape
    return a.reshape(SUBLANES, n_steps, C).transpose(1, 0, 2).reshape(T, C)


def _unpermute_rows(a, n_steps):
    T, C = a.shape
    return a.reshape(n_steps, SUBLANES, C).transpose(1, 0, 2).reshape(T, C)


def kernel(x, norm_g, attn_w_in, attn_q_g, attn_k_g, attn_w_out, ssm_w_in, ssm_A_re, ssm_A_im, ssm_log_dt, ssm_B_re, ssm_B_im, ssm_C_re, ssm_C_im, ssm_D, ssm_glu_w, ssm_glu_b, ssm_w_out, loss_target, m_norm_g, m_attn_w_in, m_attn_q_g, m_attn_k_g, m_attn_w_out, m_ssm_w_in, m_ssm_A_re, m_ssm_A_im, m_ssm_log_dt, m_ssm_B_re, m_ssm_B_im, m_ssm_C_re, m_ssm_C_im, m_ssm_D, m_ssm_glu_w, m_ssm_glu_b, m_ssm_w_out, v_norm_g, v_attn_w_in, v_attn_q_g, v_attn_k_g, v_attn_w_out, v_ssm_w_in, v_ssm_A_re, v_ssm_A_im, v_ssm_log_dt, v_ssm_B_re, v_ssm_B_im, v_ssm_C_re, v_ssm_C_im, v_ssm_D, v_ssm_glu_w, v_ssm_glu_b, v_ssm_w_out):
    B, S, D = x.shape
    T = B * S
    H = D // HEAD_DIM
    G_loc = ssm_A_re.shape[1]
    G = G_loc * N_DEV
    n_steps = T // SUBLANES
    xf = x.reshape(T, D)
    target = loss_target.reshape(T, D)

    c_in0 = _cast_bf16(attn_w_in[0], "cast_attn_w_in")
    c_out0 = _cast_bf16(attn_w_out[0], "cast_attn_w_out")
    c_in1 = _cast_bf16(ssm_w_in[0], "cast_ssm_w_in")
    c_glu = _cast_bf16(ssm_glu_w[0], "cast_ssm_glu_w")
    c_out1 = _cast_bf16(ssm_w_out[0], "cast_ssm_w_out")
    s_in0 = _exchange_start(c_in0, False, "gather_attn_w_in_start")
    tok0 = s_in0[4]

    small_shapes = [(G_loc, STATE), (G_loc, STATE), (G_loc, STATE, GROUP), (G_loc, STATE, GROUP),
                    (G_loc, GROUP, STATE), (G_loc, GROUP, STATE), (G_loc * GROUP,), (G_loc * GROUP,)]
    disc_in = (ssm_A_re[0], ssm_A_im[0], ssm_log_dt[0], ssm_B_re[0], ssm_B_im[0])
    (lam_re, lam_im, bb_re, bb_im), disc_vjp = jax.vjp(_discretize, *disc_in)
    small = _pack([lam_re, lam_im, bb_re, bb_im, ssm_C_re[0], ssm_C_im[0], ssm_D[0], ssm_glu_b[0]])
    small_all = _exchange(_after(small, tok0), False, "gather_small")
    lam_re_a, lam_im_a, bb_re_a, bb_im_a, c_re_a, c_im_a, d_a, glu_b_a = [
        t.reshape((G,) + t.shape[2:]) if t.ndim > 2 else t.reshape(-1)
        for t in _unpack(small_all, small_shapes)]
    wb = _block_diag_in(bb_re_a, bb_im_a)
    wc = _block_diag_out(c_re_a, c_im_a)
    wb_b, wc_b = wb.astype(BF16), wc.astype(BF16)
    wbt_b, wct_b = wb_b.transpose(0, 2, 1), wc_b.transpose(0, 2, 1)
    lam = jnp.concatenate([lam_re_a.reshape(-1, 1, HALF), lam_im_a.reshape(-1, 1, HALF)], axis=-1)
    d_row = d_a.reshape(1, D)
    glu_b_row = glu_b_a.reshape(1, D)
    g0, g1 = norm_g[0:1], norm_g[1:2]
    q_g, k_g = attn_q_g, attn_k_g

    h0 = _rmsnorm_fwd(_after(xf, tok0), g0, "norm0")
    w_in0 = _exchange_wait(s_in0, h0, False, "gather_attn_w_in_wait")
    s_out0 = _exchange_start(_after(c_out0, w_in0), False, "gather_attn_w_out_start")
    s_in1 = _exchange_start(_after(c_in1, w_in0), False, "gather_ssm_w_in_start")
    s_glu = _exchange_start(_after(c_glu, w_in0), False, "gather_ssm_glu_w_start")
    s_out1 = _exchange_start(_after(c_out1, w_in0), False, "gather_ssm_w_out_start")
    proj0 = _mm_nn(_after(h0, s_out0[4], s_in1[4], s_glu[4], s_out1[4]), w_in0, BF16, "attn_in")
    og, o = _attn_fwd(proj0, q_g, k_g, B, S, H, "attn_fwd")
    w_out0 = _exchange_wait(s_out0, og, False, "gather_attn_w_out_wait").reshape(1, D, D)
    x1 = _mm_nn(og, w_out0, F32, "attn_out", residual=xf)

    h1 = _rmsnorm_fwd(x1, g1, "norm1")
    w_in1 = _exchange_wait(s_in1, h1, False, "gather_ssm_w_in_wait")
    proj1 = _mm_nn(h1, w_in1, BF16, "ssm_in")
    u_p = _permute_rows(proj1[:, :D], n_steps)
    gate1 = proj1[:, D:]
    y_p, h_in = _s5_fwd(u_p, wb_b, wc_b, lam, d_row, B, "s5_fwd")
    y_ssm = _unpermute_rows(y_p, n_steps)
    (yg,) = _ew(lambda a: (_gelu(a)[0],), [y_ssm], [], [BF16], 0, "gelu")
    w_glu = _exchange_wait(s_glu, yg, False, "gather_ssm_glu_w_wait").reshape(1, D, D)
    z = _mm_nn(yg, w_glu, F32, "glu_in", bias=glu_b_row)

    def glu_fwd(y, zz, gt):
        gt = gt.astype(F32)
        return (_gelu(y)[0] * _sigmoid(zz) * (gt * _sigmoid(gt)),)

    (y3,) = _ew(glu_fwd, [y_ssm, z, gate1], [], [BF16], 0, "glu_gate")
    w_out1 = _exchange_wait(s_out1, y3, False, "gather_ssm_w_out_wait").reshape(1, D, D)
    out = _mm_nn(y3, w_out1, F32, "ssm_out", residual=x1)

    dout, dout_b, loss_part = _loss_head(out, target, "loss")
    loss = lax.psum(loss_part[0, 0], ("x", "y", "c"))

    p_w_out1 = _mm_tn(y3, dout_b, 1, BF16, "ssm_out_dw").reshape(N_DEV, D // N_DEV, D)
    sc_out1 = _exchange_start(p_w_out1, True, "scatter_ssm_w_out_start")
    dy3 = _mm_nt(_after(dout_b, sc_out1[4]), w_out1, F32, "ssm_out_dx")

    def glu_bwd(d3, y, zz, gt):
        gt = gt.astype(F32)
        sg = _sigmoid(gt)
        sz = _sigmoid(zz)
        ygv, _ = _gelu(y)
        dy2 = d3 * (gt * sg)
        dgate = d3 * (ygv * sz) * (sg * (1.0 + gt * (1.0 - sg)))
        dz = dy2 * ygv * (sz * (1.0 - sz))
        return dz, dgate, dy2 * sz, jnp.sum(dz, axis=0, keepdims=True)

    dz_b, dgate1, dyg_a, dglu_b = _ew(glu_bwd, [dy3, y_ssm, z, gate1], [], [BF16, BF16, F32], 1, "glu_gate_bwd")
    p_w_glu = _mm_tn(yg, dz_b, 1, BF16, "glu_in_dw").reshape(N_DEV, D // N_DEV, D)
    sc_glu = _exchange_start(p_w_glu, True, "scatter_ssm_glu_w_start")
    dyg_b = _mm_nt(_after(dz_b, sc_glu[4]), w_glu, F32, "glu_in_dx")

    def gelu_bwd(da, db, y):
        _, t = _gelu(y)
        dg = 0.5 * (1.0 + t) + 0.5 * y * (1.0 - t * t) * (GELU_C * (1.0 + 3.0 * 0.044715 * (y * y)))
        return ((da + db) * dg,)

    (dy_ssm,) = _ew(gelu_bwd, [dyg_a, dyg_b, y_ssm], [], [BF16], 0, "gelu_bwd")
    dy_p = _permute_rows(dy_ssm, n_steps)
    du_p, dwb, dwc, dlam, dd = _s5_bwd(u_p, dy_p, wb_b, wbt_b, wc_b, wct_b, lam, d_row, h_in, B, "s5_bwd")
    du = _unpermute_rows(du_p, n_steps)
    dproj1 = jnp.concatenate([du, dgate1], axis=1)
    p_w_in1 = _mm_tn(h1, dproj1, N_DEV, BF16, "ssm_in_dw")
    sc_in1 = _exchange_start(p_w_in1, True, "scatter_ssm_w_in_start")
    dh1 = _mm_nt(_after(dproj1, sc_in1[4]), w_in1, F32, "ssm_in_dx")
    dx1, dx1_b, dg1 = _rmsnorm_bwd(x1, dh1, dout, g1, "norm1_bwd")

    p_w_out0 = _mm_tn(og, dx1_b, 1, BF16, "attn_out_dw").reshape(N_DEV, D // N_DEV, D)
    sc_out0 = _exchange_start(p_w_out0, True, "scatter_attn_w_out_start")
    dog = _mm_nt(_after(dx1_b, sc_out0[4]), w_out0, BF16, "attn_out_dx")
    dq, dk, dv, dgate0, dqg, dkg = _attn_bwd(proj0, o, dog, q_g, k_g, B, S, H, "attn_bwd")
    dproj0 = jnp.concatenate([dq, dk, dv, dgate0], axis=1)
    p_w_in0 = _mm_tn(h0, dproj0, N_DEV, BF16, "attn_in_dw")
    sc_in0 = _exchange_start(p_w_in0, True, "scatter_attn_w_in_start")
    dh0 = _mm_nt(_after(dproj0, sc_in0[4]), w_in0, F32, "attn_in_dx")
    dx, _, dg0 = _rmsnorm_bwd(xf, dh0, dx1, g0, "norm0_bwd")

    def update(started, after, w, m, v, name):
        recv = _exchange_wait(started, after, True, "scatter_" + name + "_wait")
        return _adamw(recv, w[0], m[0], v[0], "adamw_" + name)

    r_ssm_w_out = update(sc_out1, dx, ssm_w_out, m_ssm_w_out, v_ssm_w_out, "ssm_w_out")
    r_ssm_glu_w = update(sc_glu, r_ssm_w_out[0], ssm_glu_w, m_ssm_glu_w, v_ssm_glu_w, "ssm_glu_w")
    r_ssm_w_in = update(sc_in1, r_ssm_glu_w[0], ssm_w_in, m_ssm_w_in, v_ssm_w_in, "ssm_w_in")
    r_attn_w_out = update(sc_out0, r_ssm_w_in[0], attn_w_out, m_attn_w_out, v_attn_w_out, "attn_w_out")

    dbb_re, dbb_im = _block_diag_in_grad(dwb)
    dc_re, dc_im = _block_diag_out_grad(dwc)
    dlam_re = dlam[:, 0, :HALF].reshape(G, STATE)
    dlam_im = dlam[:, 0, HALF:].reshape(G, STATE)
    by_owner = [t.reshape((N_DEV, -1)) for t in (dlam_re, dlam_im, dbb_re, dbb_im, dc_re, dc_im, dd, dglu_b)]
    small_parts = jnp.concatenate([t.reshape(N_DEV, -1, PACK_W) for t in by_owner], axis=1)
    small_sum = _sum_parts(_exchange(small_parts, True, "scatter_small"), "sum_small")
    s_lam_re, s_lam_im, s_bb_re, s_bb_im, s_c_re, s_c_im, s_d, s_glu_b = _unpack(small_sum, small_shapes)
    g_a_re, g_a_im, g_log_dt, g_b_re, g_b_im = disc_vjp((s_lam_re, s_lam_im, s_bb_re, s_bb_im))

    local_names = ["ssm_A_re", "ssm_A_im", "ssm_log_dt", "ssm_B_re", "ssm_B_im", "ssm_C_re", "ssm_C_im",
                   "ssm_D", "ssm_glu_b"]
    local_g = [g_a_re, g_a_im, g_log_dt, g_b_re, g_b_im, s_c_re, s_c_im, s_d, s_glu_b]
    local_w = [ssm_A_re, ssm_A_im, ssm_log_dt, ssm_B_re, ssm_B_im, ssm_C_re, ssm_C_im, ssm_D, ssm_glu_b]
    local_m = [m_ssm_A_re, m_ssm_A_im, m_ssm_log_dt, m_ssm_B_re, m_ssm_B_im, m_ssm_C_re, m_ssm_C_im,
               m_ssm_D, m_ssm_glu_b]
    local_v = [v_ssm_A_re, v_ssm_A_im, v_ssm_log_dt, v_ssm_B_re, v_ssm_B_im, v_ssm_C_re, v_ssm_C_im,
               v_ssm_D, v_ssm_glu_b]
    r_local = _adamw_small(local_g, local_w, local_m, local_v, None, "adamw_small")

    rep_g = [jnp.concatenate([dg0, dg1], axis=0), dqg, dkg]
    rep_w = [norm_g, attn_q_g, attn_k_g]
    rep_m = [m_norm_g, m_attn_q_g, m_attn_k_g]
    rep_v = [v_norm_g, v_attn_q_g, v_attn_k_g]
    r_rep = _adamw_small(rep_g, rep_w, rep_m, rep_v, "gather_rep", "adamw_rep")

    r_attn_w_in = update(sc_in0, r_rep[0][0], attn_w_in, m_attn_w_in, v_attn_w_in, "attn_w_in")

    res = {"attn_w_in": r_attn_w_in, "attn_w_out": r_attn_w_out, "ssm_w_in": r_ssm_w_in,
           "ssm_glu_w": r_ssm_glu_w, "ssm_w_out": r_ssm_w_out}
    ref_w = {"attn_w_in": attn_w_in, "attn_w_out": attn_w_out, "ssm_w_in": ssm_w_in,
             "ssm_glu_w": ssm_glu_w, "ssm_w_out": ssm_w_out}
    for name, r, w in zip(local_names, r_local, local_w):
        res[name], ref_w[name] = r, w
    for name, r, w in zip(["norm_g", "attn_q_g", "attn_k_g"], r_rep, rep_w):
        res[name], ref_w[name] = r, w
    order = ["norm_g", "attn_w_in", "attn_q_g", "attn_k_g", "attn_w_out", "ssm_w_in", "ssm_A_re", "ssm_A_im",
             "ssm_log_dt", "ssm_B_re", "ssm_B_im", "ssm_C_re", "ssm_C_im", "ssm_D", "ssm_glu_w", "ssm_glu_b",
             "ssm_w_out"]
    outs = [loss, dx.reshape(B, S, D)]
    for kind in range(4):
        outs += [res[n][kind].reshape(ref_w[n].shape) for n in order]
    return tuple(outs)


def _adamw_small(grads, ws, ms, vs, gather_name, name):
    sizes = [math.prod(w.shape) for w in ws]
    total = sum(sizes)
    rows = -(-total // (PACK_W * 8)) * 8
    if rows > 256:
        rows = -(-rows // 256) * 256

    def pack(ts, fill):
        flat = jnp.concatenate([t.reshape(-1).astype(F32) for t in ts])
        flat = jnp.concatenate([flat, jnp.full((rows * PACK_W - total,), fill, F32)])
        return flat.reshape(rows, PACK_W)

    g = pack(grads, 0.0)
    parts = _exchange(g, False, gather_name) if gather_name else g[None]
    res = _adamw(parts, pack(ws, 0.0), pack(ms, 0.0), pack(vs, 1.0), name)
    outs = []
    off = 0
    flats = [r.reshape(-1) for r in res]
    for n in sizes:
        outs.append(tuple(f[off:off + n] for f in flats))
        off += n
    return outs
```

```python
import functools
import math

import jax
import jax.numpy as jnp
from jax import lax
from jax.experimental import pallas as pl
from jax.experimental.pallas import tpu as pltpu

F32 = jnp.float32
BF16 = jnp.bfloat16

N_DEV = 8
HEAD_DIM = 128
GROUP = 16
STATE = 64
GROUPS_PER_BLOCK = 8
SUBLANES = 8
RMS_EPS = 1e-6
ADAM_LR, ADAM_B1, ADAM_B2, ADAM_EPS, ADAM_WD, ADAM_STEP = 0.001, 0.9, 0.999, 1e-08, 0.01, 10
VMEM_LIMIT = 56 * 1024 * 1024
GELU_C = math.sqrt(2.0 / math.pi)
PACK_W = 128


def _params(sem, **kw):
    return pltpu.CompilerParams(dimension_semantics=sem, vmem_limit_bytes=VMEM_LIMIT, **kw)


def _tile(n, t):
    t = min(n, t)
    assert n % t == 0, (n, t)
    return t


def _ones_where(cond):
    return jnp.where(cond, 1.0, 0.0).astype(BF16)


def _sigmoid(x):
    return 1.0 / (1.0 + jnp.exp(-x))


def _log_sigmoid(z):
    return jnp.minimum(z, 0.0) - jnp.log(1.0 + jnp.exp(-jnp.abs(z)))


def _dot(a, b, dims):
    return lax.dot_general(a, b, (dims, ((), ())), preferred_element_type=F32)


NN = ((1,), (0,))
NT = ((1,), (1,))
TN = ((0,), (0,))


def _exchange(x, scatter, name):
    shape = x.shape[1:] if scatter else x.shape

    def body(x_ref, out_ref, send_sems, recv_sems, local_sem):
        ix, iy, ic = lax.axis_index("x"), lax.axis_index("y"), lax.axis_index("c")
        me = 4 * ix + 2 * iy + ic

        def peer(k):
            kx, ky, kc = (k >> 2) & 1, (k >> 1) & 1, k & 1
            px, py, pc = ix ^ kx, iy ^ ky, ic ^ kc
            return (px, py, pc), 4 * px + 2 * py + pc

        mine = pltpu.make_async_copy(x_ref.at[me] if scatter else x_ref, out_ref.at[me], local_sem)
        mine.start()
        copies = []
        for k in range(1, N_DEV):
            pid, pidx = peer(k)
            cp = pltpu.make_async_remote_copy(
                src_ref=x_ref.at[pidx] if scatter else x_ref,
                dst_ref=out_ref.at[me],
                send_sem=send_sems.at[k - 1], recv_sem=recv_sems.at[k - 1],
                device_id=pid, device_id_type=pl.DeviceIdType.MESH)
            cp.start()
            copies.append(cp)
        for k in range(1, N_DEV):
            pid, pidx = peer(k)
            pltpu.make_async_remote_copy(
                src_ref=x_ref.at[pidx] if scatter else x_ref,
                dst_ref=out_ref.at[pidx],
                send_sem=send_sems.at[k - 1], recv_sem=recv_sems.at[k - 1],
                device_id=pid, device_id_type=pl.DeviceIdType.MESH).wait_recv()
        for cp in copies:
            cp.wait_send()
        mine.wait()

    return pl.pallas_call(
        body, name=name,
        out_shape=jax.ShapeDtypeStruct((N_DEV,) + tuple(shape), x.dtype),
        in_specs=[pl.BlockSpec(memory_space=pl.ANY)],
        out_specs=pl.BlockSpec(memory_space=pl.ANY),
        scratch_shapes=[pltpu.SemaphoreType.DMA((N_DEV - 1,)), pltpu.SemaphoreType.DMA((N_DEV - 1,)),
                        pltpu.SemaphoreType.DMA],
    )(x)


_HBM = pl.BlockSpec(memory_space=pltpu.HBM)
_SEM = pl.BlockSpec(memory_space=pltpu.SEMAPHORE)
_ANY = pl.BlockSpec(memory_space=pl.ANY)
_EFFECT = pltpu.SideEffectType.DATAFLOW_SIDE_EFFECTING


def _peer(k):
    ix, iy, ic = lax.axis_index("x"), lax.axis_index("y"), lax.axis_index("c")
    px, py, pc = ix ^ ((k >> 2) & 1), iy ^ ((k >> 1) & 1), ic ^ (k & 1)
    return (px, py, pc), 4 * px + 2 * py + pc


def _my_index():
    return 4 * lax.axis_index("x") + 2 * lax.axis_index("y") + lax.axis_index("c")


def _exchange_start(x, scatter, name, deps=()):
    shape = x.shape[1:] if scatter else x.shape

    def body(*refs):
        x_ref, land_ref = refs[:2]
        send_sems, recv_sems, x_thru, land_thru, token, local_sem = refs[2 + len(deps):]
        me = _my_index()
        mine = pltpu.make_async_copy(x_ref.at[me] if scatter else x_ref, land_ref.at[me], local_sem)
        mine.start()
        mine.wait()
        for k in range(1, N_DEV):
            pid, pidx = _peer(k)
            pltpu.make_async_remote_copy(
                src_ref=x_ref.at[pidx] if scatter else x_ref, dst_ref=land_ref.at[me],
                send_sem=send_sems.at[k - 1], recv_sem=recv_sems.at[k - 1],
                device_id=pid, device_id_type=pl.DeviceIdType.MESH).start()
        token[...] = jnp.zeros_like(token)

    land = lax.empty((N_DEV,) + tuple(shape), x.dtype)
    return pl.pallas_call(
        body, name=name,
        out_shape=(pltpu.SemaphoreType.DMA((N_DEV - 1,)), pltpu.SemaphoreType.DMA((N_DEV - 1,)),
                   pltpu.HBM(x.shape, x.dtype), pltpu.HBM(land.shape, land.dtype),
                   jax.ShapeDtypeStruct((8, 128), F32)),
        in_specs=(_HBM, _HBM) + (_ANY,) * len(deps),
        out_specs=(_SEM, _SEM, _HBM, _HBM, pl.BlockSpec(memory_space=pltpu.VMEM)),
        input_output_aliases={0: 2, 1: 3},
        scratch_shapes=[pltpu.SemaphoreType.DMA],
        compiler_params=pltpu.CompilerParams(has_side_effects=_EFFECT),
    )(pltpu.with_memory_space_constraint(x, pltpu.HBM), pltpu.with_memory_space_constraint(land, pltpu.HBM), *deps)


def _exchange_wait(started, after, scatter, name):
    send_sems, recv_sems, x_thru, land_thru, _ = started

    def body(x_ref, land_ref, send_sems, recv_sems, after_ref, x_dead, got_ref):
        for k in range(1, N_DEV):
            pid, pidx = _peer(k)
            cp = pltpu.make_async_remote_copy(
                src_ref=x_ref.at[pidx] if scatter else x_ref, dst_ref=land_ref.at[pidx],
                send_sem=send_sems.at[k - 1], recv_sem=recv_sems.at[k - 1],
                device_id=pid, device_id_type=pl.DeviceIdType.MESH)
            cp.wait_send()
            cp.wait_recv()

    return pl.pallas_call(
        body, name=name,
        out_shape=(pltpu.HBM(x_thru.shape, x_thru.dtype), pltpu.HBM(land_thru.shape, land_thru.dtype)),
        in_specs=(_HBM, _HBM, _SEM, _SEM, pl.BlockSpec(memory_space=pl.ANY)),
        out_specs=(_HBM, _HBM),
        input_output_aliases={0: 0, 1: 1},
        compiler_params=pltpu.CompilerParams(has_side_effects=_EFFECT),
    )(x_thru, land_thru, send_sems, recv_sems, after)[1]


def _accumulate(acc, part, step, n_steps, finish):
    if n_steps == 1:
        finish(part)
        return

    @pl.when(step == 0)
    def _():
        acc[...] = part

    @pl.when((step > 0) & (step < n_steps - 1))
    def _():
        acc[...] += part

    @pl.when(step == n_steps - 1)
    def _():
        finish(acc[...] + part)


def _mm_nn(a, b, out_dtype, name, bias=None, residual=None, deps=(), tm=1024, tn=1024, tk=2048):
    M, K = a.shape
    J, K2, Nj = b.shape
    assert K == K2
    tm, tn, tk = _tile(M, tm), _tile(Nj, tn), _tile(K, tk)
    nb, nk = Nj // tn, K // tk

    def body(*refs):
        a_ref, b_ref = refs[0], refs[1]
        i = 2
        bias_ref = res_ref = None
        if bias is not None:
            bias_ref = refs[i]; i += 1
        if residual is not None:
            res_ref = refs[i]; i += 1
        i += len(deps)
        o_ref, acc = refs[i], refs[i + 1]

        def finish(r):
            if bias_ref is not None:
                r = r + bias_ref[...]
            if res_ref is not None:
                r = r + res_ref[...]
            o_ref[...] = r.astype(out_dtype)

        _accumulate(acc, _dot(a_ref[...], b_ref[...], NN), pl.program_id(3), nk, finish)

    in_specs = [pl.BlockSpec((tm, tk), lambda j, m, n, k: (m, k)),
                pl.BlockSpec((None, tk, tn), lambda j, m, n, k: (j, k, n))]
    args = [a, b]
    if bias is not None:
        in_specs.append(pl.BlockSpec((1, tn), lambda j, m, n, k: (0, j * nb + n)))
        args.append(bias)
    if residual is not None:
        in_specs.append(pl.BlockSpec((tm, tn), lambda j, m, n, k: (m, j * nb + n)))
        args.append(residual)
    in_specs += [_ANY] * len(deps)
    args += list(deps)
    return pl.pallas_call(
        body, name=name,
        out_shape=jax.ShapeDtypeStruct((M, J * Nj), out_dtype),
        grid=(J, M // tm, nb, nk),
        in_specs=in_specs,
        out_specs=pl.BlockSpec((tm, tn), lambda j, m, n, k: (m, j * nb + n)),
        scratch_shapes=[pltpu.VMEM((tm, tn), F32)],
        compiler_params=_params(("parallel", "parallel", "parallel", "arbitrary")),
    )(*args)


def _mm_nt(a, b, out_dtype, name, deps=(), tm=1024, tp=1024, tq=2048):
    M, Q = a.shape
    J, P, Qj = b.shape
    assert Q == J * Qj
    tm, tp, tq = _tile(M, tm), _tile(P, tp), _tile(Qj, tq)
    nq = Qj // tq

    def body(*refs):
        a_ref, b_ref = refs[:2]
        o_ref, acc = refs[2 + len(deps):]

        def finish(r):
            o_ref[...] = r.astype(out_dtype)

        _accumulate(acc, _dot(a_ref[...], b_ref[...], NT), pl.program_id(2) * nq + pl.program_id(3), J * nq, finish)

    return pl.pallas_call(
        body, name=name,
        out_shape=jax.ShapeDtypeStruct((M, P), out_dtype),
        grid=(M // tm, P // tp, J, nq),
        in_specs=[pl.BlockSpec((tm, tq), lambda m, p, j, q: (m, j * nq + q)),
                  pl.BlockSpec((None, tp, tq), lambda m, p, j, q: (j, p, q))] + [_ANY] * len(deps),
        out_specs=pl.BlockSpec((tm, tp), lambda m, p, j, q: (m, p)),
        scratch_shapes=[pltpu.VMEM((tm, tp), F32)],
        compiler_params=_params(("parallel", "parallel", "arbitrary", "arbitrary")),
    )(a, b, *deps)


def _mm_tn(a, b, J, out_dtype, name, tm=1024, tn=1024, tr=2048):
    R, M = a.shape
    R2, N = b.shape
    assert R == R2 and N % J == 0
    Nj = N // J
    tm, tn, tr = _tile(M, tm), _tile(Nj, tn), _tile(R, tr)
    nb, nr = Nj // tn, R // tr

    def body(a_ref, b_ref, o_ref, acc):
        def finish(r):
            o_ref[...] = r.astype(out_dtype)

        _accumulate(acc, _dot(a_ref[...], b_ref[...], TN), pl.program_id(3), nr, finish)

    return pl.pallas_call(
        body, name=name,
        out_shape=jax.ShapeDtypeStruct((J, M, Nj), out_dtype),
        grid=(J, M // tm, nb, nr),
        in_specs=[pl.BlockSpec((tr, tm), lambda j, m, n, r: (r, m)),
                  pl.BlockSpec((tr, tn), lambda j, m, n, r: (r, j * nb + n))],
        out_specs=pl.BlockSpec((None, tm, tn), lambda j, m, n, r: (j, m, n)),
        scratch_shapes=[pltpu.VMEM((tm, tn), F32)],
        compiler_params=_params(("parallel", "parallel", "parallel", "arbitrary")),
    )(a, b)


def _ew(fn, ins, vecs, out_dtypes, n_acc, name, tr=256, tc=1024):
    T, C = ins[0].shape
    tr, tc = _tile(T, tr), _tile(C, tc)
    n_in, n_vec, n_out = len(ins), len(vecs), len(out_dtypes)

    def body(*refs):
        in_refs = refs[:n_in + n_vec]
        out_refs = refs[n_in + n_vec:n_in + n_vec + n_out]
        acc_refs = refs[n_in + n_vec + n_out:]
        res = fn(*[r[...] for r in in_refs])
        for o_ref, v in zip(out_refs, res[:n_out]):
            o_ref[...] = v.astype(o_ref.dtype)
        if n_acc:
            r = pl.program_id(1)

            @pl.when(r == 0)
            def _():
                for a_ref in acc_refs:
                    a_ref[...] = jnp.zeros_like(a_ref)

            for a_ref, v in zip(acc_refs, res[n_out:]):
                a_ref[...] += v

    blk = pl.BlockSpec((tr, tc), lambda c, r: (r, c))
    vec = pl.BlockSpec((1, tc), lambda c, r: (0, c))
    out = pl.pallas_call(
        body, name=name,
        out_shape=tuple([jax.ShapeDtypeStruct((T, C), d) for d in out_dtypes]
                        + [jax.ShapeDtypeStruct((1, C), F32)] * n_acc),
        grid=(C // tc, T // tr),
        in_specs=[blk] * n_in + [vec] * n_vec,
        out_specs=tuple([blk] * n_out + [vec] * n_acc),
        compiler_params=_params(("parallel", "arbitrary")),
    )(*ins, *vecs)
    return out


def _cast_bf16(w, name):
    return _ew(lambda a: (a,), [w], [], [BF16], 0, name)[0]


def _rmsnorm_fwd(x, g, name, deps=(), tr=128):
    T, D = x.shape
    tr = _tile(T, tr)

    def body(*refs):
        x_ref, g_ref, h_ref = refs[0], refs[1], refs[2 + len(deps)]
        xv = x_ref[...]
        r = lax.rsqrt(jnp.mean(xv * xv, axis=-1, keepdims=True) + RMS_EPS)
        h_ref[...] = ((xv * r) * g_ref[...]).astype(BF16)

    return pl.pallas_call(
        body, name=name,
        out_shape=jax.ShapeDtypeStruct((T, D), BF16),
        grid=(T // tr,),
        in_specs=[pl.BlockSpec((tr, D), lambda i: (i, 0)), pl.BlockSpec((1, D), lambda i: (0, 0))] + [_ANY] * len(deps),
        out_specs=pl.BlockSpec((tr, D), lambda i: (i, 0)),
        compiler_params=_params(("parallel",)),
    )(x, g, *deps)


def _rmsnorm_bwd(x, dh, dres, g, name, tr=128):
    T, D = x.shape
    tr = _tile(T, tr)

    def body(x_ref, dh_ref, dres_ref, g_ref, dx_ref, dxb_ref, dg_ref):
        xv = x_ref[...]
        r = lax.rsqrt(jnp.mean(xv * xv, axis=-1, keepdims=True) + RMS_EPS)
        xn = xv * r
        dhv = dh_ref[...].astype(F32)
        dxn = dhv * g_ref[...]
        dx = dres_ref[...] + r * (dxn - xn * jnp.mean(dxn * xn, axis=-1, keepdims=True))
        dx_ref[...] = dx
        dxb_ref[...] = dx.astype(BF16)

        @pl.when(pl.program_id(0) == 0)
        def _():
            dg_ref[...] = jnp.zeros_like(dg_ref)

        dg_ref[...] += jnp.sum(dhv * xn, axis=0, keepdims=True)

    blk = pl.BlockSpec((tr, D), lambda i: (i, 0))
    vec = pl.BlockSpec((1, D), lambda i: (0, 0))
    return pl.pallas_call(
        body, name=name,
        out_shape=(jax.ShapeDtypeStruct((T, D), F32), jax.ShapeDtypeStruct((T, D), BF16),
                   jax.ShapeDtypeStruct((1, D), F32)),
        grid=(T // tr,),
        in_specs=[blk, blk, blk, vec],
        out_specs=(blk, blk, vec),
        compiler_params=_params(("arbitrary",)),
    )(x, dh, dres, g)


def _loss_head(y, target, name, tr=128):
    T, D = y.shape
    tr = _tile(T, tr)
    n = T // tr

    def body(y_ref, t_ref, dy_ref, dyb_ref, loss_ref, acc):
        i = pl.program_id(0)

        @pl.when(i == 0)
        def _():
            acc[...] = jnp.zeros_like(acc)

        err = y_ref[...] - t_ref[...]
        dy = err * (1.0 / D)
        dy_ref[...] = dy
        dyb_ref[...] = dy.astype(BF16)
        acc[...] += jnp.sum(err * err, axis=0, keepdims=True)

        @pl.when(i == n - 1)
        def _():
            loss_ref[...] = jnp.sum(acc[...], axis=1, keepdims=True) * (0.5 / D)

    blk = pl.BlockSpec((tr, D), lambda i: (i, 0))
    return pl.pallas_call(
        body, name=name,
        out_shape=(jax.ShapeDtypeStruct((T, D), F32), jax.ShapeDtypeStruct((T, D), BF16),
                   jax.ShapeDtypeStruct((1, 1), F32)),
        grid=(n,),
        in_specs=[blk, blk],
        out_specs=(blk, blk, pl.BlockSpec((1, 1), lambda i: (0, 0))),
        scratch_shapes=[pltpu.VMEM((1, D), F32)],
        compiler_params=_params(("arbitrary",)),
    )(y, target)


def _gelu(x):
    t = jnp.tanh(GELU_C * (x + 0.044715 * (x * x * x)))
    return x * (0.5 * (1.0 + t)), t


def _adamw(parts, w, m, v, name, tr=256, tc=1024):
    n, R, C = parts.shape
    tr, tc = _tile(R, tr), _tile(C, tc)
    c1 = 1.0 - ADAM_B1 ** ADAM_STEP
    c2 = 1.0 - ADAM_B2 ** ADAM_STEP

    def body(p_ref, w_ref, m_ref, v_ref, g_out, d_out, m_out, v_out):
        g = p_ref[0].astype(F32)
        for k in range(1, n):
            g = g + p_ref[k].astype(F32)
        mn = ADAM_B1 * m_ref[...] + (1.0 - ADAM_B1) * g
        vn = ADAM_B2 * v_ref[...] + (1.0 - ADAM_B2) * (g * g)
        m_hat = mn / c1
        v_hat = vn / c2
        g_out[...] = g
        d_out[...] = -ADAM_LR * (m_hat / (jnp.sqrt(v_hat) + ADAM_EPS) + ADAM_WD * w_ref[...])
        m_out[...] = mn
        v_out[...] = vn

    blk = pl.BlockSpec((tr, tc), lambda r, c: (r, c))
    return pl.pallas_call(
        body, name=name,
        out_shape=tuple([jax.ShapeDtypeStruct((R, C), F32)] * 4),
        grid=(R // tr, C // tc),
        in_specs=[pl.BlockSpec((n, tr, tc), lambda r, c: (0, r, c)), blk, blk, blk],
        out_specs=(blk, blk, blk, blk),
        compiler_params=_params(("parallel", "parallel")),
    )(parts, w, m, v)


def _sum_parts(parts, name):
    n, R, C = parts.shape

    def body(p_ref, o_ref):
        g = p_ref[0].astype(F32)
        for k in range(1, n):
            g = g + p_ref[k].astype(F32)
        o_ref[...] = g

    return pl.pallas_call(
        body, name=name,
        out_shape=jax.ShapeDtypeStruct((R, C), F32),
        compiler_params=pltpu.CompilerParams(vmem_limit_bytes=VMEM_LIMIT),
    )(parts)


def _head_norm(xv):
    xv = xv.astype(F32)
    r = lax.rsqrt(jnp.mean(xv * xv, axis=-1, keepdims=True) + RMS_EPS)
    return xv * r, r


def _split_bf16(x):
    hi = x.astype(BF16)
    lo = (x - hi.astype(F32)).astype(BF16)
    return hi, lo


FWD_HEADS = 4
BWD_HEADS = 2


def _attn_specs(S, H, HP):
    def spec(part):
        return pl.BlockSpec((S, HP * HEAD_DIM), lambda b, h: (b, part * (H // HP) + h))
    return spec


def _lanes(hh):
    return slice(hh * HEAD_DIM, (hh + 1) * HEAD_DIM)


def _attn_fwd(proj, q_g, k_g, B, S, H, name):
    TQ = _tile(S, 256)
    nq = S // TQ
    scale = 1.0 / math.sqrt(HEAD_DIM)
    HP = FWD_HEADS
    ATT_W = HP * HEAD_DIM
    heads = range(HP)

    def body(q_ref, k_ref, v_ref, gate_ref, qg_ref, kg_ref, og_ref, o_ref, qn_s, kn_s):
        for hh in heads:
            qn_s[:, _lanes(hh)] = (_head_norm(q_ref[:, _lanes(hh)])[0] * qg_ref[...]).astype(BF16)
            kn_s[:, _lanes(hh)] = (_head_norm(k_ref[:, _lanes(hh)])[0] * kg_ref[...]).astype(BF16)
        row = lax.broadcasted_iota(jnp.int32, (TQ, TQ), 0)
        col = lax.broadcasted_iota(jnp.int32, (TQ, TQ), 1)
        later = _ones_where(row > col)
        causal = col < row

        def q_block(qi, _):
            q0 = pl.multiple_of(qi * TQ, TQ)

            def both(ki, state, diag):
                k0 = pl.multiple_of(ki * TQ, TQ)
                z = [_dot(qn_s[pl.ds(q0, TQ), _lanes(hh)], kn_s[pl.ds(k0, TQ), _lanes(hh)], NT) * scale
                     for hh in heads]
                ls = [_log_sigmoid(zz) for zz in z]
                l1m = [a - zz for a, zz in zip(ls, z)]
                if diag:
                    l1m = [jnp.where(causal, a, 0.0) for a in l1m]
                parts = [_split_bf16(a) for a in l1m]
                suffix = [_dot(hi, later, NN) + _dot(lo, later, NN) + state[hh][0]
                          for hh, (hi, lo) in zip(heads, parts)]
                w = [jnp.exp(a + sfx) for a, sfx in zip(ls, suffix)]
                if diag:
                    w = [jnp.where(causal, a, 0.0) for a in w]
                acc = [state[hh][1] + _dot(w[hh].astype(BF16), v_ref[pl.ds(k0, TQ), _lanes(hh)], NN)
                       for hh in heads]
                return tuple((state[hh][0] + jnp.sum(l1m[hh], axis=1, keepdims=True), acc[hh]) for hh in heads)

            zero = (jnp.zeros((TQ, 1), F32), jnp.zeros((TQ, HEAD_DIM), F32))
            state = both(qi, (zero,) * HP, True)
            state = lax.fori_loop(0, qi, lambda i, st: both(qi - 1 - i, st, False), state)
            for hh in heads:
                acc = state[hh][1]
                o_ref[pl.ds(q0, TQ), _lanes(hh)] = acc.astype(BF16)
                gate = gate_ref[pl.ds(q0, TQ), _lanes(hh)].astype(F32)
                og_ref[pl.ds(q0, TQ), _lanes(hh)] = (acc * (gate * _sigmoid(gate))).astype(BF16)
            return 0

        lax.fori_loop(0, nq, q_block, 0)

    spec = _attn_specs(S, H, HP)
    vec = pl.BlockSpec((1, HEAD_DIM), lambda b, h: (0, 0))
    out = pl.BlockSpec((S, ATT_W), lambda b, h: (b, h))
    return pl.pallas_call(
        body, name=name,
        out_shape=(jax.ShapeDtypeStruct((B * S, H * HEAD_DIM), BF16),) * 2,
        grid=(B, H // HP),
        in_specs=[spec(0), spec(1), spec(2), spec(3), vec, vec],
        out_specs=(out, out),
        scratch_shapes=[pltpu.VMEM((S, ATT_W), BF16)] * 2,
        compiler_params=_params(("parallel", "parallel")),
    )(proj, proj, proj, proj, q_g, k_g)


def _attn_bwd(proj, o, dog, q_g, k_g, B, S, H, name):
    TQ = _tile(S, 256)
    nq = S // TQ
    scale = 1.0 / math.sqrt(HEAD_DIM)
    HP = BWD_HEADS
    ATT_W = HP * HEAD_DIM
    heads = range(HP)

    def body(q_ref, k_ref, v_ref, gate_ref, o_ref, dog_ref, qg_ref, kg_ref,
             dq_ref, dk_ref, dv_ref, dgate_ref, dqg_ref, dkg_ref,
             qn_s, kn_s, do_s, dqn_s, dkn_s, dv_s, w_s, sg_s):
        first = (pl.program_id(0) == 0) & (pl.program_id(1) == 0)

        @pl.when(first)
        def _():
            dqg_ref[...] = jnp.zeros_like(dqg_ref)
            dkg_ref[...] = jnp.zeros_like(dkg_ref)

        for hh in heads:
            qn_s[:, _lanes(hh)] = (_head_norm(q_ref[:, _lanes(hh)])[0] * qg_ref[...]).astype(BF16)
            kn_s[:, _lanes(hh)] = (_head_norm(k_ref[:, _lanes(hh)])[0] * kg_ref[...]).astype(BF16)
        gate = gate_ref[...].astype(F32)
        sg = _sigmoid(gate)
        dog_v = dog_ref[...].astype(F32)
        do_s[...] = (dog_v * (gate * sg)).astype(BF16)
        dgate_ref[...] = (dog_v * o_ref[...].astype(F32) * (sg * (1.0 + gate * (1.0 - sg)))).astype(BF16)
        dkn_s[...] = jnp.zeros_like(dkn_s)
        dv_s[...] = jnp.zeros_like(dv_s)

        row = lax.broadcasted_iota(jnp.int32, (TQ, TQ), 0)
        col = lax.broadcasted_iota(jnp.int32, (TQ, TQ), 1)
        later = _ones_where(row > col)
        earlier = _ones_where(row < col)
        causal = col < row

        def q_block(qi, _):
            q0 = pl.multiple_of(qi * TQ, TQ)

            def weights_both(ki, carries, diag):
                k0 = pl.multiple_of(ki * TQ, TQ)
                z = [_dot(qn_s[pl.ds(q0, TQ), _lanes(hh)], kn_s[pl.ds(k0, TQ), _lanes(hh)], NT) * scale
                     for hh in heads]
                ls = [_log_sigmoid(zz) for zz in z]
                l1m = [a - zz for a, zz in zip(ls, z)]
                if diag:
                    l1m = [jnp.where(causal, a, 0.0) for a in l1m]
                parts = [_split_bf16(a) for a in l1m]
                suffix = [_dot(hi, later, NN) + _dot(lo, later, NN) + carries[hh]
                          for hh, (hi, lo) in zip(heads, parts)]
                for hh in heads:
                    w = jnp.exp(ls[hh] + suffix[hh])
                    if diag:
                        w = jnp.where(causal, w, 0.0)
                    w_s[hh, ki] = w
                    sg_s[hh, ki] = jnp.exp(ls[hh])
                return tuple(carries[hh] + jnp.sum(l1m[hh], axis=1, keepdims=True) for hh in heads)

            carries = weights_both(qi, (jnp.zeros((TQ, 1), F32),) * HP, True)
            lax.fori_loop(0, qi, lambda i, c: weights_both(qi - 1 - i, c, False), carries)

            def grads_both(ki, state, diag):
                k0 = pl.multiple_of(ki * TQ, TQ)
                qb = [qn_s[pl.ds(q0, TQ), _lanes(hh)] for hh in heads]
                dob = [do_s[pl.ds(q0, TQ), _lanes(hh)] for hh in heads]
                w = [w_s[hh, ki] for hh in heads]
                da = [_dot(dob[hh], v_ref[pl.ds(k0, TQ), _lanes(hh)], NT) * w[hh] for hh in heads]
                for hh in heads:
                    dv_s[pl.ds(k0, TQ), _lanes(hh)] += _dot(w[hh].astype(BF16), dob[hh], TN)
                parts = [_split_bf16(a) for a in da]
                prefix = [_dot(hi, earlier, NN) + _dot(lo, earlier, NN) + state[hh][0]
                          for hh, (hi, lo) in zip(heads, parts)]
                dzb = []
                for hh in heads:
                    sgz = sg_s[hh, ki]
                    dz = da[hh] * (1.0 - sgz) - sgz * prefix[hh]
                    if diag:
                        dz = jnp.where(causal, dz, 0.0)
                    dzb.append((dz * scale).astype(BF16))
                dq = [state[hh][1] + _dot(dzb[hh], kn_s[pl.ds(k0, TQ), _lanes(hh)], NN) for hh in heads]
                for hh in heads:
                    dkn_s[pl.ds(k0, TQ), _lanes(hh)] += _dot(dzb[hh], qb[hh], TN)
                return tuple((state[hh][0] + jnp.sum(da[hh], axis=1, keepdims=True), dq[hh]) for hh in heads)

            zero = (jnp.zeros((TQ, 1), F32), jnp.zeros((TQ, HEAD_DIM), F32))
            state = lax.fori_loop(0, qi, lambda i, st: grads_both(i, st, False), (zero,) * HP)
            state = grads_both(qi, state, True)
            for hh in heads:
                dqn_s[pl.ds(q0, TQ), _lanes(hh)] = state[hh][1]
            return 0

        lax.fori_loop(0, nq, q_block, 0)

        def norm_bwd(x_ref, g_ref, dn_s, dx_ref, dg_ref):
            for hh in heads:
                xh, r = _head_norm(x_ref[:, _lanes(hh)])
                dn = dn_s[:, _lanes(hh)]
                dg_ref[...] += jnp.sum(dn * xh, axis=0, keepdims=True)
                dxh = dn * g_ref[...]
                dx_ref[:, _lanes(hh)] = (r * (dxh - xh * jnp.mean(dxh * xh, axis=-1, keepdims=True))).astype(BF16)

        norm_bwd(q_ref, qg_ref, dqn_s, dq_ref, dqg_ref)
        norm_bwd(k_ref, kg_ref, dkn_s, dk_ref, dkg_ref)
        dv_ref[...] = dv_s[...].astype(BF16)

    spec = _attn_specs(S, H, HP)
    vec = pl.BlockSpec((1, HEAD_DIM), lambda b, h: (0, 0))
    blk = pl.BlockSpec((S, ATT_W), lambda b, h: (b, h))
    big = jax.ShapeDtypeStruct((B * S, H * HEAD_DIM), BF16)
    small = jax.ShapeDtypeStruct((1, HEAD_DIM), F32)
    return pl.pallas_call(
        body, name=name,
        out_shape=(big, big, big, big, small, small),
        grid=(B, H // HP),
        in_specs=[spec(0), spec(1), spec(2), spec(3), blk, blk, vec, vec],
        out_specs=(blk, blk, blk, blk, vec, vec),
        scratch_shapes=[pltpu.VMEM((S, ATT_W), BF16)] * 3 + [pltpu.VMEM((S, ATT_W), F32)] * 3
        + [pltpu.VMEM((HP, nq, TQ, TQ), F32)] * 2,
        compiler_params=_params(("arbitrary", "arbitrary")),
    )(proj, proj, proj, proj, o, dog, q_g, k_g)


HALF = GROUPS_PER_BLOCK * STATE


def _cmul(ar, ai, br, bi):
    return ar * br - ai * bi, ar * bi + ai * br


def _cpow(ar, ai, n):
    rr = ri = None
    while n:
        if n & 1:
            rr, ri = (ar, ai) if rr is None else _cmul(rr, ri, ar, ai)
        n >>= 1
        if n:
            ar, ai = _cmul(ar, ai, ar, ai)
    return rr, ri


def _segment_carry(er, ei, lr, li, seg_len, segs_per_seq, reverse):
    Lr, Li = _cpow(lr, li, seg_len)
    pos = lax.broadcasted_iota(jnp.int32, er.shape, 0) % segs_per_seq
    outr = jnp.zeros_like(er)
    outi = jnp.zeros_like(ei)
    pr = pi = None
    for d in range(1, segs_per_seq):
        shift = (SUBLANES - d) if reverse else d
        sr = pltpu.roll(er, shift, 0)
        si = pltpu.roll(ei, shift, 0)
        ok = (pos + d < segs_per_seq) if reverse else (pos >= d)
        sr = jnp.where(ok, sr, 0.0)
        si = jnp.where(ok, si, 0.0)
        if pr is not None:
            sr, si = _cmul(sr, si, pr, pi)
        outr = outr + sr
        outi = outi + si
        pr, pi = (Lr, Li) if pr is None else _cmul(pr, pi, Lr, Li)
    return outr, outi


def _s5_sizes(T, B):
    assert SUBLANES % B == 0
    segs_per_seq = SUBLANES // B
    n_steps = T // SUBLANES
    cj = _tile(n_steps, 64)
    return segs_per_seq, n_steps, cj


def _s5_fwd(u_p, wb, wc, lam, dvec, B, name):
    T, C = u_p.shape
    nb = C // 128
    segs_per_seq, n_steps, cj = _s5_sizes(T, B)
    n_chunks = n_steps // cj
    rows = cj * SUBLANES

    def body(u_ref, wb_ref, wc_ref, lam_ref, d_ref, y_ref, hin_ref, bu_s, h_s):
        lr = jnp.broadcast_to(lam_ref[:, :HALF], (SUBLANES, HALF))
        li = jnp.broadcast_to(lam_ref[:, HALF:], (SUBLANES, HALF))

        def scan_chunk(c, hr, hi, store):
            r0 = pl.multiple_of(c * rows, rows)
            bu_s[...] = _dot(u_ref[pl.ds(r0, rows), :], wb_ref[...], NN)

            def step(j, carry):
                hr, hi = carry
                o = pl.multiple_of(j * SUBLANES, SUBLANES)
                nr = lr * hr - li * hi + bu_s[pl.ds(o, SUBLANES), :HALF]
                ni = lr * hi + li * hr + bu_s[pl.ds(o, SUBLANES), HALF:]
                if store:
                    h_s[pl.ds(o, SUBLANES), :HALF] = nr
                    h_s[pl.ds(o, SUBLANES), HALF:] = ni
                return nr, ni

            hr, hi = lax.fori_loop(0, cj, step, (hr, hi))
            if store:
                uv = u_ref[pl.ds(r0, rows), :].astype(F32)
                y_ref[pl.ds(r0, rows), :] = _dot(h_s[...].astype(BF16), wc_ref[...], NN) + d_ref[...] * uv
            return hr, hi

        zero = jnp.zeros((SUBLANES, HALF), F32)
        er, ei = lax.fori_loop(0, n_chunks, lambda c, h: scan_chunk(c, h[0], h[1], False), (zero, zero))
        h0r, h0i = _segment_carry(er, ei, lr, li, n_steps, segs_per_seq, False)
        hin_ref[:, :HALF] = h0r
        hin_ref[:, HALF:] = h0i
        lax.fori_loop(0, n_chunks, lambda c, h: scan_chunk(c, h[0], h[1], True), (h0r, h0i))

    return pl.pallas_call(
        body, name=name,
        out_shape=(jax.ShapeDtypeStruct((T, C), F32), jax.ShapeDtypeStruct((nb, SUBLANES, 2 * HALF), F32)),
        grid=(nb,),
        in_specs=[pl.BlockSpec((T, 128), lambda g: (0, g)),
                  pl.BlockSpec((None, 128, 2 * HALF), lambda g: (g, 0, 0)),
                  pl.BlockSpec((None, 2 * HALF, 128), lambda g: (g, 0, 0)),
                  pl.BlockSpec((None, 1, 2 * HALF), lambda g: (g, 0, 0)),
                  pl.BlockSpec((1, 128), lambda g: (0, g))],
        out_specs=(pl.BlockSpec((T, 128), lambda g: (0, g)),
                   pl.BlockSpec((None, SUBLANES, 2 * HALF), lambda g: (g, 0, 0))),
        scratch_shapes=[pltpu.VMEM((rows, 2 * HALF), F32)] * 2,
        compiler_params=_params(("parallel",)),
    )(u_p, wb, wc, lam, dvec)


def _s5_bwd(u_p, dy_p, wb, wbt, wc, wct, lam, dvec, h_in, B, name):
    T, C = u_p.shape
    nb = C // 128
    segs_per_seq, n_steps, cj = _s5_sizes(T, B)
    n_chunks = n_steps // cj
    rows = cj * SUBLANES

    def body(u_ref, dy_ref, wb_ref, wbt_ref, wc_ref, wct_ref, lam_ref, d_ref, hin_ref,
             du_ref, dwb_ref, dwc_ref, dlam_ref, dd_ref, h_all, x_s, g_s):
        lr = jnp.broadcast_to(lam_ref[:, :HALF], (SUBLANES, HALF))
        li = jnp.broadcast_to(lam_ref[:, HALF:], (SUBLANES, HALF))
        zero = jnp.zeros((SUBLANES, HALF), F32)

        h_all[pl.ds(0, SUBLANES), :] = hin_ref[...]

        def fwd_chunk(c, carry):
            r0 = pl.multiple_of(c * rows, rows)
            x_s[...] = _dot(u_ref[pl.ds(r0, rows), :], wb_ref[...], NN)

            def step(j, carry):
                hr, hi = carry
                o = pl.multiple_of(j * SUBLANES, SUBLANES)
                nr = lr * hr - li * hi + x_s[pl.ds(o, SUBLANES), :HALF]
                ni = lr * hi + li * hr + x_s[pl.ds(o, SUBLANES), HALF:]
                late = pl.multiple_of(r0 + o + SUBLANES, SUBLANES)
                h_all[pl.ds(late, SUBLANES), :HALF] = nr
                h_all[pl.ds(late, SUBLANES), HALF:] = ni
                return nr, ni

            return lax.fori_loop(0, cj, step, carry)

        lax.fori_loop(0, n_chunks, fwd_chunk, (hin_ref[:, :HALF], hin_ref[:, HALF:]))

        def bwd_chunk(i, carry, store):
            c = n_chunks - 1 - i
            r0 = pl.multiple_of(c * rows, rows)
            dyv = dy_ref[pl.ds(r0, rows), :]
            x_s[...] = _dot(dyv, wct_ref[...], NN)

            def step(jj, carry):
                ar, ai, accr, acci = carry
                j = cj - 1 - jj
                o = pl.multiple_of(j * SUBLANES, SUBLANES)
                nr = lr * ar + li * ai + x_s[pl.ds(o, SUBLANES), :HALF]
                ni = lr * ai - li * ar + x_s[pl.ds(o, SUBLANES), HALF:]
                if store:
                    g_s[pl.ds(o, SUBLANES), :HALF] = nr
                    g_s[pl.ds(o, SUBLANES), HALF:] = ni
                    prev = pl.multiple_of(r0 + o, SUBLANES)
                    pr = h_all[pl.ds(prev, SUBLANES), :HALF]
                    pi = h_all[pl.ds(prev, SUBLANES), HALF:]
                    accr = accr + nr * pr + ni * pi
                    acci = acci + ni * pr - nr * pi
                return nr, ni, accr, acci

            carry = lax.fori_loop(0, cj, step, carry)
            if store:
                gb = g_s[...].astype(BF16)
                uv = u_ref[pl.ds(r0, rows), :]
                dyf = dyv.astype(F32)
                du_ref[pl.ds(r0, rows), :] = (_dot(gb, wbt_ref[...], NN) + d_ref[...] * dyf).astype(BF16)
                dwb_ref[...] += _dot(uv, gb, TN)
                hb = h_all[pl.ds(pl.multiple_of(r0 + SUBLANES, SUBLANES), rows), :].astype(BF16)
                dwc_ref[...] += _dot(hb, dyv, TN)
                dd_ref[...] += jnp.sum(dyf * uv.astype(F32), axis=0, keepdims=True)
            return carry

        er, ei, _, _ = lax.fori_loop(0, n_chunks, lambda i, c: bwd_chunk(i, c, False), (zero, zero, zero, zero))
        a0r, a0i = _segment_carry(er, ei, lr, -li, n_steps, segs_per_seq, True)
        dwb_ref[...] = jnp.zeros_like(dwb_ref)
        dwc_ref[...] = jnp.zeros_like(dwc_ref)
        dd_ref[...] = jnp.zeros_like(dd_ref)
        _, _, accr, acci = lax.fori_loop(0, n_chunks, lambda i, c: bwd_chunk(i, c, True), (a0r, a0i, zero, zero))
        dlam_ref[:, :HALF] = jnp.sum(accr, axis=0, keepdims=True)
        dlam_ref[:, HALF:] = jnp.sum(acci, axis=0, keepdims=True)

    col = pl.BlockSpec((T, 128), lambda g: (0, g))
    vec = pl.BlockSpec((1, 128), lambda g: (0, g))

    def per_block(*shape):
        return pl.BlockSpec((None,) + shape, lambda g: (g, 0, 0))

    return pl.pallas_call(
        body, name=name,
        out_shape=(jax.ShapeDtypeStruct((T, C), BF16),
                   jax.ShapeDtypeStruct((nb, 128, 2 * HALF), F32),
                   jax.ShapeDtypeStruct((nb, 2 * HALF, 128), F32),
                   jax.ShapeDtypeStruct((nb, 1, 2 * HALF), F32),
                   jax.ShapeDtypeStruct((1, C), F32)),
        grid=(nb,),
        in_specs=[col, col, per_block(128, 2 * HALF), per_block(2 * HALF, 128), per_block(2 * HALF, 128),
                  per_block(128, 2 * HALF), per_block(1, 2 * HALF), vec, per_block(SUBLANES, 2 * HALF)],
        out_specs=(col, per_block(128, 2 * HALF), per_block(2 * HALF, 128), per_block(1, 2 * HALF), vec),
        scratch_shapes=[pltpu.VMEM((T + SUBLANES, 2 * HALF), F32),
                        pltpu.VMEM((rows, 2 * HALF), F32), pltpu.VMEM((rows, 2 * HALF), F32)],
        compiler_params=_params(("parallel",)),
    )(u_p, dy_p, wb, wbt, wc, wct, lam, dvec, h_in)


def _discretize(a_re, a_im, log_dt, b_re, b_im):
    dt = jnp.exp(log_dt)[:, None]
    mag = jnp.exp(a_re * dt)
    lam_re = mag * jnp.cos(a_im * dt)
    lam_im = mag * jnp.sin(a_im * dt)
    den = a_re * a_re + a_im * a_im
    f_re = ((lam_re - 1.0) * a_re + lam_im * a_im) / den
    f_im = (lam_im * a_re - (lam_re - 1.0) * a_im) / den
    bb_re = f_re[..., None] * b_re - f_im[..., None] * b_im
    bb_im = f_re[..., None] * b_im + f_im[..., None] * b_re
    return lam_re, lam_im, bb_re, bb_im


def _block_diag_in(bb_re, bb_im):
    eye = jnp.eye(GROUPS_PER_BLOCK, dtype=F32)

    def one(bb):
        t = bb.reshape(-1, GROUPS_PER_BLOCK, STATE, GROUP)
        return jnp.einsum('gapi,ab->gaibp', t, eye).reshape(-1, 128, HALF)

    return jnp.concatenate([one(bb_re), one(bb_im)], axis=-1)


def _block_diag_in_grad(dwb):
    eye = jnp.eye(GROUPS_PER_BLOCK, dtype=F32)

    def one(d):
        t = d.reshape(-1, GROUPS_PER_BLOCK, GROUP, GROUPS_PER_BLOCK, STATE)
        return jnp.einsum('gaibp,ab->gapi', t, eye).reshape(-1, STATE, GROUP)

    return one(dwb[..., :HALF]), one(dwb[..., HALF:])


def _block_diag_out(c_re, c_im):
    eye = jnp.eye(GROUPS_PER_BLOCK, dtype=F32)

    def one(cc):
        t = cc.reshape(-1, GROUPS_PER_BLOCK, GROUP, STATE)
        return jnp.einsum('gaip,ab->gbpai', t, eye).reshape(-1, HALF, 128)

    return jnp.concatenate([one(c_re), -one(c_im)], axis=1)


def _block_diag_out_grad(dwc):
    eye = jnp.eye(GROUPS_PER_BLOCK, dtype=F32)

    def one(d):
        t = d.reshape(-1, GROUPS_PER_BLOCK, STATE, GROUPS_PER_BLOCK, GROUP)
        return jnp.einsum('gbpai,ab->gaip', t, eye).reshape(-1, GROUP, STATE)

    return one(dwc[:, :HALF]), -one(dwc[:, HALF:])


def _pack(parts):
    return jnp.concatenate([p.reshape(-1, PACK_W) for p in parts], axis=0)


def _unpack(buf, shapes):
    lead = buf.shape[:-2]
    out, r = [], 0
    for s in shapes:
        n = math.prod(s) // PACK_W
        out.append(buf[..., r:r + n, :].reshape(lead + tuple(s)))
        r += n
    return out


def _permute_rows(a, n_steps):
    T, C = a.shape
    return a.reshape(SUBLANES, n_steps, C).transpose(1, 0, 2).reshape(T, C)


def _unpermute_rows(a, n_steps):
    T, C = a.shape
    return a.reshape(n_steps, SUBLANES, C).transpose(1, 0, 2).reshape(T, C)


def kernel(x, norm_g, attn_w_in, attn_q_g, attn_k_g, attn_w_out, ssm_w_in, ssm_A_re, ssm_A_im, ssm_log_dt, ssm_B_re, ssm_B_im, ssm_C_re, ssm_C_im, ssm_D, ssm_glu_w, ssm_glu_b, ssm_w_out, loss_target, m_norm_g, m_attn_w_in, m_attn_q_g, m_attn_k_g, m_attn_w_out, m_ssm_w_in, m_ssm_A_re, m_ssm_A_im, m_ssm_log_dt, m_ssm_B_re, m_ssm_B_im, m_ssm_C_re, m_ssm_C_im, m_ssm_D, m_ssm_glu_w, m_ssm_glu_b, m_ssm_w_out, v_norm_g, v_attn_w_in, v_attn_q_g, v_attn_k_g, v_attn_w_out, v_ssm_w_in, v_ssm_A_re, v_ssm_A_im, v_ssm_log_dt, v_ssm_B_re, v_ssm_B_im, v_ssm_C_re, v_ssm_C_im, v_ssm_D, v_ssm_glu_w, v_ssm_glu_b, v_ssm_w_out):
    B, S, D = x.shape
    T = B * S
    H = D // HEAD_DIM
    G_loc = ssm_A_re.shape[1]
    G = G_loc * N_DEV
    n_steps = T // SUBLANES
    xf = x.reshape(T, D)
    target = loss_target.reshape(T, D)

    c_in0 = _cast_bf16(attn_w_in[0], "cast_attn_w_in")
    c_out0 = _cast_bf16(attn_w_out[0], "cast_attn_w_out")
    c_in1 = _cast_bf16(ssm_w_in[0], "cast_ssm_w_in")
    c_glu = _cast_bf16(ssm_glu_w[0], "cast_ssm_glu_w")
    c_out1 = _cast_bf16(ssm_w_out[0], "cast_ssm_w_out")
    s_in0 = _exchange_start(c_in0, False, "gather_attn_w_in_start")
    tok0 = s_in0[4]

    small_shapes = [(G_loc, STATE), (G_loc, STATE), (G_loc, STATE, GROUP), (G_loc, STATE, GROUP),
                    (G_loc, GROUP, STATE), (G_loc, GROUP, STATE), (G_loc * GROUP,), (G_loc * GROUP,)]
    disc_in = (ssm_A_re[0], ssm_A_im[0], ssm_log_dt[0], ssm_B_re[0], ssm_B_im[0])
    (lam_re, lam_im, bb_re, bb_im), disc_vjp = jax.vjp(_discretize, *disc_in)
    small = _pack([lam_re, lam_im, bb_re, bb_im, ssm_C_re[0], ssm_C_im[0], ssm_D[0], ssm_glu_b[0]])
    small_all = _exchange(small, False, "gather_small")
    lam_re_a, lam_im_a, bb_re_a, bb_im_a, c_re_a, c_im_a, d_a, glu_b_a = [
        t.reshape((G,) + t.shape[2:]) if t.ndim > 2 else t.reshape(-1)
        for t in _unpack(small_all, small_shapes)]
    wb = _block_diag_in(bb_re_a, bb_im_a)
    wc = _block_diag_out(c_re_a, c_im_a)
    wb_b, wc_b = wb.astype(BF16), wc.astype(BF16)
    wbt_b, wct_b = wb_b.transpose(0, 2, 1), wc_b.transpose(0, 2, 1)
    lam = jnp.concatenate([lam_re_a.reshape(-1, 1, HALF), lam_im_a.reshape(-1, 1, HALF)], axis=-1)
    d_row = d_a.reshape(1, D)
    glu_b_row = glu_b_a.reshape(1, D)
    g0, g1 = norm_g[0:1], norm_g[1:2]
    q_g, k_g = attn_q_g, attn_k_g

    h0 = _rmsnorm_fwd(xf, g0, "norm0", deps=(tok0,))
    w_in0 = _exchange_wait(s_in0, h0, False, "gather_attn_w_in_wait")
    s_out0 = _exchange_start(c_out0, False, "gather_attn_w_out_start", deps=(w_in0,))
    s_in1 = _exchange_start(c_in1, False, "gather_ssm_w_in_start", deps=(w_in0,))
    proj0 = _mm_nn(h0, w_in0, BF16, "attn_in", deps=(s_out0[4], s_in1[4]))
    og, o = _attn_fwd(proj0, q_g, k_g, B, S, H, "attn_fwd")
    w_out0 = _exchange_wait(s_out0, og, False, "gather_attn_w_out_wait").reshape(1, D, D)
    s_glu = _exchange_start(c_glu, False, "gather_ssm_glu_w_start", deps=(w_out0,))
    s_out1 = _exchange_start(c_out1, False, "gather_ssm_w_out_start", deps=(w_out0,))
    x1 = _mm_nn(og, w_out0, F32, "attn_out", residual=xf, deps=(s_glu[4], s_out1[4]))

    h1 = _rmsnorm_fwd(x1, g1, "norm1")
    w_in1 = _exchange_wait(s_in1, h1, False, "gather_ssm_w_in_wait")
    proj1 = _mm_nn(h1, w_in1, BF16, "ssm_in")
    u_p = _permute_rows(proj1[:, :D], n_steps)
    gate1 = proj1[:, D:]
    y_p, h_in = _s5_fwd(u_p, wb_b, wc_b, lam, d_row, B, "s5_fwd")
    y_ssm = _unpermute_rows(y_p, n_steps)
    (yg,) = _ew(lambda a: (_gelu(a)[0],), [y_ssm], [], [BF16], 0, "gelu")
    w_glu = _exchange_wait(s_glu, yg, False, "gather_ssm_glu_w_wait").reshape(1, D, D)
    z = _mm_nn(yg, w_glu, F32, "glu_in", bias=glu_b_row)

    def glu_fwd(y, zz, gt):
        gt = gt.astype(F32)
        return (_gelu(y)[0] * _sigmoid(zz) * (gt * _sigmoid(gt)),)

    (y3,) = _ew(glu_fwd, [y_ssm, z, gate1], [], [BF16], 0, "glu_gate")
    w_out1 = _exchange_wait(s_out1, y3, False, "gather_ssm_w_out_wait").reshape(1, D, D)
    out = _mm_nn(y3, w_out1, F32, "ssm_out", residual=x1)

    dout, dout_b, loss_part = _loss_head(out, target, "loss")
    loss = lax.psum(loss_part[0, 0], ("x", "y", "c"))

    p_w_out1 = _mm_tn(y3, dout_b, 1, BF16, "ssm_out_dw").reshape(N_DEV, D // N_DEV, D)
    sc_out1 = _exchange_start(p_w_out1, True, "scatter_ssm_w_out_start")
    dy3 = _mm_nt(dout_b, w_out1, F32, "ssm_out_dx", deps=(sc_out1[4],))

    def glu_bwd(d3, y, zz, gt):
        gt = gt.astype(F32)
        sg = _sigmoid(gt)
        sz = _sigmoid(zz)
        ygv, _ = _gelu(y)
        dy2 = d3 * (gt * sg)
        dgate = d3 * (ygv * sz) * (sg * (1.0 + gt * (1.0 - sg)))
        dz = dy2 * ygv * (sz * (1.0 - sz))
        return dz, dgate, dy2 * sz, jnp.sum(dz, axis=0, keepdims=True)

    dz_b, dgate1, dyg_a, dglu_b = _ew(glu_bwd, [dy3, y_ssm, z, gate1], [], [BF16, BF16, F32], 1, "glu_gate_bwd")
    p_w_glu = _mm_tn(yg, dz_b, 1, BF16, "glu_in_dw").reshape(N_DEV, D // N_DEV, D)
    sc_glu = _exchange_start(p_w_glu, True, "scatter_ssm_glu_w_start")
    dyg_b = _mm_nt(dz_b, w_glu, F32, "glu_in_dx", deps=(sc_glu[4],))

    def gelu_bwd(da, db, y):
        _, t = _gelu(y)
        dg = 0.5 * (1.0 + t) + 0.5 * y * (1.0 - t * t) * (GELU_C * (1.0 + 3.0 * 0.044715 * (y * y)))
        return ((da + db) * dg,)

    (dy_ssm,) = _ew(gelu_bwd, [dyg_a, dyg_b, y_ssm], [], [BF16], 0, "gelu_bwd")
    dy_p = _permute_rows(dy_ssm, n_steps)
    du_p, dwb, dwc, dlam, dd = _s5_bwd(u_p, dy_p, wb_b, wbt_b, wc_b, wct_b, lam, d_row, h_in, B, "s5_bwd")
    du = _unpermute_rows(du_p, n_steps)
    dproj1 = jnp.concatenate([du, dgate1], axis=1)
    p_w_in1 = _mm_tn(h1, dproj1, N_DEV, BF16, "ssm_in_dw")
    sc_in1 = _exchange_start(p_w_in1, True, "scatter_ssm_w_in_start")
    dh1 = _mm_nt(dproj1, w_in1, F32, "ssm_in_dx", deps=(sc_in1[4],))
    dx1, dx1_b, dg1 = _rmsnorm_bwd(x1, dh1, dout, g1, "norm1_bwd")

    p_w_out0 = _mm_tn(og, dx1_b, 1, BF16, "attn_out_dw").reshape(N_DEV, D // N_DEV, D)
    sc_out0 = _exchange_start(p_w_out0, True, "scatter_attn_w_out_start")
    dog = _mm_nt(dx1_b, w_out0, BF16, "attn_out_dx", deps=(sc_out0[4],))
    dq, dk, dv, dgate0, dqg, dkg = _attn_bwd(proj0, o, dog, q_g, k_g, B, S, H, "attn_bwd")
    dproj0 = jnp.concatenate([dq, dk, dv, dgate0], axis=1)
    p_w_in0 = _mm_tn(h0, dproj0, N_DEV, BF16, "attn_in_dw")
    sc_in0 = _exchange_start(p_w_in0, True, "scatter_attn_w_in_start")
    dh0 = _mm_nt(dproj0, w_in0, F32, "attn_in_dx", deps=(sc_in0[4],))
    dx, _, dg0 = _rmsnorm_bwd(xf, dh0, dx1, g0, "norm0_bwd")

    def update(started, after, w, m, v, name):
        recv = _exchange_wait(started, after, True, "scatter_" + name + "_wait")
        return _adamw(recv, w[0], m[0], v[0], "adamw_" + name)

    r_ssm_w_out = update(sc_out1, dx, ssm_w_out, m_ssm_w_out, v_ssm_w_out, "ssm_w_out")
    r_ssm_glu_w = update(sc_glu, r_ssm_w_out[0], ssm_glu_w, m_ssm_glu_w, v_ssm_glu_w, "ssm_glu_w")
    r_ssm_w_in = update(sc_in1, r_ssm_glu_w[0], ssm_w_in, m_ssm_w_in, v_ssm_w_in, "ssm_w_in")
    r_attn_w_out = update(sc_out0, r_ssm_w_in[0], attn_w_out, m_attn_w_out, v_attn_w_out, "attn_w_out")

    dbb_re, dbb_im = _block_diag_in_grad(dwb)
    dc_re, dc_im = _block_diag_out_grad(dwc)
    dlam_re = dlam[:, 0, :HALF].reshape(G, STATE)
    dlam_im = dlam[:, 0, HALF:].reshape(G, STATE)
    by_owner = [t.reshape((N_DEV, -1)) for t in (dlam_re, dlam_im, dbb_re, dbb_im, dc_re, dc_im, dd, dglu_b)]
    small_parts = jnp.concatenate([t.reshape(N_DEV, -1, PACK_W) for t in by_owner], axis=1)
    small_sum = _sum_parts(_exchange(small_parts, True, "scatter_small"), "sum_small")
    s_lam_re, s_lam_im, s_bb_re, s_bb_im, s_c_re, s_c_im, s_d, s_glu_b = _unpack(small_sum, small_shapes)
    g_a_re, g_a_im, g_log_dt, g_b_re, g_b_im = disc_vjp((s_lam_re, s_lam_im, s_bb_re, s_bb_im))

    local_names = ["ssm_A_re", "ssm_A_im", "ssm_log_dt", "ssm_B_re", "ssm_B_im", "ssm_C_re", "ssm_C_im",
                   "ssm_D", "ssm_glu_b"]
    local_g = [g_a_re, g_a_im, g_log_dt, g_b_re, g_b_im, s_c_re, s_c_im, s_d, s_glu_b]
    local_w = [ssm_A_re, ssm_A_im, ssm_log_dt, ssm_B_re, ssm_B_im, ssm_C_re, ssm_C_im, ssm_D, ssm_glu_b]
    local_m = [m_ssm_A_re, m_ssm_A_im, m_ssm_log_dt, m_ssm_B_re, m_ssm_B_im, m_ssm_C_re, m_ssm_C_im,
               m_ssm_D, m_ssm_glu_b]
    local_v = [v_ssm_A_re, v_ssm_A_im, v_ssm_log_dt, v_ssm_B_re, v_ssm_B_im, v_ssm_C_re, v_ssm_C_im,
               v_ssm_D, v_ssm_glu_b]
    r_local = _adamw_small(local_g, local_w, local_m, local_v, None, "adamw_small")

    rep_g = [jnp.concatenate([dg0, dg1], axis=0), dqg, dkg]
    rep_w = [norm_g, attn_q_g, attn_k_g]
    rep_m = [m_norm_g, m_attn_q_g, m_attn_k_g]
    rep_v = [v_norm_g, v_attn_q_g, v_attn_k_g]
    r_rep = _adamw_small(rep_g, rep_w, rep_m, rep_v, "gather_rep", "adamw_rep")

    r_attn_w_in = update(sc_in0, r_rep[0][0], attn_w_in, m_attn_w_in, v_attn_w_in, "attn_w_in")

    res = {"attn_w_in": r_attn_w_in, "attn_w_out": r_attn_w_out, "ssm_w_in": r_ssm_w_in,
           "ssm_glu_w": r_ssm_glu_w, "ssm_w_out": r_ssm_w_out}
    ref_w = {"attn_w_in": attn_w_in, "attn_w_out": attn_w_out, "ssm_w_in": ssm_w_in,
             "ssm_glu_w": ssm_glu_w, "ssm_w_out": ssm_w_out}
    for name, r, w in zip(local_names, r_local, local_w):
        res[name], ref_w[name] = r, w
    for name, r, w in zip(["norm_g", "attn_q_g", "attn_k_g"], r_rep, rep_w):
        res[name], ref_w[name] = r, w
    order = ["norm_g", "attn_w_in", "attn_q_g", "attn_k_g", "attn_w_out", "ssm_w_in", "ssm_A_re", "ssm_A_im",
             "ssm_log_dt", "ssm_B_re", "ssm_B_im", "ssm_C_re", "ssm_C_im", "ssm_D", "ssm_glu_w", "ssm_glu_b",
             "ssm_w_out"]
    outs = [loss, dx.reshape(B, S, D)]
    for kind in range(4):
        outs += [res[n][kind].reshape(ref_w[n].shape) for n in order]
    return tuple(outs)


def _adamw_small(grads, ws, ms, vs, gather_name, name):
    sizes = [math.prod(w.shape) for w in ws]
    total = sum(sizes)
    rows = -(-total // (PACK_W * 8)) * 8
    if rows > 256:
        rows = -(-rows // 256) * 256

    def pack(ts, fill):
        flat = jnp.concatenate([t.reshape(-1).astype(F32) for t in ts])
        flat = jnp.concatenate([flat, jnp.full((rows * PACK_W - total,), fill, F32)])
        return flat.reshape(rows, PACK_W)

    g = pack(grads, 0.0)
    parts = _exchange(g, False, gather_name) if gather_name else g[None]
    res = _adamw(parts, pack(ws, 0.0), pack(ms, 0.0), pack(vs, 1.0), name)
    outs = []
    off = 0
    flats = [r.reshape(-1) for r in res]
    for n in sizes:
        outs.append(tuple(f[off:off + n] for f in flats))
        off += n
    return outs
```

```python
import functools
import math

import jax
import jax.numpy as jnp
from jax import lax
from jax.experimental import pallas as pl
from jax.experimental.pallas import tpu as pltpu

F32 = jnp.float32
BF16 = jnp.bfloat16

N_DEV = 8
HEAD_DIM = 128
GROUP = 16
STATE = 64
GROUPS_PER_BLOCK = 8
SUBLANES = 8
RMS_EPS = 1e-6
ADAM_LR, ADAM_B1, ADAM_B2, ADAM_EPS, ADAM_WD, ADAM_STEP = 0.001, 0.9, 0.999, 1e-08, 0.01, 10
VMEM_LIMIT = 56 * 1024 * 1024
GELU_C = math.sqrt(2.0 / math.pi)
PACK_W = 128


def _params(sem, **kw):
    return pltpu.CompilerParams(dimension_semantics=sem, vmem_limit_bytes=VMEM_LIMIT, **kw)


def _tile(n, t):
    t = min(n, t)
    assert n % t == 0, (n, t)
    return t


def _ones_where(cond):
    return jnp.where(cond, 1.0, 0.0).astype(BF16)


def _sigmoid(x):
    return 1.0 / (1.0 + jnp.exp(-x))


def _log_sigmoid(z):
    return jnp.minimum(z, 0.0) - jnp.log(1.0 + jnp.exp(-jnp.abs(z)))


def _dot(a, b, dims):
    return lax.dot_general(a, b, (dims, ((), ())), preferred_element_type=F32)


NN = ((1,), (0,))
NT = ((1,), (1,))
TN = ((0,), (0,))


def _exchange(x, scatter, name):
    shape = x.shape[1:] if scatter else x.shape

    def body(x_ref, out_ref, send_sems, recv_sems, local_sem):
        ix, iy, ic = lax.axis_index("x"), lax.axis_index("y"), lax.axis_index("c")
        me = 4 * ix + 2 * iy + ic

        def peer(k):
            kx, ky, kc = (k >> 2) & 1, (k >> 1) & 1, k & 1
            px, py, pc = ix ^ kx, iy ^ ky, ic ^ kc
            return (px, py, pc), 4 * px + 2 * py + pc

        mine = pltpu.make_async_copy(x_ref.at[me] if scatter else x_ref, out_ref.at[me], local_sem)
        mine.start()
        copies = []
        for k in range(1, N_DEV):
            pid, pidx = peer(k)
            cp = pltpu.make_async_remote_copy(
                src_ref=x_ref.at[pidx] if scatter else x_ref,
                dst_ref=out_ref.at[me],
                send_sem=send_sems.at[k - 1], recv_sem=recv_sems.at[k - 1],
                device_id=pid, device_id_type=pl.DeviceIdType.MESH)
            cp.start()
            copies.append(cp)
        for k in range(1, N_DEV):
            pid, pidx = peer(k)
            pltpu.make_async_remote_copy(
                src_ref=x_ref.at[pidx] if scatter else x_ref,
                dst_ref=out_ref.at[pidx],
                send_sem=send_sems.at[k - 1], recv_sem=recv_sems.at[k - 1],
                device_id=pid, device_id_type=pl.DeviceIdType.MESH).wait_recv()
        for cp in copies:
            cp.wait_send()
        mine.wait()

    return pl.pallas_call(
        body, name=name,
        out_shape=jax.ShapeDtypeStruct((N_DEV,) + tuple(shape), x.dtype),
        in_specs=[pl.BlockSpec(memory_space=pl.ANY)],
        out_specs=pl.BlockSpec(memory_space=pl.ANY),
        scratch_shapes=[pltpu.SemaphoreType.DMA((N_DEV - 1,)), pltpu.SemaphoreType.DMA((N_DEV - 1,)),
                        pltpu.SemaphoreType.DMA],
    )(x)


_HBM = pl.BlockSpec(memory_space=pltpu.HBM)
_SEM = pl.BlockSpec(memory_space=pltpu.SEMAPHORE)
_ANY = pl.BlockSpec(memory_space=pl.ANY)
_EFFECT = pltpu.SideEffectType.DATAFLOW_SIDE_EFFECTING


def _peer(k):
    ix, iy, ic = lax.axis_index("x"), lax.axis_index("y"), lax.axis_index("c")
    px, py, pc = ix ^ ((k >> 2) & 1), iy ^ ((k >> 1) & 1), ic ^ (k & 1)
    return (px, py, pc), 4 * px + 2 * py + pc


def _my_index():
    return 4 * lax.axis_index("x") + 2 * lax.axis_index("y") + lax.axis_index("c")


def _place_mine(src, dtype, scatter, name, tr=256, tc=1024):
    R, C = src.shape[-2:]
    tr, tc = _tile(R, tr), _tile(C, tc)
    me = _my_index().astype(jnp.int32).reshape(1)

    def body(me_ref, src_ref, land_ref, out_ref):
        out_ref[...] = src_ref[...].astype(dtype)

    if scatter:
        src_spec = pl.BlockSpec((None, tr, tc), lambda r, c, me_ref: (me_ref[0], r, c))
    else:
        src_spec = pl.BlockSpec((tr, tc), lambda r, c, me_ref: (r, c))
    land = lax.empty((N_DEV, R, C), dtype)
    return pl.pallas_call(
        body, name=name,
        out_shape=jax.ShapeDtypeStruct(land.shape, dtype),
        grid_spec=pltpu.PrefetchScalarGridSpec(
            num_scalar_prefetch=1, grid=(R // tr, C // tc),
            in_specs=[src_spec, _ANY],
            out_specs=pl.BlockSpec((None, tr, tc), lambda r, c, me_ref: (me_ref[0], r, c))),
        input_output_aliases={2: 0},
        compiler_params=_params(("parallel", "parallel")),
    )(me, src, land)


def _gather_two_level(land, name):
    def body(land_in, land_ref, send_sems, recv_sems):
        x, y, c = lax.axis_index("x"), lax.axis_index("y"), lax.axis_index("c")
        me, sibling = (x, y, c), (x, y, 1 - c)
        chips = [(1 - x, y), (x, 1 - y), (1 - x, 1 - y)]

        def block(px, py, pc):
            return land_ref.at[4 * px + 2 * py + pc]

        def copy(k, blk, to):
            return pltpu.make_async_remote_copy(
                src_ref=block(*blk), dst_ref=block(*blk), send_sem=send_sems.at[k], recv_sem=recv_sems.at[k],
                device_id=to, device_id_type=pl.DeviceIdType.MESH)

        first = [copy(0, me, sibling)] + [copy(1 + j, me, (*chip, c)) for j, chip in enumerate(chips)]
        for cp in first:
            cp.start()
        passed = [copy(4 + j, (*chip, c), sibling) for j, chip in enumerate(chips)]
        for j, chip in enumerate(chips):
            copy(1 + j, (*chip, c), me).wait_recv()
            passed[j].start()
        copy(0, sibling, me).wait_recv()
        for j, chip in enumerate(chips):
            copy(4 + j, (*chip, 1 - c), me).wait_recv()
        for cp in first + passed:
            cp.wait_send()

    return pl.pallas_call(
        body, name=name,
        out_shape=jax.ShapeDtypeStruct(land.shape, land.dtype),
        in_specs=[_ANY], out_specs=_ANY,
        input_output_aliases={0: 0},
        scratch_shapes=[pltpu.SemaphoreType.DMA((N_DEV - 1,)), pltpu.SemaphoreType.DMA((N_DEV - 1,))],
    )(land)


def _exchange_start(land, x, name, deps=()):
    bufs = [land] if x is None else [land, x]
    nb = len(bufs)

    def body(*refs):
        land_ref = refs[0]
        send_sems, recv_sems = refs[nb + len(deps):nb + len(deps) + 2]
        token = refs[2 * nb + len(deps) + 2]
        me = _my_index()
        for k in range(1, N_DEV):
            pid, pidx = _peer(k)
            pltpu.make_async_remote_copy(
                src_ref=land_ref.at[me] if x is None else refs[1].at[pidx], dst_ref=land_ref.at[me],
                send_sem=send_sems.at[k - 1], recv_sem=recv_sems.at[k - 1],
                device_id=pid, device_id_type=pl.DeviceIdType.MESH).start()
        token[...] = jnp.zeros_like(token)

    out = pl.pallas_call(
        body, name=name,
        out_shape=(pltpu.SemaphoreType.DMA((N_DEV - 1,)), pltpu.SemaphoreType.DMA((N_DEV - 1,)))
        + tuple(pltpu.HBM(t.shape, t.dtype) for t in bufs) + (jax.ShapeDtypeStruct((8, 128), F32),),
        in_specs=(_HBM,) * nb + (_ANY,) * len(deps),
        out_specs=(_SEM, _SEM) + (_HBM,) * nb + (pl.BlockSpec(memory_space=pltpu.VMEM),),
        input_output_aliases={i: 2 + i for i in range(nb)},
        compiler_params=pltpu.CompilerParams(has_side_effects=_EFFECT),
    )(*[pltpu.with_memory_space_constraint(t, pltpu.HBM) for t in bufs], *deps)
    return out[0], out[1], out[2:2 + nb], out[2 + nb]


def _exchange_wait(started, after, name):
    send_sems, recv_sems, bufs, _ = started
    nb = len(bufs)

    def body(*refs):
        land_ref = refs[0]
        send_sems, recv_sems = refs[nb:nb + 2]
        me = _my_index()
        for k in range(1, N_DEV):
            pid, pidx = _peer(k)
            cp = pltpu.make_async_remote_copy(
                src_ref=land_ref.at[me] if nb == 1 else refs[1].at[pidx], dst_ref=land_ref.at[pidx],
                send_sem=send_sems.at[k - 1], recv_sem=recv_sems.at[k - 1],
                device_id=pid, device_id_type=pl.DeviceIdType.MESH)
            cp.wait_send()
            cp.wait_recv()

    return pl.pallas_call(
        body, name=name,
        out_shape=tuple(pltpu.HBM(t.shape, t.dtype) for t in bufs),
        in_specs=(_HBM,) * nb + (_SEM, _SEM, _ANY),
        out_specs=(_HBM,) * nb,
        input_output_aliases={i: i for i in range(nb)},
        compiler_params=pltpu.CompilerParams(has_side_effects=_EFFECT),
    )(*bufs, send_sems, recv_sems, after)[0]


def _accumulate(acc, part, step, n_steps, finish):
    if n_steps == 1:
        finish(part)
        return

    @pl.when(step == 0)
    def _():
        acc[...] = part

    @pl.when((step > 0) & (step < n_steps - 1))
    def _():
        acc[...] += part

    @pl.when(step == n_steps - 1)
    def _():
        finish(acc[...] + part)


def _mm_nn(a, b, out_dtype, name, bias=None, residual=None, deps=(), tm=1024, tn=1024, tk=2048):
    M, K = a.shape
    J, K2, Nj = b.shape
    assert K == K2
    tm, tn, tk = _tile(M, tm), _tile(Nj, tn), _tile(K, tk)
    nb, nk = Nj // tn, K // tk

    def body(*refs):
        a_ref, b_ref = refs[0], refs[1]
        i = 2
        bias_ref = res_ref = None
        if bias is not None:
            bias_ref = refs[i]; i += 1
        if residual is not None:
            res_ref = refs[i]; i += 1
        i += len(deps)
        o_ref, acc = refs[i], refs[i + 1]

        def finish(r):
            if bias_ref is not None:
                r = r + bias_ref[...]
            if res_ref is not None:
                r = r + res_ref[...]
            o_ref[...] = r.astype(out_dtype)

        _accumulate(acc, _dot(a_ref[...], b_ref[...], NN), pl.program_id(3), nk, finish)

    in_specs = [pl.BlockSpec((tm, tk), lambda j, m, n, k: (m, k)),
                pl.BlockSpec((None, tk, tn), lambda j, m, n, k: (j, k, n))]
    args = [a, b]
    if bias is not None:
        in_specs.append(pl.BlockSpec((1, tn), lambda j, m, n, k: (0, j * nb + n)))
        args.append(bias)
    if residual is not None:
        in_specs.append(pl.BlockSpec((tm, tn), lambda j, m, n, k: (m, j * nb + n)))
        args.append(residual)
    in_specs += [_ANY] * len(deps)
    args += list(deps)
    return pl.pallas_call(
        body, name=name,
        out_shape=jax.ShapeDtypeStruct((M, J * Nj), out_dtype),
        grid=(J, M // tm, nb, nk),
        in_specs=in_specs,
        out_specs=pl.BlockSpec((tm, tn), lambda j, m, n, k: (m, j * nb + n)),
        scratch_shapes=[pltpu.VMEM((tm, tn), F32)],
        compiler_params=_params(("parallel", "parallel", "parallel", "arbitrary")),
    )(*args)


def _mm_nt(a, b, out_dtype, name, deps=(), tm=1024, tp=1024, tq=2048):
    M, Q = a.shape
    J, P, Qj = b.shape
    assert Q == J * Qj
    tm, tp, tq = _tile(M, tm), _tile(P, tp), _tile(Qj, tq)
    nq = Qj // tq

    def body(*refs):
        a_ref, b_ref = refs[:2]
        o_ref, acc = refs[2 + len(deps):]

        def finish(r):
            o_ref[...] = r.astype(out_dtype)

        _accumulate(acc, _dot(a_ref[...], b_ref[...], NT), pl.program_id(2) * nq + pl.program_id(3), J * nq, finish)

    return pl.pallas_call(
        body, name=name,
        out_shape=jax.ShapeDtypeStruct((M, P), out_dtype),
        grid=(M // tm, P // tp, J, nq),
        in_specs=[pl.BlockSpec((tm, tq), lambda m, p, j, q: (m, j * nq + q)),
                  pl.BlockSpec((None, tp, tq), lambda m, p, j, q: (j, p, q))] + [_ANY] * len(deps),
        out_specs=pl.BlockSpec((tm, tp), lambda m, p, j, q: (m, p)),
        scratch_shapes=[pltpu.VMEM((tm, tp), F32)],
        compiler_params=_params(("parallel", "parallel", "arbitrary", "arbitrary")),
    )(a, b, *deps)


def _mm_tn(a, b, J, out_dtype, name, tm=1024, tn=1024, tr=2048):
    R, M = a.shape
    R2, N = b.shape
    assert R == R2 and N % J == 0
    Nj = N // J
    tm, tn, tr = _tile(M, tm), _tile(Nj, tn), _tile(R, tr)
    nb, nr = Nj // tn, R // tr

    def body(a_ref, b_ref, o_ref, acc):
        def finish(r):
            o_ref[...] = r.astype(out_dtype)

        _accumulate(acc, _dot(a_ref[...], b_ref[...], TN), pl.program_id(3), nr, finish)

    return pl.pallas_call(
        body, name=name,
        out_shape=jax.ShapeDtypeStruct((J, M, Nj), out_dtype),
        grid=(J, M // tm, nb, nr),
        in_specs=[pl.BlockSpec((tr, tm), lambda j, m, n, r: (r, m)),
                  pl.BlockSpec((tr, tn), lambda j, m, n, r: (r, j * nb + n))],
        out_specs=pl.BlockSpec((None, tm, tn), lambda j, m, n, r: (j, m, n)),
        scratch_shapes=[pltpu.VMEM((tm, tn), F32)],
        compiler_params=_params(("parallel", "parallel", "parallel", "arbitrary")),
    )(a, b)


def _ew(fn, ins, vecs, out_dtypes, n_acc, name, tr=256, tc=1024):
    T, C = ins[0].shape
    tr, tc = _tile(T, tr), _tile(C, tc)
    n_in, n_vec, n_out = len(ins), len(vecs), len(out_dtypes)

    def body(*refs):
        in_refs = refs[:n_in + n_vec]
        out_refs = refs[n_in + n_vec:n_in + n_vec + n_out]
        acc_refs = refs[n_in + n_vec + n_out:]
        res = fn(*[r[...] for r in in_refs])
        for o_ref, v in zip(out_refs, res[:n_out]):
            o_ref[...] = v.astype(o_ref.dtype)
        if n_acc:
            r = pl.program_id(1)

            @pl.when(r == 0)
            def _():
                for a_ref in acc_refs:
                    a_ref[...] = jnp.zeros_like(a_ref)

            for a_ref, v in zip(acc_refs, res[n_out:]):
                a_ref[...] += v

    blk = pl.BlockSpec((tr, tc), lambda c, r: (r, c))
    vec = pl.BlockSpec((1, tc), lambda c, r: (0, c))
    out = pl.pallas_call(
        body, name=name,
        out_shape=tuple([jax.ShapeDtypeStruct((T, C), d) for d in out_dtypes]
                        + [jax.ShapeDtypeStruct((1, C), F32)] * n_acc),
        grid=(C // tc, T // tr),
        in_specs=[blk] * n_in + [vec] * n_vec,
        out_specs=tuple([blk] * n_out + [vec] * n_acc),
        compiler_params=_params(("parallel", "arbitrary")),
    )(*ins, *vecs)
    return out


def _cast_bf16(w, name):
    return _ew(lambda a: (a,), [w], [], [BF16], 0, name)[0]


def _rmsnorm_fwd(x, g, name, deps=(), tr=128):
    T, D = x.shape
    tr = _tile(T, tr)

    def body(*refs):
        x_ref, g_ref, h_ref = refs[0], refs[1], refs[2 + len(deps)]
        xv = x_ref[...]
        r = lax.rsqrt(jnp.mean(xv * xv, axis=-1, keepdims=True) + RMS_EPS)
        h_ref[...] = ((xv * r) * g_ref[...]).astype(BF16)

    return pl.pallas_call(
        body, name=name,
        out_shape=jax.ShapeDtypeStruct((T, D), BF16),
        grid=(T // tr,),
        in_specs=[pl.BlockSpec((tr, D), lambda i: (i, 0)), pl.BlockSpec((1, D), lambda i: (0, 0))] + [_ANY] * len(deps),
        out_specs=pl.BlockSpec((tr, D), lambda i: (i, 0)),
        compiler_params=_params(("parallel",)),
    )(x, g, *deps)


def _rmsnorm_bwd(x, dh, dres, g, name, tr=128):
    T, D = x.shape
    tr = _tile(T, tr)

    def body(x_ref, dh_ref, dres_ref, g_ref, dx_ref, dxb_ref, dg_ref):
        xv = x_ref[...]
        r = lax.rsqrt(jnp.mean(xv * xv, axis=-1, keepdims=True) + RMS_EPS)
        xn = xv * r
        dhv = dh_ref[...].astype(F32)
        dxn = dhv * g_ref[...]
        dx = dres_ref[...] + r * (dxn - xn * jnp.mean(dxn * xn, axis=-1, keepdims=True))
        dx_ref[...] = dx
        dxb_ref[...] = dx.astype(BF16)

        @pl.when(pl.program_id(0) == 0)
        def _():
            dg_ref[...] = jnp.zeros_like(dg_ref)

        dg_ref[...] += jnp.sum(dhv * xn, axis=0, keepdims=True)

    blk = pl.BlockSpec((tr, D), lambda i: (i, 0))
    vec = pl.BlockSpec((1, D), lambda i: (0, 0))
    return pl.pallas_call(
        body, name=name,
        out_shape=(jax.ShapeDtypeStruct((T, D), F32), jax.ShapeDtypeStruct((T, D), BF16),
                   jax.ShapeDtypeStruct((1, D), F32)),
        grid=(T // tr,),
        in_specs=[blk, blk, blk, vec],
        out_specs=(blk, blk, vec),
        compiler_params=_params(("arbitrary",)),
    )(x, dh, dres, g)


def _loss_head(y, target, name, tr=128):
    T, D = y.shape
    tr = _tile(T, tr)
    n = T // tr

    def body(y_ref, t_ref, dy_ref, dyb_ref, loss_ref, acc):
        i = pl.program_id(0)

        @pl.when(i == 0)
        def _():
            acc[...] = jnp.zeros_like(acc)

        err = y_ref[...] - t_ref[...]
        dy = err * (1.0 / D)
        dy_ref[...] = dy
        dyb_ref[...] = dy.astype(BF16)
        acc[...] += jnp.sum(err * err, axis=0, keepdims=True)

        @pl.when(i == n - 1)
        def _():
            loss_ref[...] = jnp.sum(acc[...], axis=1, keepdims=True) * (0.5 / D)

    blk = pl.BlockSpec((tr, D), lambda i: (i, 0))
    return pl.pallas_call(
        body, name=name,
        out_shape=(jax.ShapeDtypeStruct((T, D), F32), jax.ShapeDtypeStruct((T, D), BF16),
                   jax.ShapeDtypeStruct((1, 1), F32)),
        grid=(n,),
        in_specs=[blk, blk],
        out_specs=(blk, blk, pl.BlockSpec((1, 1), lambda i: (0, 0))),
        scratch_shapes=[pltpu.VMEM((1, D), F32)],
        compiler_params=_params(("arbitrary",)),
    )(y, target)


def _gelu(x):
    t = jnp.tanh(GELU_C * (x + 0.044715 * (x * x * x)))
    return x * (0.5 * (1.0 + t)), t


def _adamw(parts, w, m, v, name, tr=256, tc=1024):
    n, R, C = parts.shape
    tr, tc = _tile(R, tr), _tile(C, tc)
    c1 = 1.0 - ADAM_B1 ** ADAM_STEP
    c2 = 1.0 - ADAM_B2 ** ADAM_STEP

    def body(p_ref, w_ref, m_ref, v_ref, g_out, d_out, m_out, v_out):
        g = p_ref[0].astype(F32)
        for k in range(1, n):
            g = g + p_ref[k].astype(F32)
        mn = ADAM_B1 * m_ref[...] + (1.0 - ADAM_B1) * g
        vn = ADAM_B2 * v_ref[...] + (1.0 - ADAM_B2) * (g * g)
        m_hat = mn / c1
        v_hat = vn / c2
        g_out[...] = g
        d_out[...] = -ADAM_LR * (m_hat / (jnp.sqrt(v_hat) + ADAM_EPS) + ADAM_WD * w_ref[...])
        m_out[...] = mn
        v_out[...] = vn

    blk = pl.BlockSpec((tr, tc), lambda r, c: (r, c))
    return pl.pallas_call(
        body, name=name,
        out_shape=tuple([jax.ShapeDtypeStruct((R, C), F32)] * 4),
        grid=(R // tr, C // tc),
        in_specs=[pl.BlockSpec((n, tr, tc), lambda r, c: (0, r, c)), blk, blk, blk],
        out_specs=(blk, blk, blk, blk),
        compiler_params=_params(("parallel", "parallel")),
    )(parts, w, m, v)


def _sum_parts(parts, name):
    n, R, C = parts.shape

    def body(p_ref, o_ref):
        g = p_ref[0].astype(F32)
        for k in range(1, n):
            g = g + p_ref[k].astype(F32)
        o_ref[...] = g

    return pl.pallas_call(
        body, name=name,
        out_shape=jax.ShapeDtypeStruct((R, C), F32),
        compiler_params=pltpu.CompilerParams(vmem_limit_bytes=VMEM_LIMIT),
    )(parts)


def _head_norm(xv):
    xv = xv.astype(F32)
    r = lax.rsqrt(jnp.mean(xv * xv, axis=-1, keepdims=True) + RMS_EPS)
    return xv * r, r


def _split_bf16(x):
    hi = x.astype(BF16)
    lo = (x - hi.astype(F32)).astype(BF16)
    return hi, lo


FWD_HEADS = 4
BWD_HEADS = 2


def _attn_specs(S, H, HP):
    def spec(part):
        return pl.BlockSpec((S, HP * HEAD_DIM), lambda b, h: (b, part * (H // HP) + h))
    return spec


def _lanes(hh):
    return slice(hh * HEAD_DIM, (hh + 1) * HEAD_DIM)


def _attn_fwd(proj, q_g, k_g, B, S, H, name):
    TQ = _tile(S, 256)
    nq = S // TQ
    scale = 1.0 / math.sqrt(HEAD_DIM)
    HP = FWD_HEADS
    ATT_W = HP * HEAD_DIM
    heads = range(HP)

    def body(q_ref, k_ref, v_ref, gate_ref, qg_ref, kg_ref, og_ref, o_ref, qn_s, kn_s):
        for hh in heads:
            qn_s[:, _lanes(hh)] = (_head_norm(q_ref[:, _lanes(hh)])[0] * qg_ref[...]).astype(BF16)
            kn_s[:, _lanes(hh)] = (_head_norm(k_ref[:, _lanes(hh)])[0] * kg_ref[...]).astype(BF16)
        row = lax.broadcasted_iota(jnp.int32, (TQ, TQ), 0)
        col = lax.broadcasted_iota(jnp.int32, (TQ, TQ), 1)
        later = _ones_where(row > col)
        causal = col < row

        def q_block(qi, _):
            q0 = pl.multiple_of(qi * TQ, TQ)

            def both(ki, state, diag):
                k0 = pl.multiple_of(ki * TQ, TQ)
                z = [_dot(qn_s[pl.ds(q0, TQ), _lanes(hh)], kn_s[pl.ds(k0, TQ), _lanes(hh)], NT) * scale
                     for hh in heads]
                ls = [_log_sigmoid(zz) for zz in z]
                l1m = [a - zz for a, zz in zip(ls, z)]
                if diag:
                    l1m = [jnp.where(causal, a, 0.0) for a in l1m]
                parts = [_split_bf16(a) for a in l1m]
                suffix = [_dot(hi, later, NN) + _dot(lo, later, NN) + state[hh][0]
                          for hh, (hi, lo) in zip(heads, parts)]
                w = [jnp.exp(a + sfx) for a, sfx in zip(ls, suffix)]
                if diag:
                    w = [jnp.where(causal, a, 0.0) for a in w]
                acc = [state[hh][1] + _dot(w[hh].astype(BF16), v_ref[pl.ds(k0, TQ), _lanes(hh)], NN)
                       for hh in heads]
                return tuple((state[hh][0] + jnp.sum(l1m[hh], axis=1, keepdims=True), acc[hh]) for hh in heads)

            zero = (jnp.zeros((TQ, 1), F32), jnp.zeros((TQ, HEAD_DIM), F32))
            state = both(qi, (zero,) * HP, True)
            state = lax.fori_loop(0, qi, lambda i, st: both(qi - 1 - i, st, False), state)
            for hh in heads:
                acc = state[hh][1]
                o_ref[pl.ds(q0, TQ), _lanes(hh)] = acc.astype(BF16)
                gate = gate_ref[pl.ds(q0, TQ), _lanes(hh)].astype(F32)
                og_ref[pl.ds(q0, TQ), _lanes(hh)] = (acc * (gate * _sigmoid(gate))).astype(BF16)
            return 0

        lax.fori_loop(0, nq, q_block, 0)

    spec = _attn_specs(S, H, HP)
    vec = pl.BlockSpec((1, HEAD_DIM), lambda b, h: (0, 0))
    out = pl.BlockSpec((S, ATT_W), lambda b, h: (b, h))
    return pl.pallas_call(
        body, name=name,
        out_shape=(jax.ShapeDtypeStruct((B * S, H * HEAD_DIM), BF16),) * 2,
        grid=(B, H // HP),
        in_specs=[spec(0), spec(1), spec(2), spec(3), vec, vec],
        out_specs=(out, out),
        scratch_shapes=[pltpu.VMEM((S, ATT_W), BF16)] * 2,
        compiler_params=_params(("parallel", "parallel")),
    )(proj, proj, proj, proj, q_g, k_g)


def _attn_bwd(proj, o, dog, q_g, k_g, B, S, H, name):
    TQ = _tile(S, 256)
    nq = S // TQ
    scale = 1.0 / math.sqrt(HEAD_DIM)
    HP = BWD_HEADS
    ATT_W = HP * HEAD_DIM
    heads = range(HP)

    def body(q_ref, k_ref, v_ref, gate_ref, o_ref, dog_ref, qg_ref, kg_ref,
             dq_ref, dk_ref, dv_ref, dgate_ref, dqg_ref, dkg_ref,
             qn_s, kn_s, do_s, dqn_s, dkn_s, dv_s, w_s, sg_s):
        first = (pl.program_id(0) == 0) & (pl.program_id(1) == 0)

        @pl.when(first)
        def _():
            dqg_ref[...] = jnp.zeros_like(dqg_ref)
            dkg_ref[...] = jnp.zeros_like(dkg_ref)

        for hh in heads:
            qn_s[:, _lanes(hh)] = (_head_norm(q_ref[:, _lanes(hh)])[0] * qg_ref[...]).astype(BF16)
            kn_s[:, _lanes(hh)] = (_head_norm(k_ref[:, _lanes(hh)])[0] * kg_ref[...]).astype(BF16)
        gate = gate_ref[...].astype(F32)
        sg = _sigmoid(gate)
        dog_v = dog_ref[...].astype(F32)
        do_s[...] = (dog_v * (gate * sg)).astype(BF16)
        dgate_ref[...] = (dog_v * o_ref[...].astype(F32) * (sg * (1.0 + gate * (1.0 - sg)))).astype(BF16)
        dkn_s[...] = jnp.zeros_like(dkn_s)
        dv_s[...] = jnp.zeros_like(dv_s)

        row = lax.broadcasted_iota(jnp.int32, (TQ, TQ), 0)
        col = lax.broadcasted_iota(jnp.int32, (TQ, TQ), 1)
        later = _ones_where(row > col)
        earlier = _ones_where(row < col)
        causal = col < row

        def q_block(qi, _):
            q0 = pl.multiple_of(qi * TQ, TQ)

            def weights_both(ki, carries, diag):
                k0 = pl.multiple_of(ki * TQ, TQ)
                z = [_dot(qn_s[pl.ds(q0, TQ), _lanes(hh)], kn_s[pl.ds(k0, TQ), _lanes(hh)], NT) * scale
                     for hh in heads]
                ls = [_log_sigmoid(zz) for zz in z]
                l1m = [a - zz for a, zz in zip(ls, z)]
                if diag:
                    l1m = [jnp.where(causal, a, 0.0) for a in l1m]
                parts = [_split_bf16(a) for a in l1m]
                suffix = [_dot(hi, later, NN) + _dot(lo, later, NN) + carries[hh]
                          for hh, (hi, lo) in zip(heads, parts)]
                for hh in heads:
                    w = jnp.exp(ls[hh] + suffix[hh])
                    if diag:
                        w = jnp.where(causal, w, 0.0)
                    w_s[hh, ki] = w
                    sg_s[hh, ki] = jnp.exp(ls[hh])
                return tuple(carries[hh] + jnp.sum(l1m[hh], axis=1, keepdims=True) for hh in heads)

            carries = weights_both(qi, (jnp.zeros((TQ, 1), F32),) * HP, True)
            lax.fori_loop(0, qi, lambda i, c: weights_both(qi - 1 - i, c, False), carries)

            def grads_both(ki, state, diag):
                k0 = pl.multiple_of(ki * TQ, TQ)
                qb = [qn_s[pl.ds(q0, TQ), _lanes(hh)] for hh in heads]
                dob = [do_s[pl.ds(q0, TQ), _lanes(hh)] for hh in heads]
                w = [w_s[hh, ki] for hh in heads]
                da = [_dot(dob[hh], v_ref[pl.ds(k0, TQ), _lanes(hh)], NT) * w[hh] for hh in heads]
                for hh in heads:
                    dv_s[pl.ds(k0, TQ), _lanes(hh)] += _dot(w[hh].astype(BF16), dob[hh], TN)
                parts = [_split_bf16(a) for a in da]
                prefix = [_dot(hi, earlier, NN) + _dot(lo, earlier, NN) + state[hh][0]
                          for hh, (hi, lo) in zip(heads, parts)]
                dzb = []
                for hh in heads:
                    sgz = sg_s[hh, ki]
                    dz = da[hh] * (1.0 - sgz) - sgz * prefix[hh]
                    if diag:
                        dz = jnp.where(causal, dz, 0.0)
                    dzb.append((dz * scale).astype(BF16))
                dq = [state[hh][1] + _dot(dzb[hh], kn_s[pl.ds(k0, TQ), _lanes(hh)], NN) for hh in heads]
                for hh in heads:
                    dkn_s[pl.ds(k0, TQ), _lanes(hh)] += _dot(dzb[hh], qb[hh], TN)
                return tuple((state[hh][0] + jnp.sum(da[hh], axis=1, keepdims=True), dq[hh]) for hh in heads)

            zero = (jnp.zeros((TQ, 1), F32), jnp.zeros((TQ, HEAD_DIM), F32))
            state = lax.fori_loop(0, qi, lambda i, st: grads_both(i, st, False), (zero,) * HP)
            state = grads_both(qi, state, True)
            for hh in heads:
                dqn_s[pl.ds(q0, TQ), _lanes(hh)] = state[hh][1]
            return 0

        lax.fori_loop(0, nq, q_block, 0)

        def norm_bwd(x_ref, g_ref, dn_s, dx_ref, dg_ref):
            for hh in heads:
                xh, r = _head_norm(x_ref[:, _lanes(hh)])
                dn = dn_s[:, _lanes(hh)]
                dg_ref[...] += jnp.sum(dn * xh, axis=0, keepdims=True)
                dxh = dn * g_ref[...]
                dx_ref[:, _lanes(hh)] = (r * (dxh - xh * jnp.mean(dxh * xh, axis=-1, keepdims=True))).astype(BF16)

        norm_bwd(q_ref, qg_ref, dqn_s, dq_ref, dqg_ref)
        norm_bwd(k_ref, kg_ref, dkn_s, dk_ref, dkg_ref)
        dv_ref[...] = dv_s[...].astype(BF16)

    spec = _attn_specs(S, H, HP)
    vec = pl.BlockSpec((1, HEAD_DIM), lambda b, h: (0, 0))
    blk = pl.BlockSpec((S, ATT_W), lambda b, h: (b, h))
    big = jax.ShapeDtypeStruct((B * S, H * HEAD_DIM), BF16)
    small = jax.ShapeDtypeStruct((1, HEAD_DIM), F32)
    return pl.pallas_call(
        body, name=name,
        out_shape=(big, big, big, big, small, small),
        grid=(B, H // HP),
        in_specs=[spec(0), spec(1), spec(2), spec(3), blk, blk, vec, vec],
        out_specs=(blk, blk, blk, blk, vec, vec),
        scratch_shapes=[pltpu.VMEM((S, ATT_W), BF16)] * 3 + [pltpu.VMEM((S, ATT_W), F32)] * 3
        + [pltpu.VMEM((HP, nq, TQ, TQ), F32)] * 2,
        compiler_params=_params(("arbitrary", "arbitrary")),
    )(proj, proj, proj, proj, o, dog, q_g, k_g)


HALF = GROUPS_PER_BLOCK * STATE


def _cmul(ar, ai, br, bi):
    return ar * br - ai * bi, ar * bi + ai * br


def _cpow(ar, ai, n):
    rr = ri = None
    while n:
        if n & 1:
            rr, ri = (ar, ai) if rr is None else _cmul(rr, ri, ar, ai)
        n >>= 1
        if n:
            ar, ai = _cmul(ar, ai, ar, ai)
    return rr, ri


def _segment_carry(er, ei, lr, li, seg_len, segs_per_seq, reverse):
    Lr, Li = _cpow(lr, li, seg_len)
    pos = lax.broadcasted_iota(jnp.int32, er.shape, 0) % segs_per_seq
    outr = jnp.zeros_like(er)
    outi = jnp.zeros_like(ei)
    pr = pi = None
    for d in range(1, segs_per_seq):
        shift = (SUBLANES - d) if reverse else d
        sr = pltpu.roll(er, shift, 0)
        si = pltpu.roll(ei, shift, 0)
        ok = (pos + d < segs_per_seq) if reverse else (pos >= d)
        sr = jnp.where(ok, sr, 0.0)
        si = jnp.where(ok, si, 0.0)
        if pr is not None:
            sr, si = _cmul(sr, si, pr, pi)
        outr = outr + sr
        outi = outi + si
        pr, pi = (Lr, Li) if pr is None else _cmul(pr, pi, Lr, Li)
    return outr, outi


def _s5_sizes(T, B):
    assert SUBLANES % B == 0
    segs_per_seq = SUBLANES // B
    n_steps = T // SUBLANES
    cj = _tile(n_steps, 64)
    return segs_per_seq, n_steps, cj


def _s5_fwd(u_p, wb, wc, lam, dvec, B, name):
    T, C = u_p.shape
    nb = C // 128
    segs_per_seq, n_steps, cj = _s5_sizes(T, B)
    n_chunks = n_steps // cj
    rows = cj * SUBLANES

    def body(u_ref, wb_ref, wc_ref, lam_ref, d_ref, y_ref, hin_ref, bu_s, h_s):
        lr = jnp.broadcast_to(lam_ref[:, :HALF], (SUBLANES, HALF))
        li = jnp.broadcast_to(lam_ref[:, HALF:], (SUBLANES, HALF))

        def scan_chunk(c, hr, hi, store):
            r0 = pl.multiple_of(c * rows, rows)
            bu_s[...] = _dot(u_ref[pl.ds(r0, rows), :], wb_ref[...], NN)

            def step(j, carry):
                hr, hi = carry
                o = pl.multiple_of(j * SUBLANES, SUBLANES)
                nr = lr * hr - li * hi + bu_s[pl.ds(o, SUBLANES), :HALF]
                ni = lr * hi + li * hr + bu_s[pl.ds(o, SUBLANES), HALF:]
                if store:
                    h_s[pl.ds(o, SUBLANES), :HALF] = nr
                    h_s[pl.ds(o, SUBLANES), HALF:] = ni
                return nr, ni

            hr, hi = lax.fori_loop(0, cj, step, (hr, hi))
            if store:
                uv = u_ref[pl.ds(r0, rows), :].astype(F32)
                y_ref[pl.ds(r0, rows), :] = _dot(h_s[...].astype(BF16), wc_ref[...], NN) + d_ref[...] * uv
            return hr, hi

        zero = jnp.zeros((SUBLANES, HALF), F32)
        er, ei = lax.fori_loop(0, n_chunks, lambda c, h: scan_chunk(c, h[0], h[1], False), (zero, zero))
        h0r, h0i = _segment_carry(er, ei, lr, li, n_steps, segs_per_seq, False)
        hin_ref[:, :HALF] = h0r
        hin_ref[:, HALF:] = h0i
        lax.fori_loop(0, n_chunks, lambda c, h: scan_chunk(c, h[0], h[1], True), (h0r, h0i))

    return pl.pallas_call(
        body, name=name,
        out_shape=(jax.ShapeDtypeStruct((T, C), F32), jax.ShapeDtypeStruct((nb, SUBLANES, 2 * HALF), F32)),
        grid=(nb,),
        in_specs=[pl.BlockSpec((T, 128), lambda g: (0, g)),
                  pl.BlockSpec((None, 128, 2 * HALF), lambda g: (g, 0, 0)),
                  pl.BlockSpec((None, 2 * HALF, 128), lambda g: (g, 0, 0)),
                  pl.BlockSpec((None, 1, 2 * HALF), lambda g: (g, 0, 0)),
                  pl.BlockSpec((1, 128), lambda g: (0, g))],
        out_specs=(pl.BlockSpec((T, 128), lambda g: (0, g)),
                   pl.BlockSpec((None, SUBLANES, 2 * HALF), lambda g: (g, 0, 0))),
        scratch_shapes=[pltpu.VMEM((rows, 2 * HALF), F32)] * 2,
        compiler_params=_params(("parallel",)),
    )(u_p, wb, wc, lam, dvec)


def _s5_bwd(u_p, dy_p, wb, wbt, wc, wct, lam, dvec, h_in, B, name):
    T, C = u_p.shape
    nb = C // 128
    segs_per_seq, n_steps, cj = _s5_sizes(T, B)
    n_chunks = n_steps // cj
    rows = cj * SUBLANES

    def body(u_ref, dy_ref, wb_ref, wbt_ref, wc_ref, wct_ref, lam_ref, d_ref, hin_ref,
             du_ref, dwb_ref, dwc_ref, dlam_ref, dd_ref, h_all, x_s, g_s):
        lr = jnp.broadcast_to(lam_ref[:, :HALF], (SUBLANES, HALF))
        li = jnp.broadcast_to(lam_ref[:, HALF:], (SUBLANES, HALF))
        zero = jnp.zeros((SUBLANES, HALF), F32)

        h_all[pl.ds(0, SUBLANES), :] = hin_ref[...]

        def fwd_chunk(c, carry):
            r0 = pl.multiple_of(c * rows, rows)
            x_s[...] = _dot(u_ref[pl.ds(r0, rows), :], wb_ref[...], NN)

            def step(j, carry):
                hr, hi = carry
                o = pl.multiple_of(j * SUBLANES, SUBLANES)
                nr = lr * hr - li * hi + x_s[pl.ds(o, SUBLANES), :HALF]
                ni = lr * hi + li * hr + x_s[pl.ds(o, SUBLANES), HALF:]
                late = pl.multiple_of(r0 + o + SUBLANES, SUBLANES)
                h_all[pl.ds(late, SUBLANES), :HALF] = nr
                h_all[pl.ds(late, SUBLANES), HALF:] = ni
                return nr, ni

            return lax.fori_loop(0, cj, step, carry)

        lax.fori_loop(0, n_chunks, fwd_chunk, (hin_ref[:, :HALF], hin_ref[:, HALF:]))

        def bwd_chunk(i, carry, store):
            c = n_chunks - 1 - i
            r0 = pl.multiple_of(c * rows, rows)
            dyv = dy_ref[pl.ds(r0, rows), :]
            x_s[...] = _dot(dyv, wct_ref[...], NN)

            def step(jj, carry):
                ar, ai, accr, acci = carry
                j = cj - 1 - jj
                o = pl.multiple_of(j * SUBLANES, SUBLANES)
                nr = lr * ar + li * ai + x_s[pl.ds(o, SUBLANES), :HALF]
                ni = lr * ai - li * ar + x_s[pl.ds(o, SUBLANES), HALF:]
                if store:
                    g_s[pl.ds(o, SUBLANES), :HALF] = nr
                    g_s[pl.ds(o, SUBLANES), HALF:] = ni
                    prev = pl.multiple_of(r0 + o, SUBLANES)
                    pr = h_all[pl.ds(prev, SUBLANES), :HALF]
                    pi = h_all[pl.ds(prev, SUBLANES), HALF:]
                    accr = accr + nr * pr + ni * pi
                    acci = acci + ni * pr - nr * pi
                return nr, ni, accr, acci

            carry = lax.fori_loop(0, cj, step, carry)
            if store:
                gb = g_s[...].astype(BF16)
                uv = u_ref[pl.ds(r0, rows), :]
                dyf = dyv.astype(F32)
                du_ref[pl.ds(r0, rows), :] = (_dot(gb, wbt_ref[...], NN) + d_ref[...] * dyf).astype(BF16)
                dwb_ref[...] += _dot(uv, gb, TN)
                hb = h_all[pl.ds(pl.multiple_of(r0 + SUBLANES, SUBLANES), rows), :].astype(BF16)
                dwc_ref[...] += _dot(hb, dyv, TN)
                dd_ref[...] += jnp.sum(dyf * uv.astype(F32), axis=0, keepdims=True)
            return carry

        er, ei, _, _ = lax.fori_loop(0, n_chunks, lambda i, c: bwd_chunk(i, c, False), (zero, zero, zero, zero))
        a0r, a0i = _segment_carry(er, ei, lr, -li, n_steps, segs_per_seq, True)
        dwb_ref[...] = jnp.zeros_like(dwb_ref)
        dwc_ref[...] = jnp.zeros_like(dwc_ref)
        dd_ref[...] = jnp.zeros_like(dd_ref)
        _, _, accr, acci = lax.fori_loop(0, n_chunks, lambda i, c: bwd_chunk(i, c, True), (a0r, a0i, zero, zero))
        dlam_ref[:, :HALF] = jnp.sum(accr, axis=0, keepdims=True)
        dlam_ref[:, HALF:] = jnp.sum(acci, axis=0, keepdims=True)

    col = pl.BlockSpec((T, 128), lambda g: (0, g))
    vec = pl.BlockSpec((1, 128), lambda g: (0, g))

    def per_block(*shape):
        return pl.BlockSpec((None,) + shape, lambda g: (g, 0, 0))

    return pl.pallas_call(
        body, name=name,
        out_shape=(jax.ShapeDtypeStruct((T, C), BF16),
                   jax.ShapeDtypeStruct((nb, 128, 2 * HALF), F32),
                   jax.ShapeDtypeStruct((nb, 2 * HALF, 128), F32),
                   jax.ShapeDtypeStruct((nb, 1, 2 * HALF), F32),
                   jax.ShapeDtypeStruct((1, C), F32)),
        grid=(nb,),
        in_specs=[col, col, per_block(128, 2 * HALF), per_block(2 * HALF, 128), per_block(2 * HALF, 128),
                  per_block(128, 2 * HALF), per_block(1, 2 * HALF), vec, per_block(SUBLANES, 2 * HALF)],
        out_specs=(col, per_block(128, 2 * HALF), per_block(2 * HALF, 128), per_block(1, 2 * HALF), vec),
        scratch_shapes=[pltpu.VMEM((T + SUBLANES, 2 * HALF), F32),
                        pltpu.VMEM((rows, 2 * HALF), F32), pltpu.VMEM((rows, 2 * HALF), F32)],
        compiler_params=_params(("parallel",)),
    )(u_p, dy_p, wb, wbt, wc, wct, lam, dvec, h_in)


def _discretize(a_re, a_im, log_dt, b_re, b_im):
    dt = jnp.exp(log_dt)[:, None]
    mag = jnp.exp(a_re * dt)
    lam_re = mag * jnp.cos(a_im * dt)
    lam_im = mag * jnp.sin(a_im * dt)
    den = a_re * a_re + a_im * a_im
    f_re = ((lam_re - 1.0) * a_re + lam_im * a_im) / den
    f_im = (lam_im * a_re - (lam_re - 1.0) * a_im) / den
    bb_re = f_re[..., None] * b_re - f_im[..., None] * b_im
    bb_im = f_re[..., None] * b_im + f_im[..., None] * b_re
    return lam_re, lam_im, bb_re, bb_im


def _block_diag_in(bb_re, bb_im):
    eye = jnp.eye(GROUPS_PER_BLOCK, dtype=F32)

    def one(bb):
        t = bb.reshape(-1, GROUPS_PER_BLOCK, STATE, GROUP)
        return jnp.einsum('gapi,ab->gaibp', t, eye).reshape(-1, 128, HALF)

    return jnp.concatenate([one(bb_re), one(bb_im)], axis=-1)


def _block_diag_in_grad(dwb):
    eye = jnp.eye(GROUPS_PER_BLOCK, dtype=F32)

    def one(d):
        t = d.reshape(-1, GROUPS_PER_BLOCK, GROUP, GROUPS_PER_BLOCK, STATE)
        return jnp.einsum('gaibp,ab->gapi', t, eye).reshape(-1, STATE, GROUP)

    return one(dwb[..., :HALF]), one(dwb[..., HALF:])


def _block_diag_out(c_re, c_im):
    eye = jnp.eye(GROUPS_PER_BLOCK, dtype=F32)

    def one(cc):
        t = cc.reshape(-1, GROUPS_PER_BLOCK, GROUP, STATE)
        return jnp.einsum('gaip,ab->gbpai', t, eye).reshape(-1, HALF, 128)

    return jnp.concatenate([one(c_re), -one(c_im)], axis=1)


def _block_diag_out_grad(dwc):
    eye = jnp.eye(GROUPS_PER_BLOCK, dtype=F32)

    def one(d):
        t = d.reshape(-1, GROUPS_PER_BLOCK, STATE, GROUPS_PER_BLOCK, GROUP)
        return jnp.einsum('gbpai,ab->gaip', t, eye).reshape(-1, GROUP, STATE)

    return one(dwc[:, :HALF]), -one(dwc[:, HALF:])


def _pack(parts):
    return jnp.concatenate([p.reshape(-1, PACK_W) for p in parts], axis=0)


def _unpack(buf, shapes):
    lead = buf.shape[:-2]
    out, r = [], 0
    for s in shapes:
        n = math.prod(s) // PACK_W
        out.append(buf[..., r:r + n, :].reshape(lead + tuple(s)))
        r += n
    return out


def _permute_rows(a, n_steps):
    T, C = a.shape
    return a.reshape(SUBLANES, n_steps, C).transpose(1, 0, 2).reshape(T, C)


def _unpermute_rows(a, n_steps):
    T, C = a.shape
    return a.reshape(n_steps, SUBLANES, C).transpose(1, 0, 2).reshape(T, C)


def kernel(x, norm_g, attn_w_in, attn_q_g, attn_k_g, attn_w_out, ssm_w_in, ssm_A_re, ssm_A_im, ssm_log_dt, ssm_B_re, ssm_B_im, ssm_C_re, ssm_C_im, ssm_D, ssm_glu_w, ssm_glu_b, ssm_w_out, loss_target, m_norm_g, m_attn_w_in, m_attn_q_g, m_attn_k_g, m_attn_w_out, m_ssm_w_in, m_ssm_A_re, m_ssm_A_im, m_ssm_log_dt, m_ssm_B_re, m_ssm_B_im, m_ssm_C_re, m_ssm_C_im, m_ssm_D, m_ssm_glu_w, m_ssm_glu_b, m_ssm_w_out, v_norm_g, v_attn_w_in, v_attn_q_g, v_attn_k_g, v_attn_w_out, v_ssm_w_in, v_ssm_A_re, v_ssm_A_im, v_ssm_log_dt, v_ssm_B_re, v_ssm_B_im, v_ssm_C_re, v_ssm_C_im, v_ssm_D, v_ssm_glu_w, v_ssm_glu_b, v_ssm_w_out):
    B, S, D = x.shape
    T = B * S
    H = D // HEAD_DIM
    G_loc = ssm_A_re.shape[1]
    G = G_loc * N_DEV
    n_steps = T // SUBLANES
    xf = x.reshape(T, D)
    target = loss_target.reshape(T, D)

    c_in0 = _place_mine(attn_w_in[0], BF16, False, "cast_attn_w_in")
    c_out0 = _place_mine(attn_w_out[0], BF16, False, "cast_attn_w_out")
    c_in1 = _place_mine(ssm_w_in[0], BF16, False, "cast_ssm_w_in")
    c_glu = _place_mine(ssm_glu_w[0], BF16, False, "cast_ssm_glu_w")
    c_out1 = _place_mine(ssm_w_out[0], BF16, False, "cast_ssm_w_out")
    w_in0 = _gather_two_level(c_in0, "gather_attn_w_in")

    small_shapes = [(G_loc, STATE), (G_loc, STATE), (G_loc, STATE, GROUP), (G_loc, STATE, GROUP),
                    (G_loc, GROUP, STATE), (G_loc, GROUP, STATE), (G_loc * GROUP,), (G_loc * GROUP,)]
    disc_in = (ssm_A_re[0], ssm_A_im[0], ssm_log_dt[0], ssm_B_re[0], ssm_B_im[0])
    (lam_re, lam_im, bb_re, bb_im), disc_vjp = jax.vjp(_discretize, *disc_in)
    small = _pack([lam_re, lam_im, bb_re, bb_im, ssm_C_re[0], ssm_C_im[0], ssm_D[0], ssm_glu_b[0]])
    small_all = _exchange(small, False, "gather_small")
    lam_re_a, lam_im_a, bb_re_a, bb_im_a, c_re_a, c_im_a, d_a, glu_b_a = [
        t.reshape((G,) + t.shape[2:]) if t.ndim > 2 else t.reshape(-1)
        for t in _unpack(small_all, small_shapes)]
    wb = _block_diag_in(bb_re_a, bb_im_a)
    wc = _block_diag_out(c_re_a, c_im_a)
    wb_b, wc_b = wb.astype(BF16), wc.astype(BF16)
    wbt_b, wct_b = wb_b.transpose(0, 2, 1), wc_b.transpose(0, 2, 1)
    lam = jnp.concatenate([lam_re_a.reshape(-1, 1, HALF), lam_im_a.reshape(-1, 1, HALF)], axis=-1)
    d_row = d_a.reshape(1, D)
    glu_b_row = glu_b_a.reshape(1, D)
    g0, g1 = norm_g[0:1], norm_g[1:2]
    q_g, k_g = attn_q_g, attn_k_g

    h0 = _rmsnorm_fwd(xf, g0, "norm0")
    s_out0 = _exchange_start(c_out0, None, "gather_attn_w_out_start", deps=(w_in0,))
    s_in1 = _exchange_start(c_in1, None, "gather_ssm_w_in_start", deps=(w_in0,))
    proj0 = _mm_nn(h0, w_in0, BF16, "attn_in", deps=(s_out0[3], s_in1[3]))
    og, o = _attn_fwd(proj0, q_g, k_g, B, S, H, "attn_fwd")
    w_out0 = _exchange_wait(s_out0, og, "gather_attn_w_out_wait").reshape(1, D, D)
    s_glu = _exchange_start(c_glu, None, "gather_ssm_glu_w_start", deps=(w_out0,))
    s_out1 = _exchange_start(c_out1, None, "gather_ssm_w_out_start", deps=(w_out0,))
    x1 = _mm_nn(og, w_out0, F32, "attn_out", residual=xf, deps=(s_glu[3], s_out1[3]))

    h1 = _rmsnorm_fwd(x1, g1, "norm1")
    w_in1 = _exchange_wait(s_in1, h1, "gather_ssm_w_in_wait")
    proj1 = _mm_nn(h1, w_in1, BF16, "ssm_in")
    u_p = _permute_rows(proj1[:, :D], n_steps)
    gate1 = proj1[:, D:]
    y_p, h_in = _s5_fwd(u_p, wb_b, wc_b, lam, d_row, B, "s5_fwd")
    y_ssm = _unpermute_rows(y_p, n_steps)
    (yg,) = _ew(lambda a: (_gelu(a)[0],), [y_ssm], [], [BF16], 0, "gelu")
    w_glu = _exchange_wait(s_glu, yg, "gather_ssm_glu_w_wait").reshape(1, D, D)
    z = _mm_nn(yg, w_glu, F32, "glu_in", bias=glu_b_row)

    def glu_fwd(y, zz, gt):
        gt = gt.astype(F32)
        return (_gelu(y)[0] * _sigmoid(zz) * (gt * _sigmoid(gt)),)

    (y3,) = _ew(glu_fwd, [y_ssm, z, gate1], [], [BF16], 0, "glu_gate")
    w_out1 = _exchange_wait(s_out1, y3, "gather_ssm_w_out_wait").reshape(1, D, D)
    out = _mm_nn(y3, w_out1, F32, "ssm_out", residual=x1)

    dout, dout_b, loss_part = _loss_head(out, target, "loss")
    loss = lax.psum(loss_part[0, 0], ("x", "y", "c"))

    p_w_out1 = _mm_tn(y3, dout_b, 1, BF16, "ssm_out_dw").reshape(N_DEV, D // N_DEV, D)
    sc_out1 = _exchange_start(_place_mine(p_w_out1, BF16, True, "place_ssm_w_out"), p_w_out1, "scatter_ssm_w_out_start")
    dy3 = _mm_nt(dout_b, w_out1, F32, "ssm_out_dx", deps=(sc_out1[3],))

    def glu_bwd(d3, y, zz, gt):
        gt = gt.astype(F32)
        sg = _sigmoid(gt)
        sz = _sigmoid(zz)
        ygv, _ = _gelu(y)
        dy2 = d3 * (gt * sg)
        dgate = d3 * (ygv * sz) * (sg * (1.0 + gt * (1.0 - sg)))
        dz = dy2 * ygv * (sz * (1.0 - sz))
        return dz, dgate, dy2 * sz, jnp.sum(dz, axis=0, keepdims=True)

    dz_b, dgate1, dyg_a, dglu_b = _ew(glu_bwd, [dy3, y_ssm, z, gate1], [], [BF16, BF16, F32], 1, "glu_gate_bwd")
    p_w_glu = _mm_tn(yg, dz_b, 1, BF16, "glu_in_dw").reshape(N_DEV, D // N_DEV, D)
    sc_glu = _exchange_start(_place_mine(p_w_glu, BF16, True, "place_ssm_glu_w"), p_w_glu, "scatter_ssm_glu_w_start")
    dyg_b = _mm_nt(dz_b, w_glu, F32, "glu_in_dx", deps=(sc_glu[3],))

    def gelu_bwd(da, db, y):
        _, t = _gelu(y)
        dg = 0.5 * (1.0 + t) + 0.5 * y * (1.0 - t * t) * (GELU_C * (1.0 + 3.0 * 0.044715 * (y * y)))
        return ((da + db) * dg,)

    (dy_ssm,) = _ew(gelu_bwd, [dyg_a, dyg_b, y_ssm], [], [BF16], 0, "gelu_bwd")
    dy_p = _permute_rows(dy_ssm, n_steps)
    du_p, dwb, dwc, dlam, dd = _s5_bwd(u_p, dy_p, wb_b, wbt_b, wc_b, wct_b, lam, d_row, h_in, B, "s5_bwd")
    du = _unpermute_rows(du_p, n_steps)
    dproj1 = jnp.concatenate([du, dgate1], axis=1)
    p_w_in1 = _mm_tn(h1, dproj1, N_DEV, BF16, "ssm_in_dw")
    sc_in1 = _exchange_start(_place_mine(p_w_in1, BF16, True, "place_ssm_w_in"), p_w_in1, "scatter_ssm_w_in_start")
    dh1 = _mm_nt(dproj1, w_in1, F32, "ssm_in_dx", deps=(sc_in1[3],))
    dx1, dx1_b, dg1 = _rmsnorm_bwd(x1, dh1, dout, g1, "norm1_bwd")

    p_w_out0 = _mm_tn(og, dx1_b, 1, BF16, "attn_out_dw").reshape(N_DEV, D // N_DEV, D)
    sc_out0 = _exchange_start(_place_mine(p_w_out0, BF16, True, "place_attn_w_out"), p_w_out0, "scatter_attn_w_out_start")
    dog = _mm_nt(dx1_b, w_out0, BF16, "attn_out_dx", deps=(sc_out0[3],))
    dq, dk, dv, dgate0, dqg, dkg = _attn_bwd(proj0, o, dog, q_g, k_g, B, S, H, "attn_bwd")
    dproj0 = jnp.concatenate([dq, dk, dv, dgate0], axis=1)
    p_w_in0 = _mm_tn(h0, dproj0, N_DEV, BF16, "attn_in_dw")
    sc_in0 = _exchange_start(_place_mine(p_w_in0, BF16, True, "place_attn_w_in"), p_w_in0, "scatter_attn_w_in_start")
    dh0 = _mm_nt(dproj0, w_in0, F32, "attn_in_dx", deps=(sc_in0[3],))
    dx, _, dg0 = _rmsnorm_bwd(xf, dh0, dx1, g0, "norm0_bwd")

    def update(started, after, w, m, v, name):
        recv = _exchange_wait(started, after, "scatter_" + name + "_wait")
        return _adamw(recv, w[0], m[0], v[0], "adamw_" + name)

    r_ssm_w_out = update(sc_out1, dx, ssm_w_out, m_ssm_w_out, v_ssm_w_out, "ssm_w_out")
    r_ssm_glu_w = update(sc_glu, r_ssm_w_out[0], ssm_glu_w, m_ssm_glu_w, v_ssm_glu_w, "ssm_glu_w")
    r_ssm_w_in = update(sc_in1, r_ssm_glu_w[0], ssm_w_in, m_ssm_w_in, v_ssm_w_in, "ssm_w_in")
    r_attn_w_out = update(sc_out0, r_ssm_w_in[0], attn_w_out, m_attn_w_out, v_attn_w_out, "attn_w_out")

    dbb_re, dbb_im = _block_diag_in_grad(dwb)
    dc_re, dc_im = _block_diag_out_grad(dwc)
    dlam_re = dlam[:, 0, :HALF].reshape(G, STATE)
    dlam_im = dlam[:, 0, HALF:].reshape(G, STATE)
    by_owner = [t.reshape((N_DEV, -1)) for t in (dlam_re, dlam_im, dbb_re, dbb_im, dc_re, dc_im, dd, dglu_b)]
    small_parts = jnp.concatenate([t.reshape(N_DEV, -1, PACK_W) for t in by_owner], axis=1)
    small_sum = _sum_parts(_exchange(small_parts, True, "scatter_small"), "sum_small")
    s_lam_re, s_lam_im, s_bb_re, s_bb_im, s_c_re, s_c_im, s_d, s_glu_b = _unpack(small_sum, small_shapes)
    g_a_re, g_a_im, g_log_dt, g_b_re, g_b_im = disc_vjp((s_lam_re, s_lam_im, s_bb_re, s_bb_im))

    local_names = ["ssm_A_re", "ssm_A_im", "ssm_log_dt", "ssm_B_re", "ssm_B_im", "ssm_C_re", "ssm_C_im",
                   "ssm_D", "ssm_glu_b"]
    local_g = [g_a_re, g_a_im, g_log_dt, g_b_re, g_b_im, s_c_re, s_c_im, s_d, s_glu_b]
    local_w = [ssm_A_re, ssm_A_im, ssm_log_dt, ssm_B_re, ssm_B_im, ssm_C_re, ssm_C_im, ssm_D, ssm_glu_b]
    local_m = [m_ssm_A_re, m_ssm_A_im, m_ssm_log_dt, m_ssm_B_re, m_ssm_B_im, m_ssm_C_re, m_ssm_C_im,
               m_ssm_D, m_ssm_glu_b]
    local_v = [v_ssm_A_re, v_ssm_A_im, v_ssm_log_dt, v_ssm_B_re, v_ssm_B_im, v_ssm_C_re, v_ssm_C_im,
               v_ssm_D, v_ssm_glu_b]
    r_local = _adamw_small(local_g, local_w, local_m, local_v, None, "adamw_small")

    rep_g = [jnp.concatenate([dg0, dg1], axis=0), dqg, dkg]
    rep_w = [norm_g, attn_q_g, attn_k_g]
    rep_m = [m_norm_g, m_attn_q_g, m_attn_k_g]
    rep_v = [v_norm_g, v_attn_q_g, v_attn_k_g]
    r_rep = _adamw_small(rep_g, rep_w, rep_m, rep_v, "gather_rep", "adamw_rep")

    r_attn_w_in = update(sc_in0, r_rep[0][0], attn_w_in, m_attn_w_in, v_attn_w_in, "attn_w_in")

    res = {"attn_w_in": r_attn_w_in, "attn_w_out": r_attn_w_out, "ssm_w_in": r_ssm_w_in,
           "ssm_glu_w": r_ssm_glu_w, "ssm_w_out": r_ssm_w_out}
    ref_w = {"attn_w_in": attn_w_in, "attn_w_out": attn_w_out, "ssm_w_in": ssm_w_in,
             "ssm_glu_w": ssm_glu_w, "ssm_w_out": ssm_w_out}
    for name, r, w in zip(local_names, r_local, local_w):
        res[name], ref_w[name] = r, w
    for name, r, w in zip(["norm_g", "attn_q_g", "attn_k_g"], r_rep, rep_w):
        res[name], ref_w[name] = r, w
    order = ["norm_g", "attn_w_in", "attn_q_g", "attn_k_g", "attn_w_out", "ssm_w_in", "ssm_A_re", "ssm_A_im",
             "ssm_log_dt", "ssm_B_re", "ssm_B_im", "ssm_C_re", "ssm_C_im", "ssm_D", "ssm_glu_w", "ssm_glu_b",
             "ssm_w_out"]
    outs = [loss, dx.reshape(B, S, D)]
    for kind in range(4):
        outs += [res[n][kind].reshape(ref_w[n].shape) for n in order]
    return tuple(outs)


def _adamw_small(grads, ws, ms, vs, gather_name, name):
    sizes = [math.prod(w.shape) for w in ws]
    total = sum(sizes)
    rows = -(-total // (PACK_W * 8)) * 8
    if rows > 256:
        rows = -(-rows // 256) * 256

    def pack(ts, fill):
        flat = jnp.concatenate([t.reshape(-1).astype(F32) for t in ts])
        flat = jnp.concatenate([flat, jnp.full((rows * PACK_W - total,), fill, F32)])
        return flat.reshape(rows, PACK_W)

    g = pack(grads, 0.0)
    parts = _exchange(g, False, gather_name) if gather_name else g[None]
    res = _adamw(parts, pack(ws, 0.0), pack(ms, 0.0), pack(vs, 1.0), name)
    outs = []
    off = 0
    flats = [r.reshape(-1) for r in res]
    for n in sizes:
        outs.append(tuple(f[off:off + n] for f in flats))
        off += n
    return outs
```

```python
import functools
import math

import jax
import jax.numpy as jnp
from jax import lax
from jax.experimental import pallas as pl
from jax.experimental.pallas import tpu as pltpu

F32 = jnp.float32
BF16 = jnp.bfloat16

N_DEV = 8
HEAD_DIM = 128
GROUP = 16
STATE = 64
GROUPS_PER_BLOCK = 8
SUBLANES = 8
RMS_EPS = 1e-6
ADAM_LR, ADAM_B1, ADAM_B2, ADAM_EPS, ADAM_WD, ADAM_STEP = 0.001, 0.9, 0.999, 1e-08, 0.01, 10
VMEM_LIMIT = 56 * 1024 * 1024
GELU_C = math.sqrt(2.0 / math.pi)
PACK_W = 128


def _params(sem, **kw):
    return pltpu.CompilerParams(dimension_semantics=sem, vmem_limit_bytes=VMEM_LIMIT, **kw)


def _tile(n, t):
    t = min(n, t)
    assert n % t == 0, (n, t)
    return t


def _ones_where(cond):
    return jnp.where(cond, 1.0, 0.0).astype(BF16)


def _sigmoid(x):
    return 1.0 / (1.0 + jnp.exp(-x))


def _log_sigmoid(z):
    return jnp.minimum(z, 0.0) - jnp.log(1.0 + jnp.exp(-jnp.abs(z)))


def _dot(a, b, dims):
    return lax.dot_general(a, b, (dims, ((), ())), preferred_element_type=F32)


NN = ((1,), (0,))
NT = ((1,), (1,))
TN = ((0,), (0,))


def _exchange(x, scatter, name):
    shape = x.shape[1:] if scatter else x.shape

    def body(x_ref, out_ref, send_sems, recv_sems, local_sem):
        ix, iy, ic = lax.axis_index("x"), lax.axis_index("y"), lax.axis_index("c")
        me = 4 * ix + 2 * iy + ic

        def peer(k):
            kx, ky, kc = (k >> 2) & 1, (k >> 1) & 1, k & 1
            px, py, pc = ix ^ kx, iy ^ ky, ic ^ kc
            return (px, py, pc), 4 * px + 2 * py + pc

        mine = pltpu.make_async_copy(x_ref.at[me] if scatter else x_ref, out_ref.at[me], local_sem)
        mine.start()
        copies = []
        for k in range(1, N_DEV):
            pid, pidx = peer(k)
            cp = pltpu.make_async_remote_copy(
                src_ref=x_ref.at[pidx] if scatter else x_ref,
                dst_ref=out_ref.at[me],
                send_sem=send_sems.at[k - 1], recv_sem=recv_sems.at[k - 1],
                device_id=pid, device_id_type=pl.DeviceIdType.MESH)
            cp.start()
            copies.append(cp)
        for k in range(1, N_DEV):
            pid, pidx = peer(k)
            pltpu.make_async_remote_copy(
                src_ref=x_ref.at[pidx] if scatter else x_ref,
                dst_ref=out_ref.at[pidx],
                send_sem=send_sems.at[k - 1], recv_sem=recv_sems.at[k - 1],
                device_id=pid, device_id_type=pl.DeviceIdType.MESH).wait_recv()
        for cp in copies:
            cp.wait_send()
        mine.wait()

    return pl.pallas_call(
        body, name=name,
        out_shape=jax.ShapeDtypeStruct((N_DEV,) + tuple(shape), x.dtype),
        in_specs=[pl.BlockSpec(memory_space=pl.ANY)],
        out_specs=pl.BlockSpec(memory_space=pl.ANY),
        scratch_shapes=[pltpu.SemaphoreType.DMA((N_DEV - 1,)), pltpu.SemaphoreType.DMA((N_DEV - 1,)),
                        pltpu.SemaphoreType.DMA],
    )(x)


_HBM = pl.BlockSpec(memory_space=pltpu.HBM)
_SEM = pl.BlockSpec(memory_space=pltpu.SEMAPHORE)
_ANY = pl.BlockSpec(memory_space=pl.ANY)
_EFFECT = pltpu.SideEffectType.DATAFLOW_SIDE_EFFECTING


def _peer(k):
    ix, iy, ic = lax.axis_index("x"), lax.axis_index("y"), lax.axis_index("c")
    px, py, pc = ix ^ ((k >> 2) & 1), iy ^ ((k >> 1) & 1), ic ^ (k & 1)
    return (px, py, pc), 4 * px + 2 * py + pc


def _my_index():
    return 4 * lax.axis_index("x") + 2 * lax.axis_index("y") + lax.axis_index("c")


def _my_chip():
    return 2 * lax.axis_index("x") + lax.axis_index("y")


def _place(src, idx, n_slots, dtype, name, tr=256, tc=1024):
    R, C = src.shape[-2:]
    tr, tc = _tile(R, tr), _tile(C, tc)
    idx = idx.astype(jnp.int32).reshape(1)

    def body(idx_ref, src_ref, land_ref, out_ref):
        out_ref[...] = src_ref[...].astype(dtype)

    slot = pl.BlockSpec((None, tr, tc), lambda r, c, idx_ref: (idx_ref[0], r, c))
    src_spec = slot if src.ndim == 3 else pl.BlockSpec((tr, tc), lambda r, c, idx_ref: (r, c))
    land = lax.empty((n_slots, R, C), dtype)
    return pl.pallas_call(
        body, name=name,
        out_shape=jax.ShapeDtypeStruct(land.shape, dtype),
        grid_spec=pltpu.PrefetchScalarGridSpec(
            num_scalar_prefetch=1, grid=(R // tr, C // tc), in_specs=[src_spec, _ANY], out_specs=slot),
        input_output_aliases={2: 0},
        compiler_params=_params(("parallel", "parallel")),
    )(idx, src, land)


def _pair_sum(x, got, name, tr=256, tc=1024):
    _, R, C = x.shape
    tr, tc = _tile(R, tr), _tile(C, tc)
    core = lax.axis_index("c").astype(jnp.int32).reshape(1)

    def body(core_ref, x_ref, got_ref, out_ref):
        out_ref[...] = (x_ref[...].astype(F32) + got_ref[...].astype(F32)).astype(out_ref.dtype)

    blk = pl.BlockSpec((None, tr, tc), lambda i, r, c, core_ref: (i, r, c))
    return pl.pallas_call(
        body, name=name,
        out_shape=jax.ShapeDtypeStruct(got.shape, got.dtype),
        grid_spec=pltpu.PrefetchScalarGridSpec(
            num_scalar_prefetch=1, grid=(N_DEV // 2, R // tr, C // tc),
            in_specs=[pl.BlockSpec((None, tr, tc), lambda i, r, c, core_ref: (2 * i + core_ref[0], r, c)), blk],
            out_specs=blk),
        compiler_params=_params(("parallel", "parallel", "parallel")),
    )(core, x, got)


def _gather_two_level(land, name, deps=()):
    def body(*refs):
        land_ref, send_sems, recv_sems = refs[1 + len(deps):]
        x, y, c = lax.axis_index("x"), lax.axis_index("y"), lax.axis_index("c")
        me, sibling = (x, y, c), (x, y, 1 - c)
        chips = [(1 - x, y), (x, 1 - y), (1 - x, 1 - y)]

        def block(px, py, pc):
            return land_ref.at[4 * px + 2 * py + pc]

        def copy(k, blk, to):
            return pltpu.make_async_remote_copy(
                src_ref=block(*blk), dst_ref=block(*blk), send_sem=send_sems.at[k], recv_sem=recv_sems.at[k],
                device_id=to, device_id_type=pl.DeviceIdType.MESH)

        first = [copy(0, me, sibling)] + [copy(1 + j, me, (*chip, c)) for j, chip in enumerate(chips)]
        for cp in first:
            cp.start()
        passed = [copy(4 + j, (*chip, c), sibling) for j, chip in enumerate(chips)]
        for j, chip in enumerate(chips):
            copy(1 + j, (*chip, c), me).wait_recv()
            passed[j].start()
        copy(0, sibling, me).wait_recv()
        for j, chip in enumerate(chips):
            copy(4 + j, (*chip, 1 - c), me).wait_recv()
        for cp in first + passed:
            cp.wait_send()

    return pl.pallas_call(
        body, name=name,
        out_shape=jax.ShapeDtypeStruct(land.shape, land.dtype),
        in_specs=[_ANY] * (1 + len(deps)), out_specs=_ANY,
        input_output_aliases={0: 0},
        scratch_shapes=[pltpu.SemaphoreType.DMA((N_DEV - 1,)), pltpu.SemaphoreType.DMA((N_DEV - 1,))],
    )(land, *deps)


_N_COPIES = {"gather": N_DEV - 1, "scatter": N_DEV - 1, "pair": N_DEV // 2, "chips": N_DEV // 2 - 1}


def _copies(kind, land_ref, x_ref):
    ix, iy, ic = lax.axis_index("x"), lax.axis_index("y"), lax.axis_index("c")
    me, chip = _my_index(), _my_chip()
    out = []
    if kind in ("gather", "scatter"):
        for k in range(1, N_DEV):
            pid, pidx = _peer(k)
            out.append((land_ref.at[me] if kind == "gather" else x_ref.at[pidx], land_ref.at[me], pid, land_ref.at[pidx]))
    elif kind == "pair":
        for i in range(N_DEV // 2):
            out.append((x_ref.at[2 * i + (1 - ic)], land_ref.at[i], (ix, iy, 1 - ic), land_ref.at[i]))
    else:
        for k in range(1, N_DEV // 2):
            px, py = ix ^ (k >> 1), iy ^ (k & 1)
            out.append((x_ref.at[2 * px + py], land_ref.at[chip], (px, py, ic), land_ref.at[2 * px + py]))
    return out


def _exchange_start(land, x, kind, name, deps=()):
    bufs = [land] if x is None else [land, x]
    nb, n = len(bufs), _N_COPIES[kind]

    def body(*refs):
        send_sems, recv_sems = refs[nb + len(deps):nb + len(deps) + 2]
        token = refs[2 * nb + len(deps) + 2]
        for i, (src, dst, pid, _) in enumerate(_copies(kind, refs[0], refs[nb - 1])):
            pltpu.make_async_remote_copy(src_ref=src, dst_ref=dst, send_sem=send_sems.at[i], recv_sem=recv_sems.at[i],
                                         device_id=pid, device_id_type=pl.DeviceIdType.MESH).start()
        token[...] = jnp.zeros_like(token)

    out = pl.pallas_call(
        body, name=name,
        out_shape=(pltpu.SemaphoreType.DMA((n,)), pltpu.SemaphoreType.DMA((n,)))
        + tuple(pltpu.HBM(t.shape, t.dtype) for t in bufs) + (jax.ShapeDtypeStruct((8, 128), F32),),
        in_specs=(_HBM,) * nb + (_ANY,) * len(deps),
        out_specs=(_SEM, _SEM) + (_HBM,) * nb + (pl.BlockSpec(memory_space=pltpu.VMEM),),
        input_output_aliases={i: 2 + i for i in range(nb)},
        compiler_params=pltpu.CompilerParams(has_side_effects=_EFFECT),
    )(*[pltpu.with_memory_space_constraint(t, pltpu.HBM) for t in bufs], *deps)
    return out[0], out[1], out[2:2 + nb], out[2 + nb]


def _exchange_wait(started, after, kind, name):
    send_sems, recv_sems, bufs, _ = started
    nb = len(bufs)

    def body(*refs):
        send_sems, recv_sems = refs[nb:nb + 2]
        for i, (src, _, pid, landed) in enumerate(_copies(kind, refs[0], refs[nb - 1])):
            cp = pltpu.make_async_remote_copy(src_ref=src, dst_ref=landed, send_sem=send_sems.at[i],
                                              recv_sem=recv_sems.at[i], device_id=pid,
                                              device_id_type=pl.DeviceIdType.MESH)
            cp.wait_send()
            cp.wait_recv()

    return pl.pallas_call(
        body, name=name,
        out_shape=tuple(pltpu.HBM(t.shape, t.dtype) for t in bufs),
        in_specs=(_HBM,) * nb + (_SEM, _SEM, _ANY),
        out_specs=(_HBM,) * nb,
        input_output_aliases={i: i for i in range(nb)},
        compiler_params=pltpu.CompilerParams(has_side_effects=_EFFECT),
    )(*bufs, send_sems, recv_sems, after)[0]


def _accumulate(acc, part, step, n_steps, finish):
    if n_steps == 1:
        finish(part)
        return

    @pl.when(step == 0)
    def _():
        acc[...] = part

    @pl.when((step > 0) & (step < n_steps - 1))
    def _():
        acc[...] += part

    @pl.when(step == n_steps - 1)
    def _():
        finish(acc[...] + part)


def _mm_nn(a, b, out_dtype, name, bias=None, residual=None, deps=(), tm=1024, tn=1024, tk=2048):
    M, K = a.shape
    J, K2, Nj = b.shape
    assert K == K2
    tm, tn, tk = _tile(M, tm), _tile(Nj, tn), _tile(K, tk)
    nb, nk = Nj // tn, K // tk

    def body(*refs):
        a_ref, b_ref = refs[0], refs[1]
        i = 2
        bias_ref = res_ref = None
        if bias is not None:
            bias_ref = refs[i]; i += 1
        if residual is not None:
            res_ref = refs[i]; i += 1
        i += len(deps)
        o_ref, acc = refs[i], refs[i + 1]

        def finish(r):
            if bias_ref is not None:
                r = r + bias_ref[...]
            if res_ref is not None:
                r = r + res_ref[...]
            o_ref[...] = r.astype(out_dtype)

        _accumulate(acc, _dot(a_ref[...], b_ref[...], NN), pl.program_id(3), nk, finish)

    in_specs = [pl.BlockSpec((tm, tk), lambda j, m, n, k: (m, k)),
                pl.BlockSpec((None, tk, tn), lambda j, m, n, k: (j, k, n))]
    args = [a, b]
    if bias is not None:
        in_specs.append(pl.BlockSpec((1, tn), lambda j, m, n, k: (0, j * nb + n)))
        args.append(bias)
    if residual is not None:
        in_specs.append(pl.BlockSpec((tm, tn), lambda j, m, n, k: (m, j * nb + n)))
        args.append(residual)
    in_specs += [_ANY] * len(deps)
    args += list(deps)
    return pl.pallas_call(
        body, name=name,
        out_shape=jax.ShapeDtypeStruct((M, J * Nj), out_dtype),
        grid=(J, M // tm, nb, nk),
        in_specs=in_specs,
        out_specs=pl.BlockSpec((tm, tn), lambda j, m, n, k: (m, j * nb + n)),
        scratch_shapes=[pltpu.VMEM((tm, tn), F32)],
        compiler_params=_params(("parallel", "parallel", "parallel", "arbitrary")),
    )(*args)


def _mm_nt(a, b, out_dtype, name, deps=(), tm=1024, tp=1024, tq=2048):
    M, Q = a.shape
    J, P, Qj = b.shape
    assert Q == J * Qj
    tm, tp, tq = _tile(M, tm), _tile(P, tp), _tile(Qj, tq)
    nq = Qj // tq

    def body(*refs):
        a_ref, b_ref = refs[:2]
        o_ref, acc = refs[2 + len(deps):]

        def finish(r):
            o_ref[...] = r.astype(out_dtype)

        _accumulate(acc, _dot(a_ref[...], b_ref[...], NT), pl.program_id(2) * nq + pl.program_id(3), J * nq, finish)

    return pl.pallas_call(
        body, name=name,
        out_shape=jax.ShapeDtypeStruct((M, P), out_dtype),
        grid=(M // tm, P // tp, J, nq),
        in_specs=[pl.BlockSpec((tm, tq), lambda m, p, j, q: (m, j * nq + q)),
                  pl.BlockSpec((None, tp, tq), lambda m, p, j, q: (j, p, q))] + [_ANY] * len(deps),
        out_specs=pl.BlockSpec((tm, tp), lambda m, p, j, q: (m, p)),
        scratch_shapes=[pltpu.VMEM((tm, tp), F32)],
        compiler_params=_params(("parallel", "parallel", "arbitrary", "arbitrary")),
    )(a, b, *deps)


def _mm_tn(a, b, J, out_dtype, name, tm=1024, tn=1024, tr=2048):
    R, M = a.shape
    R2, N = b.shape
    assert R == R2 and N % J == 0
    Nj = N // J
    tm, tn, tr = _tile(M, tm), _tile(Nj, tn), _tile(R, tr)
    nb, nr = Nj // tn, R // tr

    def body(a_ref, b_ref, o_ref, acc):
        def finish(r):
            o_ref[...] = r.astype(out_dtype)

        _accumulate(acc, _dot(a_ref[...], b_ref[...], TN), pl.program_id(3), nr, finish)

    return pl.pallas_call(
        body, name=name,
        out_shape=jax.ShapeDtypeStruct((J, M, Nj), out_dtype),
        grid=(J, M // tm, nb, nr),
        in_specs=[pl.BlockSpec((tr, tm), lambda j, m, n, r: (r, m)),
                  pl.BlockSpec((tr, tn), lambda j, m, n, r: (r, j * nb + n))],
        out_specs=pl.BlockSpec((None, tm, tn), lambda j, m, n, r: (j, m, n)),
        scratch_shapes=[pltpu.VMEM((tm, tn), F32)],
        compiler_params=_params(("parallel", "parallel", "parallel", "arbitrary")),
    )(a, b)


def _ew(fn, ins, vecs, out_dtypes, n_acc, name, tr=256, tc=1024):
    T, C = ins[0].shape
    tr, tc = _tile(T, tr), _tile(C, tc)
    n_in, n_vec, n_out = len(ins), len(vecs), len(out_dtypes)

    def body(*refs):
        in_refs = refs[:n_in + n_vec]
        out_refs = refs[n_in + n_vec:n_in + n_vec + n_out]
        acc_refs = refs[n_in + n_vec + n_out:]
        res = fn(*[r[...] for r in in_refs])
        for o_ref, v in zip(out_refs, res[:n_out]):
            o_ref[...] = v.astype(o_ref.dtype)
        if n_acc:
            r = pl.program_id(1)

            @pl.when(r == 0)
            def _():
                for a_ref in acc_refs:
                    a_ref[...] = jnp.zeros_like(a_ref)

            for a_ref, v in zip(acc_refs, res[n_out:]):
                a_ref[...] += v

    blk = pl.BlockSpec((tr, tc), lambda c, r: (r, c))
    vec = pl.BlockSpec((1, tc), lambda c, r: (0, c))
    out = pl.pallas_call(
        body, name=name,
        out_shape=tuple([jax.ShapeDtypeStruct((T, C), d) for d in out_dtypes]
                        + [jax.ShapeDtypeStruct((1, C), F32)] * n_acc),
        grid=(C // tc, T // tr),
        in_specs=[blk] * n_in + [vec] * n_vec,
        out_specs=tuple([blk] * n_out + [vec] * n_acc),
        compiler_params=_params(("parallel", "arbitrary")),
    )(*ins, *vecs)
    return out


def _cast_bf16(w, name):
    return _ew(lambda a: (a,), [w], [], [BF16], 0, name)[0]


def _rmsnorm_fwd(x, g, name, deps=(), tr=128):
    T, D = x.shape
    tr = _tile(T, tr)

    def body(*refs):
        x_ref, g_ref, h_ref = refs[0], refs[1], refs[2 + len(deps)]
        xv = x_ref[...]
        r = lax.rsqrt(jnp.mean(xv * xv, axis=-1, keepdims=True) + RMS_EPS)
        h_ref[...] = ((xv * r) * g_ref[...]).astype(BF16)

    return pl.pallas_call(
        body, name=name,
        out_shape=jax.ShapeDtypeStruct((T, D), BF16),
        grid=(T // tr,),
        in_specs=[pl.BlockSpec((tr, D), lambda i: (i, 0)), pl.BlockSpec((1, D), lambda i: (0, 0))] + [_ANY] * len(deps),
        out_specs=pl.BlockSpec((tr, D), lambda i: (i, 0)),
        compiler_params=_params(("parallel",)),
    )(x, g, *deps)


def _rmsnorm_bwd(x, dh, dres, g, name, tr=128):
    T, D = x.shape
    tr = _tile(T, tr)

    def body(x_ref, dh_ref, dres_ref, g_ref, dx_ref, dxb_ref, dg_ref):
        xv = x_ref[...]
        r = lax.rsqrt(jnp.mean(xv * xv, axis=-1, keepdims=True) + RMS_EPS)
        xn = xv * r
        dhv = dh_ref[...].astype(F32)
        dxn = dhv * g_ref[...]
        dx = dres_ref[...] + r * (dxn - xn * jnp.mean(dxn * xn, axis=-1, keepdims=True))
        dx_ref[...] = dx
        dxb_ref[...] = dx.astype(BF16)

        @pl.when(pl.program_id(0) == 0)
        def _():
            dg_ref[...] = jnp.zeros_like(dg_ref)

        dg_ref[...] += jnp.sum(dhv * xn, axis=0, keepdims=True)

    blk = pl.BlockSpec((tr, D), lambda i: (i, 0))
    vec = pl.BlockSpec((1, D), lambda i: (0, 0))
    return pl.pallas_call(
        body, name=name,
        out_shape=(jax.ShapeDtypeStruct((T, D), F32), jax.ShapeDtypeStruct((T, D), BF16),
                   jax.ShapeDtypeStruct((1, D), F32)),
        grid=(T // tr,),
        in_specs=[blk, blk, blk, vec],
        out_specs=(blk, blk, vec),
        compiler_params=_params(("arbitrary",)),
    )(x, dh, dres, g)


def _loss_head(y, target, name, tr=128):
    T, D = y.shape
    tr = _tile(T, tr)
    n = T // tr

    def body(y_ref, t_ref, dy_ref, dyb_ref, loss_ref, acc):
        i = pl.program_id(0)

        @pl.when(i == 0)
        def _():
            acc[...] = jnp.zeros_like(acc)

        err = y_ref[...] - t_ref[...]
        dy = err * (1.0 / D)
        dy_ref[...] = dy
        dyb_ref[...] = dy.astype(BF16)
        acc[...] += jnp.sum(err * err, axis=0, keepdims=True)

        @pl.when(i == n - 1)
        def _():
            loss_ref[...] = jnp.sum(acc[...], axis=1, keepdims=True) * (0.5 / D)

    blk = pl.BlockSpec((tr, D), lambda i: (i, 0))
    return pl.pallas_call(
        body, name=name,
        out_shape=(jax.ShapeDtypeStruct((T, D), F32), jax.ShapeDtypeStruct((T, D), BF16),
                   jax.ShapeDtypeStruct((1, 1), F32)),
        grid=(n,),
        in_specs=[blk, blk],
        out_specs=(blk, blk, pl.BlockSpec((1, 1), lambda i: (0, 0))),
        scratch_shapes=[pltpu.VMEM((1, D), F32)],
        compiler_params=_params(("arbitrary",)),
    )(y, target)


def _gelu(x):
    t = jnp.tanh(GELU_C * (x + 0.044715 * (x * x * x)))
    return x * (0.5 * (1.0 + t)), t


def _adamw(parts, w, m, v, name, tr=256, tc=1024):
    n, R, C = parts.shape
    tr, tc = _tile(R, tr), _tile(C, tc)
    c1 = 1.0 - ADAM_B1 ** ADAM_STEP
    c2 = 1.0 - ADAM_B2 ** ADAM_STEP

    def body(p_ref, w_ref, m_ref, v_ref, g_out, d_out, m_out, v_out):
        g = p_ref[0].astype(F32)
        for k in range(1, n):
            g = g + p_ref[k].astype(F32)
        mn = ADAM_B1 * m_ref[...] + (1.0 - ADAM_B1) * g
        vn = ADAM_B2 * v_ref[...] + (1.0 - ADAM_B2) * (g * g)
        m_hat = mn / c1
        v_hat = vn / c2
        g_out[...] = g
        d_out[...] = -ADAM_LR * (m_hat / (jnp.sqrt(v_hat) + ADAM_EPS) + ADAM_WD * w_ref[...])
        m_out[...] = mn
        v_out[...] = vn

    blk = pl.BlockSpec((tr, tc), lambda r, c: (r, c))
    return pl.pallas_call(
        body, name=name,
        out_shape=tuple([jax.ShapeDtypeStruct((R, C), F32)] * 4),
        grid=(R // tr, C // tc),
        in_specs=[pl.BlockSpec((n, tr, tc), lambda r, c: (0, r, c)), blk, blk, blk],
        out_specs=(blk, blk, blk, blk),
        compiler_params=_params(("parallel", "parallel")),
    )(parts, w, m, v)


def _sum_parts(parts, name):
    n, R, C = parts.shape

    def body(p_ref, o_ref):
        g = p_ref[0].astype(F32)
        for k in range(1, n):
            g = g + p_ref[k].astype(F32)
        o_ref[...] = g

    return pl.pallas_call(
        body, name=name,
        out_shape=jax.ShapeDtypeStruct((R, C), F32),
        compiler_params=pltpu.CompilerParams(vmem_limit_bytes=VMEM_LIMIT),
    )(parts)


def _head_norm(xv):
    xv = xv.astype(F32)
    r = lax.rsqrt(jnp.mean(xv * xv, axis=-1, keepdims=True) + RMS_EPS)
    return xv * r, r


FWD_HEADS = 4
BWD_HEADS = 2


def _attn_specs(S, H, HP):
    def spec(part):
        return pl.BlockSpec((S, HP * HEAD_DIM), lambda b, h: (b, part * (H // HP) + h))
    return spec


def _lanes(hh):
    return slice(hh * HEAD_DIM, (hh + 1) * HEAD_DIM)


def _attn_fwd(proj, q_g, k_g, B, S, H, name):
    TQ = _tile(S, 256)
    nq = S // TQ
    scale = 1.0 / math.sqrt(HEAD_DIM)
    HP = FWD_HEADS
    ATT_W = HP * HEAD_DIM
    heads = range(HP)

    def body(q_ref, k_ref, v_ref, gate_ref, qg_ref, kg_ref, og_ref, o_ref, qn_s, kn_s):
        for hh in heads:
            qn_s[:, _lanes(hh)] = (_head_norm(q_ref[:, _lanes(hh)])[0] * qg_ref[...]).astype(BF16)
            kn_s[:, _lanes(hh)] = (_head_norm(k_ref[:, _lanes(hh)])[0] * kg_ref[...]).astype(BF16)
        row = lax.broadcasted_iota(jnp.int32, (TQ, TQ), 0)
        col = lax.broadcasted_iota(jnp.int32, (TQ, TQ), 1)
        later = _ones_where(row > col)
        causal = col < row

        def q_block(qi, _):
            q0 = pl.multiple_of(qi * TQ, TQ)

            def both(ki, state, diag):
                k0 = pl.multiple_of(ki * TQ, TQ)
                z = [_dot(qn_s[pl.ds(q0, TQ), _lanes(hh)], kn_s[pl.ds(k0, TQ), _lanes(hh)], NT) * scale
                     for hh in heads]
                ls = [_log_sigmoid(zz) for zz in z]
                l1m = [a - zz for a, zz in zip(ls, z)]
                if diag:
                    l1m = [jnp.where(causal, a, 0.0) for a in l1m]
                suffix = [_dot(l1m[hh].astype(BF16), later, NN) + state[hh][0] for hh in heads]
                w = [jnp.exp(a + sfx) for a, sfx in zip(ls, suffix)]
                if diag:
                    w = [jnp.where(causal, a, 0.0) for a in w]
                acc = [state[hh][1] + _dot(w[hh].astype(BF16), v_ref[pl.ds(k0, TQ), _lanes(hh)], NN)
                       for hh in heads]
                return tuple((state[hh][0] + jnp.sum(l1m[hh], axis=1, keepdims=True), acc[hh]) for hh in heads)

            zero = (jnp.zeros((TQ, 1), F32), jnp.zeros((TQ, HEAD_DIM), F32))
            state = both(qi, (zero,) * HP, True)
            state = lax.fori_loop(0, qi, lambda i, st: both(qi - 1 - i, st, False), state)
            for hh in heads:
                acc = state[hh][1]
                o_ref[pl.ds(q0, TQ), _lanes(hh)] = acc.astype(BF16)
                gate = gate_ref[pl.ds(q0, TQ), _lanes(hh)].astype(F32)
                og_ref[pl.ds(q0, TQ), _lanes(hh)] = (acc * (gate * _sigmoid(gate))).astype(BF16)
            return 0

        lax.fori_loop(0, nq, q_block, 0)

    spec = _attn_specs(S, H, HP)
    vec = pl.BlockSpec((1, HEAD_DIM), lambda b, h: (0, 0))
    out = pl.BlockSpec((S, ATT_W), lambda b, h: (b, h))
    return pl.pallas_call(
        body, name=name,
        out_shape=(jax.ShapeDtypeStruct((B * S, H * HEAD_DIM), BF16),) * 2,
        grid=(B, H // HP),
        in_specs=[spec(0), spec(1), spec(2), spec(3), vec, vec],
        out_specs=(out, out),
        scratch_shapes=[pltpu.VMEM((S, ATT_W), BF16)] * 2,
        compiler_params=_params(("parallel", "parallel")),
    )(proj, proj, proj, proj, q_g, k_g)


def _attn_bwd(proj, o, dog, q_g, k_g, B, S, H, name):
    TQ = _tile(S, 256)
    nq = S // TQ
    scale = 1.0 / math.sqrt(HEAD_DIM)
    HP = BWD_HEADS
    ATT_W = HP * HEAD_DIM
    heads = range(HP)

    def body(q_ref, k_ref, v_ref, gate_ref, o_ref, dog_ref, qg_ref, kg_ref,
             dq_ref, dk_ref, dv_ref, dgate_ref, dqg_ref, dkg_ref,
             qn_s, kn_s, do_s, dqn_s, dkn_s, dv_s, w_s, sg_s):
        first = (pl.program_id(0) == 0) & (pl.program_id(1) == 0)

        @pl.when(first)
        def _():
            dqg_ref[...] = jnp.zeros_like(dqg_ref)
            dkg_ref[...] = jnp.zeros_like(dkg_ref)

        for hh in heads:
            qn_s[:, _lanes(hh)] = (_head_norm(q_ref[:, _lanes(hh)])[0] * qg_ref[...]).astype(BF16)
            kn_s[:, _lanes(hh)] = (_head_norm(k_ref[:, _lanes(hh)])[0] * kg_ref[...]).astype(BF16)
        gate = gate_ref[...].astype(F32)
        sg = _sigmoid(gate)
        dog_v = dog_ref[...].astype(F32)
        do_s[...] = (dog_v * (gate * sg)).astype(BF16)
        dgate_ref[...] = (dog_v * o_ref[...].astype(F32) * (sg * (1.0 + gate * (1.0 - sg)))).astype(BF16)
        dkn_s[...] = jnp.zeros_like(dkn_s)
        dv_s[...] = jnp.zeros_like(dv_s)

        row = lax.broadcasted_iota(jnp.int32, (TQ, TQ), 0)
        col = lax.broadcasted_iota(jnp.int32, (TQ, TQ), 1)
        later = _ones_where(row > col)
        earlier = _ones_where(row < col)
        causal = col < row

        def q_block(qi, _):
            q0 = pl.multiple_of(qi * TQ, TQ)

            def weights_both(ki, carries, diag):
                k0 = pl.multiple_of(ki * TQ, TQ)
                z = [_dot(qn_s[pl.ds(q0, TQ), _lanes(hh)], kn_s[pl.ds(k0, TQ), _lanes(hh)], NT) * scale
                     for hh in heads]
                ls = [_log_sigmoid(zz) for zz in z]
                l1m = [a - zz for a, zz in zip(ls, z)]
                if diag:
                    l1m = [jnp.where(causal, a, 0.0) for a in l1m]
                suffix = [_dot(l1m[hh].astype(BF16), later, NN) + carries[hh] for hh in heads]
                for hh in heads:
                    w = jnp.exp(ls[hh] + suffix[hh])
                    if diag:
                        w = jnp.where(causal, w, 0.0)
                    w_s[hh, ki] = w
                    sg_s[hh, ki] = jnp.exp(ls[hh])
                return tuple(carries[hh] + jnp.sum(l1m[hh], axis=1, keepdims=True) for hh in heads)

            carries = weights_both(qi, (jnp.zeros((TQ, 1), F32),) * HP, True)
            lax.fori_loop(0, qi, lambda i, c: weights_both(qi - 1 - i, c, False), carries)

            def grads_both(ki, state, diag):
                k0 = pl.multiple_of(ki * TQ, TQ)
                qb = [qn_s[pl.ds(q0, TQ), _lanes(hh)] for hh in heads]
                dob = [do_s[pl.ds(q0, TQ), _lanes(hh)] for hh in heads]
                w = [w_s[hh, ki] for hh in heads]
                da = [_dot(dob[hh], v_ref[pl.ds(k0, TQ), _lanes(hh)], NT) * w[hh] for hh in heads]
                for hh in heads:
                    dv_s[pl.ds(k0, TQ), _lanes(hh)] += _dot(w[hh].astype(BF16), dob[hh], TN)
                prefix = [_dot(da[hh].astype(BF16), earlier, NN) + state[hh][0] for hh in heads]
                dzb = []
                for hh in heads:
                    sgz = sg_s[hh, ki]
                    dz = da[hh] * (1.0 - sgz) - sgz * prefix[hh]
                    if diag:
                        dz = jnp.where(causal, dz, 0.0)
                    dzb.append((dz * scale).astype(BF16))
                dq = [state[hh][1] + _dot(dzb[hh], kn_s[pl.ds(k0, TQ), _lanes(hh)], NN) for hh in heads]
                for hh in heads:
                    dkn_s[pl.ds(k0, TQ), _lanes(hh)] += _dot(dzb[hh], qb[hh], TN)
                return tuple((state[hh][0] + jnp.sum(da[hh], axis=1, keepdims=True), dq[hh]) for hh in heads)

            zero = (jnp.zeros((TQ, 1), F32), jnp.zeros((TQ, HEAD_DIM), F32))
            state = lax.fori_loop(0, qi, lambda i, st: grads_both(i, st, False), (zero,) * HP)
            state = grads_both(qi, state, True)
            for hh in heads:
                dqn_s[pl.ds(q0, TQ), _lanes(hh)] = state[hh][1]
            return 0

        lax.fori_loop(0, nq, q_block, 0)

        def norm_bwd(x_ref, g_ref, dn_s, dx_ref, dg_ref):
            for hh in heads:
                xh, r = _head_norm(x_ref[:, _lanes(hh)])
                dn = dn_s[:, _lanes(hh)]
                dg_ref[...] += jnp.sum(dn * xh, axis=0, keepdims=True)
                dxh = dn * g_ref[...]
                dx_ref[:, _lanes(hh)] = (r * (dxh - xh * jnp.mean(dxh * xh, axis=-1, keepdims=True))).astype(BF16)

        norm_bwd(q_ref, qg_ref, dqn_s, dq_ref, dqg_ref)
        norm_bwd(k_ref, kg_ref, dkn_s, dk_ref, dkg_ref)
        dv_ref[...] = dv_s[...].astype(BF16)

    spec = _attn_specs(S, H, HP)
    vec = pl.BlockSpec((1, HEAD_DIM), lambda b, h: (0, 0))
    blk = pl.BlockSpec((S, ATT_W), lambda b, h: (b, h))
    big = jax.ShapeDtypeStruct((B * S, H * HEAD_DIM), BF16)
    small = jax.ShapeDtypeStruct((1, HEAD_DIM), F32)
    return pl.pallas_call(
        body, name=name,
        out_shape=(big, big, big, big, small, small),
        grid=(B, H // HP),
        in_specs=[spec(0), spec(1), spec(2), spec(3), blk, blk, vec, vec],
        out_specs=(blk, blk, blk, blk, vec, vec),
        scratch_shapes=[pltpu.VMEM((S, ATT_W), BF16)] * 3 + [pltpu.VMEM((S, ATT_W), F32)] * 3
        + [pltpu.VMEM((HP, nq, TQ, TQ), F32)] * 2,
        compiler_params=_params(("arbitrary", "arbitrary")),
    )(proj, proj, proj, proj, o, dog, q_g, k_g)


HALF = GROUPS_PER_BLOCK * STATE


def _cmul(ar, ai, br, bi):
    return ar * br - ai * bi, ar * bi + ai * br


def _cpow(ar, ai, n):
    rr = ri = None
    while n:
        if n & 1:
            rr, ri = (ar, ai) if rr is None else _cmul(rr, ri, ar, ai)
        n >>= 1
        if n:
            ar, ai = _cmul(ar, ai, ar, ai)
    return rr, ri


def _segment_carry(er, ei, lr, li, seg_len, segs_per_seq, reverse):
    Lr, Li = _cpow(lr, li, seg_len)
    pos = lax.broadcasted_iota(jnp.int32, er.shape, 0) % segs_per_seq
    outr = jnp.zeros_like(er)
    outi = jnp.zeros_like(ei)
    pr = pi = None
    for d in range(1, segs_per_seq):
        shift = (SUBLANES - d) if reverse else d
        sr = pltpu.roll(er, shift, 0)
        si = pltpu.roll(ei, shift, 0)
        ok = (pos + d < segs_per_seq) if reverse else (pos >= d)
        sr = jnp.where(ok, sr, 0.0)
        si = jnp.where(ok, si, 0.0)
        if pr is not None:
            sr, si = _cmul(sr, si, pr, pi)
        outr = outr + sr
        outi = outi + si
        pr, pi = (Lr, Li) if pr is None else _cmul(pr, pi, Lr, Li)
    return outr, outi


def _s5_sizes(T, B):
    assert SUBLANES % B == 0
    segs_per_seq = SUBLANES // B
    n_steps = T // SUBLANES
    cj = _tile(n_steps, 64)
    return segs_per_seq, n_steps, cj


def _s5_fwd(u_p, wb, wc, lam, dvec, B, name):
    T, C = u_p.shape
    nb = C // 128
    segs_per_seq, n_steps, cj = _s5_sizes(T, B)
    n_chunks = n_steps // cj
    rows = cj * SUBLANES

    def body(u_ref, wb_ref, wc_ref, lam_ref, d_ref, y_ref, hin_ref, bu_s, h_s):
        lr = jnp.broadcast_to(lam_ref[:, :HALF], (SUBLANES, HALF))
        li = jnp.broadcast_to(lam_ref[:, HALF:], (SUBLANES, HALF))

        def scan_chunk(c, hr, hi, store):
            r0 = pl.multiple_of(c * rows, rows)
            bu_s[...] = _dot(u_ref[pl.ds(r0, rows), :], wb_ref[...], NN)

            def step(j, carry):
                hr, hi = carry
                o = pl.multiple_of(j * SUBLANES, SUBLANES)
                nr = lr * hr - li * hi + bu_s[pl.ds(o, SUBLANES), :HALF]
                ni = lr * hi + li * hr + bu_s[pl.ds(o, SUBLANES), HALF:]
                if store:
                    h_s[pl.ds(o, SUBLANES), :HALF] = nr
                    h_s[pl.ds(o, SUBLANES), HALF:] = ni
                return nr, ni

            hr, hi = lax.fori_loop(0, cj, step, (hr, hi))
            if store:
                uv = u_ref[pl.ds(r0, rows), :].astype(F32)
                y_ref[pl.ds(r0, rows), :] = _dot(h_s[...].astype(BF16), wc_ref[...], NN) + d_ref[...] * uv
            return hr, hi

        zero = jnp.zeros((SUBLANES, HALF), F32)
        er, ei = lax.fori_loop(0, n_chunks, lambda c, h: scan_chunk(c, h[0], h[1], False), (zero, zero))
        h0r, h0i = _segment_carry(er, ei, lr, li, n_steps, segs_per_seq, False)
        hin_ref[:, :HALF] = h0r
        hin_ref[:, HALF:] = h0i
        lax.fori_loop(0, n_chunks, lambda c, h: scan_chunk(c, h[0], h[1], True), (h0r, h0i))

    return pl.pallas_call(
        body, name=name,
        out_shape=(jax.ShapeDtypeStruct((T, C), F32), jax.ShapeDtypeStruct((nb, SUBLANES, 2 * HALF), F32)),
        grid=(nb,),
        in_specs=[pl.BlockSpec((T, 128), lambda g: (0, g)),
                  pl.BlockSpec((None, 128, 2 * HALF), lambda g: (g, 0, 0)),
                  pl.BlockSpec((None, 2 * HALF, 128), lambda g: (g, 0, 0)),
                  pl.BlockSpec((None, 1, 2 * HALF), lambda g: (g, 0, 0)),
                  pl.BlockSpec((1, 128), lambda g: (0, g))],
        out_specs=(pl.BlockSpec((T, 128), lambda g: (0, g)),
                   pl.BlockSpec((None, SUBLANES, 2 * HALF), lambda g: (g, 0, 0))),
        scratch_shapes=[pltpu.VMEM((rows, 2 * HALF), F32)] * 2,
        compiler_params=_params(("parallel",)),
    )(u_p, wb, wc, lam, dvec)


def _s5_bwd(u_p, dy_p, wb, wbt, wc, wct, lam, dvec, h_in, B, name):
    T, C = u_p.shape
    nb = C // 128
    segs_per_seq, n_steps, cj = _s5_sizes(T, B)
    n_chunks = n_steps // cj
    rows = cj * SUBLANES

    def body(u_ref, dy_ref, wb_ref, wbt_ref, wc_ref, wct_ref, lam_ref, d_ref, hin_ref,
             du_ref, dwb_ref, dwc_ref, dlam_ref, dd_ref, h_all, x_s, g_s):
        lr = jnp.broadcast_to(lam_ref[:, :HALF], (SUBLANES, HALF))
        li = jnp.broadcast_to(lam_ref[:, HALF:], (SUBLANES, HALF))
        zero = jnp.zeros((SUBLANES, HALF), F32)

        h_all[pl.ds(0, SUBLANES), :] = hin_ref[...]

        def fwd_chunk(c, carry):
            r0 = pl.multiple_of(c * rows, rows)
            x_s[...] = _dot(u_ref[pl.ds(r0, rows), :], wb_ref[...], NN)

            def step(j, carry):
                hr, hi = carry
                o = pl.multiple_of(j * SUBLANES, SUBLANES)
                nr = lr * hr - li * hi + x_s[pl.ds(o, SUBLANES), :HALF]
                ni = lr * hi + li * hr + x_s[pl.ds(o, SUBLANES), HALF:]
                late = pl.multiple_of(r0 + o + SUBLANES, SUBLANES)
                h_all[pl.ds(late, SUBLANES), :HALF] = nr
                h_all[pl.ds(late, SUBLANES), HALF:] = ni
                return nr, ni

            return lax.fori_loop(0, cj, step, carry)

        lax.fori_loop(0, n_chunks, fwd_chunk, (hin_ref[:, :HALF], hin_ref[:, HALF:]))

        def bwd_chunk(i, carry, store):
            c = n_chunks - 1 - i
            r0 = pl.multiple_of(c * rows, rows)
            dyv = dy_ref[pl.ds(r0, rows), :]
            x_s[...] = _dot(dyv, wct_ref[...], NN)

            def step(jj, carry):
                ar, ai, accr, acci = carry
                j = cj - 1 - jj
                o = pl.multiple_of(j * SUBLANES, SUBLANES)
                nr = lr * ar + li * ai + x_s[pl.ds(o, SUBLANES), :HALF]
                ni = lr * ai - li * ar + x_s[pl.ds(o, SUBLANES), HALF:]
                if store:
                    g_s[pl.ds(o, SUBLANES), :HALF] = nr
                    g_s[pl.ds(o, SUBLANES), HALF:] = ni
                    prev = pl.multiple_of(r0 + o, SUBLANES)
                    pr = h_all[pl.ds(prev, SUBLANES), :HALF]
                    pi = h_all[pl.ds(prev, SUBLANES), HALF:]
                    accr = accr + nr * pr + ni * pi
                    acci = acci + ni * pr - nr * pi
                return nr, ni, accr, acci

            carry = lax.fori_loop(0, cj, step, carry)
            if store:
                gb = g_s[...].astype(BF16)
                uv = u_ref[pl.ds(r0, rows), :]
                dyf = dyv.astype(F32)
                du_ref[pl.ds(r0, rows), :] = (_dot(gb, wbt_ref[...], NN) + d_ref[...] * dyf).astype(BF16)
                dwb_ref[...] += _dot(uv, gb, TN)
                hb = h_all[pl.ds(pl.multiple_of(r0 + SUBLANES, SUBLANES), rows), :].astype(BF16)
                dwc_ref[...] += _dot(hb, dyv, TN)
                dd_ref[...] += jnp.sum(dyf * uv.astype(F32), axis=0, keepdims=True)
            return carry

        er, ei, _, _ = lax.fori_loop(0, n_chunks, lambda i, c: bwd_chunk(i, c, False), (zero, zero, zero, zero))
        a0r, a0i = _segment_carry(er, ei, lr, -li, n_steps, segs_per_seq, True)
        dwb_ref[...] = jnp.zeros_like(dwb_ref)
        dwc_ref[...] = jnp.zeros_like(dwc_ref)
        dd_ref[...] = jnp.zeros_like(dd_ref)
        _, _, accr, acci = lax.fori_loop(0, n_chunks, lambda i, c: bwd_chunk(i, c, True), (a0r, a0i, zero, zero))
        dlam_ref[:, :HALF] = jnp.sum(accr, axis=0, keepdims=True)
        dlam_ref[:, HALF:] = jnp.sum(acci, axis=0, keepdims=True)

    col = pl.BlockSpec((T, 128), lambda g: (0, g))
    vec = pl.BlockSpec((1, 128), lambda g: (0, g))

    def per_block(*shape):
        return pl.BlockSpec((None,) + shape, lambda g: (g, 0, 0))

    return pl.pallas_call(
        body, name=name,
        out_shape=(jax.ShapeDtypeStruct((T, C), BF16),
                   jax.ShapeDtypeStruct((nb, 128, 2 * HALF), F32),
                   jax.ShapeDtypeStruct((nb, 2 * HALF, 128), F32),
                   jax.ShapeDtypeStruct((nb, 1, 2 * HALF), F32),
                   jax.ShapeDtypeStruct((1, C), F32)),
        grid=(nb,),
        in_specs=[col, col, per_block(128, 2 * HALF), per_block(2 * HALF, 128), per_block(2 * HALF, 128),
                  per_block(128, 2 * HALF), per_block(1, 2 * HALF), vec, per_block(SUBLANES, 2 * HALF)],
        out_specs=(col, per_block(128, 2 * HALF), per_block(2 * HALF, 128), per_block(1, 2 * HALF), vec),
        scratch_shapes=[pltpu.VMEM((T + SUBLANES, 2 * HALF), F32),
                        pltpu.VMEM((rows, 2 * HALF), F32), pltpu.VMEM((rows, 2 * HALF), F32)],
        compiler_params=_params(("parallel",)),
    )(u_p, dy_p, wb, wbt, wc, wct, lam, dvec, h_in)


def _discretize(a_re, a_im, log_dt, b_re, b_im):
    dt = jnp.exp(log_dt)[:, None]
    mag = jnp.exp(a_re * dt)
    lam_re = mag * jnp.cos(a_im * dt)
    lam_im = mag * jnp.sin(a_im * dt)
    den = a_re * a_re + a_im * a_im
    f_re = ((lam_re - 1.0) * a_re + lam_im * a_im) / den
    f_im = (lam_im * a_re - (lam_re - 1.0) * a_im) / den
    bb_re = f_re[..., None] * b_re - f_im[..., None] * b_im
    bb_im = f_re[..., None] * b_im + f_im[..., None] * b_re
    return lam_re, lam_im, bb_re, bb_im


def _block_diag_in(bb_re, bb_im):
    eye = jnp.eye(GROUPS_PER_BLOCK, dtype=F32)

    def one(bb):
        t = bb.reshape(-1, GROUPS_PER_BLOCK, STATE, GROUP)
        return jnp.einsum('gapi,ab->gaibp', t, eye).reshape(-1, 128, HALF)

    return jnp.concatenate([one(bb_re), one(bb_im)], axis=-1)


def _block_diag_in_grad(dwb):
    eye = jnp.eye(GROUPS_PER_BLOCK, dtype=F32)

    def one(d):
        t = d.reshape(-1, GROUPS_PER_BLOCK, GROUP, GROUPS_PER_BLOCK, STATE)
        return jnp.einsum('gaibp,ab->gapi', t, eye).reshape(-1, STATE, GROUP)

    return one(dwb[..., :HALF]), one(dwb[..., HALF:])


def _block_diag_out(c_re, c_im):
    eye = jnp.eye(GROUPS_PER_BLOCK, dtype=F32)

    def one(cc):
        t = cc.reshape(-1, GROUPS_PER_BLOCK, GROUP, STATE)
        return jnp.einsum('gaip,ab->gbpai', t, eye).reshape(-1, HALF, 128)

    return jnp.concatenate([one(c_re), -one(c_im)], axis=1)


def _block_diag_out_grad(dwc):
    eye = jnp.eye(GROUPS_PER_BLOCK, dtype=F32)

    def one(d):
        t = d.reshape(-1, GROUPS_PER_BLOCK, STATE, GROUPS_PER_BLOCK, GROUP)
        return jnp.einsum('gbpai,ab->gaip', t, eye).reshape(-1, GROUP, STATE)

    return one(dwc[:, :HALF]), -one(dwc[:, HALF:])


def _pack(parts):
    return jnp.concatenate([p.reshape(-1, PACK_W) for p in parts], axis=0)


def _unpack(buf, shapes):
    lead = buf.shape[:-2]
    out, r = [], 0
    for s in shapes:
        n = math.prod(s) // PACK_W
        out.append(buf[..., r:r + n, :].reshape(lead + tuple(s)))
        r += n
    return out


def _permute_rows(a, n_steps):
    T, C = a.shape
    return a.reshape(SUBLANES, n_steps, C).transpose(1, 0, 2).reshape(T, C)


def _unpermute_rows(a, n_steps):
    T, C = a.shape
    return a.reshape(n_steps, SUBLANES, C).transpose(1, 0, 2).reshape(T, C)


def kernel(x, norm_g, attn_w_in, attn_q_g, attn_k_g, attn_w_out, ssm_w_in, ssm_A_re, ssm_A_im, ssm_log_dt, ssm_B_re, ssm_B_im, ssm_C_re, ssm_C_im, ssm_D, ssm_glu_w, ssm_glu_b, ssm_w_out, loss_target, m_norm_g, m_attn_w_in, m_attn_q_g, m_attn_k_g, m_attn_w_out, m_ssm_w_in, m_ssm_A_re, m_ssm_A_im, m_ssm_log_dt, m_ssm_B_re, m_ssm_B_im, m_ssm_C_re, m_ssm_C_im, m_ssm_D, m_ssm_glu_w, m_ssm_glu_b, m_ssm_w_out, v_norm_g, v_attn_w_in, v_attn_q_g, v_attn_k_g, v_attn_w_out, v_ssm_w_in, v_ssm_A_re, v_ssm_A_im, v_ssm_log_dt, v_ssm_B_re, v_ssm_B_im, v_ssm_C_re, v_ssm_C_im, v_ssm_D, v_ssm_glu_w, v_ssm_glu_b, v_ssm_w_out):
    B, S, D = x.shape
    T = B * S
    H = D // HEAD_DIM
    G_loc = ssm_A_re.shape[1]
    G = G_loc * N_DEV
    n_steps = T // SUBLANES
    me, chip = _my_index(), _my_chip()
    xf = x.reshape(T, D)
    target = loss_target.reshape(T, D)

    c_in0 = _place(attn_w_in[0], me, N_DEV, BF16, "cast_attn_w_in")
    c_out0 = _place(attn_w_out[0], me, N_DEV, BF16, "cast_attn_w_out")
    c_in1 = _place(ssm_w_in[0], me, N_DEV, BF16, "cast_ssm_w_in")
    c_glu = _place(ssm_glu_w[0], me, N_DEV, BF16, "cast_ssm_glu_w")
    c_out1 = _place(ssm_w_out[0], me, N_DEV, BF16, "cast_ssm_w_out")

    small_shapes = [(G_loc, STATE), (G_loc, STATE), (G_loc, STATE, GROUP), (G_loc, STATE, GROUP),
                    (G_loc, GROUP, STATE), (G_loc, GROUP, STATE), (G_loc * GROUP,), (G_loc * GROUP,)]
    disc_in = (ssm_A_re[0], ssm_A_im[0], ssm_log_dt[0], ssm_B_re[0], ssm_B_im[0])
    (lam_re, lam_im, bb_re, bb_im), disc_vjp = jax.vjp(_discretize, *disc_in)
    small = _pack([lam_re, lam_im, bb_re, bb_im, ssm_C_re[0], ssm_C_im[0], ssm_D[0], ssm_glu_b[0]])
    small_all = _exchange(small, False, "gather_small")
    w_in0 = _gather_two_level(c_in0, "gather_attn_w_in", deps=(small_all,))
    lam_re_a, lam_im_a, bb_re_a, bb_im_a, c_re_a, c_im_a, d_a, glu_b_a = [
        t.reshape((G,) + t.shape[2:]) if t.ndim > 2 else t.reshape(-1)
        for t in _unpack(small_all, small_shapes)]
    wb = _block_diag_in(bb_re_a, bb_im_a)
    wc = _block_diag_out(c_re_a, c_im_a)
    wb_b, wc_b = wb.astype(BF16), wc.astype(BF16)
    wbt_b, wct_b = wb_b.transpose(0, 2, 1), wc_b.transpose(0, 2, 1)
    lam = jnp.concatenate([lam_re_a.reshape(-1, 1, HALF), lam_im_a.reshape(-1, 1, HALF)], axis=-1)
    d_row = d_a.reshape(1, D)
    glu_b_row = glu_b_a.reshape(1, D)
    g0, g1 = norm_g[0:1], norm_g[1:2]
    q_g, k_g = attn_q_g, attn_k_g

    h0 = _rmsnorm_fwd(xf, g0, "norm0")
    s_out0 = _exchange_start(c_out0, None, "gather", "gather_attn_w_out_start", deps=(w_in0,))
    s_in1 = _exchange_start(c_in1, None, "gather", "gather_ssm_w_in_start", deps=(w_in0,))
    proj0 = _mm_nn(h0, w_in0, BF16, "attn_in", deps=(s_out0[3], s_in1[3]))
    og, o = _attn_fwd(proj0, q_g, k_g, B, S, H, "attn_fwd")
    w_out0 = _exchange_wait(s_out0, og, "gather", "gather_attn_w_out_wait").reshape(1, D, D)
    s_glu = _exchange_start(c_glu, None, "gather", "gather_ssm_glu_w_start", deps=(w_out0,))
    s_out1 = _exchange_start(c_out1, None, "gather", "gather_ssm_w_out_start", deps=(w_out0,))
    x1 = _mm_nn(og, w_out0, F32, "attn_out", residual=xf, deps=(s_glu[3], s_out1[3]))

    h1 = _rmsnorm_fwd(x1, g1, "norm1")
    w_in1 = _exchange_wait(s_in1, h1, "gather", "gather_ssm_w_in_wait")
    proj1 = _mm_nn(h1, w_in1, BF16, "ssm_in")
    u_p = _permute_rows(proj1[:, :D], n_steps)
    gate1 = proj1[:, D:]
    y_p, h_in = _s5_fwd(u_p, wb_b, wc_b, lam, d_row, B, "s5_fwd")
    y_ssm = _unpermute_rows(y_p, n_steps)
    (yg,) = _ew(lambda a: (_gelu(a)[0],), [y_ssm], [], [BF16], 0, "gelu")
    w_glu = _exchange_wait(s_glu, yg, "gather", "gather_ssm_glu_w_wait").reshape(1, D, D)
    z = _mm_nn(yg, w_glu, F32, "glu_in", bias=glu_b_row)

    def glu_fwd(y, zz, gt):
        gt = gt.astype(F32)
        return (_gelu(y)[0] * _sigmoid(zz) * (gt * _sigmoid(gt)),)

    (y3,) = _ew(glu_fwd, [y_ssm, z, gate1], [], [BF16], 0, "glu_gate")
    w_out1 = _exchange_wait(s_out1, y3, "gather", "gather_ssm_w_out_wait").reshape(1, D, D)
    out = _mm_nn(y3, w_out1, F32, "ssm_out", residual=x1)

    dout, dout_b, loss_part = _loss_head(out, target, "loss")
    loss = lax.psum(loss_part[0, 0], ("x", "y", "c"))

    p_w_out1 = _mm_tn(y3, dout_b, 1, BF16, "ssm_out_dw").reshape(N_DEV, D // N_DEV, D)
    sc_out1 = _exchange_start(_place(p_w_out1, me, N_DEV, BF16, "place_ssm_w_out"), p_w_out1, "scatter", "scatter_ssm_w_out_start")
    dy3 = _mm_nt(dout_b, w_out1, F32, "ssm_out_dx", deps=(sc_out1[3],))

    def glu_bwd(d3, y, zz, gt):
        gt = gt.astype(F32)
        sg = _sigmoid(gt)
        sz = _sigmoid(zz)
        ygv, _ = _gelu(y)
        dy2 = d3 * (gt * sg)
        dgate = d3 * (ygv * sz) * (sg * (1.0 + gt * (1.0 - sg)))
        dz = dy2 * ygv * (sz * (1.0 - sz))
        return dz, dgate, dy2 * sz, jnp.sum(dz, axis=0, keepdims=True)

    dz_b, dgate1, dyg_a, dglu_b = _ew(glu_bwd, [dy3, y_ssm, z, gate1], [], [BF16, BF16, F32], 1, "glu_gate_bwd")
    p_w_glu = _mm_tn(yg, dz_b, 1, BF16, "glu_in_dw").reshape(N_DEV, D // N_DEV, D)
    sc_glu = _exchange_start(_place(p_w_glu, me, N_DEV, BF16, "place_ssm_glu_w"), p_w_glu, "scatter", "scatter_ssm_glu_w_start")
    dyg_b = _mm_nt(dz_b, w_glu, F32, "glu_in_dx", deps=(sc_glu[3],))

    def gelu_bwd(da, db, y):
        _, t = _gelu(y)
        dg = 0.5 * (1.0 + t) + 0.5 * y * (1.0 - t * t) * (GELU_C * (1.0 + 3.0 * 0.044715 * (y * y)))
        return ((da + db) * dg,)

    (dy_ssm,) = _ew(gelu_bwd, [dyg_a, dyg_b, y_ssm], [], [BF16], 0, "gelu_bwd")
    dy_p = _permute_rows(dy_ssm, n_steps)
    du_p, dwb, dwc, dlam, dd = _s5_bwd(u_p, dy_p, wb_b, wbt_b, wc_b, wct_b, lam, d_row, h_in, B, "s5_bwd")
    du = _unpermute_rows(du_p, n_steps)
    dproj1 = jnp.concatenate([du, dgate1], axis=1)
    p_w_in1 = _mm_tn(h1, dproj1, N_DEV, BF16, "ssm_in_dw")
    sc_in1 = _exchange_start(_place(p_w_in1, me, N_DEV, BF16, "place_ssm_w_in"), p_w_in1, "scatter", "scatter_ssm_w_in_start")
    dh1 = _mm_nt(dproj1, w_in1, F32, "ssm_in_dx", deps=(sc_in1[3],))
    dx1, dx1_b, dg1 = _rmsnorm_bwd(x1, dh1, dout, g1, "norm1_bwd")

    p_w_out0 = _mm_tn(og, dx1_b, 1, BF16, "attn_out_dw").reshape(N_DEV, D // N_DEV, D)
    sc_out0 = _exchange_start(_place(p_w_out0, me, N_DEV, BF16, "place_attn_w_out"), p_w_out0, "scatter", "scatter_attn_w_out_start")
    dog = _mm_nt(dx1_b, w_out0, BF16, "attn_out_dx", deps=(sc_out0[3],))
    dq, dk, dv, dgate0, dqg, dkg = _attn_bwd(proj0, o, dog, q_g, k_g, B, S, H, "attn_bwd")
    dproj0 = jnp.concatenate([dq, dk, dv, dgate0], axis=1)
    p_w_in0 = _mm_tn(h0, dproj0, N_DEV, BF16, "attn_in_dw")
    pair = _exchange_start(lax.empty((N_DEV // 2,) + p_w_in0.shape[1:], BF16), p_w_in0, "pair", "scatter_attn_w_in_pair_start")
    q_w_in0 = _pair_sum(p_w_in0, _exchange_wait(pair, pair[3], "pair", "scatter_attn_w_in_pair_wait"), "scatter_attn_w_in_pair_sum")
    sc_in0 = _exchange_start(_place(q_w_in0, chip, N_DEV // 2, BF16, "place_attn_w_in"), q_w_in0, "chips", "scatter_attn_w_in_start")
    dh0 = _mm_nt(dproj0, w_in0, F32, "attn_in_dx", deps=(sc_in0[3],))
    dx, _, dg0 = _rmsnorm_bwd(xf, dh0, dx1, g0, "norm0_bwd")

    def update(started, after, w, m, v, name, kind="scatter"):
        recv = _exchange_wait(started, after, kind, "scatter_" + name + "_wait")
        return _adamw(recv, w[0], m[0], v[0], "adamw_" + name)

    r_ssm_w_out = update(sc_out1, dx, ssm_w_out, m_ssm_w_out, v_ssm_w_out, "ssm_w_out")
    r_ssm_glu_w = update(sc_glu, r_ssm_w_out[0], ssm_glu_w, m_ssm_glu_w, v_ssm_glu_w, "ssm_glu_w")
    r_ssm_w_in = update(sc_in1, r_ssm_glu_w[0], ssm_w_in, m_ssm_w_in, v_ssm_w_in, "ssm_w_in")
    r_attn_w_out = update(sc_out0, r_ssm_w_in[0], attn_w_out, m_attn_w_out, v_attn_w_out, "attn_w_out")

    dbb_re, dbb_im = _block_diag_in_grad(dwb)
    dc_re, dc_im = _block_diag_out_grad(dwc)
    dlam_re = dlam[:, 0, :HALF].reshape(G, STATE)
    dlam_im = dlam[:, 0, HALF:].reshape(G, STATE)
    by_owner = [t.reshape((N_DEV, -1)) for t in (dlam_re, dlam_im, dbb_re, dbb_im, dc_re, dc_im, dd, dglu_b)]
    small_parts = jnp.concatenate([t.reshape(N_DEV, -1, PACK_W) for t in by_owner], axis=1)
    small_sum = _sum_parts(_exchange(small_parts, True, "scatter_small"), "sum_small")
    s_lam_re, s_lam_im, s_bb_re, s_bb_im, s_c_re, s_c_im, s_d, s_glu_b = _unpack(small_sum, small_shapes)
    g_a_re, g_a_im, g_log_dt, g_b_re, g_b_im = disc_vjp((s_lam_re, s_lam_im, s_bb_re, s_bb_im))

    local_names = ["ssm_A_re", "ssm_A_im", "ssm_log_dt", "ssm_B_re", "ssm_B_im", "ssm_C_re", "ssm_C_im",
                   "ssm_D", "ssm_glu_b"]
    local_g = [g_a_re, g_a_im, g_log_dt, g_b_re, g_b_im, s_c_re, s_c_im, s_d, s_glu_b]
    local_w = [ssm_A_re, ssm_A_im, ssm_log_dt, ssm_B_re, ssm_B_im, ssm_C_re, ssm_C_im, ssm_D, ssm_glu_b]
    local_m = [m_ssm_A_re, m_ssm_A_im, m_ssm_log_dt, m_ssm_B_re, m_ssm_B_im, m_ssm_C_re, m_ssm_C_im,
               m_ssm_D, m_ssm_glu_b]
    local_v = [v_ssm_A_re, v_ssm_A_im, v_ssm_log_dt, v_ssm_B_re, v_ssm_B_im, v_ssm_C_re, v_ssm_C_im,
               v_ssm_D, v_ssm_glu_b]
    r_local = _adamw_small(local_g, local_w, local_m, local_v, None, "adamw_small")

    rep_g = [jnp.concatenate([dg0, dg1], axis=0), dqg, dkg]
    rep_w = [norm_g, attn_q_g, attn_k_g]
    rep_m = [m_norm_g, m_attn_q_g, m_attn_k_g]
    rep_v = [v_norm_g, v_attn_q_g, v_attn_k_g]
    r_rep = _adamw_small(rep_g, rep_w, rep_m, rep_v, "gather_rep", "adamw_rep")

    r_attn_w_in = update(sc_in0, r_rep[0][0], attn_w_in, m_attn_w_in, v_attn_w_in, "attn_w_in", "chips")

    res = {"attn_w_in": r_attn_w_in, "attn_w_out": r_attn_w_out, "ssm_w_in": r_ssm_w_in,
           "ssm_glu_w": r_ssm_glu_w, "ssm_w_out": r_ssm_w_out}
    ref_w = {"attn_w_in": attn_w_in, "attn_w_out": attn_w_out, "ssm_w_in": ssm_w_in,
             "ssm_glu_w": ssm_glu_w, "ssm_w_out": ssm_w_out}
    for name, r, w in zip(local_names, r_local, local_w):
        res[name], ref_w[name] = r, w
    for name, r, w in zip(["norm_g", "attn_q_g", "attn_k_g"], r_rep, rep_w):
        res[name], ref_w[name] = r, w
    order = ["norm_g", "attn_w_in", "attn_q_g", "attn_k_g", "attn_w_out", "ssm_w_in", "ssm_A_re", "ssm_A_im",
             "ssm_log_dt", "ssm_B_re", "ssm_B_im", "ssm_C_re", "ssm_C_im", "ssm_D", "ssm_glu_w", "ssm_glu_b",
             "ssm_w_out"]
    outs = [loss, dx.reshape(B, S, D)]
    for kind in range(4):
        outs += [res[n][kind].reshape(ref_w[n].shape) for n in order]
    return tuple(outs)


def _adamw_small(grads, ws, ms, vs, gather_name, name):
    sizes = [math.prod(w.shape) for w in ws]
    total = sum(sizes)
    rows = -(-total // (PACK_W * 8)) * 8
    if rows > 256:
        rows = -(-rows // 256) * 256

    def pack(ts, fill):
        flat = jnp.concatenate([t.reshape(-1).astype(F32) for t in ts])
        flat = jnp.concatenate([flat, jnp.full((rows * PACK_W - total,), fill, F32)])
        return flat.reshape(rows, PACK_W)

    g = pack(grads, 0.0)
    parts = _exchange(g, False, gather_name) if gather_name else g[None]
    res = _adamw(parts, pack(ws, 0.0), pack(ms, 0.0), pack(vs, 1.0), name)
    outs = []
    off = 0
    flats = [r.reshape(-1) for r in res]
    for n in sizes:
        outs.append(tuple(f[off:off + n] for f in flats))
        off += n
    return outs
```

```python
import functools
import math

import jax
import jax.numpy as jnp
from jax import lax
from jax.experimental import pallas as pl
from jax.experimental.pallas import tpu as pltpu

F32 = jnp.float32
BF16 = jnp.bfloat16

N_DEV = 8
HEAD_DIM = 128
GROUP = 16
STATE = 64
GROUPS_PER_BLOCK = 8
SUBLANES = 8
RMS_EPS = 1e-6
ADAM_LR, ADAM_B1, ADAM_B2, ADAM_EPS, ADAM_WD, ADAM_STEP = 0.001, 0.9, 0.999, 1e-08, 0.01, 10
VMEM_LIMIT = 56 * 1024 * 1024
GELU_C = math.sqrt(2.0 / math.pi)
PACK_W = 128


def _params(sem, **kw):
    return pltpu.CompilerParams(dimension_semantics=sem, vmem_limit_bytes=VMEM_LIMIT, **kw)


def _tile(n, t):
    t = min(n, t)
    assert n % t == 0, (n, t)
    return t


def _ones_where(cond):
    return jnp.where(cond, 1.0, 0.0).astype(BF16)


def _sigmoid(x):
    return 1.0 / (1.0 + jnp.exp(-x))


def _log_sigmoid(z):
    return jnp.minimum(z, 0.0) - jnp.log(1.0 + jnp.exp(-jnp.abs(z)))


def _dot(a, b, dims):
    return lax.dot_general(a, b, (dims, ((), ())), preferred_element_type=F32)


NN = ((1,), (0,))
NT = ((1,), (1,))
TN = ((0,), (0,))


def _exchange(x, scatter, name):
    shape = x.shape[1:] if scatter else x.shape

    def body(x_ref, out_ref, send_sems, recv_sems, local_sem):
        ix, iy, ic = lax.axis_index("x"), lax.axis_index("y"), lax.axis_index("c")
        me = 4 * ix + 2 * iy + ic

        def peer(k):
            kx, ky, kc = (k >> 2) & 1, (k >> 1) & 1, k & 1
            px, py, pc = ix ^ kx, iy ^ ky, ic ^ kc
            return (px, py, pc), 4 * px + 2 * py + pc

        mine = pltpu.make_async_copy(x_ref.at[me] if scatter else x_ref, out_ref.at[me], local_sem)
        mine.start()
        copies = []
        for k in range(1, N_DEV):
            pid, pidx = peer(k)
            cp = pltpu.make_async_remote_copy(
                src_ref=x_ref.at[pidx] if scatter else x_ref,
                dst_ref=out_ref.at[me],
                send_sem=send_sems.at[k - 1], recv_sem=recv_sems.at[k - 1],
                device_id=pid, device_id_type=pl.DeviceIdType.MESH)
            cp.start()
            copies.append(cp)
        for k in range(1, N_DEV):
            pid, pidx = peer(k)
            pltpu.make_async_remote_copy(
                src_ref=x_ref.at[pidx] if scatter else x_ref,
                dst_ref=out_ref.at[pidx],
                send_sem=send_sems.at[k - 1], recv_sem=recv_sems.at[k - 1],
                device_id=pid, device_id_type=pl.DeviceIdType.MESH).wait_recv()
        for cp in copies:
            cp.wait_send()
        mine.wait()

    return pl.pallas_call(
        body, name=name,
        out_shape=jax.ShapeDtypeStruct((N_DEV,) + tuple(shape), x.dtype),
        in_specs=[pl.BlockSpec(memory_space=pl.ANY)],
        out_specs=pl.BlockSpec(memory_space=pl.ANY),
        scratch_shapes=[pltpu.SemaphoreType.DMA((N_DEV - 1,)), pltpu.SemaphoreType.DMA((N_DEV - 1,)),
                        pltpu.SemaphoreType.DMA],
    )(x)


_HBM = pl.BlockSpec(memory_space=pltpu.HBM)
_SEM = pl.BlockSpec(memory_space=pltpu.SEMAPHORE)
_ANY = pl.BlockSpec(memory_space=pl.ANY)
_EFFECT = pltpu.SideEffectType.DATAFLOW_SIDE_EFFECTING


def _peer(k):
    ix, iy, ic = lax.axis_index("x"), lax.axis_index("y"), lax.axis_index("c")
    px, py, pc = ix ^ ((k >> 2) & 1), iy ^ ((k >> 1) & 1), ic ^ (k & 1)
    return (px, py, pc), 4 * px + 2 * py + pc


def _my_index():
    return 4 * lax.axis_index("x") + 2 * lax.axis_index("y") + lax.axis_index("c")


def _my_chip():
    return 2 * lax.axis_index("x") + lax.axis_index("y")


def _place(src, idx, n_slots, dtype, name, tr=256, tc=1024):
    R, C = src.shape[-2:]
    tr, tc = _tile(R, tr), _tile(C, tc)
    idx = idx.astype(jnp.int32).reshape(1)

    def body(idx_ref, src_ref, land_ref, out_ref):
        out_ref[...] = src_ref[...].astype(dtype)

    slot = pl.BlockSpec((None, tr, tc), lambda r, c, idx_ref: (idx_ref[0], r, c))
    src_spec = slot if src.ndim == 3 else pl.BlockSpec((tr, tc), lambda r, c, idx_ref: (r, c))
    land = lax.empty((n_slots, R, C), dtype)
    return pl.pallas_call(
        body, name=name,
        out_shape=jax.ShapeDtypeStruct(land.shape, dtype),
        grid_spec=pltpu.PrefetchScalarGridSpec(
            num_scalar_prefetch=1, grid=(R // tr, C // tc), in_specs=[src_spec, _ANY], out_specs=slot),
        input_output_aliases={2: 0},
        compiler_params=_params(("parallel", "parallel")),
    )(idx, src, land)


def _pair_sum(x, got, name, tr=256, tc=1024):
    _, R, C = x.shape
    tr, tc = _tile(R, tr), _tile(C, tc)
    core = lax.axis_index("c").astype(jnp.int32).reshape(1)

    def body(core_ref, x_ref, got_ref, out_ref):
        out_ref[...] = (x_ref[...].astype(F32) + got_ref[...].astype(F32)).astype(out_ref.dtype)

    blk = pl.BlockSpec((None, tr, tc), lambda i, r, c, core_ref: (i, r, c))
    return pl.pallas_call(
        body, name=name,
        out_shape=jax.ShapeDtypeStruct(got.shape, got.dtype),
        grid_spec=pltpu.PrefetchScalarGridSpec(
            num_scalar_prefetch=1, grid=(N_DEV // 2, R // tr, C // tc),
            in_specs=[pl.BlockSpec((None, tr, tc), lambda i, r, c, core_ref: (2 * i + core_ref[0], r, c)), blk],
            out_specs=blk),
        compiler_params=_params(("parallel", "parallel", "parallel")),
    )(core, x, got)


def _gather_two_level(land, name, deps=()):
    def body(*refs):
        land_ref, send_sems, recv_sems = refs[1 + len(deps):]
        x, y, c = lax.axis_index("x"), lax.axis_index("y"), lax.axis_index("c")
        me, sibling = (x, y, c), (x, y, 1 - c)
        chips = [(1 - x, y), (x, 1 - y), (1 - x, 1 - y)]

        def block(px, py, pc):
            return land_ref.at[4 * px + 2 * py + pc]

        def copy(k, blk, to):
            return pltpu.make_async_remote_copy(
                src_ref=block(*blk), dst_ref=block(*blk), send_sem=send_sems.at[k], recv_sem=recv_sems.at[k],
                device_id=to, device_id_type=pl.DeviceIdType.MESH)

        first = [copy(0, me, sibling)] + [copy(1 + j, me, (*chip, c)) for j, chip in enumerate(chips)]
        for cp in first:
            cp.start()
        passed = [copy(4 + j, (*chip, c), sibling) for j, chip in enumerate(chips)]
        for j, chip in enumerate(chips):
            copy(1 + j, (*chip, c), me).wait_recv()
            passed[j].start()
        copy(0, sibling, me).wait_recv()
        for j, chip in enumerate(chips):
            copy(4 + j, (*chip, 1 - c), me).wait_recv()
        for cp in first + passed:
            cp.wait_send()

    return pl.pallas_call(
        body, name=name,
        out_shape=jax.ShapeDtypeStruct(land.shape, land.dtype),
        in_specs=[_ANY] * (1 + len(deps)), out_specs=_ANY,
        input_output_aliases={0: 0},
        scratch_shapes=[pltpu.SemaphoreType.DMA((N_DEV - 1,)), pltpu.SemaphoreType.DMA((N_DEV - 1,))],
    )(land, *deps)


_N_COPIES = {"gather": N_DEV - 1, "scatter": N_DEV - 1, "pair": N_DEV // 2, "chips": N_DEV // 2 - 1}


def _copies(kind, land_ref, x_ref):
    ix, iy, ic = lax.axis_index("x"), lax.axis_index("y"), lax.axis_index("c")
    me, chip = _my_index(), _my_chip()
    out = []
    if kind in ("gather", "scatter"):
        for k in range(1, N_DEV):
            pid, pidx = _peer(k)
            out.append((land_ref.at[me] if kind == "gather" else x_ref.at[pidx], land_ref.at[me], pid, land_ref.at[pidx]))
    elif kind == "pair":
        for i in range(N_DEV // 2):
            out.append((x_ref.at[2 * i + (1 - ic)], land_ref.at[i], (ix, iy, 1 - ic), land_ref.at[i]))
    else:
        for k in range(1, N_DEV // 2):
            px, py = ix ^ (k >> 1), iy ^ (k & 1)
            out.append((x_ref.at[2 * px + py], land_ref.at[chip], (px, py, ic), land_ref.at[2 * px + py]))
    return out


def _exchange_start(land, x, kind, name, deps=()):
    bufs = [land] if x is None else [land, x]
    nb, n = len(bufs), _N_COPIES[kind]

    def body(*refs):
        send_sems, recv_sems = refs[nb + len(deps):nb + len(deps) + 2]
        token = refs[2 * nb + len(deps) + 2]
        for i, (src, dst, pid, _) in enumerate(_copies(kind, refs[0], refs[nb - 1])):
            pltpu.make_async_remote_copy(src_ref=src, dst_ref=dst, send_sem=send_sems.at[i], recv_sem=recv_sems.at[i],
                                         device_id=pid, device_id_type=pl.DeviceIdType.MESH).start()
        token[...] = jnp.zeros_like(token)

    out = pl.pallas_call(
        body, name=name,
        out_shape=(pltpu.SemaphoreType.DMA((n,)), pltpu.SemaphoreType.DMA((n,)))
        + tuple(pltpu.HBM(t.shape, t.dtype) for t in bufs) + (jax.ShapeDtypeStruct((8, 128), F32),),
        in_specs=(_HBM,) * nb + (_ANY,) * len(deps),
        out_specs=(_SEM, _SEM) + (_HBM,) * nb + (pl.BlockSpec(memory_space=pltpu.VMEM),),
        input_output_aliases={i: 2 + i for i in range(nb)},
        compiler_params=pltpu.CompilerParams(has_side_effects=_EFFECT),
    )(*[pltpu.with_memory_space_constraint(t, pltpu.HBM) for t in bufs], *deps)
    return out[0], out[1], out[2:2 + nb], out[2 + nb]


def _exchange_wait(started, after, kind, name, with_source=False):
    send_sems, recv_sems, bufs, _ = started
    nb = len(bufs)

    def body(*refs):
        send_sems, recv_sems = refs[nb:nb + 2]
        for i, (src, _, pid, landed) in enumerate(_copies(kind, refs[0], refs[nb - 1])):
            cp = pltpu.make_async_remote_copy(src_ref=src, dst_ref=landed, send_sem=send_sems.at[i],
                                              recv_sem=recv_sems.at[i], device_id=pid,
                                              device_id_type=pl.DeviceIdType.MESH)
            cp.wait_send()
            cp.wait_recv()

    out = pl.pallas_call(
        body, name=name,
        out_shape=tuple(pltpu.HBM(t.shape, t.dtype) for t in bufs),
        in_specs=(_HBM,) * nb + (_SEM, _SEM, _ANY),
        out_specs=(_HBM,) * nb,
        input_output_aliases={i: i for i in range(nb)},
        compiler_params=pltpu.CompilerParams(has_side_effects=_EFFECT),
    )(*bufs, send_sems, recv_sems, after)
    return tuple(out) if with_source else out[0]


def _accumulate(acc, part, step, n_steps, finish):
    if n_steps == 1:
        finish(part)
        return

    @pl.when(step == 0)
    def _():
        acc[...] = part

    @pl.when((step > 0) & (step < n_steps - 1))
    def _():
        acc[...] += part

    @pl.when(step == n_steps - 1)
    def _():
        finish(acc[...] + part)


def _mm_nn(a, b, out_dtype, name, bias=None, residual=None, deps=(), tm=1024, tn=1024, tk=2048):
    M, K = a.shape
    J, K2, Nj = b.shape
    assert K == K2
    tm, tn, tk = _tile(M, tm), _tile(Nj, tn), _tile(K, tk)
    nb, nk = Nj // tn, K // tk

    def body(*refs):
        a_ref, b_ref = refs[0], refs[1]
        i = 2
        bias_ref = res_ref = None
        if bias is not None:
            bias_ref = refs[i]; i += 1
        if residual is not None:
            res_ref = refs[i]; i += 1
        i += len(deps)
        o_ref, acc = refs[i], refs[i + 1]

        def finish(r):
            if bias_ref is not None:
                r = r + bias_ref[...]
            if res_ref is not None:
                r = r + res_ref[...]
            o_ref[...] = r.astype(out_dtype)

        _accumulate(acc, _dot(a_ref[...], b_ref[...], NN), pl.program_id(3), nk, finish)

    in_specs = [pl.BlockSpec((tm, tk), lambda j, m, n, k: (m, k)),
                pl.BlockSpec((None, tk, tn), lambda j, m, n, k: (j, k, n))]
    args = [a, b]
    if bias is not None:
        in_specs.append(pl.BlockSpec((1, tn), lambda j, m, n, k: (0, j * nb + n)))
        args.append(bias)
    if residual is not None:
        in_specs.append(pl.BlockSpec((tm, tn), lambda j, m, n, k: (m, j * nb + n)))
        args.append(residual)
    in_specs += [_ANY] * len(deps)
    args += list(deps)
    return pl.pallas_call(
        body, name=name,
        out_shape=jax.ShapeDtypeStruct((M, J * Nj), out_dtype),
        grid=(J, M // tm, nb, nk),
        in_specs=in_specs,
        out_specs=pl.BlockSpec((tm, tn), lambda j, m, n, k: (m, j * nb + n)),
        scratch_shapes=[pltpu.VMEM((tm, tn), F32)],
        compiler_params=_params(("parallel", "parallel", "parallel", "arbitrary")),
    )(*args)


def _mm_nt(a, b, out_dtype, name, deps=(), tm=1024, tp=1024, tq=2048):
    M, Q = a.shape
    J, P, Qj = b.shape
    assert Q == J * Qj
    tm, tp, tq = _tile(M, tm), _tile(P, tp), _tile(Qj, tq)
    nq = Qj // tq

    def body(*refs):
        a_ref, b_ref = refs[:2]
        o_ref, acc = refs[2 + len(deps):]

        def finish(r):
            o_ref[...] = r.astype(out_dtype)

        _accumulate(acc, _dot(a_ref[...], b_ref[...], NT), pl.program_id(2) * nq + pl.program_id(3), J * nq, finish)

    return pl.pallas_call(
        body, name=name,
        out_shape=jax.ShapeDtypeStruct((M, P), out_dtype),
        grid=(M // tm, P // tp, J, nq),
        in_specs=[pl.BlockSpec((tm, tq), lambda m, p, j, q: (m, j * nq + q)),
                  pl.BlockSpec((None, tp, tq), lambda m, p, j, q: (j, p, q))] + [_ANY] * len(deps),
        out_specs=pl.BlockSpec((tm, tp), lambda m, p, j, q: (m, p)),
        scratch_shapes=[pltpu.VMEM((tm, tp), F32)],
        compiler_params=_params(("parallel", "parallel", "arbitrary", "arbitrary")),
    )(a, b, *deps)


def _mm_tn(a, b, J, out_dtype, name, tm=1024, tn=1024, tr=2048):
    R, M = a.shape
    R2, N = b.shape
    assert R == R2 and N % J == 0
    Nj = N // J
    tm, tn, tr = _tile(M, tm), _tile(Nj, tn), _tile(R, tr)
    nb, nr = Nj // tn, R // tr

    def body(a_ref, b_ref, o_ref, acc):
        def finish(r):
            o_ref[...] = r.astype(out_dtype)

        _accumulate(acc, _dot(a_ref[...], b_ref[...], TN), pl.program_id(3), nr, finish)

    return pl.pallas_call(
        body, name=name,
        out_shape=jax.ShapeDtypeStruct((J, M, Nj), out_dtype),
        grid=(J, M // tm, nb, nr),
        in_specs=[pl.BlockSpec((tr, tm), lambda j, m, n, r: (r, m)),
                  pl.BlockSpec((tr, tn), lambda j, m, n, r: (r, j * nb + n))],
        out_specs=pl.BlockSpec((None, tm, tn), lambda j, m, n, r: (j, m, n)),
        scratch_shapes=[pltpu.VMEM((tm, tn), F32)],
        compiler_params=_params(("parallel", "parallel", "parallel", "arbitrary")),
    )(a, b)


def _ew(fn, ins, vecs, out_dtypes, n_acc, name, tr=256, tc=1024):
    T, C = ins[0].shape
    tr, tc = _tile(T, tr), _tile(C, tc)
    n_in, n_vec, n_out = len(ins), len(vecs), len(out_dtypes)

    def body(*refs):
        in_refs = refs[:n_in + n_vec]
        out_refs = refs[n_in + n_vec:n_in + n_vec + n_out]
        acc_refs = refs[n_in + n_vec + n_out:]
        res = fn(*[r[...] for r in in_refs])
        for o_ref, v in zip(out_refs, res[:n_out]):
            o_ref[...] = v.astype(o_ref.dtype)
        if n_acc:
            r = pl.program_id(1)

            @pl.when(r == 0)
            def _():
                for a_ref in acc_refs:
                    a_ref[...] = jnp.zeros_like(a_ref)

            for a_ref, v in zip(acc_refs, res[n_out:]):
                a_ref[...] += v

    blk = pl.BlockSpec((tr, tc), lambda c, r: (r, c))
    vec = pl.BlockSpec((1, tc), lambda c, r: (0, c))
    out = pl.pallas_call(
        body, name=name,
        out_shape=tuple([jax.ShapeDtypeStruct((T, C), d) for d in out_dtypes]
                        + [jax.ShapeDtypeStruct((1, C), F32)] * n_acc),
        grid=(C // tc, T // tr),
        in_specs=[blk] * n_in + [vec] * n_vec,
        out_specs=tuple([blk] * n_out + [vec] * n_acc),
        compiler_params=_params(("parallel", "arbitrary")),
    )(*ins, *vecs)
    return out


def _cast_bf16(w, name):
    return _ew(lambda a: (a,), [w], [], [BF16], 0, name)[0]


def _rmsnorm_fwd(x, g, name, deps=(), tr=128):
    T, D = x.shape
    tr = _tile(T, tr)

    def body(*refs):
        x_ref, g_ref, h_ref = refs[0], refs[1], refs[2 + len(deps)]
        xv = x_ref[...]
        r = lax.rsqrt(jnp.mean(xv * xv, axis=-1, keepdims=True) + RMS_EPS)
        h_ref[...] = ((xv * r) * g_ref[...]).astype(BF16)

    return pl.pallas_call(
        body, name=name,
        out_shape=jax.ShapeDtypeStruct((T, D), BF16),
        grid=(T // tr,),
        in_specs=[pl.BlockSpec((tr, D), lambda i: (i, 0)), pl.BlockSpec((1, D), lambda i: (0, 0))] + [_ANY] * len(deps),
        out_specs=pl.BlockSpec((tr, D), lambda i: (i, 0)),
        compiler_params=_params(("parallel",)),
    )(x, g, *deps)


def _rmsnorm_bwd(x, dh, dres, g, name, tr=128):
    T, D = x.shape
    tr = _tile(T, tr)

    def body(x_ref, dh_ref, dres_ref, g_ref, dx_ref, dxb_ref, dg_ref):
        xv = x_ref[...]
        r = lax.rsqrt(jnp.mean(xv * xv, axis=-1, keepdims=True) + RMS_EPS)
        xn = xv * r
        dhv = dh_ref[...].astype(F32)
        dxn = dhv * g_ref[...]
        dx = dres_ref[...] + r * (dxn - xn * jnp.mean(dxn * xn, axis=-1, keepdims=True))
        dx_ref[...] = dx
        dxb_ref[...] = dx.astype(BF16)

        @pl.when(pl.program_id(0) == 0)
        def _():
            dg_ref[...] = jnp.zeros_like(dg_ref)

        dg_ref[...] += jnp.sum(dhv * xn, axis=0, keepdims=True)

    blk = pl.BlockSpec((tr, D), lambda i: (i, 0))
    vec = pl.BlockSpec((1, D), lambda i: (0, 0))
    return pl.pallas_call(
        body, name=name,
        out_shape=(jax.ShapeDtypeStruct((T, D), F32), jax.ShapeDtypeStruct((T, D), BF16),
                   jax.ShapeDtypeStruct((1, D), F32)),
        grid=(T // tr,),
        in_specs=[blk, blk, blk, vec],
        out_specs=(blk, blk, vec),
        compiler_params=_params(("arbitrary",)),
    )(x, dh, dres, g)


def _loss_head(y, target, name, tr=128):
    T, D = y.shape
    tr = _tile(T, tr)
    n = T // tr

    def body(y_ref, t_ref, dy_ref, dyb_ref, loss_ref, acc):
        i = pl.program_id(0)

        @pl.when(i == 0)
        def _():
            acc[...] = jnp.zeros_like(acc)

        err = y_ref[...] - t_ref[...]
        dy = err * (1.0 / D)
        dy_ref[...] = dy
        dyb_ref[...] = dy.astype(BF16)
        acc[...] += jnp.sum(err * err, axis=0, keepdims=True)

        @pl.when(i == n - 1)
        def _():
            loss_ref[...] = jnp.sum(acc[...], axis=1, keepdims=True) * (0.5 / D)

    blk = pl.BlockSpec((tr, D), lambda i: (i, 0))
    return pl.pallas_call(
        body, name=name,
        out_shape=(jax.ShapeDtypeStruct((T, D), F32), jax.ShapeDtypeStruct((T, D), BF16),
                   jax.ShapeDtypeStruct((1, 1), F32)),
        grid=(n,),
        in_specs=[blk, blk],
        out_specs=(blk, blk, pl.BlockSpec((1, 1), lambda i: (0, 0))),
        scratch_shapes=[pltpu.VMEM((1, D), F32)],
        compiler_params=_params(("arbitrary",)),
    )(y, target)


def _gelu(x):
    t = jnp.tanh(GELU_C * (x + 0.044715 * (x * x * x)))
    return x * (0.5 * (1.0 + t)), t


def _adamw(parts, w, m, v, name, tr=256, tc=1024):
    n, R, C = parts.shape
    tr, tc = _tile(R, tr), _tile(C, tc)
    c1 = 1.0 - ADAM_B1 ** ADAM_STEP
    c2 = 1.0 - ADAM_B2 ** ADAM_STEP

    def body(p_ref, w_ref, m_ref, v_ref, g_out, d_out, m_out, v_out):
        g = p_ref[0].astype(F32)
        for k in range(1, n):
            g = g + p_ref[k].astype(F32)
        mn = ADAM_B1 * m_ref[...] + (1.0 - ADAM_B1) * g
        vn = ADAM_B2 * v_ref[...] + (1.0 - ADAM_B2) * (g * g)
        m_hat = mn / c1
        v_hat = vn / c2
        g_out[...] = g
        d_out[...] = -ADAM_LR * (m_hat / (jnp.sqrt(v_hat) + ADAM_EPS) + ADAM_WD * w_ref[...])
        m_out[...] = mn
        v_out[...] = vn

    blk = pl.BlockSpec((tr, tc), lambda r, c: (r, c))
    return pl.pallas_call(
        body, name=name,
        out_shape=tuple([jax.ShapeDtypeStruct((R, C), F32)] * 4),
        grid=(R // tr, C // tc),
        in_specs=[pl.BlockSpec((n, tr, tc), lambda r, c: (0, r, c)), blk, blk, blk],
        out_specs=(blk, blk, blk, blk),
        compiler_params=_params(("parallel", "parallel")),
    )(parts, w, m, v)


def _sum_parts(parts, name):
    n, R, C = parts.shape

    def body(p_ref, o_ref):
        g = p_ref[0].astype(F32)
        for k in range(1, n):
            g = g + p_ref[k].astype(F32)
        o_ref[...] = g

    return pl.pallas_call(
        body, name=name,
        out_shape=jax.ShapeDtypeStruct((R, C), F32),
        compiler_params=pltpu.CompilerParams(vmem_limit_bytes=VMEM_LIMIT),
    )(parts)


def _head_norm(xv):
    xv = xv.astype(F32)
    r = lax.rsqrt(jnp.mean(xv * xv, axis=-1, keepdims=True) + RMS_EPS)
    return xv * r, r


FWD_HEADS = 4
BWD_HEADS = 2


def _attn_specs(S, H, HP):
    def spec(part):
        return pl.BlockSpec((S, HP * HEAD_DIM), lambda b, h, qi: (b, part * (H // HP) + h))
    return spec


def _lanes(hh):
    return slice(hh * HEAD_DIM, (hh + 1) * HEAD_DIM)


def _attn_fwd(proj, q_g, k_g, B, S, H, name):
    TQ = _tile(S, 256)
    nq = S // TQ
    scale = 1.0 / math.sqrt(HEAD_DIM)
    HP = FWD_HEADS
    ATT_W = HP * HEAD_DIM
    heads = range(HP)

    def body(q_ref, k_ref, v_ref, gate_ref, qg_ref, kg_ref, og_ref, o_ref, w_ref, sg_ref, qn_s, kn_s):
        @pl.when(pl.program_id(2) == 0)
        def _():
            for hh in heads:
                qn_s[:, _lanes(hh)] = (_head_norm(q_ref[:, _lanes(hh)])[0] * qg_ref[...]).astype(BF16)
                kn_s[:, _lanes(hh)] = (_head_norm(k_ref[:, _lanes(hh)])[0] * kg_ref[...]).astype(BF16)

        row = lax.broadcasted_iota(jnp.int32, (TQ, TQ), 0)
        col = lax.broadcasted_iota(jnp.int32, (TQ, TQ), 1)
        later = _ones_where(row > col)
        causal = col < row

        def q_block(qi):
            q0 = pl.multiple_of(qi * TQ, TQ)

            def both(ki, state, diag):
                k0 = pl.multiple_of(ki * TQ, TQ)
                z = [_dot(qn_s[pl.ds(q0, TQ), _lanes(hh)], kn_s[pl.ds(k0, TQ), _lanes(hh)], NT) * scale
                     for hh in heads]
                ls = [_log_sigmoid(zz) for zz in z]
                l1m = [a - zz for a, zz in zip(ls, z)]
                if diag:
                    l1m = [jnp.where(causal, a, 0.0) for a in l1m]
                suffix = [_dot(l1m[hh].astype(BF16), later, NN) + state[hh][0] for hh in heads]
                w = [jnp.exp(a + sfx) for a, sfx in zip(ls, suffix)]
                if diag:
                    w = [jnp.where(causal, a, 0.0) for a in w]
                wb = [a.astype(BF16) for a in w]
                for hh in heads:
                    w_ref[hh, ki] = wb[hh]
                    sg_ref[hh, ki] = jnp.exp(ls[hh]).astype(BF16)
                acc = [state[hh][1] + _dot(wb[hh], v_ref[pl.ds(k0, TQ), _lanes(hh)], NN) for hh in heads]
                return tuple((state[hh][0] + jnp.sum(l1m[hh], axis=1, keepdims=True), acc[hh]) for hh in heads)

            zero = (jnp.zeros((TQ, 1), F32), jnp.zeros((TQ, HEAD_DIM), F32))
            state = both(qi, (zero,) * HP, True)
            state = lax.fori_loop(0, qi, lambda i, st: both(qi - 1 - i, st, False), state)
            for hh in heads:
                acc = state[hh][1]
                o_ref[:, _lanes(hh)] = acc.astype(BF16)
                gate = gate_ref[pl.ds(q0, TQ), _lanes(hh)].astype(F32)
                og_ref[:, _lanes(hh)] = (acc * (gate * _sigmoid(gate))).astype(BF16)

        q_block(pl.program_id(2))

    spec = _attn_specs(S, H, HP)
    vec = pl.BlockSpec((1, HEAD_DIM), lambda b, h, qi: (0, 0))
    out = pl.BlockSpec((TQ, ATT_W), lambda b, h, qi: (b * nq + qi, h))
    kept = pl.BlockSpec((None, HP, None, nq, TQ, TQ), lambda b, h, qi: (b, h, qi, 0, 0, 0))
    kept_shape = jax.ShapeDtypeStruct((B, H, nq, nq, TQ, TQ), BF16)
    return pl.pallas_call(
        body, name=name,
        out_shape=(jax.ShapeDtypeStruct((B * S, H * HEAD_DIM), BF16),) * 2 + (kept_shape,) * 2,
        grid=(B, H // HP, nq),
        in_specs=[spec(0), spec(1), spec(2), spec(3), vec, vec],
        out_specs=(out, out, kept, kept),
        scratch_shapes=[pltpu.VMEM((S, ATT_W), BF16)] * 2,
        compiler_params=_params(("parallel", "parallel", "arbitrary")),
    )(proj, proj, proj, proj, q_g, k_g)


def _attn_bwd(proj, o, dog, w_kept, sg_kept, q_g, k_g, B, S, H, name):
    TQ = _tile(S, 256)
    nq = S // TQ
    scale = 1.0 / math.sqrt(HEAD_DIM)
    HP = BWD_HEADS
    ATT_W = HP * HEAD_DIM
    heads = range(HP)

    def body(q_ref, k_ref, v_ref, gate_ref, o_ref, dog_ref, w_ref, sg_ref, qg_ref, kg_ref,
             dq_ref, dk_ref, dv_ref, dgate_ref, dqg_ref, dkg_ref,
             qn_s, kn_s, do_s, dkn_s, dv_s):
        qi = pl.program_id(2)

        @pl.when((pl.program_id(0) == 0) & (pl.program_id(1) == 0) & (qi == 0))
        def _():
            dqg_ref[...] = jnp.zeros_like(dqg_ref)
            dkg_ref[...] = jnp.zeros_like(dkg_ref)

        @pl.when(qi == 0)
        def _():
            for hh in heads:
                qn_s[:, _lanes(hh)] = (_head_norm(q_ref[:, _lanes(hh)])[0] * qg_ref[...]).astype(BF16)
                kn_s[:, _lanes(hh)] = (_head_norm(k_ref[:, _lanes(hh)])[0] * kg_ref[...]).astype(BF16)
            gate = gate_ref[...].astype(F32)
            sg = _sigmoid(gate)
            dog_v = dog_ref[...].astype(F32)
            do_s[...] = (dog_v * (gate * sg)).astype(BF16)
            dgate_ref[...] = (dog_v * o_ref[...].astype(F32) * (sg * (1.0 + gate * (1.0 - sg)))).astype(BF16)
            dkn_s[...] = jnp.zeros_like(dkn_s)
            dv_s[...] = jnp.zeros_like(dv_s)

        row = lax.broadcasted_iota(jnp.int32, (TQ, TQ), 0)
        col = lax.broadcasted_iota(jnp.int32, (TQ, TQ), 1)
        earlier = _ones_where(row < col)
        causal = col < row

        def norm_bwd(xv, g_ref, dn, dg_ref):
            xh, r = _head_norm(xv)
            dg_ref[...] += jnp.sum(dn * xh, axis=0, keepdims=True)
            dxh = dn * g_ref[...]
            return (r * (dxh - xh * jnp.mean(dxh * xh, axis=-1, keepdims=True))).astype(BF16)

        def q_block():
            q0 = pl.multiple_of(qi * TQ, TQ)

            def grads_both(ki, state, diag):
                k0 = pl.multiple_of(ki * TQ, TQ)
                qb = [qn_s[pl.ds(q0, TQ), _lanes(hh)] for hh in heads]
                dob = [do_s[pl.ds(q0, TQ), _lanes(hh)] for hh in heads]
                wb = [w_ref[hh, ki] for hh in heads]
                da = [_dot(dob[hh], v_ref[pl.ds(k0, TQ), _lanes(hh)], NT) * wb[hh].astype(F32) for hh in heads]
                for hh in heads:
                    dv_s[pl.ds(k0, TQ), _lanes(hh)] += _dot(wb[hh], dob[hh], TN)
                prefix = [_dot(da[hh].astype(BF16), earlier, NN) + state[hh][0] for hh in heads]
                dzb = []
                for hh in heads:
                    sgz = sg_ref[hh, ki].astype(F32)
                    dz = da[hh] * (1.0 - sgz) - sgz * prefix[hh]
                    if diag:
                        dz = jnp.where(causal, dz, 0.0)
                    dzb.append((dz * scale).astype(BF16))
                dq = [state[hh][1] + _dot(dzb[hh], kn_s[pl.ds(k0, TQ), _lanes(hh)], NN) for hh in heads]
                for hh in heads:
                    dkn_s[pl.ds(k0, TQ), _lanes(hh)] += _dot(dzb[hh], qb[hh], TN)
                return tuple((state[hh][0] + jnp.sum(da[hh], axis=1, keepdims=True), dq[hh]) for hh in heads)

            zero = (jnp.zeros((TQ, 1), F32), jnp.zeros((TQ, HEAD_DIM), F32))
            state = lax.fori_loop(0, qi, lambda i, st: grads_both(i, st, False), (zero,) * HP)
            state = grads_both(qi, state, True)
            for hh in heads:
                dq_ref[:, _lanes(hh)] = norm_bwd(q_ref[pl.ds(q0, TQ), _lanes(hh)], qg_ref, state[hh][1], dqg_ref)

        q_block()

        @pl.when(qi == nq - 1)
        def _():
            for hh in heads:
                dk_ref[:, _lanes(hh)] = norm_bwd(k_ref[:, _lanes(hh)], kg_ref, dkn_s[:, _lanes(hh)], dkg_ref)
            dv_ref[...] = dv_s[...].astype(BF16)

    spec = _attn_specs(S, H, HP)
    vec = pl.BlockSpec((1, HEAD_DIM), lambda b, h, qi: (0, 0))
    blk = pl.BlockSpec((S, ATT_W), lambda b, h, qi: (b, h))
    rows = pl.BlockSpec((TQ, ATT_W), lambda b, h, qi: (b * nq + qi, h))
    kept = pl.BlockSpec((None, HP, None, nq, TQ, TQ), lambda b, h, qi: (b, h, qi, 0, 0, 0))
    big = jax.ShapeDtypeStruct((B * S, H * HEAD_DIM), BF16)
    small = jax.ShapeDtypeStruct((1, HEAD_DIM), F32)
    return pl.pallas_call(
        body, name=name,
        out_shape=(big, big, big, big, small, small),
        grid=(B, H // HP, nq),
        in_specs=[spec(0), spec(1), spec(2), spec(3), blk, blk, kept, kept, vec, vec],
        out_specs=(rows, blk, blk, blk, vec, vec),
        scratch_shapes=[pltpu.VMEM((S, ATT_W), BF16)] * 3 + [pltpu.VMEM((S, ATT_W), F32)] * 2,
        compiler_params=_params(("arbitrary", "arbitrary", "arbitrary")),
    )(proj, proj, proj, proj, o, dog, w_kept, sg_kept, q_g, k_g)


HALF = GROUPS_PER_BLOCK * STATE


def _cmul(ar, ai, br, bi):
    return ar * br - ai * bi, ar * bi + ai * br


def _cpow(ar, ai, n):
    rr = ri = None
    while n:
        if n & 1:
            rr, ri = (ar, ai) if rr is None else _cmul(rr, ri, ar, ai)
        n >>= 1
        if n:
            ar, ai = _cmul(ar, ai, ar, ai)
    return rr, ri


def _segment_carry(er, ei, lr, li, seg_len, segs_per_seq, reverse):
    Lr, Li = _cpow(lr, li, seg_len)
    pos = lax.broadcasted_iota(jnp.int32, er.shape, 0) % segs_per_seq
    outr = jnp.zeros_like(er)
    outi = jnp.zeros_like(ei)
    pr = pi = None
    for d in range(1, segs_per_seq):
        shift = (SUBLANES - d) if reverse else d
        sr = pltpu.roll(er, shift, 0)
        si = pltpu.roll(ei, shift, 0)
        ok = (pos + d < segs_per_seq) if reverse else (pos >= d)
        sr = jnp.where(ok, sr, 0.0)
        si = jnp.where(ok, si, 0.0)
        if pr is not None:
            sr, si = _cmul(sr, si, pr, pi)
        outr = outr + sr
        outi = outi + si
        pr, pi = (Lr, Li) if pr is None else _cmul(pr, pi, Lr, Li)
    return outr, outi


def _s5_sizes(T, B):
    assert SUBLANES % B == 0
    segs_per_seq = SUBLANES // B
    n_steps = T // SUBLANES
    cj = _tile(n_steps, 64)
    return segs_per_seq, n_steps, cj


def _s5_fwd(u_p, wb, wc, lam, dvec, B, name):
    T, C = u_p.shape
    nb = C // 128
    segs_per_seq, n_steps, cj = _s5_sizes(T, B)
    n_chunks = n_steps // cj
    rows = cj * SUBLANES

    def body(u_ref, wb_ref, wc_ref, lam_ref, d_ref, y_ref, hin_ref, bu_s, h_s):
        lr = jnp.broadcast_to(lam_ref[:, :HALF], (SUBLANES, HALF))
        li = jnp.broadcast_to(lam_ref[:, HALF:], (SUBLANES, HALF))

        def scan_chunk(c, hr, hi, store):
            r0 = pl.multiple_of(c * rows, rows)
            bu_s[...] = _dot(u_ref[pl.ds(r0, rows), :], wb_ref[...], NN)

            def step(j, carry):
                hr, hi = carry
                o = pl.multiple_of(j * SUBLANES, SUBLANES)
                nr = lr * hr - li * hi + bu_s[pl.ds(o, SUBLANES), :HALF]
                ni = lr * hi + li * hr + bu_s[pl.ds(o, SUBLANES), HALF:]
                if store:
                    h_s[pl.ds(o, SUBLANES), :HALF] = nr
                    h_s[pl.ds(o, SUBLANES), HALF:] = ni
                return nr, ni

            hr, hi = lax.fori_loop(0, cj, step, (hr, hi))
            if store:
                uv = u_ref[pl.ds(r0, rows), :].astype(F32)
                y_ref[pl.ds(r0, rows), :] = _dot(h_s[...].astype(BF16), wc_ref[...], NN) + d_ref[...] * uv
            return hr, hi

        zero = jnp.zeros((SUBLANES, HALF), F32)
        er, ei = lax.fori_loop(0, n_chunks, lambda c, h: scan_chunk(c, h[0], h[1], False), (zero, zero))
        h0r, h0i = _segment_carry(er, ei, lr, li, n_steps, segs_per_seq, False)
        hin_ref[:, :HALF] = h0r
        hin_ref[:, HALF:] = h0i
        lax.fori_loop(0, n_chunks, lambda c, h: scan_chunk(c, h[0], h[1], True), (h0r, h0i))

    return pl.pallas_call(
        body, name=name,
        out_shape=(jax.ShapeDtypeStruct((T, C), F32), jax.ShapeDtypeStruct((nb, SUBLANES, 2 * HALF), F32)),
        grid=(nb,),
        in_specs=[pl.BlockSpec((T, 128), lambda g: (0, g)),
                  pl.BlockSpec((None, 128, 2 * HALF), lambda g: (g, 0, 0)),
                  pl.BlockSpec((None, 2 * HALF, 128), lambda g: (g, 0, 0)),
                  pl.BlockSpec((None, 1, 2 * HALF), lambda g: (g, 0, 0)),
                  pl.BlockSpec((1, 128), lambda g: (0, g))],
        out_specs=(pl.BlockSpec((T, 128), lambda g: (0, g)),
                   pl.BlockSpec((None, SUBLANES, 2 * HALF), lambda g: (g, 0, 0))),
        scratch_shapes=[pltpu.VMEM((rows, 2 * HALF), F32)] * 2,
        compiler_params=_params(("parallel",)),
    )(u_p, wb, wc, lam, dvec)


def _s5_bwd(u_p, dy_p, wb, wbt, wc, wct, lam, dvec, h_in, B, name):
    T, C = u_p.shape
    nb = C // 128
    segs_per_seq, n_steps, cj = _s5_sizes(T, B)
    n_chunks = n_steps // cj
    rows = cj * SUBLANES

    def body(u_ref, dy_ref, wb_ref, wbt_ref, wc_ref, wct_ref, lam_ref, d_ref, hin_ref,
             du_ref, dwb_ref, dwc_ref, dlam_ref, dd_ref, h_all, x_s, g_s):
        lr = jnp.broadcast_to(lam_ref[:, :HALF], (SUBLANES, HALF))
        li = jnp.broadcast_to(lam_ref[:, HALF:], (SUBLANES, HALF))
        zero = jnp.zeros((SUBLANES, HALF), F32)

        h_all[pl.ds(0, SUBLANES), :] = hin_ref[...]

        def fwd_chunk(c, carry):
            r0 = pl.multiple_of(c * rows, rows)
            x_s[...] = _dot(u_ref[pl.ds(r0, rows), :], wb_ref[...], NN)

            def step(j, carry):
                hr, hi = carry
                o = pl.multiple_of(j * SUBLANES, SUBLANES)
                nr = lr * hr - li * hi + x_s[pl.ds(o, SUBLANES), :HALF]
                ni = lr * hi + li * hr + x_s[pl.ds(o, SUBLANES), HALF:]
                late = pl.multiple_of(r0 + o + SUBLANES, SUBLANES)
                h_all[pl.ds(late, SUBLANES), :HALF] = nr
                h_all[pl.ds(late, SUBLANES), HALF:] = ni
                return nr, ni

            return lax.fori_loop(0, cj, step, carry)

        lax.fori_loop(0, n_chunks, fwd_chunk, (hin_ref[:, :HALF], hin_ref[:, HALF:]))

        def bwd_chunk(i, carry, store):
            c = n_chunks - 1 - i
            r0 = pl.multiple_of(c * rows, rows)
            dyv = dy_ref[pl.ds(r0, rows), :]
            x_s[...] = _dot(dyv, wct_ref[...], NN)

            def step(jj, carry):
                ar, ai, accr, acci = carry
                j = cj - 1 - jj
                o = pl.multiple_of(j * SUBLANES, SUBLANES)
                nr = lr * ar + li * ai + x_s[pl.ds(o, SUBLANES), :HALF]
                ni = lr * ai - li * ar + x_s[pl.ds(o, SUBLANES), HALF:]
                if store:
                    g_s[pl.ds(o, SUBLANES), :HALF] = nr
                    g_s[pl.ds(o, SUBLANES), HALF:] = ni
                    prev = pl.multiple_of(r0 + o, SUBLANES)
                    pr = h_all[pl.ds(prev, SUBLANES), :HALF]
                    pi = h_all[pl.ds(prev, SUBLANES), HALF:]
                    accr = accr + nr * pr + ni * pi
                    acci = acci + ni * pr - nr * pi
                return nr, ni, accr, acci

            carry = lax.fori_loop(0, cj, step, carry)
            if store:
                gb = g_s[...].astype(BF16)
                uv = u_ref[pl.ds(r0, rows), :]
                dyf = dyv.astype(F32)
                du_ref[pl.ds(r0, rows), :] = (_dot(gb, wbt_ref[...], NN) + d_ref[...] * dyf).astype(BF16)
                dwb_ref[...] += _dot(uv, gb, TN)
                hb = h_all[pl.ds(pl.multiple_of(r0 + SUBLANES, SUBLANES), rows), :].astype(BF16)
                dwc_ref[...] += _dot(hb, dyv, TN)
                dd_ref[...] += jnp.sum(dyf * uv.astype(F32), axis=0, keepdims=True)
            return carry

        er, ei, _, _ = lax.fori_loop(0, n_chunks, lambda i, c: bwd_chunk(i, c, False), (zero, zero, zero, zero))
        a0r, a0i = _segment_carry(er, ei, lr, -li, n_steps, segs_per_seq, True)
        dwb_ref[...] = jnp.zeros_like(dwb_ref)
        dwc_ref[...] = jnp.zeros_like(dwc_ref)
        dd_ref[...] = jnp.zeros_like(dd_ref)
        _, _, accr, acci = lax.fori_loop(0, n_chunks, lambda i, c: bwd_chunk(i, c, True), (a0r, a0i, zero, zero))
        dlam_ref[:, :HALF] = jnp.sum(accr, axis=0, keepdims=True)
        dlam_ref[:, HALF:] = jnp.sum(acci, axis=0, keepdims=True)

    col = pl.BlockSpec((T, 128), lambda g: (0, g))
    vec = pl.BlockSpec((1, 128), lambda g: (0, g))

    def per_block(*shape):
        return pl.BlockSpec((None,) + shape, lambda g: (g, 0, 0))

    return pl.pallas_call(
        body, name=name,
        out_shape=(jax.ShapeDtypeStruct((T, C), BF16),
                   jax.ShapeDtypeStruct((nb, 128, 2 * HALF), F32),
                   jax.ShapeDtypeStruct((nb, 2 * HALF, 128), F32),
                   jax.ShapeDtypeStruct((nb, 1, 2 * HALF), F32),
                   jax.ShapeDtypeStruct((1, C), F32)),
        grid=(nb,),
        in_specs=[col, col, per_block(128, 2 * HALF), per_block(2 * HALF, 128), per_block(2 * HALF, 128),
                  per_block(128, 2 * HALF), per_block(1, 2 * HALF), vec, per_block(SUBLANES, 2 * HALF)],
        out_specs=(col, per_block(128, 2 * HALF), per_block(2 * HALF, 128), per_block(1, 2 * HALF), vec),
        scratch_shapes=[pltpu.VMEM((T + SUBLANES, 2 * HALF), F32),
                        pltpu.VMEM((rows, 2 * HALF), F32), pltpu.VMEM((rows, 2 * HALF), F32)],
        compiler_params=_params(("parallel",)),
    )(u_p, dy_p, wb, wbt, wc, wct, lam, dvec, h_in)


def _discretize(a_re, a_im, log_dt, b_re, b_im):
    dt = jnp.exp(log_dt)[:, None]
    mag = jnp.exp(a_re * dt)
    lam_re = mag * jnp.cos(a_im * dt)
    lam_im = mag * jnp.sin(a_im * dt)
    den = a_re * a_re + a_im * a_im
    f_re = ((lam_re - 1.0) * a_re + lam_im * a_im) / den
    f_im = (lam_im * a_re - (lam_re - 1.0) * a_im) / den
    bb_re = f_re[..., None] * b_re - f_im[..., None] * b_im
    bb_im = f_re[..., None] * b_im + f_im[..., None] * b_re
    return lam_re, lam_im, bb_re, bb_im


def _block_diag_in(bb_re, bb_im):
    eye = jnp.eye(GROUPS_PER_BLOCK, dtype=F32)

    def one(bb):
        t = bb.reshape(-1, GROUPS_PER_BLOCK, STATE, GROUP)
        return jnp.einsum('gapi,ab->gaibp', t, eye).reshape(-1, 128, HALF)

    return jnp.concatenate([one(bb_re), one(bb_im)], axis=-1)


def _block_diag_in_grad(dwb):
    eye = jnp.eye(GROUPS_PER_BLOCK, dtype=F32)

    def one(d):
        t = d.reshape(-1, GROUPS_PER_BLOCK, GROUP, GROUPS_PER_BLOCK, STATE)
        return jnp.einsum('gaibp,ab->gapi', t, eye).reshape(-1, STATE, GROUP)

    return one(dwb[..., :HALF]), one(dwb[..., HALF:])


def _block_diag_out(c_re, c_im):
    eye = jnp.eye(GROUPS_PER_BLOCK, dtype=F32)

    def one(cc):
        t = cc.reshape(-1, GROUPS_PER_BLOCK, GROUP, STATE)
        return jnp.einsum('gaip,ab->gbpai', t, eye).reshape(-1, HALF, 128)

    return jnp.concatenate([one(c_re), -one(c_im)], axis=1)


def _block_diag_out_grad(dwc):
    eye = jnp.eye(GROUPS_PER_BLOCK, dtype=F32)

    def one(d):
        t = d.reshape(-1, GROUPS_PER_BLOCK, STATE, GROUPS_PER_BLOCK, GROUP)
        return jnp.einsum('gbpai,ab->gaip', t, eye).reshape(-1, GROUP, STATE)

    return one(dwc[:, :HALF]), -one(dwc[:, HALF:])


def _pack(parts):
    return jnp.concatenate([p.reshape(-1, PACK_W) for p in parts], axis=0)


def _unpack(buf, shapes):
    lead = buf.shape[:-2]
    out, r = [], 0
    for s in shapes:
        n = math.prod(s) // PACK_W
        out.append(buf[..., r:r + n, :].reshape(lead + tuple(s)))
        r += n
    return out


def _permute_rows(a, n_steps):
    T, C = a.shape
    return a.reshape(SUBLANES, n_steps, C).transpose(1, 0, 2).reshape(T, C)


def _unpermute_rows(a, n_steps):
    T, C = a.shape
    return a.reshape(n_steps, SUBLANES, C).transpose(1, 0, 2).reshape(T, C)


def kernel(x, norm_g, attn_w_in, attn_q_g, attn_k_g, attn_w_out, ssm_w_in, ssm_A_re, ssm_A_im, ssm_log_dt, ssm_B_re, ssm_B_im, ssm_C_re, ssm_C_im, ssm_D, ssm_glu_w, ssm_glu_b, ssm_w_out, loss_target, m_norm_g, m_attn_w_in, m_attn_q_g, m_attn_k_g, m_attn_w_out, m_ssm_w_in, m_ssm_A_re, m_ssm_A_im, m_ssm_log_dt, m_ssm_B_re, m_ssm_B_im, m_ssm_C_re, m_ssm_C_im, m_ssm_D, m_ssm_glu_w, m_ssm_glu_b, m_ssm_w_out, v_norm_g, v_attn_w_in, v_attn_q_g, v_attn_k_g, v_attn_w_out, v_ssm_w_in, v_ssm_A_re, v_ssm_A_im, v_ssm_log_dt, v_ssm_B_re, v_ssm_B_im, v_ssm_C_re, v_ssm_C_im, v_ssm_D, v_ssm_glu_w, v_ssm_glu_b, v_ssm_w_out):
    B, S, D = x.shape
    T = B * S
    H = D // HEAD_DIM
    G_loc = ssm_A_re.shape[1]
    G = G_loc * N_DEV
    n_steps = T // SUBLANES
    me, chip = _my_index(), _my_chip()
    xf = x.reshape(T, D)
    target = loss_target.reshape(T, D)

    c_in0 = _place(attn_w_in[0], me, N_DEV, BF16, "cast_attn_w_in")
    c_out0 = _place(attn_w_out[0], me, N_DEV, BF16, "cast_attn_w_out")
    c_in1 = _place(ssm_w_in[0], me, N_DEV, BF16, "cast_ssm_w_in")
    c_glu = _place(ssm_glu_w[0], me, N_DEV, BF16, "cast_ssm_glu_w")
    c_out1 = _place(ssm_w_out[0], me, N_DEV, BF16, "cast_ssm_w_out")

    small_shapes = [(G_loc, STATE), (G_loc, STATE), (G_loc, STATE, GROUP), (G_loc, STATE, GROUP),
                    (G_loc, GROUP, STATE), (G_loc, GROUP, STATE), (G_loc * GROUP,), (G_loc * GROUP,)]
    disc_in = (ssm_A_re[0], ssm_A_im[0], ssm_log_dt[0], ssm_B_re[0], ssm_B_im[0])
    (lam_re, lam_im, bb_re, bb_im), disc_vjp = jax.vjp(_discretize, *disc_in)
    small = _pack([lam_re, lam_im, bb_re, bb_im, ssm_C_re[0], ssm_C_im[0], ssm_D[0], ssm_glu_b[0]])
    small_all = _exchange(small, False, "gather_small")
    w_in0 = _gather_two_level(c_in0, "gather_attn_w_in", deps=(small_all,))
    lam_re_a, lam_im_a, bb_re_a, bb_im_a, c_re_a, c_im_a, d_a, glu_b_a = [
        t.reshape((G,) + t.shape[2:]) if t.ndim > 2 else t.reshape(-1)
        for t in _unpack(small_all, small_shapes)]
    wb = _block_diag_in(bb_re_a, bb_im_a)
    wc = _block_diag_out(c_re_a, c_im_a)
    wb_b, wc_b = wb.astype(BF16), wc.astype(BF16)
    wbt_b, wct_b = wb_b.transpose(0, 2, 1), wc_b.transpose(0, 2, 1)
    lam = jnp.concatenate([lam_re_a.reshape(-1, 1, HALF), lam_im_a.reshape(-1, 1, HALF)], axis=-1)
    d_row = d_a.reshape(1, D)
    glu_b_row = glu_b_a.reshape(1, D)
    g0, g1 = norm_g[0:1], norm_g[1:2]
    q_g, k_g = attn_q_g, attn_k_g

    h0 = _rmsnorm_fwd(xf, g0, "norm0")
    s_out0 = _exchange_start(c_out0, None, "gather", "gather_attn_w_out_start", deps=(w_in0,))
    s_in1 = _exchange_start(c_in1, None, "gather", "gather_ssm_w_in_start", deps=(w_in0,))
    proj0 = _mm_nn(h0, w_in0, BF16, "attn_in", deps=(s_out0[3], s_in1[3]))
    og, o, w_kept, sg_kept = _attn_fwd(proj0, q_g, k_g, B, S, H, "attn_fwd")
    w_out0 = _exchange_wait(s_out0, og, "gather", "gather_attn_w_out_wait").reshape(1, D, D)
    s_glu = _exchange_start(c_glu, None, "gather", "gather_ssm_glu_w_start", deps=(w_out0,))
    s_out1 = _exchange_start(c_out1, None, "gather", "gather_ssm_w_out_start", deps=(w_out0,))
    x1 = _mm_nn(og, w_out0, F32, "attn_out", residual=xf, deps=(s_glu[3], s_out1[3]))

    h1 = _rmsnorm_fwd(x1, g1, "norm1")
    w_in1 = _exchange_wait(s_in1, h1, "gather", "gather_ssm_w_in_wait")
    proj1 = _mm_nn(h1, w_in1, BF16, "ssm_in")
    u_p = _permute_rows(proj1[:, :D], n_steps)
    gate1 = proj1[:, D:]
    y_p, h_in = _s5_fwd(u_p, wb_b, wc_b, lam, d_row, B, "s5_fwd")
    y_ssm = _unpermute_rows(y_p, n_steps)
    (yg,) = _ew(lambda a: (_gelu(a)[0],), [y_ssm], [], [BF16], 0, "gelu")
    w_glu = _exchange_wait(s_glu, yg, "gather", "gather_ssm_glu_w_wait").reshape(1, D, D)
    z = _mm_nn(yg, w_glu, F32, "glu_in", bias=glu_b_row)

    def glu_fwd(y, zz, gt):
        gt = gt.astype(F32)
        return (_gelu(y)[0] * _sigmoid(zz) * (gt * _sigmoid(gt)),)

    (y3,) = _ew(glu_fwd, [y_ssm, z, gate1], [], [BF16], 0, "glu_gate")
    w_out1 = _exchange_wait(s_out1, y3, "gather", "gather_ssm_w_out_wait").reshape(1, D, D)
    out = _mm_nn(y3, w_out1, F32, "ssm_out", residual=x1)

    dout, dout_b, loss_part = _loss_head(out, target, "loss")
    loss = lax.psum(loss_part[0, 0], ("x", "y", "c"))

    p_w_out1 = _mm_tn(y3, dout_b, 1, BF16, "ssm_out_dw").reshape(N_DEV, D // N_DEV, D)
    sc_out1 = _exchange_start(_place(p_w_out1, me, N_DEV, BF16, "place_ssm_w_out"), p_w_out1, "scatter", "scatter_ssm_w_out_start")
    dy3 = _mm_nt(dout_b, w_out1, F32, "ssm_out_dx", deps=(sc_out1[3],))

    def glu_bwd(d3, y, zz, gt):
        gt = gt.astype(F32)
        sg = _sigmoid(gt)
        sz = _sigmoid(zz)
        ygv, _ = _gelu(y)
        dy2 = d3 * (gt * sg)
        dgate = d3 * (ygv * sz) * (sg * (1.0 + gt * (1.0 - sg)))
        dz = dy2 * ygv * (sz * (1.0 - sz))
        return dz, dgate, dy2 * sz, jnp.sum(dz, axis=0, keepdims=True)

    dz_b, dgate1, dyg_a, dglu_b = _ew(glu_bwd, [dy3, y_ssm, z, gate1], [], [BF16, BF16, F32], 1, "glu_gate_bwd")
    p_w_glu = _mm_tn(yg, dz_b, 1, BF16, "glu_in_dw").reshape(N_DEV, D // N_DEV, D)
    sc_glu = _exchange_start(_place(p_w_glu, me, N_DEV, BF16, "place_ssm_glu_w"), p_w_glu, "scatter", "scatter_ssm_glu_w_start")
    dyg_b = _mm_nt(dz_b, w_glu, F32, "glu_in_dx", deps=(sc_glu[3],))

    def gelu_bwd(da, db, y):
        _, t = _gelu(y)
        dg = 0.5 * (1.0 + t) + 0.5 * y * (1.0 - t * t) * (GELU_C * (1.0 + 3.0 * 0.044715 * (y * y)))
        return ((da + db) * dg,)

    (dy_ssm,) = _ew(gelu_bwd, [dyg_a, dyg_b, y_ssm], [], [BF16], 0, "gelu_bwd")
    dy_p = _permute_rows(dy_ssm, n_steps)
    du_p, dwb, dwc, dlam, dd = _s5_bwd(u_p, dy_p, wb_b, wbt_b, wc_b, wct_b, lam, d_row, h_in, B, "s5_bwd")
    du = _unpermute_rows(du_p, n_steps)
    dproj1 = jnp.concatenate([du, dgate1], axis=1)
    p_w_in1 = _mm_tn(h1, dproj1, N_DEV, BF16, "ssm_in_dw")
    sc_in1 = _exchange_start(_place(p_w_in1, me, N_DEV, BF16, "place_ssm_w_in"), p_w_in1, "scatter", "scatter_ssm_w_in_start")
    dh1 = _mm_nt(dproj1, w_in1, F32, "ssm_in_dx", deps=(sc_in1[3],))
    dx1, dx1_b, dg1 = _rmsnorm_bwd(x1, dh1, dout, g1, "norm1_bwd")

    p_w_out0 = _mm_tn(og, dx1_b, 1, BF16, "attn_out_dw").reshape(N_DEV, D // N_DEV, D)
    sc_out0 = _exchange_start(_place(p_w_out0, me, N_DEV, BF16, "place_attn_w_out"), p_w_out0, "scatter", "scatter_attn_w_out_start")
    dog = _mm_nt(dx1_b, w_out0, BF16, "attn_out_dx", deps=(sc_out0[3],))
    dq, dk, dv, dgate0, dqg, dkg = _attn_bwd(proj0, o, dog, w_kept, sg_kept, q_g, k_g, B, S, H, "attn_bwd")
    dproj0 = jnp.concatenate([dq, dk, dv, dgate0], axis=1)
    p_w_in0 = _mm_tn(h0, dproj0, N_DEV, BF16, "attn_in_dw")
    pair = _exchange_start(lax.empty((N_DEV // 2,) + p_w_in0.shape[1:], BF16), p_w_in0, "pair", "scatter_attn_w_in_pair_start")
    got, p_w_in0 = _exchange_wait(pair, pair[3], "pair", "scatter_attn_w_in_pair_wait", with_source=True)
    q_w_in0 = _pair_sum(p_w_in0, got, "scatter_attn_w_in_pair_sum")
    sc_in0 = _exchange_start(_place(q_w_in0, chip, N_DEV // 2, BF16, "place_attn_w_in"), q_w_in0, "chips", "scatter_attn_w_in_start")
    dh0 = _mm_nt(dproj0, w_in0, F32, "attn_in_dx", deps=(sc_in0[3],))
    dx, _, dg0 = _rmsnorm_bwd(xf, dh0, dx1, g0, "norm0_bwd")

    def update(started, after, w, m, v, name, kind="scatter"):
        recv = _exchange_wait(started, after, kind, "scatter_" + name + "_wait")
        return _adamw(recv, w[0], m[0], v[0], "adamw_" + name)

    r_ssm_w_out = update(sc_out1, dx, ssm_w_out, m_ssm_w_out, v_ssm_w_out, "ssm_w_out")
    r_ssm_glu_w = update(sc_glu, r_ssm_w_out[0], ssm_glu_w, m_ssm_glu_w, v_ssm_glu_w, "ssm_glu_w")
    r_ssm_w_in = update(sc_in1, r_ssm_glu_w[0], ssm_w_in, m_ssm_w_in, v_ssm_w_in, "ssm_w_in")
    r_attn_w_out = update(sc_out0, r_ssm_w_in[0], attn_w_out, m_attn_w_out, v_attn_w_out, "attn_w_out")

    dbb_re, dbb_im = _block_diag_in_grad(dwb)
    dc_re, dc_im = _block_diag_out_grad(dwc)
    dlam_re = dlam[:, 0, :HALF].reshape(G, STATE)
    dlam_im = dlam[:, 0, HALF:].reshape(G, STATE)
    by_owner = [t.reshape((N_DEV, -1)) for t in (dlam_re, dlam_im, dbb_re, dbb_im, dc_re, dc_im, dd, dglu_b)]
    small_parts = jnp.concatenate([t.reshape(N_DEV, -1, PACK_W) for t in by_owner], axis=1)
    small_sum = _sum_parts(_exchange(small_parts, True, "scatter_small"), "sum_small")
    s_lam_re, s_lam_im, s_bb_re, s_bb_im, s_c_re, s_c_im, s_d, s_glu_b = _unpack(small_sum, small_shapes)
    g_a_re, g_a_im, g_log_dt, g_b_re, g_b_im = disc_vjp((s_lam_re, s_lam_im, s_bb_re, s_bb_im))

    local_names = ["ssm_A_re", "ssm_A_im", "ssm_log_dt", "ssm_B_re", "ssm_B_im", "ssm_C_re", "ssm_C_im",
                   "ssm_D", "ssm_glu_b"]
    local_g = [g_a_re, g_a_im, g_log_dt, g_b_re, g_b_im, s_c_re, s_c_im, s_d, s_glu_b]
    local_w = [ssm_A_re, ssm_A_im, ssm_log_dt, ssm_B_re, ssm_B_im, ssm_C_re, ssm_C_im, ssm_D, ssm_glu_b]
    local_m = [m_ssm_A_re, m_ssm_A_im, m_ssm_log_dt, m_ssm_B_re, m_ssm_B_im, m_ssm_C_re, m_ssm_C_im,
               m_ssm_D, m_ssm_glu_b]
    local_v = [v_ssm_A_re, v_ssm_A_im, v_ssm_log_dt, v_ssm_B_re, v_ssm_B_im, v_ssm_C_re, v_ssm_C_im,
               v_ssm_D, v_ssm_glu_b]
    r_local = _adamw_small(local_g, local_w, local_m, local_v, None, "adamw_small")

    rep_g = [jnp.concatenate([dg0, dg1], axis=0), dqg, dkg]
    rep_w = [norm_g, attn_q_g, attn_k_g]
    rep_m = [m_norm_g, m_attn_q_g, m_attn_k_g]
    rep_v = [v_norm_g, v_attn_q_g, v_attn_k_g]
    r_rep = _adamw_small(rep_g, rep_w, rep_m, rep_v, "gather_rep", "adamw_rep")

    r_attn_w_in = update(sc_in0, r_rep[0][0], attn_w_in, m_attn_w_in, v_attn_w_in, "attn_w_in", "chips")

    res = {"attn_w_in": r_attn_w_in, "attn_w_out": r_attn_w_out, "ssm_w_in": r_ssm_w_in,
           "ssm_glu_w": r_ssm_glu_w, "ssm_w_out": r_ssm_w_out}
    ref_w = {"attn_w_in": attn_w_in, "attn_w_out": attn_w_out, "ssm_w_in": ssm_w_in,
             "ssm_glu_w": ssm_glu_w, "ssm_w_out": ssm_w_out}
    for name, r, w in zip(local_names, r_local, local_w):
        res[name], ref_w[name] = r, w
    for name, r, w in zip(["norm_g", "attn_q_g", "attn_k_g"], r_rep, rep_w):
        res[name], ref_w[name] = r, w
    order = ["norm_g", "attn_w_in", "attn_q_g", "attn_k_g", "attn_w_out", "ssm_w_in", "ssm_A_re", "ssm_A_im",
             "ssm_log_dt", "ssm_B_re", "ssm_B_im", "ssm_C_re", "ssm_C_im", "ssm_D", "ssm_glu_w", "ssm_glu_b",
             "ssm_w_out"]
    outs = [loss, dx.reshape(B, S, D)]
    for kind in range(4):
        outs += [res[n][kind].reshape(ref_w[n].shape) for n in order]
    return tuple(outs)


def _adamw_small(grads, ws, ms, vs, gather_name, name):
    sizes = [math.prod(w.shape) for w in ws]
    total = sum(sizes)
    rows = -(-total // (PACK_W * 8)) * 8
    if rows > 256:
        rows = -(-rows // 256) * 256

    def pack(ts, fill):
        flat = jnp.concatenate([t.reshape(-1).astype(F32) for t in ts])
        flat = jnp.concatenate([flat, jnp.full((rows * PACK_W - total,), fill, F32)])
        return flat.reshape(rows, PACK_W)

    g = pack(grads, 0.0)
    parts = _exchange(g, False, gather_name) if gather_name else g[None]
    res = _adamw(parts, pack(ws, 0.0), pack(ms, 0.0), pack(vs, 1.0), name)
    outs = []
    off = 0
    flats = [r.reshape(-1) for r in res]
    for n in sizes:
        outs.append(tuple(f[off:off + n] for f in flats))
        off += n
    return outs
```

```python
import functools
import math

import jax
import jax.numpy as jnp
from jax import lax
from jax.experimental import pallas as pl
from jax.experimental.pallas import tpu as pltpu

F32 = jnp.float32
BF16 = jnp.bfloat16

N_DEV = 8
HEAD_DIM = 128
GROUP = 16
STATE = 64
GROUPS_PER_BLOCK = 8
SUBLANES = 8
RMS_EPS = 1e-6
ADAM_LR, ADAM_B1, ADAM_B2, ADAM_EPS, ADAM_WD, ADAM_STEP = 0.001, 0.9, 0.999, 1e-08, 0.01, 10
VMEM_LIMIT = 56 * 1024 * 1024
GELU_C = math.sqrt(2.0 / math.pi)
PACK_W = 128


def _params(sem, **kw):
    return pltpu.CompilerParams(dimension_semantics=sem, vmem_limit_bytes=VMEM_LIMIT, **kw)


def _tile(n, t):
    t = min(n, t)
    assert n % t == 0, (n, t)
    return t


def _ones_where(cond):
    return jnp.where(cond, 1.0, 0.0).astype(BF16)


def _sigmoid(x):
    return 1.0 / (1.0 + jnp.exp(-x))


def _log_sigmoid(z):
    return jnp.minimum(z, 0.0) - jnp.log(1.0 + jnp.exp(-jnp.abs(z)))


def _dot(a, b, dims):
    return lax.dot_general(a, b, (dims, ((), ())), preferred_element_type=F32)


NN = ((1,), (0,))
NT = ((1,), (1,))
TN = ((0,), (0,))


def _exchange(x, scatter, name, deps=()):
    shape = x.shape[1:] if scatter else x.shape

    def body(*refs):
        x_ref = refs[0]
        out_ref, send_sems, recv_sems, local_sem = refs[1 + len(deps):]
        ix, iy, ic = lax.axis_index("x"), lax.axis_index("y"), lax.axis_index("c")
        me = 4 * ix + 2 * iy + ic

        def peer(k):
            kx, ky, kc = (k >> 2) & 1, (k >> 1) & 1, k & 1
            px, py, pc = ix ^ kx, iy ^ ky, ic ^ kc
            return (px, py, pc), 4 * px + 2 * py + pc

        mine = pltpu.make_async_copy(x_ref.at[me] if scatter else x_ref, out_ref.at[me], local_sem)
        mine.start()
        copies = []
        for k in range(1, N_DEV):
            pid, pidx = peer(k)
            cp = pltpu.make_async_remote_copy(
                src_ref=x_ref.at[pidx] if scatter else x_ref,
                dst_ref=out_ref.at[me],
                send_sem=send_sems.at[k - 1], recv_sem=recv_sems.at[k - 1],
                device_id=pid, device_id_type=pl.DeviceIdType.MESH)
            cp.start()
            copies.append(cp)
        for k in range(1, N_DEV):
            pid, pidx = peer(k)
            pltpu.make_async_remote_copy(
                src_ref=x_ref.at[pidx] if scatter else x_ref,
                dst_ref=out_ref.at[pidx],
                send_sem=send_sems.at[k - 1], recv_sem=recv_sems.at[k - 1],
                device_id=pid, device_id_type=pl.DeviceIdType.MESH).wait_recv()
        for cp in copies:
            cp.wait_send()
        mine.wait()

    return pl.pallas_call(
        body, name=name,
        out_shape=jax.ShapeDtypeStruct((N_DEV,) + tuple(shape), x.dtype),
        in_specs=[pl.BlockSpec(memory_space=pl.ANY)] * (1 + len(deps)),
        out_specs=pl.BlockSpec(memory_space=pl.ANY),
        scratch_shapes=[pltpu.SemaphoreType.DMA((N_DEV - 1,)), pltpu.SemaphoreType.DMA((N_DEV - 1,)),
                        pltpu.SemaphoreType.DMA],
    )(x, *deps)


_HBM = pl.BlockSpec(memory_space=pltpu.HBM)
_SEM = pl.BlockSpec(memory_space=pltpu.SEMAPHORE)
_ANY = pl.BlockSpec(memory_space=pl.ANY)
_EFFECT = pltpu.SideEffectType.DATAFLOW_SIDE_EFFECTING


def _peer(k):
    ix, iy, ic = lax.axis_index("x"), lax.axis_index("y"), lax.axis_index("c")
    px, py, pc = ix ^ ((k >> 2) & 1), iy ^ ((k >> 1) & 1), ic ^ (k & 1)
    return (px, py, pc), 4 * px + 2 * py + pc


def _my_index():
    return 4 * lax.axis_index("x") + 2 * lax.axis_index("y") + lax.axis_index("c")


def _my_chip():
    return 2 * lax.axis_index("x") + lax.axis_index("y")


def _place(src, idx, n_slots, dtype, name, deps=(), tr=256, tc=1024):
    R, C = src.shape[-2:]
    tr, tc = _tile(R, tr), _tile(C, tc)
    idx = idx.astype(jnp.int32).reshape(1)

    def body(*refs):
        refs[-1][...] = refs[1][...].astype(dtype)

    slot = pl.BlockSpec((None, tr, tc), lambda r, c, idx_ref: (idx_ref[0], r, c))
    src_spec = slot if src.ndim == 3 else pl.BlockSpec((tr, tc), lambda r, c, idx_ref: (r, c))
    land = lax.empty((n_slots, R, C), dtype)
    return pl.pallas_call(
        body, name=name,
        out_shape=jax.ShapeDtypeStruct(land.shape, dtype),
        grid_spec=pltpu.PrefetchScalarGridSpec(
            num_scalar_prefetch=1, grid=(R // tr, C // tc), in_specs=[src_spec, _ANY] + [_ANY] * len(deps),
            out_specs=slot),
        input_output_aliases={2: 0},
        compiler_params=_params(("parallel", "parallel")),
    )(idx, src, land, *deps)


def _pair_sum(x, got, name, tr=256, tc=1024):
    _, R, C = x.shape
    tr, tc = _tile(R, tr), _tile(C, tc)
    core = lax.axis_index("c").astype(jnp.int32).reshape(1)

    def body(core_ref, x_ref, got_ref, out_ref):
        out_ref[...] = (x_ref[...].astype(F32) + got_ref[...].astype(F32)).astype(out_ref.dtype)

    blk = pl.BlockSpec((None, tr, tc), lambda i, r, c, core_ref: (i, r, c))
    return pl.pallas_call(
        body, name=name,
        out_shape=jax.ShapeDtypeStruct(got.shape, got.dtype),
        grid_spec=pltpu.PrefetchScalarGridSpec(
            num_scalar_prefetch=1, grid=(N_DEV // 2, R // tr, C // tc),
            in_specs=[pl.BlockSpec((None, tr, tc), lambda i, r, c, core_ref: (2 * i + core_ref[0], r, c)), blk],
            out_specs=blk),
        compiler_params=_params(("parallel", "parallel", "parallel")),
    )(core, x, got)


_N_COPIES = {"gather": N_DEV - 1, "scatter": N_DEV - 1, "pair": N_DEV // 2, "chips": N_DEV // 2 - 1,
             "near": N_DEV // 2, "far": N_DEV // 2 - 1}


def _copies(kind, land_ref, x_ref):
    ix, iy, ic = lax.axis_index("x"), lax.axis_index("y"), lax.axis_index("c")
    me, chip = _my_index(), _my_chip()
    out = []
    if kind in ("gather", "scatter"):
        for k in range(1, N_DEV):
            pid, pidx = _peer(k)
            out.append((land_ref.at[me] if kind == "gather" else x_ref.at[pidx], land_ref.at[me], pid, land_ref.at[pidx]))
    elif kind == "pair":
        for i in range(N_DEV // 2):
            out.append((x_ref.at[2 * i + (1 - ic)], land_ref.at[i], (ix, iy, 1 - ic), land_ref.at[i]))
    elif kind in ("near", "far"):
        if kind == "near":
            out.append((land_ref.at[me], land_ref.at[me], (ix, iy, 1 - ic), land_ref.at[me + 1 - 2 * ic]))
        for k in range(1, N_DEV // 2):
            px, py = ix ^ (k >> 1), iy ^ (k & 1)
            theirs = 4 * px + 2 * py + ic
            if kind == "near":
                out.append((land_ref.at[me], land_ref.at[me], (px, py, ic), land_ref.at[theirs]))
            else:
                out.append((land_ref.at[theirs], land_ref.at[theirs], (ix, iy, 1 - ic),
                            land_ref.at[theirs + 1 - 2 * ic]))
    else:
        for k in range(1, N_DEV // 2):
            px, py = ix ^ (k >> 1), iy ^ (k & 1)
            out.append((x_ref.at[2 * px + py], land_ref.at[chip], (px, py, ic), land_ref.at[2 * px + py]))
    return out


def _exchange_start(land, x, kind, name, deps=()):
    bufs = [land] if x is None else [land, x]
    nb, n = len(bufs), _N_COPIES[kind]

    def body(*refs):
        send_sems, recv_sems = refs[nb + len(deps):nb + len(deps) + 2]
        token = refs[2 * nb + len(deps) + 2]
        for i, (src, dst, pid, _) in enumerate(_copies(kind, refs[0], refs[nb - 1])):
            pltpu.make_async_remote_copy(src_ref=src, dst_ref=dst, send_sem=send_sems.at[i], recv_sem=recv_sems.at[i],
                                         device_id=pid, device_id_type=pl.DeviceIdType.MESH).start()
        token[...] = jnp.zeros_like(token)

    out = pl.pallas_call(
        body, name=name,
        out_shape=(pltpu.SemaphoreType.DMA((n,)), pltpu.SemaphoreType.DMA((n,)))
        + tuple(pltpu.HBM(t.shape, t.dtype) for t in bufs) + (jax.ShapeDtypeStruct((8, 128), F32),),
        in_specs=(_HBM,) * nb + (_ANY,) * len(deps),
        out_specs=(_SEM, _SEM) + (_HBM,) * nb + (pl.BlockSpec(memory_space=pltpu.VMEM),),
        input_output_aliases={i: 2 + i for i in range(nb)},
        compiler_params=pltpu.CompilerParams(has_side_effects=_EFFECT),
    )(*[pltpu.with_memory_space_constraint(t, pltpu.HBM) for t in bufs], *deps)
    return out[0], out[1], out[2:2 + nb], out[2 + nb]


def _exchange_wait(started, after, kind, name, with_source=False):
    send_sems, recv_sems, bufs, _ = started
    nb = len(bufs)
    after = tuple(after) if isinstance(after, (tuple, list)) else (after,)

    def body(*refs):
        send_sems, recv_sems = refs[nb:nb + 2]
        for i, (src, _, pid, landed) in enumerate(_copies(kind, refs[0], refs[nb - 1])):
            cp = pltpu.make_async_remote_copy(src_ref=src, dst_ref=landed, send_sem=send_sems.at[i],
                                              recv_sem=recv_sems.at[i], device_id=pid,
                                              device_id_type=pl.DeviceIdType.MESH)
            cp.wait_send()
            cp.wait_recv()

    out = pl.pallas_call(
        body, name=name,
        out_shape=tuple(pltpu.HBM(t.shape, t.dtype) for t in bufs),
        in_specs=(_HBM,) * nb + (_SEM, _SEM) + (_ANY,) * len(after),
        out_specs=(_HBM,) * nb,
        input_output_aliases={i: i for i in range(nb)},
        compiler_params=pltpu.CompilerParams(has_side_effects=_EFFECT),
    )(*bufs, send_sems, recv_sems, *after)
    return tuple(out) if with_source else out[0]


def _accumulate(acc, part, step, n_steps, finish):
    if n_steps == 1:
        finish(part)
        return

    @pl.when(step == 0)
    def _():
        acc[...] = part

    @pl.when((step > 0) & (step < n_steps - 1))
    def _():
        acc[...] += part

    @pl.when(step == n_steps - 1)
    def _():
        finish(acc[...] + part)


def _mm_nn(a, b, out_dtype, name, bias=None, residual=None, deps=(), tm=1024, tn=1024, tk=2048):
    M, K = a.shape
    J, K2, Nj = b.shape
    assert K == K2
    tm, tn, tk = _tile(M, tm), _tile(Nj, tn), _tile(K, tk)
    nb, nk = Nj // tn, K // tk

    def body(*refs):
        a_ref, b_ref = refs[0], refs[1]
        i = 2
        bias_ref = res_ref = None
        if bias is not None:
            bias_ref = refs[i]; i += 1
        if residual is not None:
            res_ref = refs[i]; i += 1
        i += len(deps)
        o_ref, acc = refs[i], refs[i + 1]

        def finish(r):
            if bias_ref is not None:
                r = r + bias_ref[...]
            if res_ref is not None:
                r = r + res_ref[...]
            o_ref[...] = r.astype(out_dtype)

        _accumulate(acc, _dot(a_ref[...], b_ref[...], NN), pl.program_id(3), nk, finish)

    in_specs = [pl.BlockSpec((tm, tk), lambda j, m, n, k: (m, k)),
                pl.BlockSpec((None, tk, tn), lambda j, m, n, k: (j, k, n))]
    args = [a, b]
    if bias is not None:
        in_specs.append(pl.BlockSpec((1, tn), lambda j, m, n, k: (0, j * nb + n)))
        args.append(bias)
    if residual is not None:
        in_specs.append(pl.BlockSpec((tm, tn), lambda j, m, n, k: (m, j * nb + n)))
        args.append(residual)
    in_specs += [_ANY] * len(deps)
    args += list(deps)
    return pl.pallas_call(
        body, name=name,
        out_shape=jax.ShapeDtypeStruct((M, J * Nj), out_dtype),
        grid=(J, M // tm, nb, nk),
        in_specs=in_specs,
        out_specs=pl.BlockSpec((tm, tn), lambda j, m, n, k: (m, j * nb + n)),
        scratch_shapes=[pltpu.VMEM((tm, tn), F32)],
        compiler_params=_params(("parallel", "parallel", "parallel", "arbitrary")),
    )(*args)


def _mm_nt(a, b, out_dtype, name, deps=(), tm=1024, tp=1024, tq=2048):
    M, Q = a.shape
    J, P, Qj = b.shape
    assert Q == J * Qj
    tm, tp, tq = _tile(M, tm), _tile(P, tp), _tile(Qj, tq)
    nq = Qj // tq

    def body(*refs):
        a_ref, b_ref = refs[:2]
        o_ref, acc = refs[2 + len(deps):]

        def finish(r):
            o_ref[...] = r.astype(out_dtype)

        _accumulate(acc, _dot(a_ref[...], b_ref[...], NT), pl.program_id(2) * nq + pl.program_id(3), J * nq, finish)

    return pl.pallas_call(
        body, name=name,
        out_shape=jax.ShapeDtypeStruct((M, P), out_dtype),
        grid=(M // tm, P // tp, J, nq),
        in_specs=[pl.BlockSpec((tm, tq), lambda m, p, j, q: (m, j * nq + q)),
                  pl.BlockSpec((None, tp, tq), lambda m, p, j, q: (j, p, q))] + [_ANY] * len(deps),
        out_specs=pl.BlockSpec((tm, tp), lambda m, p, j, q: (m, p)),
        scratch_shapes=[pltpu.VMEM((tm, tp), F32)],
        compiler_params=_params(("parallel", "parallel", "arbitrary", "arbitrary")),
    )(a, b, *deps)


def _mm_tn(a, b, J, out_dtype, name, tm=1024, tn=1024, tr=2048):
    R, M = a.shape
    R2, N = b.shape
    assert R == R2 and N % J == 0
    Nj = N // J
    tm, tn, tr = _tile(M, tm), _tile(Nj, tn), _tile(R, tr)
    nb, nr = Nj // tn, R // tr

    def body(a_ref, b_ref, o_ref, acc):
        def finish(r):
            o_ref[...] = r.astype(out_dtype)

        _accumulate(acc, _dot(a_ref[...], b_ref[...], TN), pl.program_id(3), nr, finish)

    return pl.pallas_call(
        body, name=name,
        out_shape=jax.ShapeDtypeStruct((J, M, Nj), out_dtype),
        grid=(J, M // tm, nb, nr),
        in_specs=[pl.BlockSpec((tr, tm), lambda j, m, n, r: (r, m)),
                  pl.BlockSpec((tr, tn), lambda j, m, n, r: (r, j * nb + n))],
        out_specs=pl.BlockSpec((None, tm, tn), lambda j, m, n, r: (j, m, n)),
        scratch_shapes=[pltpu.VMEM((tm, tn), F32)],
        compiler_params=_params(("parallel", "parallel", "parallel", "arbitrary")),
    )(a, b)


def _ew(fn, ins, vecs, out_dtypes, n_acc, name, tr=256, tc=1024):
    T, C = ins[0].shape
    tr, tc = _tile(T, tr), _tile(C, tc)
    n_in, n_vec, n_out = len(ins), len(vecs), len(out_dtypes)

    def body(*refs):
        in_refs = refs[:n_in + n_vec]
        out_refs = refs[n_in + n_vec:n_in + n_vec + n_out]
        acc_refs = refs[n_in + n_vec + n_out:]
        res = fn(*[r[...] for r in in_refs])
        for o_ref, v in zip(out_refs, res[:n_out]):
            o_ref[...] = v.astype(o_ref.dtype)
        if n_acc:
            r = pl.program_id(1)

            @pl.when(r == 0)
            def _():
                for a_ref in acc_refs:
                    a_ref[...] = jnp.zeros_like(a_ref)

            for a_ref, v in zip(acc_refs, res[n_out:]):
                a_ref[...] += v

    blk = pl.BlockSpec((tr, tc), lambda c, r: (r, c))
    vec = pl.BlockSpec((1, tc), lambda c, r: (0, c))
    out = pl.pallas_call(
        body, name=name,
        out_shape=tuple([jax.ShapeDtypeStruct((T, C), d) for d in out_dtypes]
                        + [jax.ShapeDtypeStruct((1, C), F32)] * n_acc),
        grid=(C // tc, T // tr),
        in_specs=[blk] * n_in + [vec] * n_vec,
        out_specs=tuple([blk] * n_out + [vec] * n_acc),
        compiler_params=_params(("parallel", "arbitrary")),
    )(*ins, *vecs)
    return out


def _rmsnorm_fwd(x, g, name, deps=(), tr=128):
    T, D = x.shape
    tr = _tile(T, tr)

    def body(*refs):
        x_ref, g_ref, h_ref = refs[0], refs[1], refs[2 + len(deps)]
        xv = x_ref[...]
        r = lax.rsqrt(jnp.mean(xv * xv, axis=-1, keepdims=True) + RMS_EPS)
        h_ref[...] = ((xv * r) * g_ref[...]).astype(BF16)

    return pl.pallas_call(
        body, name=name,
        out_shape=jax.ShapeDtypeStruct((T, D), BF16),
        grid=(T // tr,),
        in_specs=[pl.BlockSpec((tr, D), lambda i: (i, 0)), pl.BlockSpec((1, D), lambda i: (0, 0))] + [_ANY] * len(deps),
        out_specs=pl.BlockSpec((tr, D), lambda i: (i, 0)),
        compiler_params=_params(("parallel",)),
    )(x, g, *deps)


def _rmsnorm_bwd(x, dh, dres, g, name, tr=128):
    T, D = x.shape
    tr = _tile(T, tr)

    def body(x_ref, dh_ref, dres_ref, g_ref, dx_ref, dxb_ref, dg_ref):
        xv = x_ref[...]
        r = lax.rsqrt(jnp.mean(xv * xv, axis=-1, keepdims=True) + RMS_EPS)
        xn = xv * r
        dhv = dh_ref[...].astype(F32)
        dxn = dhv * g_ref[...]
        dx = dres_ref[...] + r * (dxn - xn * jnp.mean(dxn * xn, axis=-1, keepdims=True))
        dx_ref[...] = dx
        dxb_ref[...] = dx.astype(BF16)

        @pl.when(pl.program_id(0) == 0)
        def _():
            dg_ref[...] = jnp.zeros_like(dg_ref)

        dg_ref[...] += jnp.sum(dhv * xn, axis=0, keepdims=True)

    blk = pl.BlockSpec((tr, D), lambda i: (i, 0))
    vec = pl.BlockSpec((1, D), lambda i: (0, 0))
    return pl.pallas_call(
        body, name=name,
        out_shape=(jax.ShapeDtypeStruct((T, D), F32), jax.ShapeDtypeStruct((T, D), BF16),
                   jax.ShapeDtypeStruct((1, D), F32)),
        grid=(T // tr,),
        in_specs=[blk, blk, blk, vec],
        out_specs=(blk, blk, vec),
        compiler_params=_params(("arbitrary",)),
    )(x, dh, dres, g)


def _loss_head(y, target, name, tr=128):
    T, D = y.shape
    tr = _tile(T, tr)
    n = T // tr

    def body(y_ref, t_ref, dy_ref, dyb_ref, loss_ref, acc):
        i = pl.program_id(0)

        @pl.when(i == 0)
        def _():
            acc[...] = jnp.zeros_like(acc)

        err = y_ref[...] - t_ref[...]
        dy = err * (1.0 / D)
        dy_ref[...] = dy
        dyb_ref[...] = dy.astype(BF16)
        acc[...] += jnp.sum(err * err, axis=0, keepdims=True)

        @pl.when(i == n - 1)
        def _():
            loss_ref[...] = jnp.sum(acc[...], axis=1, keepdims=True) * (0.5 / D)

    blk = pl.BlockSpec((tr, D), lambda i: (i, 0))
    return pl.pallas_call(
        body, name=name,
        out_shape=(jax.ShapeDtypeStruct((T, D), F32), jax.ShapeDtypeStruct((T, D), BF16),
                   jax.ShapeDtypeStruct((1, 1), F32)),
        grid=(n,),
        in_specs=[blk, blk],
        out_specs=(blk, blk, pl.BlockSpec((1, 1), lambda i: (0, 0))),
        scratch_shapes=[pltpu.VMEM((1, D), F32)],
        compiler_params=_params(("arbitrary",)),
    )(y, target)


def _gelu(x):
    t = jnp.tanh(GELU_C * (x + 0.044715 * (x * x * x)))
    return x * (0.5 * (1.0 + t)), t


def _adamw(parts, w, m, v, name, tr=256, tc=1024):
    n, R, C = parts.shape
    tr, tc = _tile(R, tr), _tile(C, tc)
    c1 = 1.0 - ADAM_B1 ** ADAM_STEP
    c2 = 1.0 - ADAM_B2 ** ADAM_STEP

    def body(p_ref, w_ref, m_ref, v_ref, g_out, d_out, m_out, v_out):
        g = p_ref[0].astype(F32)
        for k in range(1, n):
            g = g + p_ref[k].astype(F32)
        mn = ADAM_B1 * m_ref[...] + (1.0 - ADAM_B1) * g
        vn = ADAM_B2 * v_ref[...] + (1.0 - ADAM_B2) * (g * g)
        m_hat = mn / c1
        v_hat = vn / c2
        g_out[...] = g
        d_out[...] = -ADAM_LR * (m_hat / (jnp.sqrt(v_hat) + ADAM_EPS) + ADAM_WD * w_ref[...])
        m_out[...] = mn
        v_out[...] = vn

    blk = pl.BlockSpec((tr, tc), lambda r, c: (r, c))
    return pl.pallas_call(
        body, name=name,
        out_shape=tuple([jax.ShapeDtypeStruct((R, C), F32)] * 4),
        grid=(R // tr, C // tc),
        in_specs=[pl.BlockSpec((n, tr, tc), lambda r, c: (0, r, c)), blk, blk, blk],
        out_specs=(blk, blk, blk, blk),
        compiler_params=_params(("parallel", "parallel")),
    )(parts, w, m, v)


def _sum_parts(parts, name):
    n, R, C = parts.shape

    def body(p_ref, o_ref):
        g = p_ref[0].astype(F32)
        for k in range(1, n):
            g = g + p_ref[k].astype(F32)
        o_ref[...] = g

    return pl.pallas_call(
        body, name=name,
        out_shape=jax.ShapeDtypeStruct((R, C), F32),
        compiler_params=pltpu.CompilerParams(vmem_limit_bytes=VMEM_LIMIT),
    )(parts)


def _head_norm(xv):
    xv = xv.astype(F32)
    r = lax.rsqrt(jnp.mean(xv * xv, axis=-1, keepdims=True) + RMS_EPS)
    return xv * r, r


FWD_HEADS = 4
BWD_HEADS = 2


def _attn_specs(S, H, HP):
    def spec(part):
        return pl.BlockSpec((S, HP * HEAD_DIM), lambda b, h, qi: (b, part * (H // HP) + h))
    return spec


def _lanes(hh):
    return slice(hh * HEAD_DIM, (hh + 1) * HEAD_DIM)


def _attn_fwd(proj, q_g, k_g, B, S, H, name):
    TQ = _tile(S, 256)
    nq = S // TQ
    scale = 1.0 / math.sqrt(HEAD_DIM)
    HP = FWD_HEADS
    ATT_W = HP * HEAD_DIM
    heads = range(HP)

    def body(q_ref, k_ref, v_ref, gate_ref, qg_ref, kg_ref, og_ref, o_ref, w_ref, sg_ref, qn_s, kn_s):
        @pl.when(pl.program_id(2) == 0)
        def _():
            for hh in heads:
                qn_s[:, _lanes(hh)] = (_head_norm(q_ref[:, _lanes(hh)])[0] * qg_ref[...]).astype(BF16)
                kn_s[:, _lanes(hh)] = (_head_norm(k_ref[:, _lanes(hh)])[0] * kg_ref[...]).astype(BF16)

        row = lax.broadcasted_iota(jnp.int32, (TQ, TQ), 0)
        col = lax.broadcasted_iota(jnp.int32, (TQ, TQ), 1)
        later = _ones_where(row > col)
        causal = col < row

        def q_block(qi):
            q0 = pl.multiple_of(qi * TQ, TQ)

            def both(ki, state, diag):
                k0 = pl.multiple_of(ki * TQ, TQ)
                z = [_dot(qn_s[pl.ds(q0, TQ), _lanes(hh)], kn_s[pl.ds(k0, TQ), _lanes(hh)], NT) * scale
                     for hh in heads]
                ls = [_log_sigmoid(zz) for zz in z]
                l1m = [a - zz for a, zz in zip(ls, z)]
                if diag:
                    l1m = [jnp.where(causal, a, 0.0) for a in l1m]
                suffix = [_dot(l1m[hh].astype(BF16), later, NN) + state[hh][0] for hh in heads]
                w = [jnp.exp(a + sfx) for a, sfx in zip(ls, suffix)]
                if diag:
                    w = [jnp.where(causal, a, 0.0) for a in w]
                wb = [a.astype(BF16) for a in w]
                for hh in heads:
                    w_ref[hh, ki] = wb[hh]
                    sg_ref[hh, ki] = jnp.exp(ls[hh]).astype(BF16)
                acc = [state[hh][1] + _dot(wb[hh], v_ref[pl.ds(k0, TQ), _lanes(hh)], NN) for hh in heads]
                return tuple((state[hh][0] + jnp.sum(l1m[hh], axis=1, keepdims=True), acc[hh]) for hh in heads)

            zero = (jnp.zeros((TQ, 1), F32), jnp.zeros((TQ, HEAD_DIM), F32))
            state = both(qi, (zero,) * HP, True)
            state = lax.fori_loop(0, qi, lambda i, st: both(qi - 1 - i, st, False), state)
            for hh in heads:
                acc = state[hh][1]
                o_ref[:, _lanes(hh)] = acc.astype(BF16)
                gate = gate_ref[pl.ds(q0, TQ), _lanes(hh)].astype(F32)
                og_ref[:, _lanes(hh)] = (acc * (gate * _sigmoid(gate))).astype(BF16)

        q_block(pl.program_id(2))

    spec = _attn_specs(S, H, HP)
    vec = pl.BlockSpec((1, HEAD_DIM), lambda b, h, qi: (0, 0))
    out = pl.BlockSpec((TQ, ATT_W), lambda b, h, qi: (b * nq + qi, h))
    kept = pl.BlockSpec((None, HP, None, nq, TQ, TQ), lambda b, h, qi: (b, h, qi, 0, 0, 0))
    kept_shape = jax.ShapeDtypeStruct((B, H, nq, nq, TQ, TQ), BF16)
    return pl.pallas_call(
        body, name=name,
        out_shape=(jax.ShapeDtypeStruct((B * S, H * HEAD_DIM), BF16),) * 2 + (kept_shape,) * 2,
        grid=(B, H // HP, nq),
        in_specs=[spec(0), spec(1), spec(2), spec(3), vec, vec],
        out_specs=(out, out, kept, kept),
        scratch_shapes=[pltpu.VMEM((S, ATT_W), BF16)] * 2,
        compiler_params=_params(("parallel", "parallel", "arbitrary")),
    )(proj, proj, proj, proj, q_g, k_g)


def _attn_bwd(proj, o, dog, w_kept, sg_kept, q_g, k_g, B, S, H, name):
    TQ = _tile(S, 256)
    nq = S // TQ
    scale = 1.0 / math.sqrt(HEAD_DIM)
    HP = BWD_HEADS
    ATT_W = HP * HEAD_DIM
    heads = range(HP)

    def body(q_ref, k_ref, v_ref, gate_ref, o_ref, dog_ref, w_ref, sg_ref, qg_ref, kg_ref,
             dq_ref, dk_ref, dv_ref, dgate_ref, dqg_ref, dkg_ref,
             qn_s, kn_s, do_s, dkn_s, dv_s):
        qi = pl.program_id(2)

        @pl.when((pl.program_id(0) == 0) & (pl.program_id(1) == 0) & (qi == 0))
        def _():
            dqg_ref[...] = jnp.zeros_like(dqg_ref)
            dkg_ref[...] = jnp.zeros_like(dkg_ref)

        @pl.when(qi == 0)
        def _():
            for hh in heads:
                qn_s[:, _lanes(hh)] = (_head_norm(q_ref[:, _lanes(hh)])[0] * qg_ref[...]).astype(BF16)
                kn_s[:, _lanes(hh)] = (_head_norm(k_ref[:, _lanes(hh)])[0] * kg_ref[...]).astype(BF16)
            gate = gate_ref[...].astype(F32)
            sg = _sigmoid(gate)
            dog_v = dog_ref[...].astype(F32)
            do_s[...] = (dog_v * (gate * sg)).astype(BF16)
            dgate_ref[...] = (dog_v * o_ref[...].astype(F32) * (sg * (1.0 + gate * (1.0 - sg)))).astype(BF16)
            dkn_s[...] = jnp.zeros_like(dkn_s)
            dv_s[...] = jnp.zeros_like(dv_s)

        row = lax.broadcasted_iota(jnp.int32, (TQ, TQ), 0)
        col = lax.broadcasted_iota(jnp.int32, (TQ, TQ), 1)
        earlier = _ones_where(row < col)
        causal = col < row

        def norm_bwd(xv, g_ref, dn, dg_ref):
            xh, r = _head_norm(xv)
            dg_ref[...] += jnp.sum(dn * xh, axis=0, keepdims=True)
            dxh = dn * g_ref[...]
            return (r * (dxh - xh * jnp.mean(dxh * xh, axis=-1, keepdims=True))).astype(BF16)

        def q_block():
            q0 = pl.multiple_of(qi * TQ, TQ)

            def grads_both(ki, state, diag):
                k0 = pl.multiple_of(ki * TQ, TQ)
                qb = [qn_s[pl.ds(q0, TQ), _lanes(hh)] for hh in heads]
                dob = [do_s[pl.ds(q0, TQ), _lanes(hh)] for hh in heads]
                wb = [w_ref[hh, ki] for hh in heads]
                da = [_dot(dob[hh], v_ref[pl.ds(k0, TQ), _lanes(hh)], NT) * wb[hh].astype(F32) for hh in heads]
                for hh in heads:
                    dv_s[pl.ds(k0, TQ), _lanes(hh)] += _dot(wb[hh], dob[hh], TN)
                prefix = [_dot(da[hh].astype(BF16), earlier, NN) + state[hh][0] for hh in heads]
                dzb = []
                for hh in heads:
                    sgz = sg_ref[hh, ki].astype(F32)
                    dz = da[hh] * (1.0 - sgz) - sgz * prefix[hh]
                    if diag:
                        dz = jnp.where(causal, dz, 0.0)
                    dzb.append((dz * scale).astype(BF16))
                dq = [state[hh][1] + _dot(dzb[hh], kn_s[pl.ds(k0, TQ), _lanes(hh)], NN) for hh in heads]
                for hh in heads:
                    dkn_s[pl.ds(k0, TQ), _lanes(hh)] += _dot(dzb[hh], qb[hh], TN)
                return tuple((state[hh][0] + jnp.sum(da[hh], axis=1, keepdims=True), dq[hh]) for hh in heads)

            zero = (jnp.zeros((TQ, 1), F32), jnp.zeros((TQ, HEAD_DIM), F32))
            state = lax.fori_loop(0, qi, lambda i, st: grads_both(i, st, False), (zero,) * HP)
            state = grads_both(qi, state, True)
            for hh in heads:
                dq_ref[:, _lanes(hh)] = norm_bwd(q_ref[pl.ds(q0, TQ), _lanes(hh)], qg_ref, state[hh][1], dqg_ref)

        q_block()

        @pl.when(qi == nq - 1)
        def _():
            for hh in heads:
                dk_ref[:, _lanes(hh)] = norm_bwd(k_ref[:, _lanes(hh)], kg_ref, dkn_s[:, _lanes(hh)], dkg_ref)
            dv_ref[...] = dv_s[...].astype(BF16)

    spec = _attn_specs(S, H, HP)
    vec = pl.BlockSpec((1, HEAD_DIM), lambda b, h, qi: (0, 0))
    blk = pl.BlockSpec((S, ATT_W), lambda b, h, qi: (b, h))
    rows = pl.BlockSpec((TQ, ATT_W), lambda b, h, qi: (b * nq + qi, h))
    kept = pl.BlockSpec((None, HP, None, nq, TQ, TQ), lambda b, h, qi: (b, h, qi, 0, 0, 0))
    big = jax.ShapeDtypeStruct((B * S, H * HEAD_DIM), BF16)
    small = jax.ShapeDtypeStruct((1, HEAD_DIM), F32)
    return pl.pallas_call(
        body, name=name,
        out_shape=(big, big, big, big, small, small),
        grid=(B, H // HP, nq),
        in_specs=[spec(0), spec(1), spec(2), spec(3), blk, blk, kept, kept, vec, vec],
        out_specs=(rows, blk, blk, blk, vec, vec),
        scratch_shapes=[pltpu.VMEM((S, ATT_W), BF16)] * 3 + [pltpu.VMEM((S, ATT_W), F32)] * 2,
        compiler_params=_params(("arbitrary", "arbitrary", "arbitrary")),
    )(proj, proj, proj, proj, o, dog, w_kept, sg_kept, q_g, k_g)


HALF = GROUPS_PER_BLOCK * STATE


def _cmul(ar, ai, br, bi):
    return ar * br - ai * bi, ar * bi + ai * br


def _cpow(ar, ai, n):
    rr = ri = None
    while n:
        if n & 1:
            rr, ri = (ar, ai) if rr is None else _cmul(rr, ri, ar, ai)
        n >>= 1
        if n:
            ar, ai = _cmul(ar, ai, ar, ai)
    return rr, ri


def _segment_carry(er, ei, lr, li, seg_len, segs_per_seq, reverse):
    Lr, Li = _cpow(lr, li, seg_len)
    pos = lax.broadcasted_iota(jnp.int32, er.shape, 0) % segs_per_seq
    outr = jnp.zeros_like(er)
    outi = jnp.zeros_like(ei)
    pr = pi = None
    for d in range(1, segs_per_seq):
        shift = (SUBLANES - d) if reverse else d
        sr = pltpu.roll(er, shift, 0)
        si = pltpu.roll(ei, shift, 0)
        ok = (pos + d < segs_per_seq) if reverse else (pos >= d)
        sr = jnp.where(ok, sr, 0.0)
        si = jnp.where(ok, si, 0.0)
        if pr is not None:
            sr, si = _cmul(sr, si, pr, pi)
        outr = outr + sr
        outi = outi + si
        pr, pi = (Lr, Li) if pr is None else _cmul(pr, pi, Lr, Li)
    return outr, outi


def _s5_sizes(T, B):
    assert SUBLANES % B == 0
    segs_per_seq = SUBLANES // B
    n_steps = T // SUBLANES
    cj = _tile(n_steps, 64)
    return segs_per_seq, n_steps, cj


def _s5_fwd(u_p, wb, wc, lam, dvec, B, name):
    T, C = u_p.shape
    nb = C // 128
    segs_per_seq, n_steps, cj = _s5_sizes(T, B)
    n_chunks = n_steps // cj
    rows = cj * SUBLANES

    def body(u_ref, wb_ref, wc_ref, lam_ref, d_ref, y_ref, hin_ref, bu_s, h_s):
        lr = jnp.broadcast_to(lam_ref[:, :HALF], (SUBLANES, HALF))
        li = jnp.broadcast_to(lam_ref[:, HALF:], (SUBLANES, HALF))

        def scan_chunk(c, hr, hi, store):
            r0 = pl.multiple_of(c * rows, rows)
            if not store:
                bu_s[pl.ds(r0, rows), :] = _dot(u_ref[pl.ds(r0, rows), :], wb_ref[...], NN)

            def step(j, carry):
                hr, hi = carry
                o = pl.multiple_of(j * SUBLANES, SUBLANES)
                at = pl.multiple_of(r0 + o, SUBLANES)
                nr = lr * hr - li * hi + bu_s[pl.ds(at, SUBLANES), :HALF]
                ni = lr * hi + li * hr + bu_s[pl.ds(at, SUBLANES), HALF:]
                if store:
                    h_s[pl.ds(o, SUBLANES), :HALF] = nr
                    h_s[pl.ds(o, SUBLANES), HALF:] = ni
                return nr, ni

            hr, hi = lax.fori_loop(0, cj, step, (hr, hi))
            if store:
                uv = u_ref[pl.ds(r0, rows), :].astype(F32)
                y_ref[pl.ds(r0, rows), :] = _dot(h_s[...].astype(BF16), wc_ref[...], NN) + d_ref[...] * uv
            return hr, hi

        zero = jnp.zeros((SUBLANES, HALF), F32)
        er, ei = lax.fori_loop(0, n_chunks, lambda c, h: scan_chunk(c, h[0], h[1], False), (zero, zero))
        h0r, h0i = _segment_carry(er, ei, lr, li, n_steps, segs_per_seq, False)
        hin_ref[:, :HALF] = h0r
        hin_ref[:, HALF:] = h0i
        lax.fori_loop(0, n_chunks, lambda c, h: scan_chunk(c, h[0], h[1], True), (h0r, h0i))

    return pl.pallas_call(
        body, name=name,
        out_shape=(jax.ShapeDtypeStruct((T, C), F32), jax.ShapeDtypeStruct((nb, SUBLANES, 2 * HALF), F32)),
        grid=(nb,),
        in_specs=[pl.BlockSpec((T, 128), lambda g: (0, g)),
                  pl.BlockSpec((None, 128, 2 * HALF), lambda g: (g, 0, 0)),
                  pl.BlockSpec((None, 2 * HALF, 128), lambda g: (g, 0, 0)),
                  pl.BlockSpec((None, 1, 2 * HALF), lambda g: (g, 0, 0)),
                  pl.BlockSpec((1, 128), lambda g: (0, g))],
        out_specs=(pl.BlockSpec((T, 128), lambda g: (0, g)),
                   pl.BlockSpec((None, SUBLANES, 2 * HALF), lambda g: (g, 0, 0))),
        scratch_shapes=[pltpu.VMEM((T, 2 * HALF), F32), pltpu.VMEM((rows, 2 * HALF), F32)],
        compiler_params=_params(("parallel",)),
    )(u_p, wb, wc, lam, dvec)


def _s5_bwd(u_p, dy_p, wb, wbt, wc, wct, lam, dvec, h_in, B, name):
    T, C = u_p.shape
    nb = C // 128
    segs_per_seq, n_steps, cj = _s5_sizes(T, B)
    n_chunks = n_steps // cj
    rows = cj * SUBLANES

    def body(u_ref, dy_ref, wb_ref, wbt_ref, wc_ref, wct_ref, lam_ref, d_ref, hin_ref,
             du_ref, dwb_ref, dwc_ref, dlam_ref, dd_ref, h_all, dh_all, x_s, g_s):
        lr = jnp.broadcast_to(lam_ref[:, :HALF], (SUBLANES, HALF))
        li = jnp.broadcast_to(lam_ref[:, HALF:], (SUBLANES, HALF))
        zero = jnp.zeros((SUBLANES, HALF), F32)

        h_all[pl.ds(0, SUBLANES), :] = hin_ref[...]

        def fwd_chunk(c, carry):
            r0 = pl.multiple_of(c * rows, rows)
            x_s[...] = _dot(u_ref[pl.ds(r0, rows), :], wb_ref[...], NN)

            def step(j, carry):
                hr, hi = carry
                o = pl.multiple_of(j * SUBLANES, SUBLANES)
                nr = lr * hr - li * hi + x_s[pl.ds(o, SUBLANES), :HALF]
                ni = lr * hi + li * hr + x_s[pl.ds(o, SUBLANES), HALF:]
                late = pl.multiple_of(r0 + o + SUBLANES, SUBLANES)
                h_all[pl.ds(late, SUBLANES), :HALF] = nr
                h_all[pl.ds(late, SUBLANES), HALF:] = ni
                return nr, ni

            return lax.fori_loop(0, cj, step, carry)

        lax.fori_loop(0, n_chunks, fwd_chunk, (hin_ref[:, :HALF], hin_ref[:, HALF:]))

        def bwd_chunk(i, carry, store):
            c = n_chunks - 1 - i
            r0 = pl.multiple_of(c * rows, rows)
            dyv = dy_ref[pl.ds(r0, rows), :]
            if not store:
                dh_all[pl.ds(r0, rows), :] = _dot(dyv, wct_ref[...], NN)

            def step(jj, carry):
                ar, ai, accr, acci = carry
                j = cj - 1 - jj
                o = pl.multiple_of(j * SUBLANES, SUBLANES)
                prev = pl.multiple_of(r0 + o, SUBLANES)
                nr = lr * ar + li * ai + dh_all[pl.ds(prev, SUBLANES), :HALF]
                ni = lr * ai - li * ar + dh_all[pl.ds(prev, SUBLANES), HALF:]
                if store:
                    g_s[pl.ds(o, SUBLANES), :HALF] = nr
                    g_s[pl.ds(o, SUBLANES), HALF:] = ni
                    pr = h_all[pl.ds(prev, SUBLANES), :HALF]
                    pi = h_all[pl.ds(prev, SUBLANES), HALF:]
                    accr = accr + nr * pr + ni * pi
                    acci = acci + ni * pr - nr * pi
                return nr, ni, accr, acci

            carry = lax.fori_loop(0, cj, step, carry)
            if store:
                gb = g_s[...].astype(BF16)
                uv = u_ref[pl.ds(r0, rows), :]
                dyf = dyv.astype(F32)
                du_ref[pl.ds(r0, rows), :] = (_dot(gb, wbt_ref[...], NN) + d_ref[...] * dyf).astype(BF16)
                dwb_ref[...] += _dot(uv, gb, TN)
                hb = h_all[pl.ds(pl.multiple_of(r0 + SUBLANES, SUBLANES), rows), :].astype(BF16)
                dwc_ref[...] += _dot(hb, dyv, TN)
                dd_ref[...] += jnp.sum(dyf * uv.astype(F32), axis=0, keepdims=True)
            return carry

        er, ei, _, _ = lax.fori_loop(0, n_chunks, lambda i, c: bwd_chunk(i, c, False), (zero, zero, zero, zero))
        a0r, a0i = _segment_carry(er, ei, lr, -li, n_steps, segs_per_seq, True)
        dwb_ref[...] = jnp.zeros_like(dwb_ref)
        dwc_ref[...] = jnp.zeros_like(dwc_ref)
        dd_ref[...] = jnp.zeros_like(dd_ref)
        _, _, accr, acci = lax.fori_loop(0, n_chunks, lambda i, c: bwd_chunk(i, c, True), (a0r, a0i, zero, zero))
        dlam_ref[:, :HALF] = jnp.sum(accr, axis=0, keepdims=True)
        dlam_ref[:, HALF:] = jnp.sum(acci, axis=0, keepdims=True)

    col = pl.BlockSpec((T, 128), lambda g: (0, g))
    vec = pl.BlockSpec((1, 128), lambda g: (0, g))

    def per_block(*shape):
        return pl.BlockSpec((None,) + shape, lambda g: (g, 0, 0))

    return pl.pallas_call(
        body, name=name,
        out_shape=(jax.ShapeDtypeStruct((T, C), BF16),
                   jax.ShapeDtypeStruct((nb, 128, 2 * HALF), F32),
                   jax.ShapeDtypeStruct((nb, 2 * HALF, 128), F32),
                   jax.ShapeDtypeStruct((nb, 1, 2 * HALF), F32),
                   jax.ShapeDtypeStruct((1, C), F32)),
        grid=(nb,),
        in_specs=[col, col, per_block(128, 2 * HALF), per_block(2 * HALF, 128), per_block(2 * HALF, 128),
                  per_block(128, 2 * HALF), per_block(1, 2 * HALF), vec, per_block(SUBLANES, 2 * HALF)],
        out_specs=(col, per_block(128, 2 * HALF), per_block(2 * HALF, 128), per_block(1, 2 * HALF), vec),
        scratch_shapes=[pltpu.VMEM((T + SUBLANES, 2 * HALF), F32), pltpu.VMEM((T, 2 * HALF), F32),
                        pltpu.VMEM((rows, 2 * HALF), F32), pltpu.VMEM((rows, 2 * HALF), F32)],
        compiler_params=_params(("parallel",)),
    )(u_p, dy_p, wb, wbt, wc, wct, lam, dvec, h_in)


def _discretize(a_re, a_im, log_dt, b_re, b_im):
    dt = jnp.exp(log_dt)[:, None]
    mag = jnp.exp(a_re * dt)
    lam_re = mag * jnp.cos(a_im * dt)
    lam_im = mag * jnp.sin(a_im * dt)
    den = a_re * a_re + a_im * a_im
    f_re = ((lam_re - 1.0) * a_re + lam_im * a_im) / den
    f_im = (lam_im * a_re - (lam_re - 1.0) * a_im) / den
    bb_re = f_re[..., None] * b_re - f_im[..., None] * b_im
    bb_im = f_re[..., None] * b_im + f_im[..., None] * b_re
    return lam_re, lam_im, bb_re, bb_im


def _block_diag_in(bb_re, bb_im):
    eye = jnp.eye(GROUPS_PER_BLOCK, dtype=F32)

    def one(bb):
        t = bb.reshape(-1, GROUPS_PER_BLOCK, STATE, GROUP)
        return jnp.einsum('gapi,ab->gaibp', t, eye).reshape(-1, 128, HALF)

    return jnp.concatenate([one(bb_re), one(bb_im)], axis=-1)


def _block_diag_in_grad(dwb):
    eye = jnp.eye(GROUPS_PER_BLOCK, dtype=F32)

    def one(d):
        t = d.reshape(-1, GROUPS_PER_BLOCK, GROUP, GROUPS_PER_BLOCK, STATE)
        return jnp.einsum('gaibp,ab->gapi', t, eye).reshape(-1, STATE, GROUP)

    return one(dwb[..., :HALF]), one(dwb[..., HALF:])


def _block_diag_out(c_re, c_im):
    eye = jnp.eye(GROUPS_PER_BLOCK, dtype=F32)

    def one(cc):
        t = cc.reshape(-1, GROUPS_PER_BLOCK, GROUP, STATE)
        return jnp.einsum('gaip,ab->gbpai', t, eye).reshape(-1, HALF, 128)

    return jnp.concatenate([one(c_re), -one(c_im)], axis=1)


def _block_diag_out_grad(dwc):
    eye = jnp.eye(GROUPS_PER_BLOCK, dtype=F32)

    def one(d):
        t = d.reshape(-1, GROUPS_PER_BLOCK, STATE, GROUPS_PER_BLOCK, GROUP)
        return jnp.einsum('gbpai,ab->gaip', t, eye).reshape(-1, GROUP, STATE)

    return one(dwc[:, :HALF]), -one(dwc[:, HALF:])


def _pack(parts):
    return jnp.concatenate([p.reshape(-1, PACK_W) for p in parts], axis=0)


def _unpack(buf, shapes):
    lead = buf.shape[:-2]
    out, r = [], 0
    for s in shapes:
        n = math.prod(s) // PACK_W
        out.append(buf[..., r:r + n, :].reshape(lead + tuple(s)))
        r += n
    return out


def _permute_rows(a, n_steps):
    T, C = a.shape
    return a.reshape(SUBLANES, n_steps, C).transpose(1, 0, 2).reshape(T, C)


def _unpermute_rows(a, n_steps):
    T, C = a.shape
    return a.reshape(n_steps, SUBLANES, C).transpose(1, 0, 2).reshape(T, C)


def kernel(x, norm_g, attn_w_in, attn_q_g, attn_k_g, attn_w_out, ssm_w_in, ssm_A_re, ssm_A_im, ssm_log_dt, ssm_B_re, ssm_B_im, ssm_C_re, ssm_C_im, ssm_D, ssm_glu_w, ssm_glu_b, ssm_w_out, loss_target, m_norm_g, m_attn_w_in, m_attn_q_g, m_attn_k_g, m_attn_w_out, m_ssm_w_in, m_ssm_A_re, m_ssm_A_im, m_ssm_log_dt, m_ssm_B_re, m_ssm_B_im, m_ssm_C_re, m_ssm_C_im, m_ssm_D, m_ssm_glu_w, m_ssm_glu_b, m_ssm_w_out, v_norm_g, v_attn_w_in, v_attn_q_g, v_attn_k_g, v_attn_w_out, v_ssm_w_in, v_ssm_A_re, v_ssm_A_im, v_ssm_log_dt, v_ssm_B_re, v_ssm_B_im, v_ssm_C_re, v_ssm_C_im, v_ssm_D, v_ssm_glu_w, v_ssm_glu_b, v_ssm_w_out):
    B, S, D = x.shape
    T = B * S
    H = D // HEAD_DIM
    G_loc = ssm_A_re.shape[1]
    G = G_loc * N_DEV
    n_steps = T // SUBLANES
    me, chip = _my_index(), _my_chip()
    xf = x.reshape(T, D)
    target = loss_target.reshape(T, D)

    small_shapes = [(G_loc, STATE), (G_loc, STATE), (G_loc, STATE, GROUP), (G_loc, STATE, GROUP),
                    (G_loc, GROUP, STATE), (G_loc, GROUP, STATE), (G_loc * GROUP,), (G_loc * GROUP,)]
    disc_in = (ssm_A_re[0], ssm_A_im[0], ssm_log_dt[0], ssm_B_re[0], ssm_B_im[0])
    (lam_re, lam_im, bb_re, bb_im), disc_vjp = jax.vjp(_discretize, *disc_in)
    small = _pack([lam_re, lam_im, bb_re, bb_im, ssm_C_re[0], ssm_C_im[0], ssm_D[0], ssm_glu_b[0]])
    small_all = _exchange(small, False, "gather_small")
    c_in0 = _place(attn_w_in[0], me, N_DEV, BF16, "cast_attn_w_in")
    near = _exchange_start(c_in0, None, "near", "gather_attn_w_in_near_start", deps=(small_all,))
    beside = (near[3],)
    small_all = small_all + near[3][0, 0]
    c_out0 = _place(attn_w_out[0], me, N_DEV, BF16, "cast_attn_w_out", deps=beside)
    c_in1 = _place(ssm_w_in[0], me, N_DEV, BF16, "cast_ssm_w_in", deps=beside)
    c_glu = _place(ssm_glu_w[0], me, N_DEV, BF16, "cast_ssm_glu_w", deps=beside)
    c_out1 = _place(ssm_w_out[0], me, N_DEV, BF16, "cast_ssm_w_out", deps=beside)

    lam_re_a, lam_im_a, bb_re_a, bb_im_a, c_re_a, c_im_a, d_a, glu_b_a = [
        t.reshape((G,) + t.shape[2:]) if t.ndim > 2 else t.reshape(-1)
        for t in _unpack(small_all, small_shapes)]
    wb = _block_diag_in(bb_re_a, bb_im_a)
    wc = _block_diag_out(c_re_a, c_im_a)
    wb_b, wc_b = wb.astype(BF16), wc.astype(BF16)
    wbt_b, wct_b = wb_b.transpose(0, 2, 1), wc_b.transpose(0, 2, 1)
    lam = jnp.concatenate([lam_re_a.reshape(-1, 1, HALF), lam_im_a.reshape(-1, 1, HALF)], axis=-1)
    d_row = d_a.reshape(1, D)
    glu_b_row = glu_b_a.reshape(1, D)
    g0, g1 = norm_g[0:1], norm_g[1:2]
    q_g, k_g = attn_q_g, attn_k_g

    h0 = _rmsnorm_fwd(xf, g0, "norm0", deps=beside)
    done = (h0, c_out0, c_in1, c_glu, c_out1, wb_b, wbt_b, wc_b, wct_b, lam, d_row, glu_b_row)
    far = _exchange_start(_exchange_wait(near, done, "near", "gather_attn_w_in_near_wait"), None, "far",
                          "gather_attn_w_in_far_start")
    w_in0 = _exchange_wait(far, far[3], "far", "gather_attn_w_in_far_wait")
    s_out0 = _exchange_start(c_out0, None, "gather", "gather_attn_w_out_start", deps=(w_in0,))
    s_in1 = _exchange_start(c_in1, None, "gather", "gather_ssm_w_in_start", deps=(w_in0,))
    proj0 = _mm_nn(h0, w_in0, BF16, "attn_in", deps=(s_out0[3], s_in1[3]))
    og, o, w_kept, sg_kept = _attn_fwd(proj0, q_g, k_g, B, S, H, "attn_fwd")
    w_out0 = _exchange_wait(s_out0, og, "gather", "gather_attn_w_out_wait").reshape(1, D, D)
    s_glu = _exchange_start(c_glu, None, "gather", "gather_ssm_glu_w_start", deps=(w_out0,))
    s_out1 = _exchange_start(c_out1, None, "gather", "gather_ssm_w_out_start", deps=(w_out0,))
    x1 = _mm_nn(og, w_out0, F32, "attn_out", residual=xf, deps=(s_glu[3], s_out1[3]))

    h1 = _rmsnorm_fwd(x1, g1, "norm1")
    w_in1 = _exchange_wait(s_in1, h1, "gather", "gather_ssm_w_in_wait")
    proj1 = _mm_nn(h1, w_in1, BF16, "ssm_in")
    u_p = _permute_rows(proj1[:, :D], n_steps)
    gate1 = proj1[:, D:]
    y_p, h_in = _s5_fwd(u_p, wb_b, wc_b, lam, d_row, B, "s5_fwd")
    y_ssm = _unpermute_rows(y_p, n_steps)
    (yg,) = _ew(lambda a: (_gelu(a)[0],), [y_ssm], [], [BF16], 0, "gelu")
    w_glu = _exchange_wait(s_glu, yg, "gather", "gather_ssm_glu_w_wait").reshape(1, D, D)
    z = _mm_nn(yg, w_glu, F32, "glu_in", bias=glu_b_row)

    def glu_fwd(y, zz, gt):
        gt = gt.astype(F32)
        return (_gelu(y)[0] * _sigmoid(zz) * (gt * _sigmoid(gt)),)

    (y3,) = _ew(glu_fwd, [y_ssm, z, gate1], [], [BF16], 0, "glu_gate")
    w_out1 = _exchange_wait(s_out1, y3, "gather", "gather_ssm_w_out_wait").reshape(1, D, D)
    out = _mm_nn(y3, w_out1, F32, "ssm_out", residual=x1)

    dout, dout_b, loss_part = _loss_head(out, target, "loss")
    loss = lax.psum(loss_part[0, 0], ("x", "y", "c"))

    p_w_out1 = _mm_tn(y3, dout_b, 1, BF16, "ssm_out_dw").reshape(N_DEV, D // N_DEV, D)
    sc_out1 = _exchange_start(_place(p_w_out1, me, N_DEV, BF16, "place_ssm_w_out"), p_w_out1, "scatter", "scatter_ssm_w_out_start")
    dy3 = _mm_nt(dout_b, w_out1, F32, "ssm_out_dx", deps=(sc_out1[3],))

    def glu_bwd(d3, y, zz, gt):
        gt = gt.astype(F32)
        sg = _sigmoid(gt)
        sz = _sigmoid(zz)
        ygv, _ = _gelu(y)
        dy2 = d3 * (gt * sg)
        dgate = d3 * (ygv * sz) * (sg * (1.0 + gt * (1.0 - sg)))
        dz = dy2 * ygv * (sz * (1.0 - sz))
        return dz, dgate, dy2 * sz, jnp.sum(dz, axis=0, keepdims=True)

    dz_b, dgate1, dyg_a, dglu_b = _ew(glu_bwd, [dy3, y_ssm, z, gate1], [], [BF16, BF16, F32], 1, "glu_gate_bwd")
    p_w_glu = _mm_tn(yg, dz_b, 1, BF16, "glu_in_dw").reshape(N_DEV, D // N_DEV, D)
    sc_glu = _exchange_start(_place(p_w_glu, me, N_DEV, BF16, "place_ssm_glu_w"), p_w_glu, "scatter", "scatter_ssm_glu_w_start")
    dyg_b = _mm_nt(dz_b, w_glu, F32, "glu_in_dx", deps=(sc_glu[3],))

    def gelu_bwd(da, db, y):
        _, t = _gelu(y)
        dg = 0.5 * (1.0 + t) + 0.5 * y * (1.0 - t * t) * (GELU_C * (1.0 + 3.0 * 0.044715 * (y * y)))
        return ((da + db) * dg,)

    (dy_ssm,) = _ew(gelu_bwd, [dyg_a, dyg_b, y_ssm], [], [BF16], 0, "gelu_bwd")
    dy_p = _permute_rows(dy_ssm, n_steps)
    du_p, dwb, dwc, dlam, dd = _s5_bwd(u_p, dy_p, wb_b, wbt_b, wc_b, wct_b, lam, d_row, h_in, B, "s5_bwd")
    du = _unpermute_rows(du_p, n_steps)
    dproj1 = jnp.concatenate([du, dgate1], axis=1)
    p_w_in1 = _mm_tn(h1, dproj1, N_DEV, BF16, "ssm_in_dw")
    sc_in1 = _exchange_start(_place(p_w_in1, me, N_DEV, BF16, "place_ssm_w_in"), p_w_in1, "scatter", "scatter_ssm_w_in_start")
    dh1 = _mm_nt(dproj1, w_in1, F32, "ssm_in_dx", deps=(sc_in1[3],))
    dx1, dx1_b, dg1 = _rmsnorm_bwd(x1, dh1, dout, g1, "norm1_bwd")

    p_w_out0 = _mm_tn(og, dx1_b, 1, BF16, "attn_out_dw").reshape(N_DEV, D // N_DEV, D)
    sc_out0 = _exchange_start(_place(p_w_out0, me, N_DEV, BF16, "place_attn_w_out"), p_w_out0, "scatter", "scatter_attn_w_out_start")
    dog = _mm_nt(dx1_b, w_out0, BF16, "attn_out_dx", deps=(sc_out0[3],))
    dq, dk, dv, dgate0, dqg, dkg = _attn_bwd(proj0, o, dog, w_kept, sg_kept, q_g, k_g, B, S, H, "attn_bwd")
    dproj0 = jnp.concatenate([dq, dk, dv, dgate0], axis=1)
    p_w_in0 = _mm_tn(h0, dproj0, N_DEV, BF16, "attn_in_dw")
    pair = _exchange_start(lax.empty((N_DEV // 2,) + p_w_in0.shape[1:], BF16), p_w_in0, "pair", "scatter_attn_w_in_pair_start")
    got, p_w_in0 = _exchange_wait(pair, pair[3], "pair", "scatter_attn_w_in_pair_wait", with_source=True)
    q_w_in0 = _pair_sum(p_w_in0, got, "scatter_attn_w_in_pair_sum")
    sc_in0 = _exchange_start(_place(q_w_in0, chip, N_DEV // 2, BF16, "place_attn_w_in"), q_w_in0, "chips", "scatter_attn_w_in_start")
    dh0 = _mm_nt(dproj0, w_in0, F32, "attn_in_dx", deps=(sc_in0[3],))
    dx, _, dg0 = _rmsnorm_bwd(xf, dh0, dx1, g0, "norm0_bwd")

    def update(started, after, w, m, v, name, kind="scatter"):
        recv = _exchange_wait(started, after, kind, "scatter_" + name + "_wait")
        return _adamw(recv, w[0], m[0], v[0], "adamw_" + name)

    r_ssm_w_out = update(sc_out1, dx, ssm_w_out, m_ssm_w_out, v_ssm_w_out, "ssm_w_out")
    r_ssm_glu_w = update(sc_glu, r_ssm_w_out[0], ssm_glu_w, m_ssm_glu_w, v_ssm_glu_w, "ssm_glu_w")
    r_ssm_w_in = update(sc_in1, r_ssm_glu_w[0], ssm_w_in, m_ssm_w_in, v_ssm_w_in, "ssm_w_in")
    r_attn_w_out = update(sc_out0, r_ssm_w_in[0], attn_w_out, m_attn_w_out, v_attn_w_out, "attn_w_out")

    dbb_re, dbb_im = _block_diag_in_grad(dwb)
    dc_re, dc_im = _block_diag_out_grad(dwc)
    dlam_re = dlam[:, 0, :HALF].reshape(G, STATE)
    dlam_im = dlam[:, 0, HALF:].reshape(G, STATE)
    by_owner = [t.reshape((N_DEV, -1)) for t in (dlam_re, dlam_im, dbb_re, dbb_im, dc_re, dc_im, dd, dglu_b)]
    small_parts = jnp.concatenate([t.reshape(N_DEV, -1, PACK_W) for t in by_owner], axis=1)
    small_sum = _sum_parts(_exchange(small_parts, True, "scatter_small"), "sum_small")
    s_lam_re, s_lam_im, s_bb_re, s_bb_im, s_c_re, s_c_im, s_d, s_glu_b = _unpack(small_sum, small_shapes)
    g_a_re, g_a_im, g_log_dt, g_b_re, g_b_im = disc_vjp((s_lam_re, s_lam_im, s_bb_re, s_bb_im))

    local_names = ["ssm_A_re", "ssm_A_im", "ssm_log_dt", "ssm_B_re", "ssm_B_im", "ssm_C_re", "ssm_C_im",
                   "ssm_D", "ssm_glu_b"]
    local_g = [g_a_re, g_a_im, g_log_dt, g_b_re, g_b_im, s_c_re, s_c_im, s_d, s_glu_b]
    local_w = [ssm_A_re, ssm_A_im, ssm_log_dt, ssm_B_re, ssm_B_im, ssm_C_re, ssm_C_im, ssm_D, ssm_glu_b]
    local_m = [m_ssm_A_re, m_ssm_A_im, m_ssm_log_dt, m_ssm_B_re, m_ssm_B_im, m_ssm_C_re, m_ssm_C_im,
               m_ssm_D, m_ssm_glu_b]
    local_v = [v_ssm_A_re, v_ssm_A_im, v_ssm_log_dt, v_ssm_B_re, v_ssm_B_im, v_ssm_C_re, v_ssm_C_im,
               v_ssm_D, v_ssm_glu_b]
    r_local = _adamw_small(local_g, local_w, local_m, local_v, None, "adamw_small")

    rep_g = [jnp.concatenate([dg0, dg1], axis=0), dqg, dkg]
    rep_w = [norm_g, attn_q_g, attn_k_g]
    rep_m = [m_norm_g, m_attn_q_g, m_attn_k_g]
    rep_v = [v_norm_g, v_attn_q_g, v_attn_k_g]
    r_rep = _adamw_small(rep_g, rep_w, rep_m, rep_v, "gather_rep", "adamw_rep")

    r_attn_w_in = update(sc_in0, r_rep[0][0], attn_w_in, m_attn_w_in, v_attn_w_in, "attn_w_in", "chips")

    res = {"attn_w_in": r_attn_w_in, "attn_w_out": r_attn_w_out, "ssm_w_in": r_ssm_w_in,
           "ssm_glu_w": r_ssm_glu_w, "ssm_w_out": r_ssm_w_out}
    ref_w = {"attn_w_in": attn_w_in, "attn_w_out": attn_w_out, "ssm_w_in": ssm_w_in,
             "ssm_glu_w": ssm_glu_w, "ssm_w_out": ssm_w_out}
    for name, r, w in zip(local_names, r_local, local_w):
        res[name], ref_w[name] = r, w
    for name, r, w in zip(["norm_g", "attn_q_g", "attn_k_g"], r_rep, rep_w):
        res[name], ref_w[name] = r, w
    order = ["norm_g", "attn_w_in", "attn_q_g", "attn_k_g", "attn_w_out", "ssm_w_in", "ssm_A_re", "ssm_A_im",
             "ssm_log_dt", "ssm_B_re", "ssm_B_im", "ssm_C_re", "ssm_C_im", "ssm_D", "ssm_glu_w", "ssm_glu_b",
             "ssm_w_out"]
    outs = [loss, dx.reshape(B, S, D)]
    for kind in range(4):
        outs += [res[n][kind].reshape(ref_w[n].shape) for n in order]
    return tuple(outs)


def _adamw_small(grads, ws, ms, vs, gather_name, name):
    sizes = [math.prod(w.shape) for w in ws]
    total = sum(sizes)
    rows = -(-total // (PACK_W * 8)) * 8
    if rows > 256:
        rows = -(-rows // 256) * 256

    def pack(ts, fill):
        flat = jnp.concatenate([t.reshape(-1).astype(F32) for t in ts])
        flat = jnp.concatenate([flat, jnp.full((rows * PACK_W - total,), fill, F32)])
        return flat.reshape(rows, PACK_W)

    g = pack(grads, 0.0)
    parts = _exchange(g, False, gather_name) if gather_name else g[None]
    res = _adamw(parts, pack(ws, 0.0), pack(ms, 0.0), pack(vs, 1.0), name)
    outs = []
    off = 0
    flats = [r.reshape(-1) for r in res]
    for n in sizes:
        outs.append(tuple(f[off:off + n] for f in flats))
        off += n
    return outs
```

```python
import functools
import math

import jax
import jax.numpy as jnp
from jax import lax
from jax.experimental import pallas as pl
from jax.experimental.pallas import tpu as pltpu

F32 = jnp.float32
BF16 = jnp.bfloat16

N_DEV = 8
HEAD_DIM = 128
GROUP = 16
STATE = 64
GROUPS_PER_BLOCK = 8
SUBLANES = 8
RMS_EPS = 1e-6
ADAM_LR, ADAM_B1, ADAM_B2, ADAM_EPS, ADAM_WD, ADAM_STEP = 0.001, 0.9, 0.999, 1e-08, 0.01, 10
VMEM_LIMIT = 56 * 1024 * 1024
GELU_C = math.sqrt(2.0 / math.pi)
PACK_W = 128


def _params(sem, **kw):
    return pltpu.CompilerParams(dimension_semantics=sem, vmem_limit_bytes=VMEM_LIMIT, **kw)


def _tile(n, t):
    t = min(n, t)
    assert n % t == 0, (n, t)
    return t


def _ones_where(cond):
    return jnp.where(cond, 1.0, 0.0).astype(BF16)


def _sigmoid(x):
    return 1.0 / (1.0 + jnp.exp(-x))


def _log_sigmoid(z):
    return jnp.minimum(z, 0.0) - jnp.log(1.0 + jnp.exp(-jnp.abs(z)))


def _dot(a, b, dims):
    return lax.dot_general(a, b, (dims, ((), ())), preferred_element_type=F32)


NN = ((1,), (0,))
NT = ((1,), (1,))
TN = ((0,), (0,))


def _exchange(x, scatter, name, deps=()):
    shape = x.shape[1:] if scatter else x.shape

    def body(*refs):
        x_ref = refs[0]
        out_ref, send_sems, recv_sems, local_sem = refs[1 + len(deps):]
        ix, iy, ic = lax.axis_index("x"), lax.axis_index("y"), lax.axis_index("c")
        me = 4 * ix + 2 * iy + ic

        def peer(k):
            kx, ky, kc = (k >> 2) & 1, (k >> 1) & 1, k & 1
            px, py, pc = ix ^ kx, iy ^ ky, ic ^ kc
            return (px, py, pc), 4 * px + 2 * py + pc

        mine = pltpu.make_async_copy(x_ref.at[me] if scatter else x_ref, out_ref.at[me], local_sem)
        mine.start()
        copies = []
        for k in range(1, N_DEV):
            pid, pidx = peer(k)
            cp = pltpu.make_async_remote_copy(
                src_ref=x_ref.at[pidx] if scatter else x_ref,
                dst_ref=out_ref.at[me],
                send_sem=send_sems.at[k - 1], recv_sem=recv_sems.at[k - 1],
                device_id=pid, device_id_type=pl.DeviceIdType.MESH)
            cp.start()
            copies.append(cp)
        for k in range(1, N_DEV):
            pid, pidx = peer(k)
            pltpu.make_async_remote_copy(
                src_ref=x_ref.at[pidx] if scatter else x_ref,
                dst_ref=out_ref.at[pidx],
                send_sem=send_sems.at[k - 1], recv_sem=recv_sems.at[k - 1],
                device_id=pid, device_id_type=pl.DeviceIdType.MESH).wait_recv()
        for cp in copies:
            cp.wait_send()
        mine.wait()

    return pl.pallas_call(
        body, name=name,
        out_shape=jax.ShapeDtypeStruct((N_DEV,) + tuple(shape), x.dtype),
        in_specs=[pl.BlockSpec(memory_space=pl.ANY)] * (1 + len(deps)),
        out_specs=pl.BlockSpec(memory_space=pl.ANY),
        scratch_shapes=[pltpu.SemaphoreType.DMA((N_DEV - 1,)), pltpu.SemaphoreType.DMA((N_DEV - 1,)),
                        pltpu.SemaphoreType.DMA],
    )(x, *deps)


_HBM = pl.BlockSpec(memory_space=pltpu.HBM)
_SEM = pl.BlockSpec(memory_space=pltpu.SEMAPHORE)
_ANY = pl.BlockSpec(memory_space=pl.ANY)
_EFFECT = pltpu.SideEffectType.DATAFLOW_SIDE_EFFECTING


def _peer(k):
    ix, iy, ic = lax.axis_index("x"), lax.axis_index("y"), lax.axis_index("c")
    px, py, pc = ix ^ ((k >> 2) & 1), iy ^ ((k >> 1) & 1), ic ^ (k & 1)
    return (px, py, pc), 4 * px + 2 * py + pc


def _my_index():
    return 4 * lax.axis_index("x") + 2 * lax.axis_index("y") + lax.axis_index("c")


def _my_chip():
    return 2 * lax.axis_index("x") + lax.axis_index("y")


def _place(src, idx, n_slots, dtype, name, deps=(), tr=256, tc=1024):
    R, C = src.shape[-2:]
    tr, tc = _tile(R, tr), _tile(C, tc)
    idx = idx.astype(jnp.int32).reshape(1)

    def body(*refs):
        refs[-1][...] = refs[1][...].astype(dtype)

    slot = pl.BlockSpec((None, tr, tc), lambda r, c, idx_ref: (idx_ref[0], r, c))
    src_spec = slot if src.ndim == 3 else pl.BlockSpec((tr, tc), lambda r, c, idx_ref: (r, c))
    land = lax.empty((n_slots, R, C), dtype)
    return pl.pallas_call(
        body, name=name,
        out_shape=jax.ShapeDtypeStruct(land.shape, dtype),
        grid_spec=pltpu.PrefetchScalarGridSpec(
            num_scalar_prefetch=1, grid=(R // tr, C // tc), in_specs=[src_spec, _ANY] + [_ANY] * len(deps),
            out_specs=slot),
        input_output_aliases={2: 0},
        compiler_params=_params(("parallel", "parallel")),
    )(idx, src, land, *deps)


def _pair_sum(x, got, name, tr=256, tc=1024):
    _, R, C = x.shape
    tr, tc = _tile(R, tr), _tile(C, tc)
    core = lax.axis_index("c").astype(jnp.int32).reshape(1)

    def body(core_ref, x_ref, got_ref, out_ref):
        out_ref[...] = (x_ref[...].astype(F32) + got_ref[...].astype(F32)).astype(out_ref.dtype)

    blk = pl.BlockSpec((None, tr, tc), lambda i, r, c, core_ref: (i, r, c))
    return pl.pallas_call(
        body, name=name,
        out_shape=jax.ShapeDtypeStruct(got.shape, got.dtype),
        grid_spec=pltpu.PrefetchScalarGridSpec(
            num_scalar_prefetch=1, grid=(N_DEV // 2, R // tr, C // tc),
            in_specs=[pl.BlockSpec((None, tr, tc), lambda i, r, c, core_ref: (2 * i + core_ref[0], r, c)), blk],
            out_specs=blk),
        compiler_params=_params(("parallel", "parallel", "parallel")),
    )(core, x, got)


_N_COPIES = {"gather": N_DEV - 1, "scatter": N_DEV - 1, "pair": N_DEV // 2, "chips": N_DEV // 2 - 1,
             "near": N_DEV // 2, "far": N_DEV // 2 - 1}


def _copies(kind, land_ref, x_ref):
    ix, iy, ic = lax.axis_index("x"), lax.axis_index("y"), lax.axis_index("c")
    me, chip = _my_index(), _my_chip()
    out = []
    if kind in ("gather", "scatter"):
        for k in range(1, N_DEV):
            pid, pidx = _peer(k)
            out.append((land_ref.at[me] if kind == "gather" else x_ref.at[pidx], land_ref.at[me], pid, land_ref.at[pidx]))
    elif kind == "pair":
        for i in range(N_DEV // 2):
            out.append((x_ref.at[2 * i + (1 - ic)], land_ref.at[i], (ix, iy, 1 - ic), land_ref.at[i]))
    elif kind in ("near", "far"):
        if kind == "near":
            out.append((land_ref.at[me], land_ref.at[me], (ix, iy, 1 - ic), land_ref.at[me + 1 - 2 * ic]))
        for k in range(1, N_DEV // 2):
            px, py = ix ^ (k >> 1), iy ^ (k & 1)
            theirs = 4 * px + 2 * py + ic
            if kind == "near":
                out.append((land_ref.at[me], land_ref.at[me], (px, py, ic), land_ref.at[theirs]))
            else:
                out.append((land_ref.at[theirs], land_ref.at[theirs], (ix, iy, 1 - ic),
                            land_ref.at[theirs + 1 - 2 * ic]))
    else:
        for k in range(1, N_DEV // 2):
            px, py = ix ^ (k >> 1), iy ^ (k & 1)
            out.append((x_ref.at[2 * px + py], land_ref.at[chip], (px, py, ic), land_ref.at[2 * px + py]))
    return out


def _exchange_start(land, x, kind, name, deps=()):
    bufs = [land] if x is None else [land, x]
    nb, n = len(bufs), _N_COPIES[kind]

    def body(*refs):
        send_sems, recv_sems = refs[nb + len(deps):nb + len(deps) + 2]
        token = refs[2 * nb + len(deps) + 2]
        for i, (src, dst, pid, _) in enumerate(_copies(kind, refs[0], refs[nb - 1])):
            pltpu.make_async_remote_copy(src_ref=src, dst_ref=dst, send_sem=send_sems.at[i], recv_sem=recv_sems.at[i],
                                         device_id=pid, device_id_type=pl.DeviceIdType.MESH).start()
        token[...] = jnp.zeros_like(token)

    out = pl.pallas_call(
        body, name=name,
        out_shape=(pltpu.SemaphoreType.DMA((n,)), pltpu.SemaphoreType.DMA((n,)))
        + tuple(pltpu.HBM(t.shape, t.dtype) for t in bufs) + (jax.ShapeDtypeStruct((8, 128), F32),),
        in_specs=(_HBM,) * nb + (_ANY,) * len(deps),
        out_specs=(_SEM, _SEM) + (_HBM,) * nb + (pl.BlockSpec(memory_space=pltpu.VMEM),),
        input_output_aliases={i: 2 + i for i in range(nb)},
        compiler_params=pltpu.CompilerParams(has_side_effects=_EFFECT),
    )(*[pltpu.with_memory_space_constraint(t, pltpu.HBM) for t in bufs], *deps)
    return out[0], out[1], out[2:2 + nb], out[2 + nb]


def _exchange_wait(started, after, kind, name, with_source=False):
    send_sems, recv_sems, bufs, _ = started
    nb = len(bufs)
    after = tuple(after) if isinstance(after, (tuple, list)) else (after,)

    def body(*refs):
        send_sems, recv_sems = refs[nb:nb + 2]
        for i, (src, _, pid, landed) in enumerate(_copies(kind, refs[0], refs[nb - 1])):
            cp = pltpu.make_async_remote_copy(src_ref=src, dst_ref=landed, send_sem=send_sems.at[i],
                                              recv_sem=recv_sems.at[i], device_id=pid,
                                              device_id_type=pl.DeviceIdType.MESH)
            cp.wait_send()
            cp.wait_recv()

    out = pl.pallas_call(
        body, name=name,
        out_shape=tuple(pltpu.HBM(t.shape, t.dtype) for t in bufs),
        in_specs=(_HBM,) * nb + (_SEM, _SEM) + (_ANY,) * len(after),
        out_specs=(_HBM,) * nb,
        input_output_aliases={i: i for i in range(nb)},
        compiler_params=pltpu.CompilerParams(has_side_effects=_EFFECT),
    )(*bufs, send_sems, recv_sems, *after)
    return tuple(out) if with_source else out[0]


FUSED_TK = 1024


def _accumulate(acc, part, step, n_steps, finish):
    if n_steps == 1:
        finish(part)
        return

    @pl.when(step == 0)
    def _():
        acc[...] = part

    @pl.when((step > 0) & (step < n_steps - 1))
    def _():
        acc[...] += part

    @pl.when(step == n_steps - 1)
    def _():
        finish(acc[...] + part)


def _mm_nn(a, b, out_dtype, name, bias=None, residual=None, deps=(), extras=(), epilogue=None,
           out_dtypes=None, tile_sums=False, tm=1024, tn=1024, tk=2048):
    M, K = a.shape
    J, K2, Nj = b.shape
    assert K == K2
    tm, tn, tk = _tile(M, tm), _tile(Nj, tn), _tile(K, tk)
    nb, nk = Nj // tn, K // tk
    out_dtypes = [out_dtype] if out_dtypes is None else list(out_dtypes)
    n_out = len(out_dtypes) + bool(tile_sums)

    def body(*refs):
        a_ref, b_ref = refs[0], refs[1]
        i = 2
        bias_ref = res_ref = None
        if bias is not None:
            bias_ref = refs[i]; i += 1
        if residual is not None:
            res_ref = refs[i]; i += 1
        extra_refs = refs[i:i + len(extras)]
        i += len(extras) + len(deps)
        o_refs, acc = refs[i:i + n_out], refs[i + n_out]

        def finish(r):
            if bias_ref is not None:
                r = r + bias_ref[...]
            if res_ref is not None:
                r = r + res_ref[...]
            res = (r,) if epilogue is None else epilogue(r, *[e[...] for e in extra_refs])
            for o_ref, v in zip(o_refs, res[:len(out_dtypes)]):
                o_ref[...] = v.astype(o_ref.dtype)
            if tile_sums:
                total = jnp.sum(res[-1], axis=1, keepdims=True)
                first = (lax.broadcasted_iota(jnp.int32, (8, 128), 0) == 0) & (lax.broadcasted_iota(jnp.int32, (8, 128), 1) == 0)
                o_refs[-1][...] = jnp.where(first, total, 0.0)

        _accumulate(acc, _dot(a_ref[...], b_ref[...], NN), pl.program_id(3), nk, finish)

    tile = pl.BlockSpec((tm, tn), lambda j, m, n, k: (m, j * nb + n))
    in_specs = [pl.BlockSpec((tm, tk), lambda j, m, n, k: (m, k)),
                pl.BlockSpec((None, tk, tn), lambda j, m, n, k: (j, k, n))]
    args = [a, b]
    if bias is not None:
        in_specs.append(pl.BlockSpec((1, tn), lambda j, m, n, k: (0, j * nb + n)))
        args.append(bias)
    if residual is not None:
        in_specs.append(tile)
        args.append(residual)
    for arr, off in extras:
        in_specs.append(pl.BlockSpec((tm, tn), lambda j, m, n, k, off=off: (m, off + j * nb + n)))
        args.append(arr)
    in_specs += [_ANY] * len(deps)
    args += list(deps)
    out_shape = [jax.ShapeDtypeStruct((M, J * Nj), d) for d in out_dtypes]
    out_specs = [tile] * len(out_dtypes)
    if tile_sums:
        out_shape.append(jax.ShapeDtypeStruct((M // tm * 8, J * nb * 128), F32))
        out_specs.append(pl.BlockSpec((8, 128), lambda j, m, n, k: (m, j * nb + n)))
    out = pl.pallas_call(
        body, name=name,
        out_shape=tuple(out_shape),
        grid=(J, M // tm, nb, nk),
        in_specs=in_specs,
        out_specs=tuple(out_specs),
        scratch_shapes=[pltpu.VMEM((tm, tn), F32)],
        compiler_params=_params(("parallel", "parallel", "parallel", "arbitrary")),
    )(*args)
    return out[0] if n_out == 1 else out


def _mm_nt(a, b, out_dtype, name, deps=(), extras=(), epilogue=None, tm=1024, tp=1024, tq=2048):
    M, Q = a.shape
    J, P, Qj = b.shape
    assert Q == J * Qj
    tm, tp, tq = _tile(M, tm), _tile(P, tp), _tile(Qj, tq)
    nq = Qj // tq

    def body(*refs):
        a_ref, b_ref = refs[:2]
        extra_refs = refs[2:2 + len(extras)]
        o_ref, acc = refs[2 + len(extras) + len(deps):]

        def finish(r):
            if epilogue is not None:
                r = epilogue(r, *[e[...] for e in extra_refs])
            o_ref[...] = r.astype(out_dtype)

        _accumulate(acc, _dot(a_ref[...], b_ref[...], NT), pl.program_id(2) * nq + pl.program_id(3), J * nq, finish)

    tile = pl.BlockSpec((tm, tp), lambda m, p, j, q: (m, p))
    return pl.pallas_call(
        body, name=name,
        out_shape=jax.ShapeDtypeStruct((M, P), out_dtype),
        grid=(M // tm, P // tp, J, nq),
        in_specs=[pl.BlockSpec((tm, tq), lambda m, p, j, q: (m, j * nq + q)),
                  pl.BlockSpec((None, tp, tq), lambda m, p, j, q: (j, p, q))] + [tile] * len(extras) + [_ANY] * len(deps),
        out_specs=tile,
        scratch_shapes=[pltpu.VMEM((tm, tp), F32)],
        compiler_params=_params(("parallel", "parallel", "arbitrary", "arbitrary")),
    )(a, b, *extras, *deps)


def _mm_tn(a, b, J, out_dtype, name, tm=1024, tn=1024, tr=2048):
    R, M = a.shape
    R2, N = b.shape
    assert R == R2 and N % J == 0
    Nj = N // J
    tm, tn, tr = _tile(M, tm), _tile(Nj, tn), _tile(R, tr)
    nb, nr = Nj // tn, R // tr

    def body(a_ref, b_ref, o_ref, acc):
        def finish(r):
            o_ref[...] = r.astype(out_dtype)

        _accumulate(acc, _dot(a_ref[...], b_ref[...], TN), pl.program_id(3), nr, finish)

    return pl.pallas_call(
        body, name=name,
        out_shape=jax.ShapeDtypeStruct((J, M, Nj), out_dtype),
        grid=(J, M // tm, nb, nr),
        in_specs=[pl.BlockSpec((tr, tm), lambda j, m, n, r: (r, m)),
                  pl.BlockSpec((tr, tn), lambda j, m, n, r: (r, j * nb + n))],
        out_specs=pl.BlockSpec((None, tm, tn), lambda j, m, n, r: (j, m, n)),
        scratch_shapes=[pltpu.VMEM((tm, tn), F32)],
        compiler_params=_params(("parallel", "parallel", "parallel", "arbitrary")),
    )(a, b)


def _ew(fn, ins, vecs, out_dtypes, n_acc, name, tr=256, tc=1024):
    T, C = ins[0].shape
    tr, tc = _tile(T, tr), _tile(C, tc)
    n_in, n_vec, n_out = len(ins), len(vecs), len(out_dtypes)

    def body(*refs):
        in_refs = refs[:n_in + n_vec]
        out_refs = refs[n_in + n_vec:n_in + n_vec + n_out]
        acc_refs = refs[n_in + n_vec + n_out:]
        res = fn(*[r[...] for r in in_refs])
        for o_ref, v in zip(out_refs, res[:n_out]):
            o_ref[...] = v.astype(o_ref.dtype)
        if n_acc:
            r = pl.program_id(1)

            @pl.when(r == 0)
            def _():
                for a_ref in acc_refs:
                    a_ref[...] = jnp.zeros_like(a_ref)

            for a_ref, v in zip(acc_refs, res[n_out:]):
                a_ref[...] += v

    blk = pl.BlockSpec((tr, tc), lambda c, r: (r, c))
    vec = pl.BlockSpec((1, tc), lambda c, r: (0, c))
    out = pl.pallas_call(
        body, name=name,
        out_shape=tuple([jax.ShapeDtypeStruct((T, C), d) for d in out_dtypes]
                        + [jax.ShapeDtypeStruct((1, C), F32)] * n_acc),
        grid=(C // tc, T // tr),
        in_specs=[blk] * n_in + [vec] * n_vec,
        out_specs=tuple([blk] * n_out + [vec] * n_acc),
        compiler_params=_params(("parallel", "arbitrary")),
    )(*ins, *vecs)
    return out


def _rmsnorm_fwd(x, g, name, deps=(), tr=128):
    T, D = x.shape
    tr = _tile(T, tr)

    def body(*refs):
        x_ref, g_ref, h_ref = refs[0], refs[1], refs[2 + len(deps)]
        xv = x_ref[...]
        r = lax.rsqrt(jnp.mean(xv * xv, axis=-1, keepdims=True) + RMS_EPS)
        h_ref[...] = ((xv * r) * g_ref[...]).astype(BF16)

    return pl.pallas_call(
        body, name=name,
        out_shape=jax.ShapeDtypeStruct((T, D), BF16),
        grid=(T // tr,),
        in_specs=[pl.BlockSpec((tr, D), lambda i: (i, 0)), pl.BlockSpec((1, D), lambda i: (0, 0))] + [_ANY] * len(deps),
        out_specs=pl.BlockSpec((tr, D), lambda i: (i, 0)),
        compiler_params=_params(("parallel",)),
    )(x, g, *deps)


def _rmsnorm_bwd(x, dh, dres, g, name, tr=128):
    T, D = x.shape
    tr = _tile(T, tr)

    def body(x_ref, dh_ref, dres_ref, g_ref, dx_ref, dxb_ref, dg_ref):
        xv = x_ref[...]
        r = lax.rsqrt(jnp.mean(xv * xv, axis=-1, keepdims=True) + RMS_EPS)
        xn = xv * r
        dhv = dh_ref[...].astype(F32)
        dxn = dhv * g_ref[...]
        dx = dres_ref[...] + r * (dxn - xn * jnp.mean(dxn * xn, axis=-1, keepdims=True))
        dx_ref[...] = dx
        dxb_ref[...] = dx.astype(BF16)

        @pl.when(pl.program_id(0) == 0)
        def _():
            dg_ref[...] = jnp.zeros_like(dg_ref)

        dg_ref[...] += jnp.sum(dhv * xn, axis=0, keepdims=True)

    blk = pl.BlockSpec((tr, D), lambda i: (i, 0))
    vec = pl.BlockSpec((1, D), lambda i: (0, 0))
    return pl.pallas_call(
        body, name=name,
        out_shape=(jax.ShapeDtypeStruct((T, D), F32), jax.ShapeDtypeStruct((T, D), BF16),
                   jax.ShapeDtypeStruct((1, D), F32)),
        grid=(T // tr,),
        in_specs=[blk, blk, blk, vec],
        out_specs=(blk, blk, vec),
        compiler_params=_params(("arbitrary",)),
    )(x, dh, dres, g)


def _total(x, scale, name):
    def body(x_ref, o_ref):
        o_ref[...] = jnp.sum(jnp.sum(x_ref[...], axis=1, keepdims=True), axis=0, keepdims=True) * scale

    return pl.pallas_call(
        body, name=name, out_shape=jax.ShapeDtypeStruct((1, 1), F32),
        compiler_params=pltpu.CompilerParams(vmem_limit_bytes=VMEM_LIMIT),
    )(x)


def _gelu(x):
    t = jnp.tanh(GELU_C * (x + 0.044715 * (x * x * x)))
    return x * (0.5 * (1.0 + t)), t


def _adamw(parts, w, m, v, name, tr=256, tc=1024):
    n, R, C = parts.shape
    tr, tc = _tile(R, tr), _tile(C, tc)
    c1 = 1.0 - ADAM_B1 ** ADAM_STEP
    c2 = 1.0 - ADAM_B2 ** ADAM_STEP

    def body(p_ref, w_ref, m_ref, v_ref, g_out, d_out, m_out, v_out):
        g = p_ref[0].astype(F32)
        for k in range(1, n):
            g = g + p_ref[k].astype(F32)
        mn = ADAM_B1 * m_ref[...] + (1.0 - ADAM_B1) * g
        vn = ADAM_B2 * v_ref[...] + (1.0 - ADAM_B2) * (g * g)
        m_hat = mn / c1
        v_hat = vn / c2
        g_out[...] = g
        d_out[...] = -ADAM_LR * (m_hat / (jnp.sqrt(v_hat) + ADAM_EPS) + ADAM_WD * w_ref[...])
        m_out[...] = mn
        v_out[...] = vn

    blk = pl.BlockSpec((tr, tc), lambda r, c: (r, c))
    return pl.pallas_call(
        body, name=name,
        out_shape=tuple([jax.ShapeDtypeStruct((R, C), F32)] * 4),
        grid=(R // tr, C // tc),
        in_specs=[pl.BlockSpec((n, tr, tc), lambda r, c: (0, r, c)), blk, blk, blk],
        out_specs=(blk, blk, blk, blk),
        compiler_params=_params(("parallel", "parallel")),
    )(parts, w, m, v)


def _sum_parts(parts, name):
    n, R, C = parts.shape

    def body(p_ref, o_ref):
        g = p_ref[0].astype(F32)
        for k in range(1, n):
            g = g + p_ref[k].astype(F32)
        o_ref[...] = g

    return pl.pallas_call(
        body, name=name,
        out_shape=jax.ShapeDtypeStruct((R, C), F32),
        compiler_params=pltpu.CompilerParams(vmem_limit_bytes=VMEM_LIMIT),
    )(parts)


def _head_norm(xv):
    xv = xv.astype(F32)
    r = lax.rsqrt(jnp.mean(xv * xv, axis=-1, keepdims=True) + RMS_EPS)
    return xv * r, r


FWD_HEADS = 4
BWD_HEADS = 2


def _attn_specs(S, H, HP):
    def spec(part):
        return pl.BlockSpec((S, HP * HEAD_DIM), lambda b, h, qi: (b, part * (H // HP) + h))
    return spec


def _lanes(hh):
    return slice(hh * HEAD_DIM, (hh + 1) * HEAD_DIM)


def _attn_fwd(proj, q_g, k_g, B, S, H, name):
    TQ = _tile(S, 256)
    nq = S // TQ
    scale = 1.0 / math.sqrt(HEAD_DIM)
    HP = FWD_HEADS
    ATT_W = HP * HEAD_DIM
    heads = range(HP)

    def body(q_ref, k_ref, v_ref, gate_ref, qg_ref, kg_ref, og_ref, o_ref, w_ref, sg_ref, qn_s, kn_s):
        @pl.when(pl.program_id(2) == 0)
        def _():
            for hh in heads:
                qn_s[:, _lanes(hh)] = (_head_norm(q_ref[:, _lanes(hh)])[0] * qg_ref[...]).astype(BF16)
                kn_s[:, _lanes(hh)] = (_head_norm(k_ref[:, _lanes(hh)])[0] * kg_ref[...]).astype(BF16)

        row = lax.broadcasted_iota(jnp.int32, (TQ, TQ), 0)
        col = lax.broadcasted_iota(jnp.int32, (TQ, TQ), 1)
        later = _ones_where(row > col)
        causal = col < row

        def q_block(qi):
            q0 = pl.multiple_of(qi * TQ, TQ)

            def both(ki, state, diag):
                k0 = pl.multiple_of(ki * TQ, TQ)
                z = [_dot(qn_s[pl.ds(q0, TQ), _lanes(hh)], kn_s[pl.ds(k0, TQ), _lanes(hh)], NT) * scale
                     for hh in heads]
                ls = [_log_sigmoid(zz) for zz in z]
                l1m = [a - zz for a, zz in zip(ls, z)]
                if diag:
                    l1m = [jnp.where(causal, a, 0.0) for a in l1m]
                suffix = [_dot(l1m[hh].astype(BF16), later, NN) + state[hh][0] for hh in heads]
                w = [jnp.exp(a + sfx) for a, sfx in zip(ls, suffix)]
                if diag:
                    w = [jnp.where(causal, a, 0.0) for a in w]
                wb = [a.astype(BF16) for a in w]
                for hh in heads:
                    w_ref[hh, ki] = wb[hh]
                    sg_ref[hh, ki] = jnp.exp(ls[hh]).astype(BF16)
                acc = [state[hh][1] + _dot(wb[hh], v_ref[pl.ds(k0, TQ), _lanes(hh)], NN) for hh in heads]
                return tuple((state[hh][0] + jnp.sum(l1m[hh], axis=1, keepdims=True), acc[hh]) for hh in heads)

            zero = (jnp.zeros((TQ, 1), F32), jnp.zeros((TQ, HEAD_DIM), F32))
            state = both(qi, (zero,) * HP, True)
            state = lax.fori_loop(0, qi, lambda i, st: both(qi - 1 - i, st, False), state)
            for hh in heads:
                acc = state[hh][1]
                o_ref[:, _lanes(hh)] = acc.astype(BF16)
                gate = gate_ref[pl.ds(q0, TQ), _lanes(hh)].astype(F32)
                og_ref[:, _lanes(hh)] = (acc * (gate * _sigmoid(gate))).astype(BF16)

        q_block(pl.program_id(2))

    spec = _attn_specs(S, H, HP)
    vec = pl.BlockSpec((1, HEAD_DIM), lambda b, h, qi: (0, 0))
    out = pl.BlockSpec((TQ, ATT_W), lambda b, h, qi: (b * nq + qi, h))
    kept = pl.BlockSpec((None, HP, None, nq, TQ, TQ), lambda b, h, qi: (b, h, qi, 0, 0, 0))
    kept_shape = jax.ShapeDtypeStruct((B, H, nq, nq, TQ, TQ), BF16)
    return pl.pallas_call(
        body, name=name,
        out_shape=(jax.ShapeDtypeStruct((B * S, H * HEAD_DIM), BF16),) * 2 + (kept_shape,) * 2,
        grid=(B, H // HP, nq),
        in_specs=[spec(0), spec(1), spec(2), spec(3), vec, vec],
        out_specs=(out, out, kept, kept),
        scratch_shapes=[pltpu.VMEM((S, ATT_W), BF16)] * 2,
        compiler_params=_params(("parallel", "parallel", "arbitrary")),
    )(proj, proj, proj, proj, q_g, k_g)


def _attn_bwd(proj, o, dog, w_kept, sg_kept, q_g, k_g, B, S, H, name):
    TQ = _tile(S, 256)
    nq = S // TQ
    scale = 1.0 / math.sqrt(HEAD_DIM)
    HP = BWD_HEADS
    ATT_W = HP * HEAD_DIM
    heads = range(HP)

    def body(q_ref, k_ref, v_ref, gate_ref, o_ref, dog_ref, w_ref, sg_ref, qg_ref, kg_ref,
             dq_ref, dk_ref, dv_ref, dgate_ref, dqg_ref, dkg_ref,
             qn_s, kn_s, do_s, dkn_s, dv_s):
        qi = pl.program_id(2)

        @pl.when((pl.program_id(0) == 0) & (pl.program_id(1) == 0) & (qi == 0))
        def _():
            dqg_ref[...] = jnp.zeros_like(dqg_ref)
            dkg_ref[...] = jnp.zeros_like(dkg_ref)

        @pl.when(qi == 0)
        def _():
            for hh in heads:
                qn_s[:, _lanes(hh)] = (_head_norm(q_ref[:, _lanes(hh)])[0] * qg_ref[...]).astype(BF16)
                kn_s[:, _lanes(hh)] = (_head_norm(k_ref[:, _lanes(hh)])[0] * kg_ref[...]).astype(BF16)
            gate = gate_ref[...].astype(F32)
            sg = _sigmoid(gate)
            dog_v = dog_ref[...].astype(F32)
            do_s[...] = (dog_v * (gate * sg)).astype(BF16)
            dgate_ref[...] = (dog_v * o_ref[...].astype(F32) * (sg * (1.0 + gate * (1.0 - sg)))).astype(BF16)
            dkn_s[...] = jnp.zeros_like(dkn_s)
            dv_s[...] = jnp.zeros_like(dv_s)

        row = lax.broadcasted_iota(jnp.int32, (TQ, TQ), 0)
        col = lax.broadcasted_iota(jnp.int32, (TQ, TQ), 1)
        earlier = _ones_where(row < col)
        causal = col < row

        def norm_bwd(xv, g_ref, dn, dg_ref):
            xh, r = _head_norm(xv)
            dg_ref[...] += jnp.sum(dn * xh, axis=0, keepdims=True)
            dxh = dn * g_ref[...]
            return (r * (dxh - xh * jnp.mean(dxh * xh, axis=-1, keepdims=True))).astype(BF16)

        def q_block():
            q0 = pl.multiple_of(qi * TQ, TQ)

            def grads_both(ki, state, diag):
                k0 = pl.multiple_of(ki * TQ, TQ)
                qb = [qn_s[pl.ds(q0, TQ), _lanes(hh)] for hh in heads]
                dob = [do_s[pl.ds(q0, TQ), _lanes(hh)] for hh in heads]
                wb = [w_ref[hh, ki] for hh in heads]
                da = [_dot(dob[hh], v_ref[pl.ds(k0, TQ), _lanes(hh)], NT) * wb[hh].astype(F32) for hh in heads]
                for hh in heads:
                    dv_s[pl.ds(k0, TQ), _lanes(hh)] += _dot(wb[hh], dob[hh], TN)
                prefix = [_dot(da[hh].astype(BF16), earlier, NN) + state[hh][0] for hh in heads]
                dzb = []
                for hh in heads:
                    sgz = sg_ref[hh, ki].astype(F32)
                    dz = da[hh] * (1.0 - sgz) - sgz * prefix[hh]
                    if diag:
                        dz = jnp.where(causal, dz, 0.0)
                    dzb.append((dz * scale).astype(BF16))
                dq = [state[hh][1] + _dot(dzb[hh], kn_s[pl.ds(k0, TQ), _lanes(hh)], NN) for hh in heads]
                for hh in heads:
                    dkn_s[pl.ds(k0, TQ), _lanes(hh)] += _dot(dzb[hh], qb[hh], TN)
                return tuple((state[hh][0] + jnp.sum(da[hh], axis=1, keepdims=True), dq[hh]) for hh in heads)

            zero = (jnp.zeros((TQ, 1), F32), jnp.zeros((TQ, HEAD_DIM), F32))
            state = lax.fori_loop(0, qi, lambda i, st: grads_both(i, st, False), (zero,) * HP)
            state = grads_both(qi, state, True)
            for hh in heads:
                dq_ref[:, _lanes(hh)] = norm_bwd(q_ref[pl.ds(q0, TQ), _lanes(hh)], qg_ref, state[hh][1], dqg_ref)

        q_block()

        @pl.when(qi == nq - 1)
        def _():
            for hh in heads:
                dk_ref[:, _lanes(hh)] = norm_bwd(k_ref[:, _lanes(hh)], kg_ref, dkn_s[:, _lanes(hh)], dkg_ref)
            dv_ref[...] = dv_s[...].astype(BF16)

    spec = _attn_specs(S, H, HP)
    vec = pl.BlockSpec((1, HEAD_DIM), lambda b, h, qi: (0, 0))
    blk = pl.BlockSpec((S, ATT_W), lambda b, h, qi: (b, h))
    rows = pl.BlockSpec((TQ, ATT_W), lambda b, h, qi: (b * nq + qi, h))
    kept = pl.BlockSpec((None, HP, None, nq, TQ, TQ), lambda b, h, qi: (b, h, qi, 0, 0, 0))
    big = jax.ShapeDtypeStruct((B * S, H * HEAD_DIM), BF16)
    small = jax.ShapeDtypeStruct((1, HEAD_DIM), F32)
    return pl.pallas_call(
        body, name=name,
        out_shape=(big, big, big, big, small, small),
        grid=(B, H // HP, nq),
        in_specs=[spec(0), spec(1), spec(2), spec(3), blk, blk, kept, kept, vec, vec],
        out_specs=(rows, blk, blk, blk, vec, vec),
        scratch_shapes=[pltpu.VMEM((S, ATT_W), BF16)] * 3 + [pltpu.VMEM((S, ATT_W), F32)] * 2,
        compiler_params=_params(("arbitrary", "arbitrary", "arbitrary")),
    )(proj, proj, proj, proj, o, dog, w_kept, sg_kept, q_g, k_g)


HALF = GROUPS_PER_BLOCK * STATE


def _cmul(ar, ai, br, bi):
    return ar * br - ai * bi, ar * bi + ai * br


def _cpow(ar, ai, n):
    rr = ri = None
    while n:
        if n & 1:
            rr, ri = (ar, ai) if rr is None else _cmul(rr, ri, ar, ai)
        n >>= 1
        if n:
            ar, ai = _cmul(ar, ai, ar, ai)
    return rr, ri


def _segment_carry(er, ei, lr, li, seg_len, segs_per_seq, reverse):
    Lr, Li = _cpow(lr, li, seg_len)
    pos = lax.broadcasted_iota(jnp.int32, er.shape, 0) % segs_per_seq
    outr = jnp.zeros_like(er)
    outi = jnp.zeros_like(ei)
    pr = pi = None
    for d in range(1, segs_per_seq):
        shift = (SUBLANES - d) if reverse else d
        sr = pltpu.roll(er, shift, 0)
        si = pltpu.roll(ei, shift, 0)
        ok = (pos + d < segs_per_seq) if reverse else (pos >= d)
        sr = jnp.where(ok, sr, 0.0)
        si = jnp.where(ok, si, 0.0)
        if pr is not None:
            sr, si = _cmul(sr, si, pr, pi)
        outr = outr + sr
        outi = outi + si
        pr, pi = (Lr, Li) if pr is None else _cmul(pr, pi, Lr, Li)
    return outr, outi


def _s5_sizes(T, B):
    assert SUBLANES % B == 0
    segs_per_seq = SUBLANES // B
    n_steps = T // SUBLANES
    cj = _tile(n_steps, 64)
    return segs_per_seq, n_steps, cj


def _s5_fwd(u_p, wb, wc, lam, dvec, B, name):
    T, C = u_p.shape
    nb = C // 128
    segs_per_seq, n_steps, cj = _s5_sizes(T, B)
    n_chunks = n_steps // cj
    rows = cj * SUBLANES

    def body(u_ref, wb_ref, wc_ref, lam_ref, d_ref, y_ref, hin_ref, bu_s, h_s):
        lr = jnp.broadcast_to(lam_ref[:, :HALF], (SUBLANES, HALF))
        li = jnp.broadcast_to(lam_ref[:, HALF:], (SUBLANES, HALF))

        def scan_chunk(c, hr, hi, store):
            r0 = pl.multiple_of(c * rows, rows)
            if not store:
                bu_s[pl.ds(r0, rows), :] = _dot(u_ref[pl.ds(r0, rows), :], wb_ref[...], NN)

            def step(j, carry):
                hr, hi = carry
                o = pl.multiple_of(j * SUBLANES, SUBLANES)
                at = pl.multiple_of(r0 + o, SUBLANES)
                nr = lr * hr - li * hi + bu_s[pl.ds(at, SUBLANES), :HALF]
                ni = lr * hi + li * hr + bu_s[pl.ds(at, SUBLANES), HALF:]
                if store:
                    h_s[pl.ds(o, SUBLANES), :HALF] = nr
                    h_s[pl.ds(o, SUBLANES), HALF:] = ni
                return nr, ni

            hr, hi = lax.fori_loop(0, cj, step, (hr, hi))
            if store:
                uv = u_ref[pl.ds(r0, rows), :].astype(F32)
                y_ref[pl.ds(r0, rows), :] = _dot(h_s[...].astype(BF16), wc_ref[...], NN) + d_ref[...] * uv
            return hr, hi

        zero = jnp.zeros((SUBLANES, HALF), F32)
        er, ei = lax.fori_loop(0, n_chunks, lambda c, h: scan_chunk(c, h[0], h[1], False), (zero, zero))
        h0r, h0i = _segment_carry(er, ei, lr, li, n_steps, segs_per_seq, False)
        hin_ref[:, :HALF] = h0r
        hin_ref[:, HALF:] = h0i
        lax.fori_loop(0, n_chunks, lambda c, h: scan_chunk(c, h[0], h[1], True), (h0r, h0i))

    return pl.pallas_call(
        body, name=name,
        out_shape=(jax.ShapeDtypeStruct((T, C), F32), jax.ShapeDtypeStruct((nb, SUBLANES, 2 * HALF), F32)),
        grid=(nb,),
        in_specs=[pl.BlockSpec((T, 128), lambda g: (0, g)),
                  pl.BlockSpec((None, 128, 2 * HALF), lambda g: (g, 0, 0)),
                  pl.BlockSpec((None, 2 * HALF, 128), lambda g: (g, 0, 0)),
                  pl.BlockSpec((None, 1, 2 * HALF), lambda g: (g, 0, 0)),
                  pl.BlockSpec((1, 128), lambda g: (0, g))],
        out_specs=(pl.BlockSpec((T, 128), lambda g: (0, g)),
                   pl.BlockSpec((None, SUBLANES, 2 * HALF), lambda g: (g, 0, 0))),
        scratch_shapes=[pltpu.VMEM((T, 2 * HALF), F32), pltpu.VMEM((rows, 2 * HALF), F32)],
        compiler_params=_params(("parallel",)),
    )(u_p, wb, wc, lam, dvec)


def _s5_bwd(u_p, dy_p, wb, wbt, wc, wct, lam, dvec, h_in, B, name):
    T, C = u_p.shape
    nb = C // 128
    segs_per_seq, n_steps, cj = _s5_sizes(T, B)
    n_chunks = n_steps // cj
    rows = cj * SUBLANES

    def body(u_ref, dy_ref, wb_ref, wbt_ref, wc_ref, wct_ref, lam_ref, d_ref, hin_ref,
             du_ref, dwb_ref, dwc_ref, dlam_ref, dd_ref, h_all, dh_all, x_s, g_s):
        lr = jnp.broadcast_to(lam_ref[:, :HALF], (SUBLANES, HALF))
        li = jnp.broadcast_to(lam_ref[:, HALF:], (SUBLANES, HALF))
        zero = jnp.zeros((SUBLANES, HALF), F32)

        h_all[pl.ds(0, SUBLANES), :] = hin_ref[...]

        def fwd_chunk(c, carry):
            r0 = pl.multiple_of(c * rows, rows)
            x_s[...] = _dot(u_ref[pl.ds(r0, rows), :], wb_ref[...], NN)

            def step(j, carry):
                hr, hi = carry
                o = pl.multiple_of(j * SUBLANES, SUBLANES)
                nr = lr * hr - li * hi + x_s[pl.ds(o, SUBLANES), :HALF]
                ni = lr * hi + li * hr + x_s[pl.ds(o, SUBLANES), HALF:]
                late = pl.multiple_of(r0 + o + SUBLANES, SUBLANES)
                h_all[pl.ds(late, SUBLANES), :HALF] = nr
                h_all[pl.ds(late, SUBLANES), HALF:] = ni
                return nr, ni

            return lax.fori_loop(0, cj, step, carry)

        lax.fori_loop(0, n_chunks, fwd_chunk, (hin_ref[:, :HALF], hin_ref[:, HALF:]))

        def bwd_chunk(i, carry, store):
            c = n_chunks - 1 - i
            r0 = pl.multiple_of(c * rows, rows)
            dyv = dy_ref[pl.ds(r0, rows), :]
            if not store:
                dh_all[pl.ds(r0, rows), :] = _dot(dyv, wct_ref[...], NN)

            def step(jj, carry):
                ar, ai, accr, acci = carry
                j = cj - 1 - jj
                o = pl.multiple_of(j * SUBLANES, SUBLANES)
                prev = pl.multiple_of(r0 + o, SUBLANES)
                nr = lr * ar + li * ai + dh_all[pl.ds(prev, SUBLANES), :HALF]
                ni = lr * ai - li * ar + dh_all[pl.ds(prev, SUBLANES), HALF:]
                if store:
                    g_s[pl.ds(o, SUBLANES), :HALF] = nr
                    g_s[pl.ds(o, SUBLANES), HALF:] = ni
                    pr = h_all[pl.ds(prev, SUBLANES), :HALF]
                    pi = h_all[pl.ds(prev, SUBLANES), HALF:]
                    accr = accr + nr * pr + ni * pi
                    acci = acci + ni * pr - nr * pi
                return nr, ni, accr, acci

            carry = lax.fori_loop(0, cj, step, carry)
            if store:
                gb = g_s[...].astype(BF16)
                uv = u_ref[pl.ds(r0, rows), :]
                dyf = dyv.astype(F32)
                du_ref[pl.ds(r0, rows), :] = (_dot(gb, wbt_ref[...], NN) + d_ref[...] * dyf).astype(BF16)
                dwb_ref[...] += _dot(uv, gb, TN)
                hb = h_all[pl.ds(pl.multiple_of(r0 + SUBLANES, SUBLANES), rows), :].astype(BF16)
                dwc_ref[...] += _dot(hb, dyv, TN)
                dd_ref[...] += jnp.sum(dyf * uv.astype(F32), axis=0, keepdims=True)
            return carry

        er, ei, _, _ = lax.fori_loop(0, n_chunks, lambda i, c: bwd_chunk(i, c, False), (zero, zero, zero, zero))
        a0r, a0i = _segment_carry(er, ei, lr, -li, n_steps, segs_per_seq, True)
        dwb_ref[...] = jnp.zeros_like(dwb_ref)
        dwc_ref[...] = jnp.zeros_like(dwc_ref)
        dd_ref[...] = jnp.zeros_like(dd_ref)
        _, _, accr, acci = lax.fori_loop(0, n_chunks, lambda i, c: bwd_chunk(i, c, True), (a0r, a0i, zero, zero))
        dlam_ref[:, :HALF] = jnp.sum(accr, axis=0, keepdims=True)
        dlam_ref[:, HALF:] = jnp.sum(acci, axis=0, keepdims=True)

    col = pl.BlockSpec((T, 128), lambda g: (0, g))
    vec = pl.BlockSpec((1, 128), lambda g: (0, g))

    def per_block(*shape):
        return pl.BlockSpec((None,) + shape, lambda g: (g, 0, 0))

    return pl.pallas_call(
        body, name=name,
        out_shape=(jax.ShapeDtypeStruct((T, C), BF16),
                   jax.ShapeDtypeStruct((nb, 128, 2 * HALF), F32),
                   jax.ShapeDtypeStruct((nb, 2 * HALF, 128), F32),
                   jax.ShapeDtypeStruct((nb, 1, 2 * HALF), F32),
                   jax.ShapeDtypeStruct((1, C), F32)),
        grid=(nb,),
        in_specs=[col, col, per_block(128, 2 * HALF), per_block(2 * HALF, 128), per_block(2 * HALF, 128),
                  per_block(128, 2 * HALF), per_block(1, 2 * HALF), vec, per_block(SUBLANES, 2 * HALF)],
        out_specs=(col, per_block(128, 2 * HALF), per_block(2 * HALF, 128), per_block(1, 2 * HALF), vec),
        scratch_shapes=[pltpu.VMEM((T + SUBLANES, 2 * HALF), F32), pltpu.VMEM((T, 2 * HALF), F32),
                        pltpu.VMEM((rows, 2 * HALF), F32), pltpu.VMEM((rows, 2 * HALF), F32)],
        compiler_params=_params(("parallel",)),
    )(u_p, dy_p, wb, wbt, wc, wct, lam, dvec, h_in)


def _discretize(a_re, a_im, log_dt, b_re, b_im):
    dt = jnp.exp(log_dt)[:, None]
    mag = jnp.exp(a_re * dt)
    lam_re = mag * jnp.cos(a_im * dt)
    lam_im = mag * jnp.sin(a_im * dt)
    den = a_re * a_re + a_im * a_im
    f_re = ((lam_re - 1.0) * a_re + lam_im * a_im) / den
    f_im = (lam_im * a_re - (lam_re - 1.0) * a_im) / den
    bb_re = f_re[..., None] * b_re - f_im[..., None] * b_im
    bb_im = f_re[..., None] * b_im + f_im[..., None] * b_re
    return lam_re, lam_im, bb_re, bb_im


def _block_diag_in(bb_re, bb_im):
    eye = jnp.eye(GROUPS_PER_BLOCK, dtype=F32)

    def one(bb):
        t = bb.reshape(-1, GROUPS_PER_BLOCK, STATE, GROUP)
        return jnp.einsum('gapi,ab->gaibp', t, eye).reshape(-1, 128, HALF)

    return jnp.concatenate([one(bb_re), one(bb_im)], axis=-1)


def _block_diag_in_grad(dwb):
    eye = jnp.eye(GROUPS_PER_BLOCK, dtype=F32)

    def one(d):
        t = d.reshape(-1, GROUPS_PER_BLOCK, GROUP, GROUPS_PER_BLOCK, STATE)
        return jnp.einsum('gaibp,ab->gapi', t, eye).reshape(-1, STATE, GROUP)

    return one(dwb[..., :HALF]), one(dwb[..., HALF:])


def _block_diag_out(c_re, c_im):
    eye = jnp.eye(GROUPS_PER_BLOCK, dtype=F32)

    def one(cc):
        t = cc.reshape(-1, GROUPS_PER_BLOCK, GROUP, STATE)
        return jnp.einsum('gaip,ab->gbpai', t, eye).reshape(-1, HALF, 128)

    return jnp.concatenate([one(c_re), -one(c_im)], axis=1)


def _block_diag_out_grad(dwc):
    eye = jnp.eye(GROUPS_PER_BLOCK, dtype=F32)

    def one(d):
        t = d.reshape(-1, GROUPS_PER_BLOCK, STATE, GROUPS_PER_BLOCK, GROUP)
        return jnp.einsum('gbpai,ab->gaip', t, eye).reshape(-1, GROUP, STATE)

    return one(dwc[:, :HALF]), -one(dwc[:, HALF:])


def _pack(parts):
    return jnp.concatenate([p.reshape(-1, PACK_W) for p in parts], axis=0)


def _unpack(buf, shapes):
    lead = buf.shape[:-2]
    out, r = [], 0
    for s in shapes:
        n = math.prod(s) // PACK_W
        out.append(buf[..., r:r + n, :].reshape(lead + tuple(s)))
        r += n
    return out


def _permute_rows(a, n_steps):
    T, C = a.shape
    return a.reshape(SUBLANES, n_steps, C).transpose(1, 0, 2).reshape(T, C)


def _unpermute_rows(a, n_steps):
    T, C = a.shape
    return a.reshape(n_steps, SUBLANES, C).transpose(1, 0, 2).reshape(T, C)


def kernel(x, norm_g, attn_w_in, attn_q_g, attn_k_g, attn_w_out, ssm_w_in, ssm_A_re, ssm_A_im, ssm_log_dt, ssm_B_re, ssm_B_im, ssm_C_re, ssm_C_im, ssm_D, ssm_glu_w, ssm_glu_b, ssm_w_out, loss_target, m_norm_g, m_attn_w_in, m_attn_q_g, m_attn_k_g, m_attn_w_out, m_ssm_w_in, m_ssm_A_re, m_ssm_A_im, m_ssm_log_dt, m_ssm_B_re, m_ssm_B_im, m_ssm_C_re, m_ssm_C_im, m_ssm_D, m_ssm_glu_w, m_ssm_glu_b, m_ssm_w_out, v_norm_g, v_attn_w_in, v_attn_q_g, v_attn_k_g, v_attn_w_out, v_ssm_w_in, v_ssm_A_re, v_ssm_A_im, v_ssm_log_dt, v_ssm_B_re, v_ssm_B_im, v_ssm_C_re, v_ssm_C_im, v_ssm_D, v_ssm_glu_w, v_ssm_glu_b, v_ssm_w_out):
    B, S, D = x.shape
    T = B * S
    H = D // HEAD_DIM
    G_loc = ssm_A_re.shape[1]
    G = G_loc * N_DEV
    n_steps = T // SUBLANES
    me, chip = _my_index(), _my_chip()
    xf = x.reshape(T, D)
    target = loss_target.reshape(T, D)

    small_shapes = [(G_loc, STATE), (G_loc, STATE), (G_loc, STATE, GROUP), (G_loc, STATE, GROUP),
                    (G_loc, GROUP, STATE), (G_loc, GROUP, STATE), (G_loc * GROUP,), (G_loc * GROUP,)]
    disc_in = (ssm_A_re[0], ssm_A_im[0], ssm_log_dt[0], ssm_B_re[0], ssm_B_im[0])
    (lam_re, lam_im, bb_re, bb_im), disc_vjp = jax.vjp(_discretize, *disc_in)
    small = _pack([lam_re, lam_im, bb_re, bb_im, ssm_C_re[0], ssm_C_im[0], ssm_D[0], ssm_glu_b[0]])
    small_all = _exchange(small, False, "gather_small")
    c_in0 = _place(attn_w_in[0], me, N_DEV, BF16, "cast_attn_w_in")
    near = _exchange_start(c_in0, None, "near", "gather_attn_w_in_near_start", deps=(small_all,))
    beside = (near[3],)
    small_all = small_all + near[3][0, 0]
    c_out0 = _place(attn_w_out[0], me, N_DEV, BF16, "cast_attn_w_out", deps=beside)
    c_in1 = _place(ssm_w_in[0], me, N_DEV, BF16, "cast_ssm_w_in", deps=beside)
    c_glu = _place(ssm_glu_w[0], me, N_DEV, BF16, "cast_ssm_glu_w", deps=beside)
    c_out1 = _place(ssm_w_out[0], me, N_DEV, BF16, "cast_ssm_w_out", deps=beside)

    lam_re_a, lam_im_a, bb_re_a, bb_im_a, c_re_a, c_im_a, d_a, glu_b_a = [
        t.reshape((G,) + t.shape[2:]) if t.ndim > 2 else t.reshape(-1)
        for t in _unpack(small_all, small_shapes)]
    wb = _block_diag_in(bb_re_a, bb_im_a)
    wc = _block_diag_out(c_re_a, c_im_a)
    wb_b, wc_b = wb.astype(BF16), wc.astype(BF16)
    wbt_b, wct_b = wb_b.transpose(0, 2, 1), wc_b.transpose(0, 2, 1)
    lam = jnp.concatenate([lam_re_a.reshape(-1, 1, HALF), lam_im_a.reshape(-1, 1, HALF)], axis=-1)
    d_row = d_a.reshape(1, D)
    glu_b_row = glu_b_a.reshape(1, D)
    g0, g1 = norm_g[0:1], norm_g[1:2]
    q_g, k_g = attn_q_g, attn_k_g

    h0 = _rmsnorm_fwd(xf, g0, "norm0", deps=beside)
    done = (h0, c_out0, c_in1, c_glu, c_out1, wb_b, wbt_b, wc_b, wct_b, lam, d_row, glu_b_row)
    far = _exchange_start(_exchange_wait(near, done, "near", "gather_attn_w_in_near_wait"), None, "far",
                          "gather_attn_w_in_far_start")
    w_in0 = _exchange_wait(far, far[3], "far", "gather_attn_w_in_far_wait")
    s_out0 = _exchange_start(c_out0, None, "gather", "gather_attn_w_out_start", deps=(w_in0,))
    s_in1 = _exchange_start(c_in1, None, "gather", "gather_ssm_w_in_start", deps=(w_in0,))
    proj0 = _mm_nn(h0, w_in0, BF16, "attn_in", deps=(s_out0[3], s_in1[3]))
    og, o, w_kept, sg_kept = _attn_fwd(proj0, q_g, k_g, B, S, H, "attn_fwd")
    w_out0 = _exchange_wait(s_out0, og, "gather", "gather_attn_w_out_wait").reshape(1, D, D)
    s_glu = _exchange_start(c_glu, None, "gather", "gather_ssm_glu_w_start", deps=(w_out0,))
    s_out1 = _exchange_start(c_out1, None, "gather", "gather_ssm_w_out_start", deps=(w_out0,))
    x1 = _mm_nn(og, w_out0, F32, "attn_out", residual=xf, deps=(s_glu[3], s_out1[3]))

    h1 = _rmsnorm_fwd(x1, g1, "norm1")
    w_in1 = _exchange_wait(s_in1, h1, "gather", "gather_ssm_w_in_wait")
    proj1 = _mm_nn(h1, w_in1, BF16, "ssm_in")
    u_p = _permute_rows(proj1[:, :D], n_steps)
    gate1 = proj1[:, D:]
    y_p, h_in = _s5_fwd(u_p, wb_b, wc_b, lam, d_row, B, "s5_fwd")
    y_ssm = _unpermute_rows(y_p, n_steps)
    (yg,) = _ew(lambda a: (_gelu(a)[0],), [y_ssm], [], [BF16], 0, "gelu")
    w_glu = _exchange_wait(s_glu, yg, "gather", "gather_ssm_glu_w_wait").reshape(1, D, D)

    def glu_gate(zz, y, gt):
        gt = gt.astype(F32)
        return zz, _gelu(y)[0] * _sigmoid(zz) * (gt * _sigmoid(gt))

    z, y3 = _mm_nn(yg, w_glu, F32, "glu_in", bias=glu_b_row, extras=[(y_ssm, 0), (proj1, D // _tile(D, 1024))],
                   epilogue=glu_gate, out_dtypes=[F32, BF16], tk=FUSED_TK)
    w_out1 = _exchange_wait(s_out1, y3, "gather", "gather_ssm_w_out_wait").reshape(1, D, D)

    def loss_head(out_tile, tgt):
        err = out_tile - tgt
        dy = err * (1.0 / D)
        return dy, dy, jnp.sum(err * err, axis=0, keepdims=True)

    dout, dout_b, sq = _mm_nn(y3, w_out1, F32, "ssm_out", residual=x1, extras=[(target, 0)], epilogue=loss_head,
                              out_dtypes=[F32, BF16], tile_sums=True, tk=FUSED_TK)
    loss_part = _total(sq, 0.5 / D, "loss")
    loss = lax.psum(loss_part[0, 0], ("x", "y", "c"))

    p_w_out1 = _mm_tn(y3, dout_b, 1, BF16, "ssm_out_dw").reshape(N_DEV, D // N_DEV, D)
    sc_out1 = _exchange_start(_place(p_w_out1, me, N_DEV, BF16, "place_ssm_w_out"), p_w_out1, "scatter", "scatter_ssm_w_out_start")
    dy3 = _mm_nt(dout_b, w_out1, F32, "ssm_out_dx", deps=(sc_out1[3],))

    def glu_bwd(d3, y, zz, gt):
        gt = gt.astype(F32)
        sg = _sigmoid(gt)
        sz = _sigmoid(zz)
        ygv, _ = _gelu(y)
        dy2 = d3 * (gt * sg)
        dgate = d3 * (ygv * sz) * (sg * (1.0 + gt * (1.0 - sg)))
        dz = dy2 * ygv * (sz * (1.0 - sz))
        return dz, dgate, dy2 * sz, jnp.sum(dz, axis=0, keepdims=True)

    dz_b, dgate1, dyg_a, dglu_b = _ew(glu_bwd, [dy3, y_ssm, z, gate1], [], [BF16, BF16, F32], 1, "glu_gate_bwd")
    p_w_glu = _mm_tn(yg, dz_b, 1, BF16, "glu_in_dw").reshape(N_DEV, D // N_DEV, D)
    sc_glu = _exchange_start(_place(p_w_glu, me, N_DEV, BF16, "place_ssm_glu_w"), p_w_glu, "scatter", "scatter_ssm_glu_w_start")

    def gelu_bwd(db, da, y):
        _, t = _gelu(y)
        dg = 0.5 * (1.0 + t) + 0.5 * y * (1.0 - t * t) * (GELU_C * (1.0 + 3.0 * 0.044715 * (y * y)))
        return (da + db) * dg

    dy_ssm = _mm_nt(dz_b, w_glu, BF16, "glu_in_dx", deps=(sc_glu[3],), extras=[dyg_a, y_ssm], epilogue=gelu_bwd,
                    tq=FUSED_TK)
    dy_p = _permute_rows(dy_ssm, n_steps)
    du_p, dwb, dwc, dlam, dd = _s5_bwd(u_p, dy_p, wb_b, wbt_b, wc_b, wct_b, lam, d_row, h_in, B, "s5_bwd")
    du = _unpermute_rows(du_p, n_steps)
    dproj1 = jnp.concatenate([du, dgate1], axis=1)
    p_w_in1 = _mm_tn(h1, dproj1, N_DEV, BF16, "ssm_in_dw")
    sc_in1 = _exchange_start(_place(p_w_in1, me, N_DEV, BF16, "place_ssm_w_in"), p_w_in1, "scatter", "scatter_ssm_w_in_start")
    dh1 = _mm_nt(dproj1, w_in1, F32, "ssm_in_dx", deps=(sc_in1[3],))
    dx1, dx1_b, dg1 = _rmsnorm_bwd(x1, dh1, dout, g1, "norm1_bwd")

    p_w_out0 = _mm_tn(og, dx1_b, 1, BF16, "attn_out_dw").reshape(N_DEV, D // N_DEV, D)
    sc_out0 = _exchange_start(_place(p_w_out0, me, N_DEV, BF16, "place_attn_w_out"), p_w_out0, "scatter", "scatter_attn_w_out_start")
    dog = _mm_nt(dx1_b, w_out0, BF16, "attn_out_dx", deps=(sc_out0[3],))
    dq, dk, dv, dgate0, dqg, dkg = _attn_bwd(proj0, o, dog, w_kept, sg_kept, q_g, k_g, B, S, H, "attn_bwd")
    dproj0 = jnp.concatenate([dq, dk, dv, dgate0], axis=1)
    p_w_in0 = _mm_tn(h0, dproj0, N_DEV, BF16, "attn_in_dw")
    pair = _exchange_start(lax.empty((N_DEV // 2,) + p_w_in0.shape[1:], BF16), p_w_in0, "pair", "scatter_attn_w_in_pair_start")
    got, p_w_in0 = _exchange_wait(pair, pair[3], "pair", "scatter_attn_w_in_pair_wait", with_source=True)
    q_w_in0 = _pair_sum(p_w_in0, got, "scatter_attn_w_in_pair_sum")
    sc_in0 = _exchange_start(_place(q_w_in0, chip, N_DEV // 2, BF16, "place_attn_w_in"), q_w_in0, "chips", "scatter_attn_w_in_start")
    dh0 = _mm_nt(dproj0, w_in0, F32, "attn_in_dx", deps=(sc_in0[3],))
    dx, _, dg0 = _rmsnorm_bwd(xf, dh0, dx1, g0, "norm0_bwd")

    def update(started, after, w, m, v, name, kind="scatter"):
        recv = _exchange_wait(started, after, kind, "scatter_" + name + "_wait")
        return _adamw(recv, w[0], m[0], v[0], "adamw_" + name)

    r_ssm_w_out = update(sc_out1, dx, ssm_w_out, m_ssm_w_out, v_ssm_w_out, "ssm_w_out")
    r_ssm_glu_w = update(sc_glu, r_ssm_w_out[0], ssm_glu_w, m_ssm_glu_w, v_ssm_glu_w, "ssm_glu_w")
    r_ssm_w_in = update(sc_in1, r_ssm_glu_w[0], ssm_w_in, m_ssm_w_in, v_ssm_w_in, "ssm_w_in")
    r_attn_w_out = update(sc_out0, r_ssm_w_in[0], attn_w_out, m_attn_w_out, v_attn_w_out, "attn_w_out")

    dbb_re, dbb_im = _block_diag_in_grad(dwb)
    dc_re, dc_im = _block_diag_out_grad(dwc)
    dlam_re = dlam[:, 0, :HALF].reshape(G, STATE)
    dlam_im = dlam[:, 0, HALF:].reshape(G, STATE)
    by_owner = [t.reshape((N_DEV, -1)) for t in (dlam_re, dlam_im, dbb_re, dbb_im, dc_re, dc_im, dd, dglu_b)]
    small_parts = jnp.concatenate([t.reshape(N_DEV, -1, PACK_W) for t in by_owner], axis=1)
    small_sum = _sum_parts(_exchange(small_parts, True, "scatter_small"), "sum_small")
    s_lam_re, s_lam_im, s_bb_re, s_bb_im, s_c_re, s_c_im, s_d, s_glu_b = _unpack(small_sum, small_shapes)
    g_a_re, g_a_im, g_log_dt, g_b_re, g_b_im = disc_vjp((s_lam_re, s_lam_im, s_bb_re, s_bb_im))

    local_names = ["ssm_A_re", "ssm_A_im", "ssm_log_dt", "ssm_B_re", "ssm_B_im", "ssm_C_re", "ssm_C_im",
                   "ssm_D", "ssm_glu_b"]
    local_g = [g_a_re, g_a_im, g_log_dt, g_b_re, g_b_im, s_c_re, s_c_im, s_d, s_glu_b]
    local_w = [ssm_A_re, ssm_A_im, ssm_log_dt, ssm_B_re, ssm_B_im, ssm_C_re, ssm_C_im, ssm_D, ssm_glu_b]
    local_m = [m_ssm_A_re, m_ssm_A_im, m_ssm_log_dt, m_ssm_B_re, m_ssm_B_im, m_ssm_C_re, m_ssm_C_im,
               m_ssm_D, m_ssm_glu_b]
    local_v = [v_ssm_A_re, v_ssm_A_im, v_ssm_log_dt, v_ssm_B_re, v_ssm_B_im, v_ssm_C_re, v_ssm_C_im,
               v_ssm_D, v_ssm_glu_b]
    r_local = _adamw_small(local_g, local_w, local_m, local_v, None, "adamw_small")

    rep_g = [jnp.concatenate([dg0, dg1], axis=0), dqg, dkg]
    rep_w = [norm_g, attn_q_g, attn_k_g]
    rep_m = [m_norm_g, m_attn_q_g, m_attn_k_g]
    rep_v = [v_norm_g, v_attn_q_g, v_attn_k_g]
    r_rep = _adamw_small(rep_g, rep_w, rep_m, rep_v, "gather_rep", "adamw_rep")

    r_attn_w_in = update(sc_in0, r_rep[0][0], attn_w_in, m_attn_w_in, v_attn_w_in, "attn_w_in", "chips")

    res = {"attn_w_in": r_attn_w_in, "attn_w_out": r_attn_w_out, "ssm_w_in": r_ssm_w_in,
           "ssm_glu_w": r_ssm_glu_w, "ssm_w_out": r_ssm_w_out}
    ref_w = {"attn_w_in": attn_w_in, "attn_w_out": attn_w_out, "ssm_w_in": ssm_w_in,
             "ssm_glu_w": ssm_glu_w, "ssm_w_out": ssm_w_out}
    for name, r, w in zip(local_names, r_local, local_w):
        res[name], ref_w[name] = r, w
    for name, r, w in zip(["norm_g", "attn_q_g", "attn_k_g"], r_rep, rep_w):
        res[name], ref_w[name] = r, w
    order = ["norm_g", "attn_w_in", "attn_q_g", "attn_k_g", "attn_w_out", "ssm_w_in", "ssm_A_re", "ssm_A_im",
             "ssm_log_dt", "ssm_B_re", "ssm_B_im", "ssm_C_re", "ssm_C_im", "ssm_D", "ssm_glu_w", "ssm_glu_b",
             "ssm_w_out"]
    outs = [loss, dx.reshape(B, S, D)]
    for kind in range(4):
        outs += [res[n][kind].reshape(ref_w[n].shape) for n in order]
    return tuple(outs)


def _adamw_small(grads, ws, ms, vs, gather_name, name):
    sizes = [math.prod(w.shape) for w in ws]
    total = sum(sizes)
    rows = -(-total // (PACK_W * 8)) * 8
    if rows > 256:
        rows = -(-rows // 256) * 256

    def pack(ts, fill):
        flat = jnp.concatenate([t.reshape(-1).astype(F32) for t in ts])
        flat = jnp.concatenate([flat, jnp.full((rows * PACK_W - total,), fill, F32)])
        return flat.reshape(rows, PACK_W)

    g = pack(grads, 0.0)
    parts = _exchange(g, False, gather_name) if gather_name else g[None]
    res = _adamw(parts, pack(ws, 0.0), pack(ms, 0.0), pack(vs, 1.0), name)
    outs = []
    off = 0
    flats = [r.reshape(-1) for r in res]
    for n in sizes:
        outs.append(tuple(f[off:off + n] for f in flats))
        off += n
    return outs
```

```python
import functools
import math

import jax
import jax.numpy as jnp
from jax import lax
from jax.experimental import pallas as pl
from jax.experimental.pallas import tpu as pltpu

F32 = jnp.float32
BF16 = jnp.bfloat16

N_DEV = 8
HEAD_DIM = 128
GROUP = 16
STATE = 64
GROUPS_PER_BLOCK = 8
SUBLANES = 8
RMS_EPS = 1e-6
ADAM_LR, ADAM_B1, ADAM_B2, ADAM_EPS, ADAM_WD, ADAM_STEP = 0.001, 0.9, 0.999, 1e-08, 0.01, 10
VMEM_LIMIT = 56 * 1024 * 1024
GELU_C = math.sqrt(2.0 / math.pi)
PACK_W = 128


def _params(sem, **kw):
    return pltpu.CompilerParams(dimension_semantics=sem, vmem_limit_bytes=VMEM_LIMIT, **kw)


def _tile(n, t):
    t = min(n, t)
    assert n % t == 0, (n, t)
    return t


def _ones_where(cond):
    return jnp.where(cond, 1.0, 0.0).astype(BF16)


def _sigmoid(x):
    return 1.0 / (1.0 + jnp.exp(-x))


def _log_sigmoid(z):
    return jnp.minimum(z, 0.0) - jnp.log(1.0 + jnp.exp(-jnp.abs(z)))


def _dot(a, b, dims):
    return lax.dot_general(a, b, (dims, ((), ())), preferred_element_type=F32)


NN = ((1,), (0,))
NT = ((1,), (1,))
TN = ((0,), (0,))


def _exchange(x, scatter, name, deps=()):
    shape = x.shape[1:] if scatter else x.shape

    def body(*refs):
        x_ref = refs[0]
        out_ref, send_sems, recv_sems, local_sem = refs[1 + len(deps):]
        ix, iy, ic = lax.axis_index("x"), lax.axis_index("y"), lax.axis_index("c")
        me = 4 * ix + 2 * iy + ic

        def peer(k):
            kx, ky, kc = (k >> 2) & 1, (k >> 1) & 1, k & 1
            px, py, pc = ix ^ kx, iy ^ ky, ic ^ kc
            return (px, py, pc), 4 * px + 2 * py + pc

        mine = pltpu.make_async_copy(x_ref.at[me] if scatter else x_ref, out_ref.at[me], local_sem)
        mine.start()
        copies = []
        for k in range(1, N_DEV):
            pid, pidx = peer(k)
            cp = pltpu.make_async_remote_copy(
                src_ref=x_ref.at[pidx] if scatter else x_ref,
                dst_ref=out_ref.at[me],
                send_sem=send_sems.at[k - 1], recv_sem=recv_sems.at[k - 1],
                device_id=pid, device_id_type=pl.DeviceIdType.MESH)
            cp.start()
            copies.append(cp)
        for k in range(1, N_DEV):
            pid, pidx = peer(k)
            pltpu.make_async_remote_copy(
                src_ref=x_ref.at[pidx] if scatter else x_ref,
                dst_ref=out_ref.at[pidx],
                send_sem=send_sems.at[k - 1], recv_sem=recv_sems.at[k - 1],
                device_id=pid, device_id_type=pl.DeviceIdType.MESH).wait_recv()
        for cp in copies:
            cp.wait_send()
        mine.wait()

    return pl.pallas_call(
        body, name=name,
        out_shape=jax.ShapeDtypeStruct((N_DEV,) + tuple(shape), x.dtype),
        in_specs=[pl.BlockSpec(memory_space=pl.ANY)] * (1 + len(deps)),
        out_specs=pl.BlockSpec(memory_space=pl.ANY),
        scratch_shapes=[pltpu.SemaphoreType.DMA((N_DEV - 1,)), pltpu.SemaphoreType.DMA((N_DEV - 1,)),
                        pltpu.SemaphoreType.DMA],
    )(x, *deps)


_HBM = pl.BlockSpec(memory_space=pltpu.HBM)
_SEM = pl.BlockSpec(memory_space=pltpu.SEMAPHORE)
_ANY = pl.BlockSpec(memory_space=pl.ANY)
_EFFECT = pltpu.SideEffectType.DATAFLOW_SIDE_EFFECTING


def _peer(k):
    ix, iy, ic = lax.axis_index("x"), lax.axis_index("y"), lax.axis_index("c")
    px, py, pc = ix ^ ((k >> 2) & 1), iy ^ ((k >> 1) & 1), ic ^ (k & 1)
    return (px, py, pc), 4 * px + 2 * py + pc


def _my_index():
    return 4 * lax.axis_index("x") + 2 * lax.axis_index("y") + lax.axis_index("c")


def _my_chip():
    return 2 * lax.axis_index("x") + lax.axis_index("y")


def _place(src, idx, n_slots, dtype, name, deps=(), tr=256, tc=1024):
    R, C = src.shape[-2:]
    tr, tc = _tile(R, tr), _tile(C, tc)
    idx = idx.astype(jnp.int32).reshape(1)

    def body(*refs):
        refs[-1][...] = refs[1][...].astype(dtype)

    slot = pl.BlockSpec((None, tr, tc), lambda r, c, idx_ref: (idx_ref[0], r, c))
    src_spec = slot if src.ndim == 3 else pl.BlockSpec((tr, tc), lambda r, c, idx_ref: (r, c))
    land = lax.empty((n_slots, R, C), dtype)
    return pl.pallas_call(
        body, name=name,
        out_shape=jax.ShapeDtypeStruct(land.shape, dtype),
        grid_spec=pltpu.PrefetchScalarGridSpec(
            num_scalar_prefetch=1, grid=(R // tr, C // tc), in_specs=[src_spec, _ANY] + [_ANY] * len(deps),
            out_specs=slot),
        input_output_aliases={2: 0},
        compiler_params=_params(("parallel", "parallel")),
    )(idx, src, land, *deps)


def _pair_sum(x, got, name, tr=256, tc=1024):
    _, R, C = x.shape
    tr, tc = _tile(R, tr), _tile(C, tc)
    core = lax.axis_index("c").astype(jnp.int32).reshape(1)

    def body(core_ref, x_ref, got_ref, out_ref):
        out_ref[...] = (x_ref[...].astype(F32) + got_ref[...].astype(F32)).astype(out_ref.dtype)

    blk = pl.BlockSpec((None, tr, tc), lambda i, r, c, core_ref: (i, r, c))
    return pl.pallas_call(
        body, name=name,
        out_shape=jax.ShapeDtypeStruct(got.shape, got.dtype),
        grid_spec=pltpu.PrefetchScalarGridSpec(
            num_scalar_prefetch=1, grid=(N_DEV // 2, R // tr, C // tc),
            in_specs=[pl.BlockSpec((None, tr, tc), lambda i, r, c, core_ref: (2 * i + core_ref[0], r, c)), blk],
            out_specs=blk),
        compiler_params=_params(("parallel", "parallel", "parallel")),
    )(core, x, got)


_N_COPIES = {"gather": N_DEV - 1, "scatter": N_DEV - 1, "pair": N_DEV // 2, "chips": N_DEV // 2 - 1,
             "near": N_DEV // 2, "far": N_DEV // 2 - 1}


def _copies(kind, land_ref, x_ref):
    ix, iy, ic = lax.axis_index("x"), lax.axis_index("y"), lax.axis_index("c")
    me, chip = _my_index(), _my_chip()
    out = []
    if kind in ("gather", "scatter"):
        for k in range(1, N_DEV):
            pid, pidx = _peer(k)
            out.append((land_ref.at[me] if kind == "gather" else x_ref.at[pidx], land_ref.at[me], pid, land_ref.at[pidx]))
    elif kind == "pair":
        for i in range(N_DEV // 2):
            out.append((x_ref.at[2 * i + (1 - ic)], land_ref.at[i], (ix, iy, 1 - ic), land_ref.at[i]))
    elif kind in ("near", "far"):
        if kind == "near":
            out.append((land_ref.at[me], land_ref.at[me], (ix, iy, 1 - ic), land_ref.at[me + 1 - 2 * ic]))
        for k in range(1, N_DEV // 2):
            px, py = ix ^ (k >> 1), iy ^ (k & 1)
            theirs = 4 * px + 2 * py + ic
            if kind == "near":
                out.append((land_ref.at[me], land_ref.at[me], (px, py, ic), land_ref.at[theirs]))
            else:
                out.append((land_ref.at[theirs], land_ref.at[theirs], (ix, iy, 1 - ic),
                            land_ref.at[theirs + 1 - 2 * ic]))
    else:
        for k in range(1, N_DEV // 2):
            px, py = ix ^ (k >> 1), iy ^ (k & 1)
            out.append((x_ref.at[2 * px + py], land_ref.at[chip], (px, py, ic), land_ref.at[2 * px + py]))
    return out


def _exchange_start(land, x, kind, name, deps=()):
    bufs = [land] if x is None else [land, x]
    nb, n = len(bufs), _N_COPIES[kind]

    def body(*refs):
        send_sems, recv_sems = refs[nb + len(deps):nb + len(deps) + 2]
        token = refs[2 * nb + len(deps) + 2]
        for i, (src, dst, pid, _) in enumerate(_copies(kind, refs[0], refs[nb - 1])):
            pltpu.make_async_remote_copy(src_ref=src, dst_ref=dst, send_sem=send_sems.at[i], recv_sem=recv_sems.at[i],
                                         device_id=pid, device_id_type=pl.DeviceIdType.MESH).start()
        token[...] = jnp.zeros_like(token)

    out = pl.pallas_call(
        body, name=name,
        out_shape=(pltpu.SemaphoreType.DMA((n,)), pltpu.SemaphoreType.DMA((n,)))
        + tuple(pltpu.HBM(t.shape, t.dtype) for t in bufs) + (jax.ShapeDtypeStruct((8, 128), F32),),
        in_specs=(_HBM,) * nb + (_ANY,) * len(deps),
        out_specs=(_SEM, _SEM) + (_HBM,) * nb + (pl.BlockSpec(memory_space=pltpu.VMEM),),
        input_output_aliases={i: 2 + i for i in range(nb)},
        compiler_params=pltpu.CompilerParams(has_side_effects=_EFFECT),
    )(*[pltpu.with_memory_space_constraint(t, pltpu.HBM) for t in bufs], *deps)
    return out[0], out[1], out[2:2 + nb], out[2 + nb]


def _exchange_wait(started, after, kind, name, with_source=False):
    send_sems, recv_sems, bufs, _ = started
    nb = len(bufs)
    after = tuple(after) if isinstance(after, (tuple, list)) else (after,)

    def body(*refs):
        send_sems, recv_sems = refs[nb:nb + 2]
        for i, (src, _, pid, landed) in enumerate(_copies(kind, refs[0], refs[nb - 1])):
            cp = pltpu.make_async_remote_copy(src_ref=src, dst_ref=landed, send_sem=send_sems.at[i],
                                              recv_sem=recv_sems.at[i], device_id=pid,
                                              device_id_type=pl.DeviceIdType.MESH)
            cp.wait_send()
            cp.wait_recv()

    out = pl.pallas_call(
        body, name=name,
        out_shape=tuple(pltpu.HBM(t.shape, t.dtype) for t in bufs),
        in_specs=(_HBM,) * nb + (_SEM, _SEM) + (_ANY,) * len(after),
        out_specs=(_HBM,) * nb,
        input_output_aliases={i: i for i in range(nb)},
        compiler_params=pltpu.CompilerParams(has_side_effects=_EFFECT),
    )(*bufs, send_sems, recv_sems, *after)
    return tuple(out) if with_source else out[0]


FUSED_TK = 1024
FULL_K = 4096


def _accumulate(acc, part, step, n_steps, finish):
    if n_steps == 1:
        finish(part)
        return

    @pl.when(step == 0)
    def _():
        acc[...] = part

    @pl.when((step > 0) & (step < n_steps - 1))
    def _():
        acc[...] += part

    @pl.when(step == n_steps - 1)
    def _():
        finish(acc[...] + part)


def _mm_nn(a, b, out_dtype, name, bias=None, residual=None, deps=(), extras=(), epilogue=None,
           out_dtypes=None, tile_sums=False, tm=1024, tn=1024, tk=2048):
    M, K = a.shape
    J, K2, Nj = b.shape
    assert K == K2
    tm, tn, tk = _tile(M, tm), _tile(Nj, tn), _tile(K, tk)
    nb, nk = Nj // tn, K // tk
    out_dtypes = [out_dtype] if out_dtypes is None else list(out_dtypes)
    n_out = len(out_dtypes) + bool(tile_sums)

    def body(*refs):
        a_ref, b_ref = refs[0], refs[1]
        i = 2
        bias_ref = res_ref = None
        if bias is not None:
            bias_ref = refs[i]; i += 1
        if residual is not None:
            res_ref = refs[i]; i += 1
        extra_refs = refs[i:i + len(extras)]
        i += len(extras) + len(deps)
        o_refs, acc = refs[i:i + n_out], refs[i + n_out]

        def finish(r):
            if bias_ref is not None:
                r = r + bias_ref[...]
            if res_ref is not None:
                r = r + res_ref[...]
            res = (r,) if epilogue is None else epilogue(r, *[e[...] for e in extra_refs])
            for o_ref, v in zip(o_refs, res[:len(out_dtypes)]):
                o_ref[...] = v.astype(o_ref.dtype)
            if tile_sums:
                total = jnp.sum(res[-1], axis=1, keepdims=True)
                first = (lax.broadcasted_iota(jnp.int32, (8, 128), 0) == 0) & (lax.broadcasted_iota(jnp.int32, (8, 128), 1) == 0)
                o_refs[-1][...] = jnp.where(first, total, 0.0)

        _accumulate(acc, _dot(a_ref[...], b_ref[...], NN), pl.program_id(3), nk, finish)

    tile = pl.BlockSpec((tm, tn), lambda j, m, n, k: (m, j * nb + n))
    in_specs = [pl.BlockSpec((tm, tk), lambda j, m, n, k: (m, k)),
                pl.BlockSpec((None, tk, tn), lambda j, m, n, k: (j, k, n))]
    args = [a, b]
    if bias is not None:
        in_specs.append(pl.BlockSpec((1, tn), lambda j, m, n, k: (0, j * nb + n)))
        args.append(bias)
    if residual is not None:
        in_specs.append(tile)
        args.append(residual)
    for arr, off in extras:
        in_specs.append(pl.BlockSpec((tm, tn), lambda j, m, n, k, off=off: (m, off + j * nb + n)))
        args.append(arr)
    in_specs += [_ANY] * len(deps)
    args += list(deps)
    out_shape = [jax.ShapeDtypeStruct((M, J * Nj), d) for d in out_dtypes]
    out_specs = [tile] * len(out_dtypes)
    if tile_sums:
        out_shape.append(jax.ShapeDtypeStruct((M // tm * 8, J * nb * 128), F32))
        out_specs.append(pl.BlockSpec((8, 128), lambda j, m, n, k: (m, j * nb + n)))
    out = pl.pallas_call(
        body, name=name,
        out_shape=tuple(out_shape),
        grid=(J, M // tm, nb, nk),
        in_specs=in_specs,
        out_specs=tuple(out_specs),
        scratch_shapes=[pltpu.VMEM((tm, tn), F32)],
        compiler_params=_params(("parallel", "parallel", "parallel", "arbitrary")),
    )(*args)
    return out[0] if n_out == 1 else out


def _mm_nt(a, b, out_dtype, name, deps=(), extras=(), epilogue=None, tm=1024, tp=1024, tq=2048):
    M, Q = a.shape
    J, P, Qj = b.shape
    assert Q == J * Qj
    tm, tp = _tile(M, tm), _tile(P, tp)
    jb = max(1, min(J, tq // Qj))
    assert J % jb == 0
    tq = _tile(Qj, tq)
    nq = Qj // tq

    def body(*refs):
        a_ref, b_ref = refs[:2]
        extra_refs = refs[2:2 + len(extras)]
        o_ref, acc = refs[2 + len(extras) + len(deps):]

        def finish(r):
            if epilogue is not None:
                r = epilogue(r, *[e[...] for e in extra_refs])
            o_ref[...] = r.astype(out_dtype)

        part = _dot(a_ref[:, 0:tq], b_ref[0], NT)
        for jj in range(1, jb):
            part = part + _dot(a_ref[:, jj * tq:(jj + 1) * tq], b_ref[jj], NT)
        _accumulate(acc, part, pl.program_id(2) * nq + pl.program_id(3), J // jb * nq, finish)

    tile = pl.BlockSpec((tm, tp), lambda m, p, j, q: (m, p))
    return pl.pallas_call(
        body, name=name,
        out_shape=jax.ShapeDtypeStruct((M, P), out_dtype),
        grid=(M // tm, P // tp, J // jb, nq),
        in_specs=[pl.BlockSpec((tm, jb * tq), lambda m, p, j, q: (m, j * nq + q)),
                  pl.BlockSpec((jb, tp, tq), lambda m, p, j, q: (j, p, q))] + [tile] * len(extras) + [_ANY] * len(deps),
        out_specs=tile,
        scratch_shapes=[pltpu.VMEM((tm, tp), F32)],
        compiler_params=_params(("parallel", "parallel", "arbitrary", "arbitrary")),
    )(a, b, *extras, *deps)


def _mm_tn(a, b, J, out_dtype, name, tm=1024, tn=1024, tr=FULL_K):
    R, M = a.shape
    R2, N = b.shape
    assert R == R2 and N % J == 0
    Nj = N // J
    tm, tn, tr = _tile(M, tm), _tile(Nj, tn), _tile(R, tr)
    nb, nr = Nj // tn, R // tr

    def body(a_ref, b_ref, o_ref, acc):
        def finish(r):
            o_ref[...] = r.astype(out_dtype)

        _accumulate(acc, _dot(a_ref[...], b_ref[...], TN), pl.program_id(3), nr, finish)

    return pl.pallas_call(
        body, name=name,
        out_shape=jax.ShapeDtypeStruct((J, M, Nj), out_dtype),
        grid=(J, M // tm, nb, nr),
        in_specs=[pl.BlockSpec((tr, tm), lambda j, m, n, r: (r, m)),
                  pl.BlockSpec((tr, tn), lambda j, m, n, r: (r, j * nb + n))],
        out_specs=pl.BlockSpec((None, tm, tn), lambda j, m, n, r: (j, m, n)),
        scratch_shapes=[pltpu.VMEM((tm, tn), F32)],
        compiler_params=_params(("parallel", "parallel", "parallel", "arbitrary")),
    )(a, b)


def _ew(fn, ins, vecs, out_dtypes, n_acc, name, tr=256, tc=1024):
    T, C = ins[0].shape
    tr, tc = _tile(T, tr), _tile(C, tc)
    n_in, n_vec, n_out = len(ins), len(vecs), len(out_dtypes)

    def body(*refs):
        in_refs = refs[:n_in + n_vec]
        out_refs = refs[n_in + n_vec:n_in + n_vec + n_out]
        acc_refs = refs[n_in + n_vec + n_out:]
        res = fn(*[r[...] for r in in_refs])
        for o_ref, v in zip(out_refs, res[:n_out]):
            o_ref[...] = v.astype(o_ref.dtype)
        if n_acc:
            r = pl.program_id(1)

            @pl.when(r == 0)
            def _():
                for a_ref in acc_refs:
                    a_ref[...] = jnp.zeros_like(a_ref)

            for a_ref, v in zip(acc_refs, res[n_out:]):
                a_ref[...] += v

    blk = pl.BlockSpec((tr, tc), lambda c, r: (r, c))
    vec = pl.BlockSpec((1, tc), lambda c, r: (0, c))
    out = pl.pallas_call(
        body, name=name,
        out_shape=tuple([jax.ShapeDtypeStruct((T, C), d) for d in out_dtypes]
                        + [jax.ShapeDtypeStruct((1, C), F32)] * n_acc),
        grid=(C // tc, T // tr),
        in_specs=[blk] * n_in + [vec] * n_vec,
        out_specs=tuple([blk] * n_out + [vec] * n_acc),
        compiler_params=_params(("parallel", "arbitrary")),
    )(*ins, *vecs)
    return out


def _rmsnorm_fwd(x, g, name, deps=(), tr=128):
    T, D = x.shape
    tr = _tile(T, tr)

    def body(*refs):
        x_ref, g_ref, h_ref = refs[0], refs[1], refs[2 + len(deps)]
        xv = x_ref[...]
        r = lax.rsqrt(jnp.mean(xv * xv, axis=-1, keepdims=True) + RMS_EPS)
        h_ref[...] = ((xv * r) * g_ref[...]).astype(BF16)

    return pl.pallas_call(
        body, name=name,
        out_shape=jax.ShapeDtypeStruct((T, D), BF16),
        grid=(T // tr,),
        in_specs=[pl.BlockSpec((tr, D), lambda i: (i, 0)), pl.BlockSpec((1, D), lambda i: (0, 0))] + [_ANY] * len(deps),
        out_specs=pl.BlockSpec((tr, D), lambda i: (i, 0)),
        compiler_params=_params(("parallel",)),
    )(x, g, *deps)


def _rmsnorm_bwd(x, dh, dres, g, name, tr=128):
    T, D = x.shape
    tr = _tile(T, tr)

    def body(x_ref, dh_ref, dres_ref, g_ref, dx_ref, dxb_ref, dg_ref):
        xv = x_ref[...]
        r = lax.rsqrt(jnp.mean(xv * xv, axis=-1, keepdims=True) + RMS_EPS)
        xn = xv * r
        dhv = dh_ref[...].astype(F32)
        dxn = dhv * g_ref[...]
        dx = dres_ref[...] + r * (dxn - xn * jnp.mean(dxn * xn, axis=-1, keepdims=True))
        dx_ref[...] = dx
        dxb_ref[...] = dx.astype(BF16)

        @pl.when(pl.program_id(0) == 0)
        def _():
            dg_ref[...] = jnp.zeros_like(dg_ref)

        dg_ref[...] += jnp.sum(dhv * xn, axis=0, keepdims=True)

    blk = pl.BlockSpec((tr, D), lambda i: (i, 0))
    vec = pl.BlockSpec((1, D), lambda i: (0, 0))
    return pl.pallas_call(
        body, name=name,
        out_shape=(jax.ShapeDtypeStruct((T, D), F32), jax.ShapeDtypeStruct((T, D), BF16),
                   jax.ShapeDtypeStruct((1, D), F32)),
        grid=(T // tr,),
        in_specs=[blk, blk, blk, vec],
        out_specs=(blk, blk, vec),
        compiler_params=_params(("arbitrary",)),
    )(x, dh, dres, g)


def _total(x, scale, name):
    def body(x_ref, o_ref):
        o_ref[...] = jnp.sum(jnp.sum(x_ref[...], axis=1, keepdims=True), axis=0, keepdims=True) * scale

    return pl.pallas_call(
        body, name=name, out_shape=jax.ShapeDtypeStruct((1, 1), F32),
        compiler_params=pltpu.CompilerParams(vmem_limit_bytes=VMEM_LIMIT),
    )(x)


def _gelu(x):
    t = jnp.tanh(GELU_C * (x + 0.044715 * (x * x * x)))
    return x * (0.5 * (1.0 + t)), t


def _adamw(parts, w, m, v, name, tr=256, tc=1024):
    n, R, C = parts.shape
    tr, tc = _tile(R, tr), _tile(C, tc)
    c1 = 1.0 - ADAM_B1 ** ADAM_STEP
    c2 = 1.0 - ADAM_B2 ** ADAM_STEP

    def body(p_ref, w_ref, m_ref, v_ref, g_out, d_out, m_out, v_out):
        g = p_ref[0].astype(F32)
        for k in range(1, n):
            g = g + p_ref[k].astype(F32)
        mn = ADAM_B1 * m_ref[...] + (1.0 - ADAM_B1) * g
        vn = ADAM_B2 * v_ref[...] + (1.0 - ADAM_B2) * (g * g)
        m_hat = mn / c1
        v_hat = vn / c2
        g_out[...] = g
        d_out[...] = -ADAM_LR * (m_hat / (jnp.sqrt(v_hat) + ADAM_EPS) + ADAM_WD * w_ref[...])
        m_out[...] = mn
        v_out[...] = vn

    blk = pl.BlockSpec((tr, tc), lambda r, c: (r, c))
    return pl.pallas_call(
        body, name=name,
        out_shape=tuple([jax.ShapeDtypeStruct((R, C), F32)] * 4),
        grid=(R // tr, C // tc),
        in_specs=[pl.BlockSpec((n, tr, tc), lambda r, c: (0, r, c)), blk, blk, blk],
        out_specs=(blk, blk, blk, blk),
        compiler_params=_params(("parallel", "parallel")),
    )(parts, w, m, v)


def _sum_parts(parts, name):
    n, R, C = parts.shape

    def body(p_ref, o_ref):
        g = p_ref[0].astype(F32)
        for k in range(1, n):
            g = g + p_ref[k].astype(F32)
        o_ref[...] = g

    return pl.pallas_call(
        body, name=name,
        out_shape=jax.ShapeDtypeStruct((R, C), F32),
        compiler_params=pltpu.CompilerParams(vmem_limit_bytes=VMEM_LIMIT),
    )(parts)


def _head_norm(xv):
    xv = xv.astype(F32)
    r = lax.rsqrt(jnp.mean(xv * xv, axis=-1, keepdims=True) + RMS_EPS)
    return xv * r, r


FWD_HEADS = 4
BWD_HEADS = 2


def _attn_specs(S, H, HP):
    def spec(part):
        return pl.BlockSpec((S, HP * HEAD_DIM), lambda b, h, qi: (b, part * (H // HP) + h))
    return spec


def _lanes(hh):
    return slice(hh * HEAD_DIM, (hh + 1) * HEAD_DIM)


def _attn_fwd(proj, q_g, k_g, B, S, H, name):
    TQ = _tile(S, 256)
    nq = S // TQ
    scale = 1.0 / math.sqrt(HEAD_DIM)
    HP = FWD_HEADS
    ATT_W = HP * HEAD_DIM
    heads = range(HP)

    def body(q_ref, k_ref, v_ref, gate_ref, qg_ref, kg_ref, og_ref, o_ref, w_ref, sg_ref, qn_s, kn_s):
        @pl.when(pl.program_id(2) == 0)
        def _():
            for hh in heads:
                qn_s[:, _lanes(hh)] = (_head_norm(q_ref[:, _lanes(hh)])[0] * qg_ref[...]).astype(BF16)
                kn_s[:, _lanes(hh)] = (_head_norm(k_ref[:, _lanes(hh)])[0] * kg_ref[...]).astype(BF16)

        row = lax.broadcasted_iota(jnp.int32, (TQ, TQ), 0)
        col = lax.broadcasted_iota(jnp.int32, (TQ, TQ), 1)
        later = _ones_where(row > col)
        causal = col < row

        def q_block(qi):
            q0 = pl.multiple_of(qi * TQ, TQ)

            def both(ki, state, diag):
                k0 = pl.multiple_of(ki * TQ, TQ)
                z = [_dot(qn_s[pl.ds(q0, TQ), _lanes(hh)], kn_s[pl.ds(k0, TQ), _lanes(hh)], NT) * scale
                     for hh in heads]
                ls = [_log_sigmoid(zz) for zz in z]
                l1m = [a - zz for a, zz in zip(ls, z)]
                if diag:
                    l1m = [jnp.where(causal, a, 0.0) for a in l1m]
                suffix = [_dot(l1m[hh].astype(BF16), later, NN) + state[hh][0] for hh in heads]
                w = [jnp.exp(a + sfx) for a, sfx in zip(ls, suffix)]
                if diag:
                    w = [jnp.where(causal, a, 0.0) for a in w]
                wb = [a.astype(BF16) for a in w]
                for hh in heads:
                    w_ref[hh, ki] = wb[hh]
                    sg_ref[hh, ki] = jnp.exp(ls[hh]).astype(BF16)
                acc = [state[hh][1] + _dot(wb[hh], v_ref[pl.ds(k0, TQ), _lanes(hh)], NN) for hh in heads]
                return tuple((state[hh][0] + jnp.sum(l1m[hh], axis=1, keepdims=True), acc[hh]) for hh in heads)

            zero = (jnp.zeros((TQ, 1), F32), jnp.zeros((TQ, HEAD_DIM), F32))
            state = both(qi, (zero,) * HP, True)
            state = lax.fori_loop(0, qi, lambda i, st: both(qi - 1 - i, st, False), state)
            for hh in heads:
                acc = state[hh][1]
                o_ref[:, _lanes(hh)] = acc.astype(BF16)
                gate = gate_ref[pl.ds(q0, TQ), _lanes(hh)].astype(F32)
                og_ref[:, _lanes(hh)] = (acc * (gate * _sigmoid(gate))).astype(BF16)

        q_block(pl.program_id(2))

    spec = _attn_specs(S, H, HP)
    vec = pl.BlockSpec((1, HEAD_DIM), lambda b, h, qi: (0, 0))
    out = pl.BlockSpec((TQ, ATT_W), lambda b, h, qi: (b * nq + qi, h))
    kept = pl.BlockSpec((None, HP, None, nq, TQ, TQ), lambda b, h, qi: (b, h, qi, 0, 0, 0))
    kept_shape = jax.ShapeDtypeStruct((B, H, nq, nq, TQ, TQ), BF16)
    return pl.pallas_call(
        body, name=name,
        out_shape=(jax.ShapeDtypeStruct((B * S, H * HEAD_DIM), BF16),) * 2 + (kept_shape,) * 2,
        grid=(B, H // HP, nq),
        in_specs=[spec(0), spec(1), spec(2), spec(3), vec, vec],
        out_specs=(out, out, kept, kept),
        scratch_shapes=[pltpu.VMEM((S, ATT_W), BF16)] * 2,
        compiler_params=_params(("parallel", "parallel", "arbitrary")),
    )(proj, proj, proj, proj, q_g, k_g)


def _attn_bwd(proj, o, dog, w_kept, sg_kept, q_g, k_g, B, S, H, name):
    TQ = _tile(S, 256)
    nq = S // TQ
    scale = 1.0 / math.sqrt(HEAD_DIM)
    HP = BWD_HEADS
    ATT_W = HP * HEAD_DIM
    heads = range(HP)

    def body(q_ref, k_ref, v_ref, gate_ref, o_ref, dog_ref, w_ref, sg_ref, qg_ref, kg_ref,
             dq_ref, dk_ref, dv_ref, dgate_ref, dqg_ref, dkg_ref,
             qn_s, kn_s, do_s, dkn_s, dv_s):
        qi = pl.program_id(2)

        @pl.when((pl.program_id(0) == 0) & (pl.program_id(1) == 0) & (qi == 0))
        def _():
            dqg_ref[...] = jnp.zeros_like(dqg_ref)
            dkg_ref[...] = jnp.zeros_like(dkg_ref)

        @pl.when(qi == 0)
        def _():
            for hh in heads:
                qn_s[:, _lanes(hh)] = (_head_norm(q_ref[:, _lanes(hh)])[0] * qg_ref[...]).astype(BF16)
                kn_s[:, _lanes(hh)] = (_head_norm(k_ref[:, _lanes(hh)])[0] * kg_ref[...]).astype(BF16)
            gate = gate_ref[...].astype(F32)
            sg = _sigmoid(gate)
            dog_v = dog_ref[...].astype(F32)
            do_s[...] = (dog_v * (gate * sg)).astype(BF16)
            dgate_ref[...] = (dog_v * o_ref[...].astype(F32) * (sg * (1.0 + gate * (1.0 - sg)))).astype(BF16)
            dkn_s[...] = jnp.zeros_like(dkn_s)
            dv_s[...] = jnp.zeros_like(dv_s)

        row = lax.broadcasted_iota(jnp.int32, (TQ, TQ), 0)
        col = lax.broadcasted_iota(jnp.int32, (TQ, TQ), 1)
        earlier = _ones_where(row < col)
        causal = col < row

        def norm_bwd(xv, g_ref, dn, dg_ref):
            xh, r = _head_norm(xv)
            dg_ref[...] += jnp.sum(dn * xh, axis=0, keepdims=True)
            dxh = dn * g_ref[...]
            return (r * (dxh - xh * jnp.mean(dxh * xh, axis=-1, keepdims=True))).astype(BF16)

        def q_block():
            q0 = pl.multiple_of(qi * TQ, TQ)

            def grads_both(ki, state, diag):
                k0 = pl.multiple_of(ki * TQ, TQ)
                qb = [qn_s[pl.ds(q0, TQ), _lanes(hh)] for hh in heads]
                dob = [do_s[pl.ds(q0, TQ), _lanes(hh)] for hh in heads]
                wb = [w_ref[hh, ki] for hh in heads]
                da = [_dot(dob[hh], v_ref[pl.ds(k0, TQ), _lanes(hh)], NT) * wb[hh].astype(F32) for hh in heads]
                for hh in heads:
                    dv_s[pl.ds(k0, TQ), _lanes(hh)] += _dot(wb[hh], dob[hh], TN)
                prefix = [_dot(da[hh].astype(BF16), earlier, NN) + state[hh][0] for hh in heads]
                dzb = []
                for hh in heads:
                    sgz = sg_ref[hh, ki].astype(F32)
                    dz = da[hh] * (1.0 - sgz) - sgz * prefix[hh]
                    if diag:
                        dz = jnp.where(causal, dz, 0.0)
                    dzb.append((dz * scale).astype(BF16))
                dq = [state[hh][1] + _dot(dzb[hh], kn_s[pl.ds(k0, TQ), _lanes(hh)], NN) for hh in heads]
                for hh in heads:
                    dkn_s[pl.ds(k0, TQ), _lanes(hh)] += _dot(dzb[hh], qb[hh], TN)
                return tuple((state[hh][0] + jnp.sum(da[hh], axis=1, keepdims=True), dq[hh]) for hh in heads)

            zero = (jnp.zeros((TQ, 1), F32), jnp.zeros((TQ, HEAD_DIM), F32))
            state = lax.fori_loop(0, qi, lambda i, st: grads_both(i, st, False), (zero,) * HP)
            state = grads_both(qi, state, True)
            for hh in heads:
                dq_ref[:, _lanes(hh)] = norm_bwd(q_ref[pl.ds(q0, TQ), _lanes(hh)], qg_ref, state[hh][1], dqg_ref)

        q_block()

        @pl.when(qi == nq - 1)
        def _():
            for hh in heads:
                dk_ref[:, _lanes(hh)] = norm_bwd(k_ref[:, _lanes(hh)], kg_ref, dkn_s[:, _lanes(hh)], dkg_ref)
            dv_ref[...] = dv_s[...].astype(BF16)

    spec = _attn_specs(S, H, HP)
    vec = pl.BlockSpec((1, HEAD_DIM), lambda b, h, qi: (0, 0))
    blk = pl.BlockSpec((S, ATT_W), lambda b, h, qi: (b, h))
    rows = pl.BlockSpec((TQ, ATT_W), lambda b, h, qi: (b * nq + qi, h))
    kept = pl.BlockSpec((None, HP, None, nq, TQ, TQ), lambda b, h, qi: (b, h, qi, 0, 0, 0))
    big = jax.ShapeDtypeStruct((B * S, H * HEAD_DIM), BF16)
    small = jax.ShapeDtypeStruct((1, HEAD_DIM), F32)
    return pl.pallas_call(
        body, name=name,
        out_shape=(big, big, big, big, small, small),
        grid=(B, H // HP, nq),
        in_specs=[spec(0), spec(1), spec(2), spec(3), blk, blk, kept, kept, vec, vec],
        out_specs=(rows, blk, blk, blk, vec, vec),
        scratch_shapes=[pltpu.VMEM((S, ATT_W), BF16)] * 3 + [pltpu.VMEM((S, ATT_W), F32)] * 2,
        compiler_params=_params(("arbitrary", "arbitrary", "arbitrary")),
    )(proj, proj, proj, proj, o, dog, w_kept, sg_kept, q_g, k_g)


HALF = GROUPS_PER_BLOCK * STATE


def _cmul(ar, ai, br, bi):
    return ar * br - ai * bi, ar * bi + ai * br


def _cpow(ar, ai, n):
    rr = ri = None
    while n:
        if n & 1:
            rr, ri = (ar, ai) if rr is None else _cmul(rr, ri, ar, ai)
        n >>= 1
        if n:
            ar, ai = _cmul(ar, ai, ar, ai)
    return rr, ri


def _segment_carry(er, ei, lr, li, seg_len, segs_per_seq, reverse):
    Lr, Li = _cpow(lr, li, seg_len)
    pos = lax.broadcasted_iota(jnp.int32, er.shape, 0) % segs_per_seq
    outr = jnp.zeros_like(er)
    outi = jnp.zeros_like(ei)
    pr = pi = None
    for d in range(1, segs_per_seq):
        shift = (SUBLANES - d) if reverse else d
        sr = pltpu.roll(er, shift, 0)
        si = pltpu.roll(ei, shift, 0)
        ok = (pos + d < segs_per_seq) if reverse else (pos >= d)
        sr = jnp.where(ok, sr, 0.0)
        si = jnp.where(ok, si, 0.0)
        if pr is not None:
            sr, si = _cmul(sr, si, pr, pi)
        outr = outr + sr
        outi = outi + si
        pr, pi = (Lr, Li) if pr is None else _cmul(pr, pi, Lr, Li)
    return outr, outi


def _s5_sizes(T, B):
    assert SUBLANES % B == 0
    segs_per_seq = SUBLANES // B
    n_steps = T // SUBLANES
    cj = _tile(n_steps, 64)
    return segs_per_seq, n_steps, cj


def _s5_fwd(u_p, wb, wc, lam, dvec, B, name):
    T, C = u_p.shape
    nb = C // 128
    segs_per_seq, n_steps, cj = _s5_sizes(T, B)
    n_chunks = n_steps // cj
    rows = cj * SUBLANES

    def body(u_ref, wb_ref, wc_ref, lam_ref, d_ref, y_ref, hin_ref, bu_s, h_s):
        lr = jnp.broadcast_to(lam_ref[:, :HALF], (SUBLANES, HALF))
        li = jnp.broadcast_to(lam_ref[:, HALF:], (SUBLANES, HALF))

        def scan_chunk(c, hr, hi, store):
            r0 = pl.multiple_of(c * rows, rows)
            if not store:
                bu_s[pl.ds(r0, rows), :] = _dot(u_ref[pl.ds(r0, rows), :], wb_ref[...], NN)

            def step(j, carry):
                hr, hi = carry
                o = pl.multiple_of(j * SUBLANES, SUBLANES)
                at = pl.multiple_of(r0 + o, SUBLANES)
                nr = lr * hr - li * hi + bu_s[pl.ds(at, SUBLANES), :HALF]
                ni = lr * hi + li * hr + bu_s[pl.ds(at, SUBLANES), HALF:]
                if store:
                    h_s[pl.ds(o, SUBLANES), :HALF] = nr
                    h_s[pl.ds(o, SUBLANES), HALF:] = ni
                return nr, ni

            hr, hi = lax.fori_loop(0, cj, step, (hr, hi))
            if store:
                uv = u_ref[pl.ds(r0, rows), :].astype(F32)
                y_ref[pl.ds(r0, rows), :] = _dot(h_s[...].astype(BF16), wc_ref[...], NN) + d_ref[...] * uv
            return hr, hi

        zero = jnp.zeros((SUBLANES, HALF), F32)
        er, ei = lax.fori_loop(0, n_chunks, lambda c, h: scan_chunk(c, h[0], h[1], False), (zero, zero))
        h0r, h0i = _segment_carry(er, ei, lr, li, n_steps, segs_per_seq, False)
        hin_ref[:, :HALF] = h0r
        hin_ref[:, HALF:] = h0i
        lax.fori_loop(0, n_chunks, lambda c, h: scan_chunk(c, h[0], h[1], True), (h0r, h0i))

    return pl.pallas_call(
        body, name=name,
        out_shape=(jax.ShapeDtypeStruct((T, C), F32), jax.ShapeDtypeStruct((nb, SUBLANES, 2 * HALF), F32)),
        grid=(nb,),
        in_specs=[pl.BlockSpec((T, 128), lambda g: (0, g)),
                  pl.BlockSpec((None, 128, 2 * HALF), lambda g: (g, 0, 0)),
                  pl.BlockSpec((None, 2 * HALF, 128), lambda g: (g, 0, 0)),
                  pl.BlockSpec((None, 1, 2 * HALF), lambda g: (g, 0, 0)),
                  pl.BlockSpec((1, 128), lambda g: (0, g))],
        out_specs=(pl.BlockSpec((T, 128), lambda g: (0, g)),
                   pl.BlockSpec((None, SUBLANES, 2 * HALF), lambda g: (g, 0, 0))),
        scratch_shapes=[pltpu.VMEM((T, 2 * HALF), F32), pltpu.VMEM((rows, 2 * HALF), F32)],
        compiler_params=_params(("parallel",)),
    )(u_p, wb, wc, lam, dvec)


def _s5_bwd(u_p, dy_p, wb, wbt, wc, wct, lam, dvec, h_in, B, name):
    T, C = u_p.shape
    nb = C // 128
    segs_per_seq, n_steps, cj = _s5_sizes(T, B)
    n_chunks = n_steps // cj
    rows = cj * SUBLANES

    def body(u_ref, dy_ref, wb_ref, wbt_ref, wc_ref, wct_ref, lam_ref, d_ref, hin_ref,
             du_ref, dwb_ref, dwc_ref, dlam_ref, dd_ref, h_all, dh_all, x_s, g_s):
        lr = jnp.broadcast_to(lam_ref[:, :HALF], (SUBLANES, HALF))
        li = jnp.broadcast_to(lam_ref[:, HALF:], (SUBLANES, HALF))
        zero = jnp.zeros((SUBLANES, HALF), F32)

        h_all[pl.ds(0, SUBLANES), :] = hin_ref[...]

        def fwd_chunk(c, carry):
            r0 = pl.multiple_of(c * rows, rows)
            x_s[...] = _dot(u_ref[pl.ds(r0, rows), :], wb_ref[...], NN)

            def step(j, carry):
                hr, hi = carry
                o = pl.multiple_of(j * SUBLANES, SUBLANES)
                nr = lr * hr - li * hi + x_s[pl.ds(o, SUBLANES), :HALF]
                ni = lr * hi + li * hr + x_s[pl.ds(o, SUBLANES), HALF:]
                late = pl.multiple_of(r0 + o + SUBLANES, SUBLANES)
                h_all[pl.ds(late, SUBLANES), :HALF] = nr
                h_all[pl.ds(late, SUBLANES), HALF:] = ni
                return nr, ni

            return lax.fori_loop(0, cj, step, carry)

        lax.fori_loop(0, n_chunks, fwd_chunk, (hin_ref[:, :HALF], hin_ref[:, HALF:]))

        def bwd_chunk(i, carry, store):
            c = n_chunks - 1 - i
            r0 = pl.multiple_of(c * rows, rows)
            dyv = dy_ref[pl.ds(r0, rows), :]
            if not store:
                dh_all[pl.ds(r0, rows), :] = _dot(dyv, wct_ref[...], NN)

            def step(jj, carry):
                ar, ai, accr, acci = carry
                j = cj - 1 - jj
                o = pl.multiple_of(j * SUBLANES, SUBLANES)
                prev = pl.multiple_of(r0 + o, SUBLANES)
                nr = lr * ar + li * ai + dh_all[pl.ds(prev, SUBLANES), :HALF]
                ni = lr * ai - li * ar + dh_all[pl.ds(prev, SUBLANES), HALF:]
                if store:
                    g_s[pl.ds(o, SUBLANES), :HALF] = nr
                    g_s[pl.ds(o, SUBLANES), HALF:] = ni
                    pr = h_all[pl.ds(prev, SUBLANES), :HALF]
                    pi = h_all[pl.ds(prev, SUBLANES), HALF:]
                    accr = accr + nr * pr + ni * pi
                    acci = acci + ni * pr - nr * pi
                return nr, ni, accr, acci

            carry = lax.fori_loop(0, cj, step, carry)
            if store:
                gb = g_s[...].astype(BF16)
                uv = u_ref[pl.ds(r0, rows), :]
                dyf = dyv.astype(F32)
                du_ref[pl.ds(r0, rows), :] = (_dot(gb, wbt_ref[...], NN) + d_ref[...] * dyf).astype(BF16)
                dwb_ref[...] += _dot(uv, gb, TN)
                hb = h_all[pl.ds(pl.multiple_of(r0 + SUBLANES, SUBLANES), rows), :].astype(BF16)
                dwc_ref[...] += _dot(hb, dyv, TN)
                dd_ref[...] += jnp.sum(dyf * uv.astype(F32), axis=0, keepdims=True)
            return carry

        er, ei, _, _ = lax.fori_loop(0, n_chunks, lambda i, c: bwd_chunk(i, c, False), (zero, zero, zero, zero))
        a0r, a0i = _segment_carry(er, ei, lr, -li, n_steps, segs_per_seq, True)
        dwb_ref[...] = jnp.zeros_like(dwb_ref)
        dwc_ref[...] = jnp.zeros_like(dwc_ref)
        dd_ref[...] = jnp.zeros_like(dd_ref)
        _, _, accr, acci = lax.fori_loop(0, n_chunks, lambda i, c: bwd_chunk(i, c, True), (a0r, a0i, zero, zero))
        dlam_ref[:, :HALF] = jnp.sum(accr, axis=0, keepdims=True)
        dlam_ref[:, HALF:] = jnp.sum(acci, axis=0, keepdims=True)

    col = pl.BlockSpec((T, 128), lambda g: (0, g))
    vec = pl.BlockSpec((1, 128), lambda g: (0, g))

    def per_block(*shape):
        return pl.BlockSpec((None,) + shape, lambda g: (g, 0, 0))

    return pl.pallas_call(
        body, name=name,
        out_shape=(jax.ShapeDtypeStruct((T, C), BF16),
                   jax.ShapeDtypeStruct((nb, 128, 2 * HALF), F32),
                   jax.ShapeDtypeStruct((nb, 2 * HALF, 128), F32),
                   jax.ShapeDtypeStruct((nb, 1, 2 * HALF), F32),
                   jax.ShapeDtypeStruct((1, C), F32)),
        grid=(nb,),
        in_specs=[col, col, per_block(128, 2 * HALF), per_block(2 * HALF, 128), per_block(2 * HALF, 128),
                  per_block(128, 2 * HALF), per_block(1, 2 * HALF), vec, per_block(SUBLANES, 2 * HALF)],
        out_specs=(col, per_block(128, 2 * HALF), per_block(2 * HALF, 128), per_block(1, 2 * HALF), vec),
        scratch_shapes=[pltpu.VMEM((T + SUBLANES, 2 * HALF), F32), pltpu.VMEM((T, 2 * HALF), F32),
                        pltpu.VMEM((rows, 2 * HALF), F32), pltpu.VMEM((rows, 2 * HALF), F32)],
        compiler_params=_params(("parallel",)),
    )(u_p, dy_p, wb, wbt, wc, wct, lam, dvec, h_in)


def _discretize(a_re, a_im, log_dt, b_re, b_im):
    dt = jnp.exp(log_dt)[:, None]
    mag = jnp.exp(a_re * dt)
    lam_re = mag * jnp.cos(a_im * dt)
    lam_im = mag * jnp.sin(a_im * dt)
    den = a_re * a_re + a_im * a_im
    f_re = ((lam_re - 1.0) * a_re + lam_im * a_im) / den
    f_im = (lam_im * a_re - (lam_re - 1.0) * a_im) / den
    bb_re = f_re[..., None] * b_re - f_im[..., None] * b_im
    bb_im = f_re[..., None] * b_im + f_im[..., None] * b_re
    return lam_re, lam_im, bb_re, bb_im


def _block_diag_in(bb_re, bb_im):
    eye = jnp.eye(GROUPS_PER_BLOCK, dtype=F32)

    def one(bb):
        t = bb.reshape(-1, GROUPS_PER_BLOCK, STATE, GROUP)
        return jnp.einsum('gapi,ab->gaibp', t, eye).reshape(-1, 128, HALF)

    return jnp.concatenate([one(bb_re), one(bb_im)], axis=-1)


def _block_diag_in_grad(dwb):
    eye = jnp.eye(GROUPS_PER_BLOCK, dtype=F32)

    def one(d):
        t = d.reshape(-1, GROUPS_PER_BLOCK, GROUP, GROUPS_PER_BLOCK, STATE)
        return jnp.einsum('gaibp,ab->gapi', t, eye).reshape(-1, STATE, GROUP)

    return one(dwb[..., :HALF]), one(dwb[..., HALF:])


def _block_diag_out(c_re, c_im):
    eye = jnp.eye(GROUPS_PER_BLOCK, dtype=F32)

    def one(cc):
        t = cc.reshape(-1, GROUPS_PER_BLOCK, GROUP, STATE)
        return jnp.einsum('gaip,ab->gbpai', t, eye).reshape(-1, HALF, 128)

    return jnp.concatenate([one(c_re), -one(c_im)], axis=1)


def _block_diag_out_grad(dwc):
    eye = jnp.eye(GROUPS_PER_BLOCK, dtype=F32)

    def one(d):
        t = d.reshape(-1, GROUPS_PER_BLOCK, STATE, GROUPS_PER_BLOCK, GROUP)
        return jnp.einsum('gbpai,ab->gaip', t, eye).reshape(-1, GROUP, STATE)

    return one(dwc[:, :HALF]), -one(dwc[:, HALF:])


def _pack(parts):
    return jnp.concatenate([p.reshape(-1, PACK_W) for p in parts], axis=0)


def _unpack(buf, shapes):
    lead = buf.shape[:-2]
    out, r = [], 0
    for s in shapes:
        n = math.prod(s) // PACK_W
        out.append(buf[..., r:r + n, :].reshape(lead + tuple(s)))
        r += n
    return out


def _permute_rows(a, n_steps):
    T, C = a.shape
    return a.reshape(SUBLANES, n_steps, C).transpose(1, 0, 2).reshape(T, C)


def _unpermute_rows(a, n_steps):
    T, C = a.shape
    return a.reshape(n_steps, SUBLANES, C).transpose(1, 0, 2).reshape(T, C)


def kernel(x, norm_g, attn_w_in, attn_q_g, attn_k_g, attn_w_out, ssm_w_in, ssm_A_re, ssm_A_im, ssm_log_dt, ssm_B_re, ssm_B_im, ssm_C_re, ssm_C_im, ssm_D, ssm_glu_w, ssm_glu_b, ssm_w_out, loss_target, m_norm_g, m_attn_w_in, m_attn_q_g, m_attn_k_g, m_attn_w_out, m_ssm_w_in, m_ssm_A_re, m_ssm_A_im, m_ssm_log_dt, m_ssm_B_re, m_ssm_B_im, m_ssm_C_re, m_ssm_C_im, m_ssm_D, m_ssm_glu_w, m_ssm_glu_b, m_ssm_w_out, v_norm_g, v_attn_w_in, v_attn_q_g, v_attn_k_g, v_attn_w_out, v_ssm_w_in, v_ssm_A_re, v_ssm_A_im, v_ssm_log_dt, v_ssm_B_re, v_ssm_B_im, v_ssm_C_re, v_ssm_C_im, v_ssm_D, v_ssm_glu_w, v_ssm_glu_b, v_ssm_w_out):
    B, S, D = x.shape
    T = B * S
    H = D // HEAD_DIM
    G_loc = ssm_A_re.shape[1]
    G = G_loc * N_DEV
    n_steps = T // SUBLANES
    me, chip = _my_index(), _my_chip()
    xf = x.reshape(T, D)
    target = loss_target.reshape(T, D)

    small_shapes = [(G_loc, STATE), (G_loc, STATE), (G_loc, STATE, GROUP), (G_loc, STATE, GROUP),
                    (G_loc, GROUP, STATE), (G_loc, GROUP, STATE), (G_loc * GROUP,), (G_loc * GROUP,)]
    disc_in = (ssm_A_re[0], ssm_A_im[0], ssm_log_dt[0], ssm_B_re[0], ssm_B_im[0])
    (lam_re, lam_im, bb_re, bb_im), disc_vjp = jax.vjp(_discretize, *disc_in)
    small = _pack([lam_re, lam_im, bb_re, bb_im, ssm_C_re[0], ssm_C_im[0], ssm_D[0], ssm_glu_b[0]])
    small_all = _exchange(small, False, "gather_small")
    c_in0 = _place(attn_w_in[0], me, N_DEV, BF16, "cast_attn_w_in")
    near = _exchange_start(c_in0, None, "near", "gather_attn_w_in_near_start", deps=(small_all,))
    beside = (near[3],)
    small_all = small_all + near[3][0, 0]
    c_out0 = _place(attn_w_out[0], me, N_DEV, BF16, "cast_attn_w_out", deps=beside)
    c_in1 = _place(ssm_w_in[0], me, N_DEV, BF16, "cast_ssm_w_in", deps=beside)
    c_glu = _place(ssm_glu_w[0], me, N_DEV, BF16, "cast_ssm_glu_w", deps=beside)
    c_out1 = _place(ssm_w_out[0], me, N_DEV, BF16, "cast_ssm_w_out", deps=beside)

    lam_re_a, lam_im_a, bb_re_a, bb_im_a, c_re_a, c_im_a, d_a, glu_b_a = [
        t.reshape((G,) + t.shape[2:]) if t.ndim > 2 else t.reshape(-1)
        for t in _unpack(small_all, small_shapes)]
    wb = _block_diag_in(bb_re_a, bb_im_a)
    wc = _block_diag_out(c_re_a, c_im_a)
    wb_b, wc_b = wb.astype(BF16), wc.astype(BF16)
    wbt_b, wct_b = wb_b.transpose(0, 2, 1), wc_b.transpose(0, 2, 1)
    lam = jnp.concatenate([lam_re_a.reshape(-1, 1, HALF), lam_im_a.reshape(-1, 1, HALF)], axis=-1)
    d_row = d_a.reshape(1, D)
    glu_b_row = glu_b_a.reshape(1, D)
    g0, g1 = norm_g[0:1], norm_g[1:2]
    q_g, k_g = attn_q_g, attn_k_g

    h0 = _rmsnorm_fwd(xf, g0, "norm0", deps=beside)
    done = (h0, c_out0, c_in1, c_glu, c_out1, wb_b, wbt_b, wc_b, wct_b, lam, d_row, glu_b_row)
    far = _exchange_start(_exchange_wait(near, done, "near", "gather_attn_w_in_near_wait"), None, "far",
                          "gather_attn_w_in_far_start")
    w_in0 = _exchange_wait(far, far[3], "far", "gather_attn_w_in_far_wait")
    s_out0 = _exchange_start(c_out0, None, "gather", "gather_attn_w_out_start", deps=(w_in0,))
    s_in1 = _exchange_start(c_in1, None, "gather", "gather_ssm_w_in_start", deps=(w_in0,))
    proj0 = _mm_nn(h0, w_in0, BF16, "attn_in", deps=(s_out0[3], s_in1[3]), tk=FULL_K)
    og, o, w_kept, sg_kept = _attn_fwd(proj0, q_g, k_g, B, S, H, "attn_fwd")
    w_out0 = _exchange_wait(s_out0, og, "gather", "gather_attn_w_out_wait").reshape(1, D, D)
    s_glu = _exchange_start(c_glu, None, "gather", "gather_ssm_glu_w_start", deps=(w_out0,))
    s_out1 = _exchange_start(c_out1, None, "gather", "gather_ssm_w_out_start", deps=(w_out0,))
    x1 = _mm_nn(og, w_out0, F32, "attn_out", residual=xf, deps=(s_glu[3], s_out1[3]))

    h1 = _rmsnorm_fwd(x1, g1, "norm1")
    w_in1 = _exchange_wait(s_in1, h1, "gather", "gather_ssm_w_in_wait")
    proj1 = _mm_nn(h1, w_in1, BF16, "ssm_in", tk=FULL_K)
    u_p = _permute_rows(proj1[:, :D], n_steps)
    gate1 = proj1[:, D:]
    y_p, h_in = _s5_fwd(u_p, wb_b, wc_b, lam, d_row, B, "s5_fwd")
    y_ssm = _unpermute_rows(y_p, n_steps)
    (yg,) = _ew(lambda a: (_gelu(a)[0],), [y_ssm], [], [BF16], 0, "gelu")
    w_glu = _exchange_wait(s_glu, yg, "gather", "gather_ssm_glu_w_wait").reshape(1, D, D)

    def glu_gate(zz, y, gt):
        gt = gt.astype(F32)
        return zz, _gelu(y)[0] * _sigmoid(zz) * (gt * _sigmoid(gt))

    z, y3 = _mm_nn(yg, w_glu, F32, "glu_in", bias=glu_b_row, extras=[(y_ssm, 0), (proj1, D // _tile(D, 1024))],
                   epilogue=glu_gate, out_dtypes=[F32, BF16], tk=FUSED_TK)
    w_out1 = _exchange_wait(s_out1, y3, "gather", "gather_ssm_w_out_wait").reshape(1, D, D)

    def loss_head(out_tile, tgt):
        err = out_tile - tgt
        dy = err * (1.0 / D)
        return dy, dy, jnp.sum(err * err, axis=0, keepdims=True)

    dout, dout_b, sq = _mm_nn(y3, w_out1, F32, "ssm_out", residual=x1, extras=[(target, 0)], epilogue=loss_head,
                              out_dtypes=[F32, BF16], tile_sums=True, tk=FUSED_TK)
    loss_part = _total(sq, 0.5 / D, "loss")
    loss = lax.psum(loss_part[0, 0], ("x", "y", "c"))

    p_w_out1 = _mm_tn(y3, dout_b, 1, BF16, "ssm_out_dw").reshape(N_DEV, D // N_DEV, D)
    sc_out1 = _exchange_start(_place(p_w_out1, me, N_DEV, BF16, "place_ssm_w_out"), p_w_out1, "scatter", "scatter_ssm_w_out_start")
    dy3 = _mm_nt(dout_b, w_out1, F32, "ssm_out_dx", deps=(sc_out1[3],), tq=FULL_K)

    def glu_bwd(d3, y, zz, gt):
        gt = gt.astype(F32)
        sg = _sigmoid(gt)
        sz = _sigmoid(zz)
        ygv, _ = _gelu(y)
        dy2 = d3 * (gt * sg)
        dgate = d3 * (ygv * sz) * (sg * (1.0 + gt * (1.0 - sg)))
        dz = dy2 * ygv * (sz * (1.0 - sz))
        return dz, dgate, dy2 * sz, jnp.sum(dz, axis=0, keepdims=True)

    dz_b, dgate1, dyg_a, dglu_b = _ew(glu_bwd, [dy3, y_ssm, z, gate1], [], [BF16, BF16, F32], 1, "glu_gate_bwd")
    p_w_glu = _mm_tn(yg, dz_b, 1, BF16, "glu_in_dw").reshape(N_DEV, D // N_DEV, D)
    sc_glu = _exchange_start(_place(p_w_glu, me, N_DEV, BF16, "place_ssm_glu_w"), p_w_glu, "scatter", "scatter_ssm_glu_w_start")

    def gelu_bwd(db, da, y):
        _, t = _gelu(y)
        dg = 0.5 * (1.0 + t) + 0.5 * y * (1.0 - t * t) * (GELU_C * (1.0 + 3.0 * 0.044715 * (y * y)))
        return (da + db) * dg

    dy_ssm = _mm_nt(dz_b, w_glu, BF16, "glu_in_dx", deps=(sc_glu[3],), extras=[dyg_a, y_ssm], epilogue=gelu_bwd,
                    tq=FUSED_TK)
    dy_p = _permute_rows(dy_ssm, n_steps)
    du_p, dwb, dwc, dlam, dd = _s5_bwd(u_p, dy_p, wb_b, wbt_b, wc_b, wct_b, lam, d_row, h_in, B, "s5_bwd")
    du = _unpermute_rows(du_p, n_steps)
    dproj1 = jnp.concatenate([du, dgate1], axis=1)
    p_w_in1 = _mm_tn(h1, dproj1, N_DEV, BF16, "ssm_in_dw")
    sc_in1 = _exchange_start(_place(p_w_in1, me, N_DEV, BF16, "place_ssm_w_in"), p_w_in1, "scatter", "scatter_ssm_w_in_start")
    dh1 = _mm_nt(dproj1, w_in1, F32, "ssm_in_dx", deps=(sc_in1[3],), tq=FULL_K)
    dx1, dx1_b, dg1 = _rmsnorm_bwd(x1, dh1, dout, g1, "norm1_bwd")

    p_w_out0 = _mm_tn(og, dx1_b, 1, BF16, "attn_out_dw").reshape(N_DEV, D // N_DEV, D)
    sc_out0 = _exchange_start(_place(p_w_out0, me, N_DEV, BF16, "place_attn_w_out"), p_w_out0, "scatter", "scatter_attn_w_out_start")
    dog = _mm_nt(dx1_b, w_out0, BF16, "attn_out_dx", deps=(sc_out0[3],), tq=FULL_K)
    dq, dk, dv, dgate0, dqg, dkg = _attn_bwd(proj0, o, dog, w_kept, sg_kept, q_g, k_g, B, S, H, "attn_bwd")
    dproj0 = jnp.concatenate([dq, dk, dv, dgate0], axis=1)
    p_w_in0 = _mm_tn(h0, dproj0, N_DEV, BF16, "attn_in_dw")
    pair = _exchange_start(lax.empty((N_DEV // 2,) + p_w_in0.shape[1:], BF16), p_w_in0, "pair", "scatter_attn_w_in_pair_start")
    got, p_w_in0 = _exchange_wait(pair, pair[3], "pair", "scatter_attn_w_in_pair_wait", with_source=True)
    q_w_in0 = _pair_sum(p_w_in0, got, "scatter_attn_w_in_pair_sum")
    sc_in0 = _exchange_start(_place(q_w_in0, chip, N_DEV // 2, BF16, "place_attn_w_in"), q_w_in0, "chips", "scatter_attn_w_in_start")
    dh0 = _mm_nt(dproj0, w_in0, F32, "attn_in_dx", deps=(sc_in0[3],), tq=FULL_K)
    dx, _, dg0 = _rmsnorm_bwd(xf, dh0, dx1, g0, "norm0_bwd")

    def update(started, after, w, m, v, name, kind="scatter"):
        recv = _exchange_wait(started, after, kind, "scatter_" + name + "_wait")
        return _adamw(recv, w[0], m[0], v[0], "adamw_" + name)

    r_ssm_w_out = update(sc_out1, dx, ssm_w_out, m_ssm_w_out, v_ssm_w_out, "ssm_w_out")
    r_ssm_glu_w = update(sc_glu, r_ssm_w_out[0], ssm_glu_w, m_ssm_glu_w, v_ssm_glu_w, "ssm_glu_w")
    r_ssm_w_in = update(sc_in1, r_ssm_glu_w[0], ssm_w_in, m_ssm_w_in, v_ssm_w_in, "ssm_w_in")
    r_attn_w_out = update(sc_out0, r_ssm_w_in[0], attn_w_out, m_attn_w_out, v_attn_w_out, "attn_w_out")

    dbb_re, dbb_im = _block_diag_in_grad(dwb)
    dc_re, dc_im = _block_diag_out_grad(dwc)
    dlam_re = dlam[:, 0, :HALF].reshape(G, STATE)
    dlam_im = dlam[:, 0, HALF:].reshape(G, STATE)
    by_owner = [t.reshape((N_DEV, -1)) for t in (dlam_re, dlam_im, dbb_re, dbb_im, dc_re, dc_im, dd, dglu_b)]
    small_parts = jnp.concatenate([t.reshape(N_DEV, -1, PACK_W) for t in by_owner], axis=1)
    small_sum = _sum_parts(_exchange(small_parts, True, "scatter_small"), "sum_small")
    s_lam_re, s_lam_im, s_bb_re, s_bb_im, s_c_re, s_c_im, s_d, s_glu_b = _unpack(small_sum, small_shapes)
    g_a_re, g_a_im, g_log_dt, g_b_re, g_b_im = disc_vjp((s_lam_re, s_lam_im, s_bb_re, s_bb_im))

    local_names = ["ssm_A_re", "ssm_A_im", "ssm_log_dt", "ssm_B_re", "ssm_B_im", "ssm_C_re", "ssm_C_im",
                   "ssm_D", "ssm_glu_b"]
    local_g = [g_a_re, g_a_im, g_log_dt, g_b_re, g_b_im, s_c_re, s_c_im, s_d, s_glu_b]
    local_w = [ssm_A_re, ssm_A_im, ssm_log_dt, ssm_B_re, ssm_B_im, ssm_C_re, ssm_C_im, ssm_D, ssm_glu_b]
    local_m = [m_ssm_A_re, m_ssm_A_im, m_ssm_log_dt, m_ssm_B_re, m_ssm_B_im, m_ssm_C_re, m_ssm_C_im,
               m_ssm_D, m_ssm_glu_b]
    local_v = [v_ssm_A_re, v_ssm_A_im, v_ssm_log_dt, v_ssm_B_re, v_ssm_B_im, v_ssm_C_re, v_ssm_C_im,
               v_ssm_D, v_ssm_glu_b]
    r_local = _adamw_small(local_g, local_w, local_m, local_v, None, "adamw_small")

    rep_g = [jnp.concatenate([dg0, dg1], axis=0), dqg, dkg]
    rep_w = [norm_g, attn_q_g, attn_k_g]
    rep_m = [m_norm_g, m_attn_q_g, m_attn_k_g]
    rep_v = [v_norm_g, v_attn_q_g, v_attn_k_g]
    r_rep = _adamw_small(rep_g, rep_w, rep_m, rep_v, "gather_rep", "adamw_rep")

    r_attn_w_in = update(sc_in0, r_rep[0][0], attn_w_in, m_attn_w_in, v_attn_w_in, "attn_w_in", "chips")

    res = {"attn_w_in": r_attn_w_in, "attn_w_out": r_attn_w_out, "ssm_w_in": r_ssm_w_in,
           "ssm_glu_w": r_ssm_glu_w, "ssm_w_out": r_ssm_w_out}
    ref_w = {"attn_w_in": attn_w_in, "attn_w_out": attn_w_out, "ssm_w_in": ssm_w_in,
             "ssm_glu_w": ssm_glu_w, "ssm_w_out": ssm_w_out}
    for name, r, w in zip(local_names, r_local, local_w):
        res[name], ref_w[name] = r, w
    for name, r, w in zip(["norm_g", "attn_q_g", "attn_k_g"], r_rep, rep_w):
        res[name], ref_w[name] = r, w
    order = ["norm_g", "attn_w_in", "attn_q_g", "attn_k_g", "attn_w_out", "ssm_w_in", "ssm_A_re", "ssm_A_im",
             "ssm_log_dt", "ssm_B_re", "ssm_B_im", "ssm_C_re", "ssm_C_im", "ssm_D", "ssm_glu_w", "ssm_glu_b",
             "ssm_w_out"]
    outs = [loss, dx.reshape(B, S, D)]
    for kind in range(4):
        outs += [res[n][kind].reshape(ref_w[n].shape) for n in order]
    return tuple(outs)


def _adamw_small(grads, ws, ms, vs, gather_name, name):
    sizes = [math.prod(w.shape) for w in ws]
    total = sum(sizes)
    rows = -(-total // (PACK_W * 8)) * 8
    if rows > 256:
        rows = -(-rows // 256) * 256

    def pack(ts, fill):
        flat = jnp.concatenate([t.reshape(-1).astype(F32) for t in ts])
        flat = jnp.concatenate([flat, jnp.full((rows * PACK_W - total,), fill, F32)])
        return flat.reshape(rows, PACK_W)

    g = pack(grads, 0.0)
    parts = _exchange(g, False, gather_name) if gather_name else g[None]
    res = _adamw(parts, pack(ws, 0.0), pack(ms, 0.0), pack(vs, 1.0), name)
    outs = []
    off = 0
    flats = [r.reshape(-1) for r in res]
    for n in sizes:
        outs.append(tuple(f[off:off + n] for f in flats))
        off += n
    return outs
```

```python
import functools
import math

import jax
import jax.numpy as jnp
from jax import lax
from jax.experimental import pallas as pl
from jax.experimental.pallas import tpu as pltpu

F32 = jnp.float32
BF16 = jnp.bfloat16

N_DEV = 8
HEAD_DIM = 128
GROUP = 16
STATE = 64
GROUPS_PER_BLOCK = 8
SUBLANES = 8
RMS_EPS = 1e-6
ADAM_LR, ADAM_B1, ADAM_B2, ADAM_EPS, ADAM_WD, ADAM_STEP = 0.001, 0.9, 0.999, 1e-08, 0.01, 10
VMEM_LIMIT = 56 * 1024 * 1024
GELU_C = math.sqrt(2.0 / math.pi)
PACK_W = 128


def _params(sem, **kw):
    return pltpu.CompilerParams(dimension_semantics=sem, vmem_limit_bytes=VMEM_LIMIT, **kw)


def _tile(n, t):
    t = min(n, t)
    assert n % t == 0, (n, t)
    return t


def _ones_where(cond):
    return jnp.where(cond, 1.0, 0.0).astype(BF16)


def _sigmoid(x):
    return 1.0 / (1.0 + jnp.exp(-x))


def _log_sigmoid(z):
    return jnp.minimum(z, 0.0) - jnp.log(1.0 + jnp.exp(-jnp.abs(z)))


def _dot(a, b, dims):
    return lax.dot_general(a, b, (dims, ((), ())), preferred_element_type=F32)


NN = ((1,), (0,))
NT = ((1,), (1,))
TN = ((0,), (0,))


def _exchange(x, scatter, name, deps=()):
    shape = x.shape[1:] if scatter else x.shape

    def body(*refs):
        x_ref = refs[0]
        out_ref, send_sems, recv_sems, local_sem = refs[1 + len(deps):]
        ix, iy, ic = lax.axis_index("x"), lax.axis_index("y"), lax.axis_index("c")
        me = 4 * ix + 2 * iy + ic

        def peer(k):
            kx, ky, kc = (k >> 2) & 1, (k >> 1) & 1, k & 1
            px, py, pc = ix ^ kx, iy ^ ky, ic ^ kc
            return (px, py, pc), 4 * px + 2 * py + pc

        mine = pltpu.make_async_copy(x_ref.at[me] if scatter else x_ref, out_ref.at[me], local_sem)
        mine.start()
        copies = []
        for k in range(1, N_DEV):
            pid, pidx = peer(k)
            cp = pltpu.make_async_remote_copy(
                src_ref=x_ref.at[pidx] if scatter else x_ref,
                dst_ref=out_ref.at[me],
                send_sem=send_sems.at[k - 1], recv_sem=recv_sems.at[k - 1],
                device_id=pid, device_id_type=pl.DeviceIdType.MESH)
            cp.start()
            copies.append(cp)
        for k in range(1, N_DEV):
            pid, pidx = peer(k)
            pltpu.make_async_remote_copy(
                src_ref=x_ref.at[pidx] if scatter else x_ref,
                dst_ref=out_ref.at[pidx],
                send_sem=send_sems.at[k - 1], recv_sem=recv_sems.at[k - 1],
                device_id=pid, device_id_type=pl.DeviceIdType.MESH).wait_recv()
        for cp in copies:
            cp.wait_send()
        mine.wait()

    return pl.pallas_call(
        body, name=name,
        out_shape=jax.ShapeDtypeStruct((N_DEV,) + tuple(shape), x.dtype),
        in_specs=[pl.BlockSpec(memory_space=pl.ANY)] * (1 + len(deps)),
        out_specs=pl.BlockSpec(memory_space=pl.ANY),
        scratch_shapes=[pltpu.SemaphoreType.DMA((N_DEV - 1,)), pltpu.SemaphoreType.DMA((N_DEV - 1,)),
                        pltpu.SemaphoreType.DMA],
    )(x, *deps)


_HBM = pl.BlockSpec(memory_space=pltpu.HBM)
_SEM = pl.BlockSpec(memory_space=pltpu.SEMAPHORE)
_ANY = pl.BlockSpec(memory_space=pl.ANY)
_EFFECT = pltpu.SideEffectType.DATAFLOW_SIDE_EFFECTING


def _peer(k):
    ix, iy, ic = lax.axis_index("x"), lax.axis_index("y"), lax.axis_index("c")
    px, py, pc = ix ^ ((k >> 2) & 1), iy ^ ((k >> 1) & 1), ic ^ (k & 1)
    return (px, py, pc), 4 * px + 2 * py + pc


def _my_index():
    return 4 * lax.axis_index("x") + 2 * lax.axis_index("y") + lax.axis_index("c")


def _my_chip():
    return 2 * lax.axis_index("x") + lax.axis_index("y")


def _place(src, idx, n_slots, dtype, name, deps=(), tr=256, tc=1024):
    R, C = src.shape[-2:]
    tr, tc = _tile(R, tr), _tile(C, tc)
    idx = idx.astype(jnp.int32).reshape(1)

    def body(*refs):
        refs[-1][...] = refs[1][...].astype(dtype)

    slot = pl.BlockSpec((None, tr, tc), lambda r, c, idx_ref: (idx_ref[0], r, c))
    src_spec = slot if src.ndim == 3 else pl.BlockSpec((tr, tc), lambda r, c, idx_ref: (r, c))
    land = lax.empty((n_slots, R, C), dtype)
    return pl.pallas_call(
        body, name=name,
        out_shape=jax.ShapeDtypeStruct(land.shape, dtype),
        grid_spec=pltpu.PrefetchScalarGridSpec(
            num_scalar_prefetch=1, grid=(R // tr, C // tc), in_specs=[src_spec, _ANY] + [_ANY] * len(deps),
            out_specs=slot),
        input_output_aliases={2: 0},
        compiler_params=_params(("parallel", "parallel")),
    )(idx, src, land, *deps)


def _pair_sum(x, got, name, tr=256, tc=1024):
    _, R, C = x.shape
    tr, tc = _tile(R, tr), _tile(C, tc)
    core = lax.axis_index("c").astype(jnp.int32).reshape(1)

    def body(core_ref, x_ref, got_ref, out_ref):
        out_ref[...] = (x_ref[...].astype(F32) + got_ref[...].astype(F32)).astype(out_ref.dtype)

    blk = pl.BlockSpec((None, tr, tc), lambda i, r, c, core_ref: (i, r, c))
    return pl.pallas_call(
        body, name=name,
        out_shape=jax.ShapeDtypeStruct(got.shape, got.dtype),
        grid_spec=pltpu.PrefetchScalarGridSpec(
            num_scalar_prefetch=1, grid=(N_DEV // 2, R // tr, C // tc),
            in_specs=[pl.BlockSpec((None, tr, tc), lambda i, r, c, core_ref: (2 * i + core_ref[0], r, c)), blk],
            out_specs=blk),
        compiler_params=_params(("parallel", "parallel", "parallel")),
    )(core, x, got)


_N_COPIES = {"gather": N_DEV - 1, "scatter": N_DEV - 1, "pair": N_DEV // 2, "chips": N_DEV // 2 - 1,
             "near": N_DEV // 2, "far": N_DEV // 2 - 1}


def _copies(kind, land_ref, x_ref):
    ix, iy, ic = lax.axis_index("x"), lax.axis_index("y"), lax.axis_index("c")
    me, chip = _my_index(), _my_chip()
    out = []
    if kind in ("gather", "scatter"):
        for k in range(1, N_DEV):
            pid, pidx = _peer(k)
            out.append((land_ref.at[me] if kind == "gather" else x_ref.at[pidx], land_ref.at[me], pid, land_ref.at[pidx]))
    elif kind == "pair":
        for i in range(N_DEV // 2):
            out.append((x_ref.at[2 * i + (1 - ic)], land_ref.at[i], (ix, iy, 1 - ic), land_ref.at[i]))
    elif kind in ("near", "far"):
        if kind == "near":
            out.append((land_ref.at[me], land_ref.at[me], (ix, iy, 1 - ic), land_ref.at[me + 1 - 2 * ic]))
        for k in range(1, N_DEV // 2):
            px, py = ix ^ (k >> 1), iy ^ (k & 1)
            theirs = 4 * px + 2 * py + ic
            if kind == "near":
                out.append((land_ref.at[me], land_ref.at[me], (px, py, ic), land_ref.at[theirs]))
            else:
                out.append((land_ref.at[theirs], land_ref.at[theirs], (ix, iy, 1 - ic),
                            land_ref.at[theirs + 1 - 2 * ic]))
    else:
        for k in range(1, N_DEV // 2):
            px, py = ix ^ (k >> 1), iy ^ (k & 1)
            out.append((x_ref.at[2 * px + py], land_ref.at[chip], (px, py, ic), land_ref.at[2 * px + py]))
    return out


def _exchange_start(land, x, kind, name, deps=()):
    bufs = [land] if x is None else [land, x]
    nb, n = len(bufs), _N_COPIES[kind]

    def body(*refs):
        send_sems, recv_sems = refs[nb + len(deps):nb + len(deps) + 2]
        token = refs[2 * nb + len(deps) + 2]
        for i, (src, dst, pid, _) in enumerate(_copies(kind, refs[0], refs[nb - 1])):
            pltpu.make_async_remote_copy(src_ref=src, dst_ref=dst, send_sem=send_sems.at[i], recv_sem=recv_sems.at[i],
                                         device_id=pid, device_id_type=pl.DeviceIdType.MESH).start()
        token[...] = jnp.zeros_like(token)

    out = pl.pallas_call(
        body, name=name,
        out_shape=(pltpu.SemaphoreType.DMA((n,)), pltpu.SemaphoreType.DMA((n,)))
        + tuple(pltpu.HBM(t.shape, t.dtype) for t in bufs) + (jax.ShapeDtypeStruct((8, 128), F32),),
        in_specs=(_HBM,) * nb + (_ANY,) * len(deps),
        out_specs=(_SEM, _SEM) + (_HBM,) * nb + (pl.BlockSpec(memory_space=pltpu.VMEM),),
        input_output_aliases={i: 2 + i for i in range(nb)},
        compiler_params=pltpu.CompilerParams(has_side_effects=_EFFECT),
    )(*[pltpu.with_memory_space_constraint(t, pltpu.HBM) for t in bufs], *deps)
    return out[0], out[1], out[2:2 + nb], out[2 + nb]


def _exchange_wait(started, after, kind, name, with_source=False):
    send_sems, recv_sems, bufs, _ = started
    nb = len(bufs)
    after = tuple(after) if isinstance(after, (tuple, list)) else (after,)

    def body(*refs):
        send_sems, recv_sems = refs[nb:nb + 2]
        for i, (src, _, pid, landed) in enumerate(_copies(kind, refs[0], refs[nb - 1])):
            cp = pltpu.make_async_remote_copy(src_ref=src, dst_ref=landed, send_sem=send_sems.at[i],
                                              recv_sem=recv_sems.at[i], device_id=pid,
                                              device_id_type=pl.DeviceIdType.MESH)
            cp.wait_send()
            cp.wait_recv()

    out = pl.pallas_call(
        body, name=name,
        out_shape=tuple(pltpu.HBM(t.shape, t.dtype) for t in bufs),
        in_specs=(_HBM,) * nb + (_SEM, _SEM) + (_ANY,) * len(after),
        out_specs=(_HBM,) * nb,
        input_output_aliases={i: i for i in range(nb)},
        compiler_params=pltpu.CompilerParams(has_side_effects=_EFFECT),
    )(*bufs, send_sems, recv_sems, *after)
    return tuple(out) if with_source else out[0]


FUSED_TK = 1024
FULL_K = 4096


def _accumulate(acc, part, step, n_steps, finish):
    if n_steps == 1:
        finish(part)
        return

    @pl.when(step == 0)
    def _():
        acc[...] = part

    @pl.when((step > 0) & (step < n_steps - 1))
    def _():
        acc[...] += part

    @pl.when(step == n_steps - 1)
    def _():
        finish(acc[...] + part)


def _mm_nn(a, b, out_dtype, name, bias=None, residual=None, deps=(), extras=(), epilogue=None,
           out_dtypes=None, tile_sums=False, tm=1024, tn=1024, tk=2048):
    M, K = a.shape
    J, K2, Nj = b.shape
    assert K == K2
    tm, tn, tk = _tile(M, tm), _tile(Nj, tn), _tile(K, tk)
    nb, nk = Nj // tn, K // tk
    out_dtypes = [out_dtype] if out_dtypes is None else list(out_dtypes)
    n_out = len(out_dtypes) + bool(tile_sums)

    def body(*refs):
        a_ref, b_ref = refs[0], refs[1]
        i = 2
        bias_ref = res_ref = None
        if bias is not None:
            bias_ref = refs[i]; i += 1
        if residual is not None:
            res_ref = refs[i]; i += 1
        extra_refs = refs[i:i + len(extras)]
        i += len(extras) + len(deps)
        o_refs, acc = refs[i:i + n_out], refs[i + n_out]

        def finish(r):
            if bias_ref is not None:
                r = r + bias_ref[...]
            if res_ref is not None:
                r = r + res_ref[...]
            res = (r,) if epilogue is None else epilogue(r, *[e[...] for e in extra_refs])
            for o_ref, v in zip(o_refs, res[:len(out_dtypes)]):
                o_ref[...] = v.astype(o_ref.dtype)
            if tile_sums:
                total = jnp.sum(res[-1], axis=1, keepdims=True)
                first = (lax.broadcasted_iota(jnp.int32, (8, 128), 0) == 0) & (lax.broadcasted_iota(jnp.int32, (8, 128), 1) == 0)
                o_refs[-1][...] = jnp.where(first, total, 0.0)

        _accumulate(acc, _dot(a_ref[...], b_ref[...], NN), pl.program_id(3), nk, finish)

    tile = pl.BlockSpec((tm, tn), lambda j, m, n, k: (m, j * nb + n))
    in_specs = [pl.BlockSpec((tm, tk), lambda j, m, n, k: (m, k)),
                pl.BlockSpec((None, tk, tn), lambda j, m, n, k: (j, k, n))]
    args = [a, b]
    if bias is not None:
        in_specs.append(pl.BlockSpec((1, tn), lambda j, m, n, k: (0, j * nb + n)))
        args.append(bias)
    if residual is not None:
        in_specs.append(tile)
        args.append(residual)
    for arr, off in extras:
        in_specs.append(pl.BlockSpec((tm, tn), lambda j, m, n, k, off=off: (m, off + j * nb + n)))
        args.append(arr)
    in_specs += [_ANY] * len(deps)
    args += list(deps)
    out_shape = [jax.ShapeDtypeStruct((M, J * Nj), d) for d in out_dtypes]
    out_specs = [tile] * len(out_dtypes)
    if tile_sums:
        out_shape.append(jax.ShapeDtypeStruct((M // tm * 8, J * nb * 128), F32))
        out_specs.append(pl.BlockSpec((8, 128), lambda j, m, n, k: (m, j * nb + n)))
    out = pl.pallas_call(
        body, name=name,
        out_shape=tuple(out_shape),
        grid=(J, M // tm, nb, nk),
        in_specs=in_specs,
        out_specs=tuple(out_specs),
        scratch_shapes=[pltpu.VMEM((tm, tn), F32)],
        compiler_params=_params(("parallel", "parallel", "parallel", "arbitrary")),
    )(*args)
    return out[0] if n_out == 1 else out


def _mm_nt(a, b, out_dtype, name, deps=(), extras=(), epilogue=None, tm=1024, tp=1024, tq=2048):
    M, Q = a.shape
    J, P, Qj = b.shape
    assert Q == J * Qj
    tm, tp = _tile(M, tm), _tile(P, tp)
    jb = max(1, min(J, tq // Qj))
    assert J % jb == 0
    tq = _tile(Qj, tq)
    nq = Qj // tq

    def body(*refs):
        a_ref, b_ref = refs[:2]
        extra_refs = refs[2:2 + len(extras)]
        o_ref, acc = refs[2 + len(extras) + len(deps):]

        def finish(r):
            if epilogue is not None:
                r = epilogue(r, *[e[...] for e in extra_refs])
            o_ref[...] = r.astype(out_dtype)

        part = _dot(a_ref[:, 0:tq], b_ref[0], NT)
        for jj in range(1, jb):
            part = part + _dot(a_ref[:, jj * tq:(jj + 1) * tq], b_ref[jj], NT)
        _accumulate(acc, part, pl.program_id(2) * nq + pl.program_id(3), J // jb * nq, finish)

    tile = pl.BlockSpec((tm, tp), lambda m, p, j, q: (m, p))
    return pl.pallas_call(
        body, name=name,
        out_shape=jax.ShapeDtypeStruct((M, P), out_dtype),
        grid=(M // tm, P // tp, J // jb, nq),
        in_specs=[pl.BlockSpec((tm, jb * tq), lambda m, p, j, q: (m, j * nq + q)),
                  pl.BlockSpec((jb, tp, tq), lambda m, p, j, q: (j, p, q))] + [tile] * len(extras) + [_ANY] * len(deps),
        out_specs=tile,
        scratch_shapes=[pltpu.VMEM((tm, tp), F32)],
        compiler_params=_params(("parallel", "parallel", "arbitrary", "arbitrary")),
    )(a, b, *extras, *deps)


def _mm_tn(a, b, J, out_dtype, name, tm=1024, tn=1024, tr=FULL_K):
    R, M = a.shape
    R2, N = b.shape
    assert R == R2 and N % J == 0
    Nj = N // J
    tm, tn, tr = _tile(M, tm), _tile(Nj, tn), _tile(R, tr)
    nb, nr = Nj // tn, R // tr

    def body(a_ref, b_ref, o_ref, acc):
        def finish(r):
            o_ref[...] = r.astype(out_dtype)

        _accumulate(acc, _dot(a_ref[...], b_ref[...], TN), pl.program_id(3), nr, finish)

    return pl.pallas_call(
        body, name=name,
        out_shape=jax.ShapeDtypeStruct((J, M, Nj), out_dtype),
        grid=(J, M // tm, nb, nr),
        in_specs=[pl.BlockSpec((tr, tm), lambda j, m, n, r: (r, m)),
                  pl.BlockSpec((tr, tn), lambda j, m, n, r: (r, j * nb + n))],
        out_specs=pl.BlockSpec((None, tm, tn), lambda j, m, n, r: (j, m, n)),
        scratch_shapes=[pltpu.VMEM((tm, tn), F32)],
        compiler_params=_params(("parallel", "parallel", "parallel", "arbitrary")),
    )(a, b)


def _ew(fn, ins, vecs, out_dtypes, n_acc, name, tr=256, tc=1024):
    T, C = ins[0].shape
    tr, tc = _tile(T, tr), _tile(C, tc)
    n_in, n_vec, n_out = len(ins), len(vecs), len(out_dtypes)

    def body(*refs):
        in_refs = refs[:n_in + n_vec]
        out_refs = refs[n_in + n_vec:n_in + n_vec + n_out]
        acc_refs = refs[n_in + n_vec + n_out:]
        res = fn(*[r[...] for r in in_refs])
        for o_ref, v in zip(out_refs, res[:n_out]):
            o_ref[...] = v.astype(o_ref.dtype)
        if n_acc:
            r = pl.program_id(1)

            @pl.when(r == 0)
            def _():
                for a_ref in acc_refs:
                    a_ref[...] = jnp.zeros_like(a_ref)

            for a_ref, v in zip(acc_refs, res[n_out:]):
                a_ref[...] += v

    blk = pl.BlockSpec((tr, tc), lambda c, r: (r, c))
    vec = pl.BlockSpec((1, tc), lambda c, r: (0, c))
    out = pl.pallas_call(
        body, name=name,
        out_shape=tuple([jax.ShapeDtypeStruct((T, C), d) for d in out_dtypes]
                        + [jax.ShapeDtypeStruct((1, C), F32)] * n_acc),
        grid=(C // tc, T // tr),
        in_specs=[blk] * n_in + [vec] * n_vec,
        out_specs=tuple([blk] * n_out + [vec] * n_acc),
        compiler_params=_params(("parallel", "arbitrary")),
    )(*ins, *vecs)
    return out


def _rmsnorm_fwd(x, g, name, deps=(), tr=128):
    T, D = x.shape
    tr = _tile(T, tr)

    def body(*refs):
        x_ref, g_ref, h_ref = refs[0], refs[1], refs[2 + len(deps)]
        xv = x_ref[...]
        r = lax.rsqrt(jnp.mean(xv * xv, axis=-1, keepdims=True) + RMS_EPS)
        h_ref[...] = ((xv * r) * g_ref[...]).astype(BF16)

    return pl.pallas_call(
        body, name=name,
        out_shape=jax.ShapeDtypeStruct((T, D), BF16),
        grid=(T // tr,),
        in_specs=[pl.BlockSpec((tr, D), lambda i: (i, 0)), pl.BlockSpec((1, D), lambda i: (0, 0))] + [_ANY] * len(deps),
        out_specs=pl.BlockSpec((tr, D), lambda i: (i, 0)),
        compiler_params=_params(("parallel",)),
    )(x, g, *deps)


def _rmsnorm_bwd(x, dh, dres, g, name, tr=128):
    T, D = x.shape
    tr = _tile(T, tr)

    def body(x_ref, dh_ref, dres_ref, g_ref, dx_ref, dxb_ref, dg_ref):
        xv = x_ref[...]
        r = lax.rsqrt(jnp.mean(xv * xv, axis=-1, keepdims=True) + RMS_EPS)
        xn = xv * r
        dhv = dh_ref[...].astype(F32)
        dxn = dhv * g_ref[...]
        dx = dres_ref[...] + r * (dxn - xn * jnp.mean(dxn * xn, axis=-1, keepdims=True))
        dx_ref[...] = dx
        dxb_ref[...] = dx.astype(BF16)

        @pl.when(pl.program_id(0) == 0)
        def _():
            dg_ref[...] = jnp.zeros_like(dg_ref)

        dg_ref[...] += jnp.sum(dhv * xn, axis=0, keepdims=True)

    blk = pl.BlockSpec((tr, D), lambda i: (i, 0))
    vec = pl.BlockSpec((1, D), lambda i: (0, 0))
    return pl.pallas_call(
        body, name=name,
        out_shape=(jax.ShapeDtypeStruct((T, D), F32), jax.ShapeDtypeStruct((T, D), BF16),
                   jax.ShapeDtypeStruct((1, D), F32)),
        grid=(T // tr,),
        in_specs=[blk, blk, blk, vec],
        out_specs=(blk, blk, vec),
        compiler_params=_params(("arbitrary",)),
    )(x, dh, dres, g)


def _total(x, scale, name):
    def body(x_ref, o_ref):
        o_ref[...] = jnp.sum(jnp.sum(x_ref[...], axis=1, keepdims=True), axis=0, keepdims=True) * scale

    return pl.pallas_call(
        body, name=name, out_shape=jax.ShapeDtypeStruct((1, 1), F32),
        compiler_params=pltpu.CompilerParams(vmem_limit_bytes=VMEM_LIMIT),
    )(x)


def _gelu(x):
    t = jnp.tanh(GELU_C * (x + 0.044715 * (x * x * x)))
    return x * (0.5 * (1.0 + t)), t


def _adamw(parts, w, m, v, name, tr=256, tc=1024):
    n, R, C = parts.shape
    tr, tc = _tile(R, tr), _tile(C, tc)
    c1 = 1.0 - ADAM_B1 ** ADAM_STEP
    c2 = 1.0 - ADAM_B2 ** ADAM_STEP

    def body(p_ref, w_ref, m_ref, v_ref, g_out, d_out, m_out, v_out):
        g = p_ref[0].astype(F32)
        for k in range(1, n):
            g = g + p_ref[k].astype(F32)
        mn = ADAM_B1 * m_ref[...] + (1.0 - ADAM_B1) * g
        vn = ADAM_B2 * v_ref[...] + (1.0 - ADAM_B2) * (g * g)
        m_hat = mn / c1
        v_hat = vn / c2
        g_out[...] = g
        d_out[...] = -ADAM_LR * (m_hat / (jnp.sqrt(v_hat) + ADAM_EPS) + ADAM_WD * w_ref[...])
        m_out[...] = mn
        v_out[...] = vn

    blk = pl.BlockSpec((tr, tc), lambda r, c: (r, c))
    return pl.pallas_call(
        body, name=name,
        out_shape=tuple([jax.ShapeDtypeStruct((R, C), F32)] * 4),
        grid=(R // tr, C // tc),
        in_specs=[pl.BlockSpec((n, tr, tc), lambda r, c: (0, r, c)), blk, blk, blk],
        out_specs=(blk, blk, blk, blk),
        compiler_params=_params(("parallel", "parallel")),
    )(parts, w, m, v)


def _sum_parts(parts, name):
    n, R, C = parts.shape

    def body(p_ref, o_ref):
        g = p_ref[0].astype(F32)
        for k in range(1, n):
            g = g + p_ref[k].astype(F32)
        o_ref[...] = g

    return pl.pallas_call(
        body, name=name,
        out_shape=jax.ShapeDtypeStruct((R, C), F32),
        compiler_params=pltpu.CompilerParams(vmem_limit_bytes=VMEM_LIMIT),
    )(parts)


def _head_norm(xv):
    xv = xv.astype(F32)
    r = lax.rsqrt(jnp.mean(xv * xv, axis=-1, keepdims=True) + RMS_EPS)
    return xv * r, r


FWD_HEADS = 4
BWD_HEADS = 2


def _attn_specs(S, H, HP):
    def spec(part):
        return pl.BlockSpec((S, HP * HEAD_DIM), lambda b, h, qi: (b, part * (H // HP) + h))
    return spec


def _lanes(hh):
    return slice(hh * HEAD_DIM, (hh + 1) * HEAD_DIM)


def _attn_fwd(proj, q_g, k_g, B, S, H, name):
    TQ = _tile(S, 256)
    nq = S // TQ
    scale = 1.0 / math.sqrt(HEAD_DIM)
    HP = FWD_HEADS
    ATT_W = HP * HEAD_DIM
    heads = range(HP)

    def body(q_ref, k_ref, v_ref, gate_ref, qg_ref, kg_ref, og_ref, o_ref, w_ref, sg_ref, qn_s, kn_s):
        @pl.when(pl.program_id(2) == 0)
        def _():
            for hh in heads:
                qn_s[:, _lanes(hh)] = (_head_norm(q_ref[:, _lanes(hh)])[0] * qg_ref[...]).astype(BF16)
                kn_s[:, _lanes(hh)] = (_head_norm(k_ref[:, _lanes(hh)])[0] * kg_ref[...]).astype(BF16)

        row = lax.broadcasted_iota(jnp.int32, (TQ, TQ), 0)
        col = lax.broadcasted_iota(jnp.int32, (TQ, TQ), 1)
        later = _ones_where(row > col)
        causal = col < row

        def q_block(qi):
            q0 = pl.multiple_of(qi * TQ, TQ)

            def both(ki, state, diag):
                k0 = pl.multiple_of(ki * TQ, TQ)
                z = [_dot(qn_s[pl.ds(q0, TQ), _lanes(hh)], kn_s[pl.ds(k0, TQ), _lanes(hh)], NT) * scale
                     for hh in heads]
                ls = [_log_sigmoid(zz) for zz in z]
                l1m = [a - zz for a, zz in zip(ls, z)]
                if diag:
                    l1m = [jnp.where(causal, a, 0.0) for a in l1m]
                suffix = [_dot(l1m[hh].astype(BF16), later, NN) + state[hh][0] for hh in heads]
                w = [jnp.exp(a + sfx) for a, sfx in zip(ls, suffix)]
                if diag:
                    w = [jnp.where(causal, a, 0.0) for a in w]
                wb = [a.astype(BF16) for a in w]
                for hh in heads:
                    w_ref[hh, ki] = wb[hh]
                    sg_ref[hh, ki] = jnp.exp(ls[hh]).astype(BF16)
                acc = [state[hh][1] + _dot(wb[hh], v_ref[pl.ds(k0, TQ), _lanes(hh)], NN) for hh in heads]
                return tuple((state[hh][0] + jnp.sum(l1m[hh], axis=1, keepdims=True), acc[hh]) for hh in heads)

            zero = (jnp.zeros((TQ, 1), F32), jnp.zeros((TQ, HEAD_DIM), F32))
            state = both(qi, (zero,) * HP, True)
            state = lax.fori_loop(0, qi, lambda i, st: both(qi - 1 - i, st, False), state)
            for hh in heads:
                acc = state[hh][1]
                o_ref[:, _lanes(hh)] = acc.astype(BF16)
                gate = gate_ref[pl.ds(q0, TQ), _lanes(hh)].astype(F32)
                og_ref[:, _lanes(hh)] = (acc * (gate * _sigmoid(gate))).astype(BF16)

        q_block(pl.program_id(2))

    spec = _attn_specs(S, H, HP)
    vec = pl.BlockSpec((1, HEAD_DIM), lambda b, h, qi: (0, 0))
    out = pl.BlockSpec((TQ, ATT_W), lambda b, h, qi: (b * nq + qi, h))
    kept = pl.BlockSpec((None, HP, None, nq, TQ, TQ), lambda b, h, qi: (b, h, qi, 0, 0, 0))
    kept_shape = jax.ShapeDtypeStruct((B, H, nq, nq, TQ, TQ), BF16)
    return pl.pallas_call(
        body, name=name,
        out_shape=(jax.ShapeDtypeStruct((B * S, H * HEAD_DIM), BF16),) * 2 + (kept_shape,) * 2,
        grid=(B, H // HP, nq),
        in_specs=[spec(0), spec(1), spec(2), spec(3), vec, vec],
        out_specs=(out, out, kept, kept),
        scratch_shapes=[pltpu.VMEM((S, ATT_W), BF16)] * 2,
        compiler_params=_params(("parallel", "parallel", "arbitrary")),
    )(proj, proj, proj, proj, q_g, k_g)


def _attn_bwd(proj, o, dog, w_kept, sg_kept, q_g, k_g, B, S, H, name):
    TQ = _tile(S, 256)
    nq = S // TQ
    scale = 1.0 / math.sqrt(HEAD_DIM)
    HP = BWD_HEADS
    ATT_W = HP * HEAD_DIM
    heads = range(HP)

    def body(q_ref, k_ref, v_ref, gate_ref, o_ref, dog_ref, w_ref, sg_ref, qg_ref, kg_ref,
             dq_ref, dk_ref, dv_ref, dgate_ref, dqg_ref, dkg_ref,
             qn_s, kn_s, do_s, dkn_s, dv_s):
        qi = pl.program_id(2)

        @pl.when((pl.program_id(0) == 0) & (pl.program_id(1) == 0) & (qi == 0))
        def _():
            dqg_ref[...] = jnp.zeros_like(dqg_ref)
            dkg_ref[...] = jnp.zeros_like(dkg_ref)

        @pl.when(qi == 0)
        def _():
            for hh in heads:
                qn_s[:, _lanes(hh)] = (_head_norm(q_ref[:, _lanes(hh)])[0] * qg_ref[...]).astype(BF16)
                kn_s[:, _lanes(hh)] = (_head_norm(k_ref[:, _lanes(hh)])[0] * kg_ref[...]).astype(BF16)
            gate = gate_ref[...].astype(F32)
            sg = _sigmoid(gate)
            dog_v = dog_ref[...].astype(F32)
            do_s[...] = (dog_v * (gate * sg)).astype(BF16)
            dgate_ref[...] = (dog_v * o_ref[...].astype(F32) * (sg * (1.0 + gate * (1.0 - sg)))).astype(BF16)
            dkn_s[...] = jnp.zeros_like(dkn_s)
            dv_s[...] = jnp.zeros_like(dv_s)

        row = lax.broadcasted_iota(jnp.int32, (TQ, TQ), 0)
        col = lax.broadcasted_iota(jnp.int32, (TQ, TQ), 1)
        earlier = _ones_where(row < col)
        causal = col < row

        def norm_bwd(xv, g_ref, dn, dg_ref):
            xh, r = _head_norm(xv)
            dg_ref[...] += jnp.sum(dn * xh, axis=0, keepdims=True)
            dxh = dn * g_ref[...]
            return (r * (dxh - xh * jnp.mean(dxh * xh, axis=-1, keepdims=True))).astype(BF16)

        def q_block():
            q0 = pl.multiple_of(qi * TQ, TQ)

            def grads_both(ki, state, diag):
                k0 = pl.multiple_of(ki * TQ, TQ)
                qb = [qn_s[pl.ds(q0, TQ), _lanes(hh)] for hh in heads]
                dob = [do_s[pl.ds(q0, TQ), _lanes(hh)] for hh in heads]
                wb = [w_ref[hh, ki] for hh in heads]
                da = [_dot(dob[hh], v_ref[pl.ds(k0, TQ), _lanes(hh)], NT) * wb[hh].astype(F32) for hh in heads]
                for hh in heads:
                    dv_s[pl.ds(k0, TQ), _lanes(hh)] += _dot(wb[hh], dob[hh], TN)
                prefix = [_dot(da[hh].astype(BF16), earlier, NN) + state[hh][0] for hh in heads]
                dzb = []
                for hh in heads:
                    sgz = sg_ref[hh, ki].astype(F32)
                    dz = da[hh] * (1.0 - sgz) - sgz * prefix[hh]
                    if diag:
                        dz = jnp.where(causal, dz, 0.0)
                    dzb.append((dz * scale).astype(BF16))
                dq = [state[hh][1] + _dot(dzb[hh], kn_s[pl.ds(k0, TQ), _lanes(hh)], NN) for hh in heads]
                for hh in heads:
                    dkn_s[pl.ds(k0, TQ), _lanes(hh)] += _dot(dzb[hh], qb[hh], TN)
                return tuple((state[hh][0] + jnp.sum(da[hh], axis=1, keepdims=True), dq[hh]) for hh in heads)

            zero = (jnp.zeros((TQ, 1), F32), jnp.zeros((TQ, HEAD_DIM), F32))
            state = lax.fori_loop(0, qi, lambda i, st: grads_both(i, st, False), (zero,) * HP)
            state = grads_both(qi, state, True)
            for hh in heads:
                dq_ref[:, _lanes(hh)] = norm_bwd(q_ref[pl.ds(q0, TQ), _lanes(hh)], qg_ref, state[hh][1], dqg_ref)

        q_block()

        @pl.when(qi == nq - 1)
        def _():
            for hh in heads:
                dk_ref[:, _lanes(hh)] = norm_bwd(k_ref[:, _lanes(hh)], kg_ref, dkn_s[:, _lanes(hh)], dkg_ref)
            dv_ref[...] = dv_s[...].astype(BF16)

    spec = _attn_specs(S, H, HP)
    vec = pl.BlockSpec((1, HEAD_DIM), lambda b, h, qi: (0, 0))
    blk = pl.BlockSpec((S, ATT_W), lambda b, h, qi: (b, h))
    rows = pl.BlockSpec((TQ, ATT_W), lambda b, h, qi: (b * nq + qi, h))
    kept = pl.BlockSpec((None, HP, None, nq, TQ, TQ), lambda b, h, qi: (b, h, qi, 0, 0, 0))
    big = jax.ShapeDtypeStruct((B * S, H * HEAD_DIM), BF16)
    small = jax.ShapeDtypeStruct((1, HEAD_DIM), F32)
    return pl.pallas_call(
        body, name=name,
        out_shape=(big, big, big, big, small, small),
        grid=(B, H // HP, nq),
        in_specs=[spec(0), spec(1), spec(2), spec(3), blk, blk, kept, kept, vec, vec],
        out_specs=(rows, blk, blk, blk, vec, vec),
        scratch_shapes=[pltpu.VMEM((S, ATT_W), BF16)] * 3 + [pltpu.VMEM((S, ATT_W), F32)] * 2,
        compiler_params=_params(("arbitrary", "arbitrary", "arbitrary")),
    )(proj, proj, proj, proj, o, dog, w_kept, sg_kept, q_g, k_g)


HALF = GROUPS_PER_BLOCK * STATE


def _cmul(ar, ai, br, bi):
    return ar * br - ai * bi, ar * bi + ai * br


def _cpow(ar, ai, n):
    rr = ri = None
    while n:
        if n & 1:
            rr, ri = (ar, ai) if rr is None else _cmul(rr, ri, ar, ai)
        n >>= 1
        if n:
            ar, ai = _cmul(ar, ai, ar, ai)
    return rr, ri


def _segment_carry(er, ei, lr, li, seg_len, segs_per_seq, reverse):
    Lr, Li = _cpow(lr, li, seg_len)
    pos = lax.broadcasted_iota(jnp.int32, er.shape, 0) % segs_per_seq
    outr = jnp.zeros_like(er)
    outi = jnp.zeros_like(ei)
    pr = pi = None
    for d in range(1, segs_per_seq):
        shift = (SUBLANES - d) if reverse else d
        sr = pltpu.roll(er, shift, 0)
        si = pltpu.roll(ei, shift, 0)
        ok = (pos + d < segs_per_seq) if reverse else (pos >= d)
        sr = jnp.where(ok, sr, 0.0)
        si = jnp.where(ok, si, 0.0)
        if pr is not None:
            sr, si = _cmul(sr, si, pr, pi)
        outr = outr + sr
        outi = outi + si
        pr, pi = (Lr, Li) if pr is None else _cmul(pr, pi, Lr, Li)
    return outr, outi


def _s5_sizes(T, B):
    assert SUBLANES % B == 0
    segs_per_seq = SUBLANES // B
    n_steps = T // SUBLANES
    cj = _tile(n_steps, 64)
    return segs_per_seq, n_steps, cj


def _s5_fwd(u_p, wb, wc, lam, dvec, B, name):
    T, C = u_p.shape
    nb = C // 128
    segs_per_seq, n_steps, cj = _s5_sizes(T, B)
    n_chunks = n_steps // cj
    rows = cj * SUBLANES

    def body(u_ref, wb_ref, wc_ref, lam_ref, d_ref, y_ref, hin_ref, bu_s, h_s):
        lr = jnp.broadcast_to(lam_ref[:, :HALF], (SUBLANES, HALF))
        li = jnp.broadcast_to(lam_ref[:, HALF:], (SUBLANES, HALF))

        def scan_chunk(c, hr, hi, store):
            r0 = pl.multiple_of(c * rows, rows)
            if not store:
                bu_s[pl.ds(r0, rows), :] = _dot(u_ref[pl.ds(r0, rows), :], wb_ref[...], NN)

            def step(j, carry):
                hr, hi = carry
                o = pl.multiple_of(j * SUBLANES, SUBLANES)
                at = pl.multiple_of(r0 + o, SUBLANES)
                nr = lr * hr - li * hi + bu_s[pl.ds(at, SUBLANES), :HALF]
                ni = lr * hi + li * hr + bu_s[pl.ds(at, SUBLANES), HALF:]
                if store:
                    h_s[pl.ds(o, SUBLANES), :HALF] = nr
                    h_s[pl.ds(o, SUBLANES), HALF:] = ni
                return nr, ni

            hr, hi = lax.fori_loop(0, cj, step, (hr, hi))
            if store:
                uv = u_ref[pl.ds(r0, rows), :].astype(F32)
                y_ref[pl.ds(r0, rows), :] = _dot(h_s[...].astype(BF16), wc_ref[...], NN) + d_ref[...] * uv
            return hr, hi

        zero = jnp.zeros((SUBLANES, HALF), F32)
        er, ei = lax.fori_loop(0, n_chunks, lambda c, h: scan_chunk(c, h[0], h[1], False), (zero, zero))
        h0r, h0i = _segment_carry(er, ei, lr, li, n_steps, segs_per_seq, False)
        hin_ref[:, :HALF] = h0r
        hin_ref[:, HALF:] = h0i
        lax.fori_loop(0, n_chunks, lambda c, h: scan_chunk(c, h[0], h[1], True), (h0r, h0i))

    return pl.pallas_call(
        body, name=name,
        out_shape=(jax.ShapeDtypeStruct((T, C), F32), jax.ShapeDtypeStruct((nb, SUBLANES, 2 * HALF), F32)),
        grid=(nb,),
        in_specs=[pl.BlockSpec((T, 128), lambda g: (0, g)),
                  pl.BlockSpec((None, 128, 2 * HALF), lambda g: (g, 0, 0)),
                  pl.BlockSpec((None, 2 * HALF, 128), lambda g: (g, 0, 0)),
                  pl.BlockSpec((None, 1, 2 * HALF), lambda g: (g, 0, 0)),
                  pl.BlockSpec((1, 128), lambda g: (0, g))],
        out_specs=(pl.BlockSpec((T, 128), lambda g: (0, g)),
                   pl.BlockSpec((None, SUBLANES, 2 * HALF), lambda g: (g, 0, 0))),
        scratch_shapes=[pltpu.VMEM((T, 2 * HALF), F32), pltpu.VMEM((rows, 2 * HALF), F32)],
        compiler_params=_params(("parallel",)),
    )(u_p, wb, wc, lam, dvec)


def _s5_bwd(u_p, dy_p, wb, wbt, wc, wct, lam, dvec, h_in, B, name):
    T, C = u_p.shape
    nb = C // 128
    segs_per_seq, n_steps, cj = _s5_sizes(T, B)
    n_chunks = n_steps // cj
    rows = cj * SUBLANES

    def body(u_ref, dy_ref, wb_ref, wbt_ref, wc_ref, wct_ref, lam_ref, d_ref, hin_ref,
             du_ref, dwb_ref, dwc_ref, dlam_ref, dd_ref, h_all, dh_all, x_s, g_s):
        lr = jnp.broadcast_to(lam_ref[:, :HALF], (SUBLANES, HALF))
        li = jnp.broadcast_to(lam_ref[:, HALF:], (SUBLANES, HALF))
        zero = jnp.zeros((SUBLANES, HALF), F32)

        h_all[pl.ds(0, SUBLANES), :] = hin_ref[...]

        def fwd_chunk(c, carry):
            r0 = pl.multiple_of(c * rows, rows)
            x_s[...] = _dot(u_ref[pl.ds(r0, rows), :], wb_ref[...], NN)

            def step(j, carry):
                hr, hi = carry
                o = pl.multiple_of(j * SUBLANES, SUBLANES)
                nr = lr * hr - li * hi + x_s[pl.ds(o, SUBLANES), :HALF]
                ni = lr * hi + li * hr + x_s[pl.ds(o, SUBLANES), HALF:]
                late = pl.multiple_of(r0 + o + SUBLANES, SUBLANES)
                h_all[pl.ds(late, SUBLANES), :HALF] = nr
                h_all[pl.ds(late, SUBLANES), HALF:] = ni
                return nr, ni

            return lax.fori_loop(0, cj, step, carry)

        lax.fori_loop(0, n_chunks, fwd_chunk, (hin_ref[:, :HALF], hin_ref[:, HALF:]))

        def bwd_chunk(i, carry, store):
            c = n_chunks - 1 - i
            r0 = pl.multiple_of(c * rows, rows)
            dyv = dy_ref[pl.ds(r0, rows), :]
            if not store:
                dh_all[pl.ds(r0, rows), :] = _dot(dyv, wct_ref[...], NN)

            def step(jj, carry):
                ar, ai, accr, acci = carry
                j = cj - 1 - jj
                o = pl.multiple_of(j * SUBLANES, SUBLANES)
                prev = pl.multiple_of(r0 + o, SUBLANES)
                nr = lr * ar + li * ai + dh_all[pl.ds(prev, SUBLANES), :HALF]
                ni = lr * ai - li * ar + dh_all[pl.ds(prev, SUBLANES), HALF:]
                if store:
                    g_s[pl.ds(o, SUBLANES), :HALF] = nr
                    g_s[pl.ds(o, SUBLANES), HALF:] = ni
                    pr = h_all[pl.ds(prev, SUBLANES), :HALF]
                    pi = h_all[pl.ds(prev, SUBLANES), HALF:]
                    accr = accr + nr * pr + ni * pi
                    acci = acci + ni * pr - nr * pi
                return nr, ni, accr, acci

            carry = lax.fori_loop(0, cj, step, carry)
            if store:
                gb = g_s[...].astype(BF16)
                uv = u_ref[pl.ds(r0, rows), :]
                dyf = dyv.astype(F32)
                du_ref[pl.ds(r0, rows), :] = (_dot(gb, wbt_ref[...], NN) + d_ref[...] * dyf).astype(BF16)
                dwb_ref[...] += _dot(uv, gb, TN)
                hb = h_all[pl.ds(pl.multiple_of(r0 + SUBLANES, SUBLANES), rows), :].astype(BF16)
                dwc_ref[...] += _dot(hb, dyv, TN)
                dd_ref[...] += jnp.sum(dyf * uv.astype(F32), axis=0, keepdims=True)
            return carry

        er, ei, _, _ = lax.fori_loop(0, n_chunks, lambda i, c: bwd_chunk(i, c, False), (zero, zero, zero, zero))
        a0r, a0i = _segment_carry(er, ei, lr, -li, n_steps, segs_per_seq, True)
        dwb_ref[...] = jnp.zeros_like(dwb_ref)
        dwc_ref[...] = jnp.zeros_like(dwc_ref)
        dd_ref[...] = jnp.zeros_like(dd_ref)
        _, _, accr, acci = lax.fori_loop(0, n_chunks, lambda i, c: bwd_chunk(i, c, True), (a0r, a0i, zero, zero))
        dlam_ref[:, :HALF] = jnp.sum(accr, axis=0, keepdims=True)
        dlam_ref[:, HALF:] = jnp.sum(acci, axis=0, keepdims=True)

    col = pl.BlockSpec((T, 128), lambda g: (0, g))
    vec = pl.BlockSpec((1, 128), lambda g: (0, g))

    def per_block(*shape):
        return pl.BlockSpec((None,) + shape, lambda g: (g, 0, 0))

    return pl.pallas_call(
        body, name=name,
        out_shape=(jax.ShapeDtypeStruct((T, C), BF16),
                   jax.ShapeDtypeStruct((nb, 128, 2 * HALF), F32),
                   jax.ShapeDtypeStruct((nb, 2 * HALF, 128), F32),
                   jax.ShapeDtypeStruct((nb, 1, 2 * HALF), F32),
                   jax.ShapeDtypeStruct((1, C), F32)),
        grid=(nb,),
        in_specs=[col, col, per_block(128, 2 * HALF), per_block(2 * HALF, 128), per_block(2 * HALF, 128),
                  per_block(128, 2 * HALF), per_block(1, 2 * HALF), vec, per_block(SUBLANES, 2 * HALF)],
        out_specs=(col, per_block(128, 2 * HALF), per_block(2 * HALF, 128), per_block(1, 2 * HALF), vec),
        scratch_shapes=[pltpu.VMEM((T + SUBLANES, 2 * HALF), F32), pltpu.VMEM((T, 2 * HALF), F32),
                        pltpu.VMEM((rows, 2 * HALF), F32), pltpu.VMEM((rows, 2 * HALF), F32)],
        compiler_params=_params(("parallel",)),
    )(u_p, dy_p, wb, wbt, wc, wct, lam, dvec, h_in)


def _discretize(a_re, a_im, log_dt, b_re, b_im):
    dt = jnp.exp(log_dt)[:, None]
    mag = jnp.exp(a_re * dt)
    lam_re = mag * jnp.cos(a_im * dt)
    lam_im = mag * jnp.sin(a_im * dt)
    den = a_re * a_re + a_im * a_im
    f_re = ((lam_re - 1.0) * a_re + lam_im * a_im) / den
    f_im = (lam_im * a_re - (lam_re - 1.0) * a_im) / den
    bb_re = f_re[..., None] * b_re - f_im[..., None] * b_im
    bb_im = f_re[..., None] * b_im + f_im[..., None] * b_re
    return lam_re, lam_im, bb_re, bb_im


def _block_diag_in(bb_re, bb_im):
    eye = jnp.eye(GROUPS_PER_BLOCK, dtype=F32)

    def one(bb):
        t = bb.reshape(-1, GROUPS_PER_BLOCK, STATE, GROUP)
        return jnp.einsum('gapi,ab->gaibp', t, eye).reshape(-1, 128, HALF)

    return jnp.concatenate([one(bb_re), one(bb_im)], axis=-1)


def _block_diag_in_grad(dwb):
    eye = jnp.eye(GROUPS_PER_BLOCK, dtype=F32)

    def one(d):
        t = d.reshape(-1, GROUPS_PER_BLOCK, GROUP, GROUPS_PER_BLOCK, STATE)
        return jnp.einsum('gaibp,ab->gapi', t, eye).reshape(-1, STATE, GROUP)

    return one(dwb[..., :HALF]), one(dwb[..., HALF:])


def _block_diag_out(c_re, c_im):
    eye = jnp.eye(GROUPS_PER_BLOCK, dtype=F32)

    def one(cc):
        t = cc.reshape(-1, GROUPS_PER_BLOCK, GROUP, STATE)
        return jnp.einsum('gaip,ab->gbpai', t, eye).reshape(-1, HALF, 128)

    return jnp.concatenate([one(c_re), -one(c_im)], axis=1)


def _block_diag_out_grad(dwc):
    eye = jnp.eye(GROUPS_PER_BLOCK, dtype=F32)

    def one(d):
        t = d.reshape(-1, GROUPS_PER_BLOCK, STATE, GROUPS_PER_BLOCK, GROUP)
        return jnp.einsum('gbpai,ab->gaip', t, eye).reshape(-1, GROUP, STATE)

    return one(dwc[:, :HALF]), -one(dwc[:, HALF:])


def _pack(parts):
    return jnp.concatenate([p.reshape(-1, PACK_W) for p in parts], axis=0)


def _unpack(buf, shapes):
    lead = buf.shape[:-2]
    out, r = [], 0
    for s in shapes:
        n = math.prod(s) // PACK_W
        out.append(buf[..., r:r + n, :].reshape(lead + tuple(s)))
        r += n
    return out


def _permute_rows(a, n_steps):
    T, C = a.shape
    return a.reshape(SUBLANES, n_steps, C).transpose(1, 0, 2).reshape(T, C)


def _unpermute_rows(a, n_steps):
    T, C = a.shape
    return a.reshape(n_steps, SUBLANES, C).transpose(1, 0, 2).reshape(T, C)


def kernel(x, norm_g, attn_w_in, attn_q_g, attn_k_g, attn_w_out, ssm_w_in, ssm_A_re, ssm_A_im, ssm_log_dt, ssm_B_re, ssm_B_im, ssm_C_re, ssm_C_im, ssm_D, ssm_glu_w, ssm_glu_b, ssm_w_out, loss_target, m_norm_g, m_attn_w_in, m_attn_q_g, m_attn_k_g, m_attn_w_out, m_ssm_w_in, m_ssm_A_re, m_ssm_A_im, m_ssm_log_dt, m_ssm_B_re, m_ssm_B_im, m_ssm_C_re, m_ssm_C_im, m_ssm_D, m_ssm_glu_w, m_ssm_glu_b, m_ssm_w_out, v_norm_g, v_attn_w_in, v_attn_q_g, v_attn_k_g, v_attn_w_out, v_ssm_w_in, v_ssm_A_re, v_ssm_A_im, v_ssm_log_dt, v_ssm_B_re, v_ssm_B_im, v_ssm_C_re, v_ssm_C_im, v_ssm_D, v_ssm_glu_w, v_ssm_glu_b, v_ssm_w_out):
    B, S, D = x.shape
    T = B * S
    H = D // HEAD_DIM
    G_loc = ssm_A_re.shape[1]
    G = G_loc * N_DEV
    n_steps = T // SUBLANES
    me, chip = _my_index(), _my_chip()
    xf = x.reshape(T, D)
    target = loss_target.reshape(T, D)

    small_shapes = [(G_loc, STATE), (G_loc, STATE), (G_loc, STATE, GROUP), (G_loc, STATE, GROUP),
                    (G_loc, GROUP, STATE), (G_loc, GROUP, STATE), (G_loc * GROUP,), (G_loc * GROUP,)]
    disc_in = (ssm_A_re[0], ssm_A_im[0], ssm_log_dt[0], ssm_B_re[0], ssm_B_im[0])
    (lam_re, lam_im, bb_re, bb_im), disc_vjp = jax.vjp(_discretize, *disc_in)
    small = _pack([lam_re, lam_im, bb_re, bb_im, ssm_C_re[0], ssm_C_im[0], ssm_D[0], ssm_glu_b[0]])
    small_all = _exchange(small, False, "gather_small")
    c_in0 = _place(attn_w_in[0], me, N_DEV, BF16, "cast_attn_w_in")
    near = _exchange_start(c_in0, None, "near", "gather_attn_w_in_near_start", deps=(small_all,))
    beside = (near[3],)
    small_all = small_all + near[3][0, 0]
    c_out0 = _place(attn_w_out[0], me, N_DEV, BF16, "cast_attn_w_out", deps=beside)
    c_in1 = _place(ssm_w_in[0], me, N_DEV, BF16, "cast_ssm_w_in", deps=beside)
    c_glu = _place(ssm_glu_w[0], me, N_DEV, BF16, "cast_ssm_glu_w", deps=beside)
    c_out1 = _place(ssm_w_out[0], me, N_DEV, BF16, "cast_ssm_w_out", deps=beside)

    lam_re_a, lam_im_a, bb_re_a, bb_im_a, c_re_a, c_im_a, d_a, glu_b_a = [
        t.reshape((G,) + t.shape[2:]) if t.ndim > 2 else t.reshape(-1)
        for t in _unpack(small_all, small_shapes)]
    wb = _block_diag_in(bb_re_a, bb_im_a)
    wc = _block_diag_out(c_re_a, c_im_a)
    wb_b, wc_b = wb.astype(BF16), wc.astype(BF16)
    wbt_b, wct_b = wb_b.transpose(0, 2, 1), wc_b.transpose(0, 2, 1)
    lam = jnp.concatenate([lam_re_a.reshape(-1, 1, HALF), lam_im_a.reshape(-1, 1, HALF)], axis=-1)
    d_row = d_a.reshape(1, D)
    glu_b_row = glu_b_a.reshape(1, D)
    g0, g1 = norm_g[0:1], norm_g[1:2]
    q_g, k_g = attn_q_g, attn_k_g

    h0 = _rmsnorm_fwd(xf, g0, "norm0", deps=beside)
    done = (h0, c_out0, c_in1, c_glu, c_out1, wb_b, wbt_b, wc_b, wct_b, lam, d_row, glu_b_row)
    far = _exchange_start(_exchange_wait(near, done, "near", "gather_attn_w_in_near_wait"), None, "far",
                          "gather_attn_w_in_far_start")
    w_in0 = _exchange_wait(far, far[3], "far", "gather_attn_w_in_far_wait")
    s_out0 = _exchange_start(c_out0, None, "gather", "gather_attn_w_out_start", deps=(w_in0,))
    s_in1 = _exchange_start(c_in1, None, "gather", "gather_ssm_w_in_start", deps=(w_in0,))
    proj0 = _mm_nn(h0, w_in0, BF16, "attn_in", deps=(s_out0[3], s_in1[3]), tk=FULL_K)
    og, o, w_kept, sg_kept = _attn_fwd(proj0, q_g, k_g, B, S, H, "attn_fwd")
    w_out0 = _exchange_wait(s_out0, og, "gather", "gather_attn_w_out_wait").reshape(1, D, D)
    s_glu = _exchange_start(c_glu, None, "gather", "gather_ssm_glu_w_start", deps=(w_out0,))
    s_out1 = _exchange_start(c_out1, None, "gather", "gather_ssm_w_out_start", deps=(w_out0,))
    x1 = _mm_nn(og, w_out0, F32, "attn_out", residual=xf, deps=(s_glu[3], s_out1[3]))

    h1 = _rmsnorm_fwd(x1, g1, "norm1")
    w_in1 = _exchange_wait(s_in1, h1, "gather", "gather_ssm_w_in_wait")
    proj1 = _mm_nn(h1, w_in1, BF16, "ssm_in", tk=FULL_K)
    u_p = _permute_rows(proj1[:, :D], n_steps)
    gate1 = proj1[:, D:]
    y_p, h_in = _s5_fwd(u_p, wb_b, wc_b, lam, d_row, B, "s5_fwd")
    y_ssm = _unpermute_rows(y_p, n_steps)
    (yg,) = _ew(lambda a: (_gelu(a)[0],), [y_ssm], [], [BF16], 0, "gelu")
    w_glu = _exchange_wait(s_glu, yg, "gather", "gather_ssm_glu_w_wait").reshape(1, D, D)

    def glu_gate(zz, y, gt):
        gt = gt.astype(F32)
        return zz, _gelu(y)[0] * _sigmoid(zz) * (gt * _sigmoid(gt))

    z, y3 = _mm_nn(yg, w_glu, F32, "glu_in", bias=glu_b_row, extras=[(y_ssm, 0), (proj1, D // _tile(D, 1024))],
                   epilogue=glu_gate, out_dtypes=[F32, BF16], tk=FUSED_TK)
    w_out1 = _exchange_wait(s_out1, y3, "gather", "gather_ssm_w_out_wait").reshape(1, D, D)

    def loss_head(out_tile, tgt):
        err = out_tile - tgt
        dy = err * (1.0 / D)
        return dy, dy, jnp.sum(err * err, axis=0, keepdims=True)

    dout, dout_b, sq = _mm_nn(y3, w_out1, F32, "ssm_out", residual=x1, extras=[(target, 0)], epilogue=loss_head,
                              out_dtypes=[F32, BF16], tile_sums=True, tk=FUSED_TK)
    loss_part = _total(sq, 0.5 / D, "loss")
    loss = lax.psum(loss_part[0, 0], ("x", "y", "c"))

    p_w_out1 = _mm_tn(y3, dout_b, 1, BF16, "ssm_out_dw").reshape(N_DEV, D // N_DEV, D)
    sc_out1 = _exchange_start(_place(p_w_out1, me, N_DEV, BF16, "place_ssm_w_out"), p_w_out1, "scatter", "scatter_ssm_w_out_start")
    dy3 = _mm_nt(dout_b, w_out1, F32, "ssm_out_dx", deps=(sc_out1[3],), tq=FULL_K)

    def glu_bwd(d3, y, zz, gt):
        gt = gt.astype(F32)
        sg = _sigmoid(gt)
        sz = _sigmoid(zz)
        ygv, _ = _gelu(y)
        dy2 = d3 * (gt * sg)
        dgate = d3 * (ygv * sz) * (sg * (1.0 + gt * (1.0 - sg)))
        dz = dy2 * ygv * (sz * (1.0 - sz))
        return dz, dgate, dy2 * sz, jnp.sum(dz, axis=0, keepdims=True)

    dz_b, dgate1, dyg_a, dglu_b = _ew(glu_bwd, [dy3, y_ssm, z, gate1], [], [BF16, BF16, F32], 1, "glu_gate_bwd")
    p_w_glu = _mm_tn(yg, dz_b, 1, BF16, "glu_in_dw").reshape(N_DEV, D // N_DEV, D)
    sc_glu = _exchange_start(_place(p_w_glu, me, N_DEV, BF16, "place_ssm_glu_w"), p_w_glu, "scatter", "scatter_ssm_glu_w_start")

    def gelu_bwd(db, da, y):
        _, t = _gelu(y)
        dg = 0.5 * (1.0 + t) + 0.5 * y * (1.0 - t * t) * (GELU_C * (1.0 + 3.0 * 0.044715 * (y * y)))
        return (da + db) * dg

    dy_ssm = _mm_nt(dz_b, w_glu, BF16, "glu_in_dx", deps=(sc_glu[3],), extras=[dyg_a, y_ssm], epilogue=gelu_bwd,
                    tq=FUSED_TK)
    dy_p = _permute_rows(dy_ssm, n_steps)
    du_p, dwb, dwc, dlam, dd = _s5_bwd(u_p, dy_p, wb_b, wbt_b, wc_b, wct_b, lam, d_row, h_in, B, "s5_bwd")
    du = _unpermute_rows(du_p, n_steps)
    dproj1 = jnp.concatenate([du, dgate1], axis=1)
    p_w_in1 = _mm_tn(h1, dproj1, N_DEV, BF16, "ssm_in_dw")
    sc_in1 = _exchange_start(_place(p_w_in1, me, N_DEV, BF16, "place_ssm_w_in"), p_w_in1, "scatter", "scatter_ssm_w_in_start")
    dh1 = _mm_nt(dproj1, w_in1, F32, "ssm_in_dx", deps=(sc_in1[3],), tq=FULL_K)
    dx1, dx1_b, dg1 = _rmsnorm_bwd(x1, dh1, dout, g1, "norm1_bwd")

    p_w_out0 = _mm_tn(og, dx1_b, 1, BF16, "attn_out_dw").reshape(N_DEV, D // N_DEV, D)
    sc_out0 = _exchange_start(_place(p_w_out0, me, N_DEV, BF16, "place_attn_w_out"), p_w_out0, "scatter", "scatter_attn_w_out_start")
    dog = _mm_nt(dx1_b, w_out0, BF16, "attn_out_dx", deps=(sc_out0[3],), tq=FULL_K)
    dq, dk, dv, dgate0, dqg, dkg = _attn_bwd(proj0, o, dog, w_kept, sg_kept, q_g, k_g, B, S, H, "attn_bwd")
    dproj0 = jnp.concatenate([dq, dk, dv, dgate0], axis=1)
    p_w_in0 = _mm_tn(h0, dproj0, N_DEV, BF16, "attn_in_dw")
    pair = _exchange_start(lax.empty((N_DEV // 2,) + p_w_in0.shape[1:], BF16), p_w_in0, "pair", "scatter_attn_w_in_pair_start")

    def update(started, after, w, m, v, name, kind="scatter"):
        recv = _exchange_wait(started, after, kind, "scatter_" + name + "_wait")
        return _adamw(recv, w[0], m[0], v[0], "adamw_" + name)

    r_ssm_w_out = update(sc_out1, pair[3], ssm_w_out, m_ssm_w_out, v_ssm_w_out, "ssm_w_out")
    r_ssm_glu_w = update(sc_glu, r_ssm_w_out[0], ssm_glu_w, m_ssm_glu_w, v_ssm_glu_w, "ssm_glu_w")
    r_ssm_w_in = update(sc_in1, r_ssm_glu_w[0], ssm_w_in, m_ssm_w_in, v_ssm_w_in, "ssm_w_in")
    r_attn_w_out = update(sc_out0, r_ssm_w_in[0], attn_w_out, m_attn_w_out, v_attn_w_out, "attn_w_out")

    got, p_w_in0 = _exchange_wait(pair, r_attn_w_out[0], "pair", "scatter_attn_w_in_pair_wait", with_source=True)
    q_w_in0 = _pair_sum(p_w_in0, got, "scatter_attn_w_in_pair_sum")
    sc_in0 = _exchange_start(_place(q_w_in0, chip, N_DEV // 2, BF16, "place_attn_w_in"), q_w_in0, "chips", "scatter_attn_w_in_start")
    dh0 = _mm_nt(dproj0, w_in0, F32, "attn_in_dx", deps=(sc_in0[3],), tq=FULL_K)
    dx, _, dg0 = _rmsnorm_bwd(xf, dh0, dx1, g0, "norm0_bwd")

    dbb_re, dbb_im = _block_diag_in_grad(dwb)
    dc_re, dc_im = _block_diag_out_grad(dwc)
    dlam_re = dlam[:, 0, :HALF].reshape(G, STATE)
    dlam_im = dlam[:, 0, HALF:].reshape(G, STATE)
    by_owner = [t.reshape((N_DEV, -1)) for t in (dlam_re, dlam_im, dbb_re, dbb_im, dc_re, dc_im, dd, dglu_b)]
    small_parts = jnp.concatenate([t.reshape(N_DEV, -1, PACK_W) for t in by_owner], axis=1)
    small_sum = _sum_parts(_exchange(small_parts, True, "scatter_small"), "sum_small")
    s_lam_re, s_lam_im, s_bb_re, s_bb_im, s_c_re, s_c_im, s_d, s_glu_b = _unpack(small_sum, small_shapes)
    g_a_re, g_a_im, g_log_dt, g_b_re, g_b_im = disc_vjp((s_lam_re, s_lam_im, s_bb_re, s_bb_im))

    local_names = ["ssm_A_re", "ssm_A_im", "ssm_log_dt", "ssm_B_re", "ssm_B_im", "ssm_C_re", "ssm_C_im",
                   "ssm_D", "ssm_glu_b"]
    local_g = [g_a_re, g_a_im, g_log_dt, g_b_re, g_b_im, s_c_re, s_c_im, s_d, s_glu_b]
    local_w = [ssm_A_re, ssm_A_im, ssm_log_dt, ssm_B_re, ssm_B_im, ssm_C_re, ssm_C_im, ssm_D, ssm_glu_b]
    local_m = [m_ssm_A_re, m_ssm_A_im, m_ssm_log_dt, m_ssm_B_re, m_ssm_B_im, m_ssm_C_re, m_ssm_C_im,
               m_ssm_D, m_ssm_glu_b]
    local_v = [v_ssm_A_re, v_ssm_A_im, v_ssm_log_dt, v_ssm_B_re, v_ssm_B_im, v_ssm_C_re, v_ssm_C_im,
               v_ssm_D, v_ssm_glu_b]
    r_local = _adamw_small(local_g, local_w, local_m, local_v, None, "adamw_small")

    rep_g = [jnp.concatenate([dg0, dg1], axis=0), dqg, dkg]
    rep_w = [norm_g, attn_q_g, attn_k_g]
    rep_m = [m_norm_g, m_attn_q_g, m_attn_k_g]
    rep_v = [v_norm_g, v_attn_q_g, v_attn_k_g]
    r_rep = _adamw_small(rep_g, rep_w, rep_m, rep_v, "gather_rep", "adamw_rep")

    r_attn_w_in = update(sc_in0, r_rep[0][0], attn_w_in, m_attn_w_in, v_attn_w_in, "attn_w_in", "chips")

    res = {"attn_w_in": r_attn_w_in, "attn_w_out": r_attn_w_out, "ssm_w_in": r_ssm_w_in,
           "ssm_glu_w": r_ssm_glu_w, "ssm_w_out": r_ssm_w_out}
    ref_w = {"attn_w_in": attn_w_in, "attn_w_out": attn_w_out, "ssm_w_in": ssm_w_in,
             "ssm_glu_w": ssm_glu_w, "ssm_w_out": ssm_w_out}
    for name, r, w in zip(local_names, r_local, local_w):
        res[name], ref_w[name] = r, w
    for name, r, w in zip(["norm_g", "attn_q_g", "attn_k_g"], r_rep, rep_w):
        res[name], ref_w[name] = r, w
    order = ["norm_g", "attn_w_in", "attn_q_g", "attn_k_g", "attn_w_out", "ssm_w_in", "ssm_A_re", "ssm_A_im",
             "ssm_log_dt", "ssm_B_re", "ssm_B_im", "ssm_C_re", "ssm_C_im", "ssm_D", "ssm_glu_w", "ssm_glu_b",
             "ssm_w_out"]
    outs = [loss, dx.reshape(B, S, D)]
    for kind in range(4):
        outs += [res[n][kind].reshape(ref_w[n].shape) for n in order]
    return tuple(outs)


def _adamw_small(grads, ws, ms, vs, gather_name, name):
    sizes = [math.prod(w.shape) for w in ws]
    total = sum(sizes)
    rows = -(-total // (PACK_W * 8)) * 8
    if rows > 256:
        rows = -(-rows // 256) * 256

    def pack(ts, fill):
        flat = jnp.concatenate([t.reshape(-1).astype(F32) for t in ts])
        flat = jnp.concatenate([flat, jnp.full((rows * PACK_W - total,), fill, F32)])
        return flat.reshape(rows, PACK_W)

    g = pack(grads, 0.0)
    parts = _exchange(g, False, gather_name) if gather_name else g[None]
    res = _adamw(parts, pack(ws, 0.0), pack(ms, 0.0), pack(vs, 1.0), name)
    outs = []
    off = 0
    flats = [r.reshape(-1) for r in res]
    for n in sizes:
        outs.append(tuple(f[off:off + n] for f in flats))
        off += n
    return outs
```

```python
import functools
import math

import jax
import jax.numpy as jnp
from jax import lax
from jax.experimental import pallas as pl
from jax.experimental.pallas import tpu as pltpu

F32 = jnp.float32
BF16 = jnp.bfloat16

N_DEV = 8
HEAD_DIM = 128
GROUP = 16
STATE = 64
GROUPS_PER_BLOCK = 8
SUBLANES = 8
RMS_EPS = 1e-6
ADAM_LR, ADAM_B1, ADAM_B2, ADAM_EPS, ADAM_WD, ADAM_STEP = 0.001, 0.9, 0.999, 1e-08, 0.01, 10
VMEM_LIMIT = 56 * 1024 * 1024
GELU_C = math.sqrt(2.0 / math.pi)
PACK_W = 128


def _params(sem, **kw):
    return pltpu.CompilerParams(dimension_semantics=sem, vmem_limit_bytes=VMEM_LIMIT, **kw)


def _tile(n, t):
    t = min(n, t)
    assert n % t == 0, (n, t)
    return t


def _ones_where(cond):
    return jnp.where(cond, 1.0, 0.0).astype(BF16)


def _sigmoid(x):
    return 1.0 / (1.0 + jnp.exp(-x))


def _log_sigmoid(z):
    return jnp.minimum(z, 0.0) - jnp.log(1.0 + jnp.exp(-jnp.abs(z)))


def _dot(a, b, dims):
    return lax.dot_general(a, b, (dims, ((), ())), preferred_element_type=F32)


NN = ((1,), (0,))
NT = ((1,), (1,))
TN = ((0,), (0,))


def _exchange(x, scatter, name, deps=()):
    shape = x.shape[1:] if scatter else x.shape

    def body(*refs):
        x_ref = refs[0]
        out_ref, send_sems, recv_sems, local_sem = refs[1 + len(deps):]
        ix, iy, ic = lax.axis_index("x"), lax.axis_index("y"), lax.axis_index("c")
        me = 4 * ix + 2 * iy + ic

        def peer(k):
            kx, ky, kc = (k >> 2) & 1, (k >> 1) & 1, k & 1
            px, py, pc = ix ^ kx, iy ^ ky, ic ^ kc
            return (px, py, pc), 4 * px + 2 * py + pc

        mine = pltpu.make_async_copy(x_ref.at[me] if scatter else x_ref, out_ref.at[me], local_sem)
        mine.start()
        copies = []
        for k in range(1, N_DEV):
            pid, pidx = peer(k)
            cp = pltpu.make_async_remote_copy(
                src_ref=x_ref.at[pidx] if scatter else x_ref,
                dst_ref=out_ref.at[me],
                send_sem=send_sems.at[k - 1], recv_sem=recv_sems.at[k - 1],
                device_id=pid, device_id_type=pl.DeviceIdType.MESH)
            cp.start()
            copies.append(cp)
        for k in range(1, N_DEV):
            pid, pidx = peer(k)
            pltpu.make_async_remote_copy(
                src_ref=x_ref.at[pidx] if scatter else x_ref,
                dst_ref=out_ref.at[pidx],
                send_sem=send_sems.at[k - 1], recv_sem=recv_sems.at[k - 1],
                device_id=pid, device_id_type=pl.DeviceIdType.MESH).wait_recv()
        for cp in copies:
            cp.wait_send()
        mine.wait()

    return pl.pallas_call(
        body, name=name,
        out_shape=jax.ShapeDtypeStruct((N_DEV,) + tuple(shape), x.dtype),
        in_specs=[pl.BlockSpec(memory_space=pl.ANY)] * (1 + len(deps)),
        out_specs=pl.BlockSpec(memory_space=pl.ANY),
        scratch_shapes=[pltpu.SemaphoreType.DMA((N_DEV - 1,)), pltpu.SemaphoreType.DMA((N_DEV - 1,)),
                        pltpu.SemaphoreType.DMA],
    )(x, *deps)


_HBM = pl.BlockSpec(memory_space=pltpu.HBM)
_SEM = pl.BlockSpec(memory_space=pltpu.SEMAPHORE)
_ANY = pl.BlockSpec(memory_space=pl.ANY)
_EFFECT = pltpu.SideEffectType.DATAFLOW_SIDE_EFFECTING


def _peer(k):
    ix, iy, ic = lax.axis_index("x"), lax.axis_index("y"), lax.axis_index("c")
    px, py, pc = ix ^ ((k >> 2) & 1), iy ^ ((k >> 1) & 1), ic ^ (k & 1)
    return (px, py, pc), 4 * px + 2 * py + pc


def _my_index():
    return 4 * lax.axis_index("x") + 2 * lax.axis_index("y") + lax.axis_index("c")


def _my_chip():
    return 2 * lax.axis_index("x") + lax.axis_index("y")


def _place(src, idx, n_slots, dtype, name, deps=(), tr=256, tc=1024):
    R, C = src.shape[-2:]
    tr, tc = _tile(R, tr), _tile(C, tc)
    idx = idx.astype(jnp.int32).reshape(1)

    def body(*refs):
        refs[-1][...] = refs[1][...].astype(dtype)

    slot = pl.BlockSpec((None, tr, tc), lambda r, c, idx_ref: (idx_ref[0], r, c))
    src_spec = slot if src.ndim == 3 else pl.BlockSpec((tr, tc), lambda r, c, idx_ref: (r, c))
    land = lax.empty((n_slots, R, C), dtype)
    return pl.pallas_call(
        body, name=name,
        out_shape=jax.ShapeDtypeStruct(land.shape, dtype),
        grid_spec=pltpu.PrefetchScalarGridSpec(
            num_scalar_prefetch=1, grid=(R // tr, C // tc), in_specs=[src_spec, _ANY] + [_ANY] * len(deps),
            out_specs=slot),
        input_output_aliases={2: 0},
        compiler_params=_params(("parallel", "parallel")),
    )(idx, src, land, *deps)


def _pair_sum(x, got, name, tr=256, tc=1024):
    _, R, C = x.shape
    tr, tc = _tile(R, tr), _tile(C, tc)
    core = lax.axis_index("c").astype(jnp.int32).reshape(1)

    def body(core_ref, x_ref, got_ref, out_ref):
        out_ref[...] = (x_ref[...].astype(F32) + got_ref[...].astype(F32)).astype(out_ref.dtype)

    blk = pl.BlockSpec((None, tr, tc), lambda i, r, c, core_ref: (i, r, c))
    return pl.pallas_call(
        body, name=name,
        out_shape=jax.ShapeDtypeStruct(got.shape, got.dtype),
        grid_spec=pltpu.PrefetchScalarGridSpec(
            num_scalar_prefetch=1, grid=(N_DEV // 2, R // tr, C // tc),
            in_specs=[pl.BlockSpec((None, tr, tc), lambda i, r, c, core_ref: (2 * i + core_ref[0], r, c)), blk],
            out_specs=blk),
        compiler_params=_params(("parallel", "parallel", "parallel")),
    )(core, x, got)


_N_COPIES = {"gather": N_DEV - 1, "scatter": N_DEV - 1, "pair": N_DEV // 2, "chips": N_DEV // 2 - 1,
             "near": N_DEV // 2, "far": N_DEV // 2 - 1}


def _copies(kind, land_ref, x_ref):
    ix, iy, ic = lax.axis_index("x"), lax.axis_index("y"), lax.axis_index("c")
    me, chip = _my_index(), _my_chip()
    out = []
    if kind in ("gather", "scatter"):
        for k in range(1, N_DEV):
            pid, pidx = _peer(k)
            out.append((land_ref.at[me] if kind == "gather" else x_ref.at[pidx], land_ref.at[me], pid, land_ref.at[pidx]))
    elif kind == "pair":
        for i in range(N_DEV // 2):
            out.append((x_ref.at[2 * i + (1 - ic)], land_ref.at[i], (ix, iy, 1 - ic), land_ref.at[i]))
    elif kind in ("near", "far"):
        if kind == "near":
            out.append((land_ref.at[me], land_ref.at[me], (ix, iy, 1 - ic), land_ref.at[me + 1 - 2 * ic]))
        for k in range(1, N_DEV // 2):
            px, py = ix ^ (k >> 1), iy ^ (k & 1)
            theirs = 4 * px + 2 * py + ic
            if kind == "near":
                out.append((land_ref.at[me], land_ref.at[me], (px, py, ic), land_ref.at[theirs]))
            else:
                out.append((land_ref.at[theirs], land_ref.at[theirs], (ix, iy, 1 - ic),
                            land_ref.at[theirs + 1 - 2 * ic]))
    else:
        for k in range(1, N_DEV // 2):
            px, py = ix ^ (k >> 1), iy ^ (k & 1)
            out.append((x_ref.at[2 * px + py], land_ref.at[chip], (px, py, ic), land_ref.at[2 * px + py]))
    return out


def _exchange_start(land, x, kind, name, deps=()):
    bufs = [land] if x is None else [land, x]
    nb, n = len(bufs), _N_COPIES[kind]

    def body(*refs):
        send_sems, recv_sems = refs[nb + len(deps):nb + len(deps) + 2]
        token = refs[2 * nb + len(deps) + 2]
        for i, (src, dst, pid, _) in enumerate(_copies(kind, refs[0], refs[nb - 1])):
            pltpu.make_async_remote_copy(src_ref=src, dst_ref=dst, send_sem=send_sems.at[i], recv_sem=recv_sems.at[i],
                                         device_id=pid, device_id_type=pl.DeviceIdType.MESH).start()
        token[...] = jnp.zeros_like(token)

    out = pl.pallas_call(
        body, name=name,
        out_shape=(pltpu.SemaphoreType.DMA((n,)), pltpu.SemaphoreType.DMA((n,)))
        + tuple(pltpu.HBM(t.shape, t.dtype) for t in bufs) + (jax.ShapeDtypeStruct((8, 128), F32),),
        in_specs=(_HBM,) * nb + (_ANY,) * len(deps),
        out_specs=(_SEM, _SEM) + (_HBM,) * nb + (pl.BlockSpec(memory_space=pltpu.VMEM),),
        input_output_aliases={i: 2 + i for i in range(nb)},
        compiler_params=pltpu.CompilerParams(has_side_effects=_EFFECT),
    )(*[pltpu.with_memory_space_constraint(t, pltpu.HBM) for t in bufs], *deps)
    return out[0], out[1], out[2:2 + nb], out[2 + nb]


def _exchange_wait(started, after, kind, name, with_source=False):
    send_sems, recv_sems, bufs, _ = started
    nb = len(bufs)
    after = tuple(after) if isinstance(after, (tuple, list)) else (after,)

    def body(*refs):
        send_sems, recv_sems = refs[nb:nb + 2]
        for i, (src, _, pid, landed) in enumerate(_copies(kind, refs[0], refs[nb - 1])):
            cp = pltpu.make_async_remote_copy(src_ref=src, dst_ref=landed, send_sem=send_sems.at[i],
                                              recv_sem=recv_sems.at[i], device_id=pid,
                                              device_id_type=pl.DeviceIdType.MESH)
            cp.wait_send()
            cp.wait_recv()

    out = pl.pallas_call(
        body, name=name,
        out_shape=tuple(pltpu.HBM(t.shape, t.dtype) for t in bufs),
        in_specs=(_HBM,) * nb + (_SEM, _SEM) + (_ANY,) * len(after),
        out_specs=(_HBM,) * nb,
        input_output_aliases={i: i for i in range(nb)},
        compiler_params=pltpu.CompilerParams(has_side_effects=_EFFECT),
    )(*bufs, send_sems, recv_sems, *after)
    return tuple(out) if with_source else out[0]


FUSED_TM = 512
FULL_K = 4096


def _accumulate(acc, part, step, n_steps, finish):
    if n_steps == 1:
        finish(part)
        return

    @pl.when(step == 0)
    def _():
        acc[...] = part

    @pl.when((step > 0) & (step < n_steps - 1))
    def _():
        acc[...] += part

    @pl.when(step == n_steps - 1)
    def _():
        finish(acc[...] + part)


def _mm_nn(a, b, out_dtype, name, bias=None, residual=None, deps=(), extras=(), epilogue=None,
           out_dtypes=None, tile_sums=False, tm=1024, tn=1024, tk=2048):
    M, K = a.shape
    J, K2, Nj = b.shape
    assert K == K2
    tm, tn, tk = _tile(M, tm), _tile(Nj, tn), _tile(K, tk)
    nb, nk = Nj // tn, K // tk
    out_dtypes = [out_dtype] if out_dtypes is None else list(out_dtypes)
    n_out = len(out_dtypes) + bool(tile_sums)

    def body(*refs):
        a_ref, b_ref = refs[0], refs[1]
        i = 2
        bias_ref = res_ref = None
        if bias is not None:
            bias_ref = refs[i]; i += 1
        if residual is not None:
            res_ref = refs[i]; i += 1
        extra_refs = refs[i:i + len(extras)]
        i += len(extras) + len(deps)
        o_refs, acc = refs[i:i + n_out], refs[i + n_out]

        def finish(r):
            if bias_ref is not None:
                r = r + bias_ref[...]
            if res_ref is not None:
                r = r + res_ref[...]
            res = (r,) if epilogue is None else epilogue(r, *[e[...] for e in extra_refs])
            for o_ref, v in zip(o_refs, res[:len(out_dtypes)]):
                o_ref[...] = v.astype(o_ref.dtype)
            if tile_sums:
                total = jnp.sum(res[-1], axis=1, keepdims=True)
                first = (lax.broadcasted_iota(jnp.int32, (8, 128), 0) == 0) & (lax.broadcasted_iota(jnp.int32, (8, 128), 1) == 0)
                o_refs[-1][...] = jnp.where(first, total, 0.0)

        _accumulate(acc, _dot(a_ref[...], b_ref[...], NN), pl.program_id(3), nk, finish)

    tile = pl.BlockSpec((tm, tn), lambda j, m, n, k: (m, j * nb + n))
    in_specs = [pl.BlockSpec((tm, tk), lambda j, m, n, k: (m, k)),
                pl.BlockSpec((None, tk, tn), lambda j, m, n, k: (j, k, n))]
    args = [a, b]
    if bias is not None:
        in_specs.append(pl.BlockSpec((1, tn), lambda j, m, n, k: (0, j * nb + n)))
        args.append(bias)
    if residual is not None:
        in_specs.append(tile)
        args.append(residual)
    for arr, off in extras:
        in_specs.append(pl.BlockSpec((tm, tn), lambda j, m, n, k, off=off: (m, off + j * nb + n)))
        args.append(arr)
    in_specs += [_ANY] * len(deps)
    args += list(deps)
    out_shape = [jax.ShapeDtypeStruct((M, J * Nj), d) for d in out_dtypes]
    out_specs = [tile] * len(out_dtypes)
    if tile_sums:
        out_shape.append(jax.ShapeDtypeStruct((M // tm * 8, J * nb * 128), F32))
        out_specs.append(pl.BlockSpec((8, 128), lambda j, m, n, k: (m, j * nb + n)))
    out = pl.pallas_call(
        body, name=name,
        out_shape=tuple(out_shape),
        grid=(J, M // tm, nb, nk),
        in_specs=in_specs,
        out_specs=tuple(out_specs),
        scratch_shapes=[pltpu.VMEM((tm, tn), F32)],
        compiler_params=_params(("parallel", "parallel", "parallel", "arbitrary")),
    )(*args)
    return out[0] if n_out == 1 else out


def _mm_nt(a, b, out_dtype, name, deps=(), extras=(), epilogue=None, tm=1024, tp=1024, tq=2048):
    M, Q = a.shape
    J, P, Qj = b.shape
    assert Q == J * Qj
    tm, tp = _tile(M, tm), _tile(P, tp)
    jb = max(1, min(J, tq // Qj))
    assert J % jb == 0
    tq = _tile(Qj, tq)
    nq = Qj // tq

    def body(*refs):
        a_ref, b_ref = refs[:2]
        extra_refs = refs[2:2 + len(extras)]
        o_ref, acc = refs[2 + len(extras) + len(deps):]

        def finish(r):
            if epilogue is not None:
                r = epilogue(r, *[e[...] for e in extra_refs])
            o_ref[...] = r.astype(out_dtype)

        part = _dot(a_ref[:, 0:tq], b_ref[0], NT)
        for jj in range(1, jb):
            part = part + _dot(a_ref[:, jj * tq:(jj + 1) * tq], b_ref[jj], NT)
        _accumulate(acc, part, pl.program_id(2) * nq + pl.program_id(3), J // jb * nq, finish)

    tile = pl.BlockSpec((tm, tp), lambda m, p, j, q: (m, p))
    return pl.pallas_call(
        body, name=name,
        out_shape=jax.ShapeDtypeStruct((M, P), out_dtype),
        grid=(M // tm, P // tp, J // jb, nq),
        in_specs=[pl.BlockSpec((tm, jb * tq), lambda m, p, j, q: (m, j * nq + q)),
                  pl.BlockSpec((jb, tp, tq), lambda m, p, j, q: (j, p, q))] + [tile] * len(extras) + [_ANY] * len(deps),
        out_specs=tile,
        scratch_shapes=[pltpu.VMEM((tm, tp), F32)],
        compiler_params=_params(("parallel", "parallel", "arbitrary", "arbitrary")),
    )(a, b, *extras, *deps)


def _mm_tn(a, b, J, out_dtype, name, tm=1024, tn=1024, tr=FULL_K):
    R, M = a.shape
    R2, N = b.shape
    assert R == R2 and N % J == 0
    Nj = N // J
    tm, tn, tr = _tile(M, tm), _tile(Nj, tn), _tile(R, tr)
    nb, nr = Nj // tn, R // tr

    def body(a_ref, b_ref, o_ref, acc):
        def finish(r):
            o_ref[...] = r.astype(out_dtype)

        _accumulate(acc, _dot(a_ref[...], b_ref[...], TN), pl.program_id(3), nr, finish)

    return pl.pallas_call(
        body, name=name,
        out_shape=jax.ShapeDtypeStruct((J, M, Nj), out_dtype),
        grid=(J, M // tm, nb, nr),
        in_specs=[pl.BlockSpec((tr, tm), lambda j, m, n, r: (r, m)),
                  pl.BlockSpec((tr, tn), lambda j, m, n, r: (r, j * nb + n))],
        out_specs=pl.BlockSpec((None, tm, tn), lambda j, m, n, r: (j, m, n)),
        scratch_shapes=[pltpu.VMEM((tm, tn), F32)],
        compiler_params=_params(("parallel", "parallel", "parallel", "arbitrary")),
    )(a, b)


def _ew(fn, ins, vecs, out_dtypes, n_acc, name, tr=256, tc=1024):
    T, C = ins[0].shape
    tr, tc = _tile(T, tr), _tile(C, tc)
    n_in, n_vec, n_out = len(ins), len(vecs), len(out_dtypes)

    def body(*refs):
        in_refs = refs[:n_in + n_vec]
        out_refs = refs[n_in + n_vec:n_in + n_vec + n_out]
        acc_refs = refs[n_in + n_vec + n_out:]
        res = fn(*[r[...] for r in in_refs])
        for o_ref, v in zip(out_refs, res[:n_out]):
            o_ref[...] = v.astype(o_ref.dtype)
        if n_acc:
            r = pl.program_id(1)

            @pl.when(r == 0)
            def _():
                for a_ref in acc_refs:
                    a_ref[...] = jnp.zeros_like(a_ref)

            for a_ref, v in zip(acc_refs, res[n_out:]):
                a_ref[...] += v

    blk = pl.BlockSpec((tr, tc), lambda c, r: (r, c))
    vec = pl.BlockSpec((1, tc), lambda c, r: (0, c))
    out = pl.pallas_call(
        body, name=name,
        out_shape=tuple([jax.ShapeDtypeStruct((T, C), d) for d in out_dtypes]
                        + [jax.ShapeDtypeStruct((1, C), F32)] * n_acc),
        grid=(C // tc, T // tr),
        in_specs=[blk] * n_in + [vec] * n_vec,
        out_specs=tuple([blk] * n_out + [vec] * n_acc),
        compiler_params=_params(("parallel", "arbitrary")),
    )(*ins, *vecs)
    return out


def _rmsnorm_fwd(x, g, name, deps=(), tr=128):
    T, D = x.shape
    tr = _tile(T, tr)

    def body(*refs):
        x_ref, g_ref, h_ref = refs[0], refs[1], refs[2 + len(deps)]
        xv = x_ref[...]
        r = lax.rsqrt(jnp.mean(xv * xv, axis=-1, keepdims=True) + RMS_EPS)
        h_ref[...] = ((xv * r) * g_ref[...]).astype(BF16)

    return pl.pallas_call(
        body, name=name,
        out_shape=jax.ShapeDtypeStruct((T, D), BF16),
        grid=(T // tr,),
        in_specs=[pl.BlockSpec((tr, D), lambda i: (i, 0)), pl.BlockSpec((1, D), lambda i: (0, 0))] + [_ANY] * len(deps),
        out_specs=pl.BlockSpec((tr, D), lambda i: (i, 0)),
        compiler_params=_params(("parallel",)),
    )(x, g, *deps)


def _rmsnorm_bwd(x, dh, dres, g, name, tr=128):
    T, D = x.shape
    tr = _tile(T, tr)

    def body(x_ref, dh_ref, dres_ref, g_ref, dx_ref, dxb_ref, dg_ref):
        xv = x_ref[...]
        r = lax.rsqrt(jnp.mean(xv * xv, axis=-1, keepdims=True) + RMS_EPS)
        xn = xv * r
        dhv = dh_ref[...].astype(F32)
        dxn = dhv * g_ref[...]
        dx = dres_ref[...] + r * (dxn - xn * jnp.mean(dxn * xn, axis=-1, keepdims=True))
        dx_ref[...] = dx
        dxb_ref[...] = dx.astype(BF16)

        @pl.when(pl.program_id(0) == 0)
        def _():
            dg_ref[...] = jnp.zeros_like(dg_ref)

        dg_ref[...] += jnp.sum(dhv * xn, axis=0, keepdims=True)

    blk = pl.BlockSpec((tr, D), lambda i: (i, 0))
    vec = pl.BlockSpec((1, D), lambda i: (0, 0))
    return pl.pallas_call(
        body, name=name,
        out_shape=(jax.ShapeDtypeStruct((T, D), F32), jax.ShapeDtypeStruct((T, D), BF16),
                   jax.ShapeDtypeStruct((1, D), F32)),
        grid=(T // tr,),
        in_specs=[blk, blk, blk, vec],
        out_specs=(blk, blk, vec),
        compiler_params=_params(("arbitrary",)),
    )(x, dh, dres, g)


def _total(x, scale, name):
    def body(x_ref, o_ref):
        o_ref[...] = jnp.sum(jnp.sum(x_ref[...], axis=1, keepdims=True), axis=0, keepdims=True) * scale

    return pl.pallas_call(
        body, name=name, out_shape=jax.ShapeDtypeStruct((1, 1), F32),
        compiler_params=pltpu.CompilerParams(vmem_limit_bytes=VMEM_LIMIT),
    )(x)


def _gelu(x):
    t = jnp.tanh(GELU_C * (x + 0.044715 * (x * x * x)))
    return x * (0.5 * (1.0 + t)), t


def _adamw(parts, w, m, v, name, tr=256, tc=1024):
    n, R, C = parts.shape
    tr, tc = _tile(R, tr), _tile(C, tc)
    c1 = 1.0 - ADAM_B1 ** ADAM_STEP
    c2 = 1.0 - ADAM_B2 ** ADAM_STEP

    def body(p_ref, w_ref, m_ref, v_ref, g_out, d_out, m_out, v_out):
        g = p_ref[0].astype(F32)
        for k in range(1, n):
            g = g + p_ref[k].astype(F32)
        mn = ADAM_B1 * m_ref[...] + (1.0 - ADAM_B1) * g
        vn = ADAM_B2 * v_ref[...] + (1.0 - ADAM_B2) * (g * g)
        m_hat = mn / c1
        v_hat = vn / c2
        g_out[...] = g
        d_out[...] = -ADAM_LR * (m_hat / (jnp.sqrt(v_hat) + ADAM_EPS) + ADAM_WD * w_ref[...])
        m_out[...] = mn
        v_out[...] = vn

    blk = pl.BlockSpec((tr, tc), lambda r, c: (r, c))
    return pl.pallas_call(
        body, name=name,
        out_shape=tuple([jax.ShapeDtypeStruct((R, C), F32)] * 4),
        grid=(R // tr, C // tc),
        in_specs=[pl.BlockSpec((n, tr, tc), lambda r, c: (0, r, c)), blk, blk, blk],
        out_specs=(blk, blk, blk, blk),
        compiler_params=_params(("parallel", "parallel")),
    )(parts, w, m, v)


def _sum_parts(parts, name):
    n, R, C = parts.shape

    def body(p_ref, o_ref):
        g = p_ref[0].astype(F32)
        for k in range(1, n):
            g = g + p_ref[k].astype(F32)
        o_ref[...] = g

    return pl.pallas_call(
        body, name=name,
        out_shape=jax.ShapeDtypeStruct((R, C), F32),
        compiler_params=pltpu.CompilerParams(vmem_limit_bytes=VMEM_LIMIT),
    )(parts)


def _head_norm(xv):
    xv = xv.astype(F32)
    r = lax.rsqrt(jnp.mean(xv * xv, axis=-1, keepdims=True) + RMS_EPS)
    return xv * r, r


FWD_HEADS = 4
BWD_HEADS = 2


def _attn_specs(S, H, HP):
    def spec(part):
        return pl.BlockSpec((S, HP * HEAD_DIM), lambda b, h, qi: (b, part * (H // HP) + h))
    return spec


def _lanes(hh):
    return slice(hh * HEAD_DIM, (hh + 1) * HEAD_DIM)


def _attn_fwd(proj, q_g, k_g, B, S, H, name):
    TQ = _tile(S, 256)
    nq = S // TQ
    scale = 1.0 / math.sqrt(HEAD_DIM)
    HP = FWD_HEADS
    ATT_W = HP * HEAD_DIM
    heads = range(HP)

    def body(q_ref, k_ref, v_ref, gate_ref, qg_ref, kg_ref, og_ref, o_ref, w_ref, sg_ref, qn_s, kn_s):
        @pl.when(pl.program_id(2) == 0)
        def _():
            for hh in heads:
                qn_s[:, _lanes(hh)] = (_head_norm(q_ref[:, _lanes(hh)])[0] * qg_ref[...]).astype(BF16)
                kn_s[:, _lanes(hh)] = (_head_norm(k_ref[:, _lanes(hh)])[0] * kg_ref[...]).astype(BF16)

        row = lax.broadcasted_iota(jnp.int32, (TQ, TQ), 0)
        col = lax.broadcasted_iota(jnp.int32, (TQ, TQ), 1)
        later = _ones_where(row > col)
        causal = col < row

        def q_block(qi):
            q0 = pl.multiple_of(qi * TQ, TQ)

            def both(ki, state, diag):
                k0 = pl.multiple_of(ki * TQ, TQ)
                z = [_dot(qn_s[pl.ds(q0, TQ), _lanes(hh)], kn_s[pl.ds(k0, TQ), _lanes(hh)], NT) * scale
                     for hh in heads]
                ls = [_log_sigmoid(zz) for zz in z]
                l1m = [a - zz for a, zz in zip(ls, z)]
                if diag:
                    l1m = [jnp.where(causal, a, 0.0) for a in l1m]
                suffix = [_dot(l1m[hh].astype(BF16), later, NN) + state[hh][0] for hh in heads]
                w = [jnp.exp(a + sfx) for a, sfx in zip(ls, suffix)]
                if diag:
                    w = [jnp.where(causal, a, 0.0) for a in w]
                wb = [a.astype(BF16) for a in w]
                for hh in heads:
                    w_ref[hh, ki] = wb[hh]
                    sg_ref[hh, ki] = jnp.exp(ls[hh]).astype(BF16)
                acc = [state[hh][1] + _dot(wb[hh], v_ref[pl.ds(k0, TQ), _lanes(hh)], NN) for hh in heads]
                return tuple((state[hh][0] + jnp.sum(l1m[hh], axis=1, keepdims=True), acc[hh]) for hh in heads)

            zero = (jnp.zeros((TQ, 1), F32), jnp.zeros((TQ, HEAD_DIM), F32))
            state = both(qi, (zero,) * HP, True)
            state = lax.fori_loop(0, qi, lambda i, st: both(qi - 1 - i, st, False), state)
            for hh in heads:
                acc = state[hh][1]
                o_ref[:, _lanes(hh)] = acc.astype(BF16)
                gate = gate_ref[pl.ds(q0, TQ), _lanes(hh)].astype(F32)
                og_ref[:, _lanes(hh)] = (acc * (gate * _sigmoid(gate))).astype(BF16)

        q_block(pl.program_id(2))

    spec = _attn_specs(S, H, HP)
    vec = pl.BlockSpec((1, HEAD_DIM), lambda b, h, qi: (0, 0))
    out = pl.BlockSpec((TQ, ATT_W), lambda b, h, qi: (b * nq + qi, h))
    kept = pl.BlockSpec((None, HP, None, nq, TQ, TQ), lambda b, h, qi: (b, h, qi, 0, 0, 0))
    kept_shape = jax.ShapeDtypeStruct((B, H, nq, nq, TQ, TQ), BF16)
    return pl.pallas_call(
        body, name=name,
        out_shape=(jax.ShapeDtypeStruct((B * S, H * HEAD_DIM), BF16),) * 2 + (kept_shape,) * 2,
        grid=(B, H // HP, nq),
        in_specs=[spec(0), spec(1), spec(2), spec(3), vec, vec],
        out_specs=(out, out, kept, kept),
        scratch_shapes=[pltpu.VMEM((S, ATT_W), BF16)] * 2,
        compiler_params=_params(("parallel", "parallel", "arbitrary")),
    )(proj, proj, proj, proj, q_g, k_g)


def _attn_bwd(proj, o, dog, w_kept, sg_kept, q_g, k_g, B, S, H, name):
    TQ = _tile(S, 256)
    nq = S // TQ
    scale = 1.0 / math.sqrt(HEAD_DIM)
    HP = BWD_HEADS
    ATT_W = HP * HEAD_DIM
    heads = range(HP)

    def body(q_ref, k_ref, v_ref, gate_ref, o_ref, dog_ref, w_ref, sg_ref, qg_ref, kg_ref,
             dq_ref, dk_ref, dv_ref, dgate_ref, dqg_ref, dkg_ref,
             qn_s, kn_s, do_s, dkn_s, dv_s):
        qi = pl.program_id(2)

        @pl.when((pl.program_id(0) == 0) & (pl.program_id(1) == 0) & (qi == 0))
        def _():
            dqg_ref[...] = jnp.zeros_like(dqg_ref)
            dkg_ref[...] = jnp.zeros_like(dkg_ref)

        @pl.when(qi == 0)
        def _():
            for hh in heads:
                qn_s[:, _lanes(hh)] = (_head_norm(q_ref[:, _lanes(hh)])[0] * qg_ref[...]).astype(BF16)
                kn_s[:, _lanes(hh)] = (_head_norm(k_ref[:, _lanes(hh)])[0] * kg_ref[...]).astype(BF16)
            gate = gate_ref[...].astype(F32)
            sg = _sigmoid(gate)
            dog_v = dog_ref[...].astype(F32)
            do_s[...] = (dog_v * (gate * sg)).astype(BF16)
            dgate_ref[...] = (dog_v * o_ref[...].astype(F32) * (sg * (1.0 + gate * (1.0 - sg)))).astype(BF16)
            dkn_s[...] = jnp.zeros_like(dkn_s)
            dv_s[...] = jnp.zeros_like(dv_s)

        row = lax.broadcasted_iota(jnp.int32, (TQ, TQ), 0)
        col = lax.broadcasted_iota(jnp.int32, (TQ, TQ), 1)
        earlier = _ones_where(row < col)
        causal = col < row

        def norm_bwd(xv, g_ref, dn, dg_ref):
            xh, r = _head_norm(xv)
            dg_ref[...] += jnp.sum(dn * xh, axis=0, keepdims=True)
            dxh = dn * g_ref[...]
            return (r * (dxh - xh * jnp.mean(dxh * xh, axis=-1, keepdims=True))).astype(BF16)

        def q_block():
            q0 = pl.multiple_of(qi * TQ, TQ)

            def grads_both(ki, state, diag):
                k0 = pl.multiple_of(ki * TQ, TQ)
                qb = [qn_s[pl.ds(q0, TQ), _lanes(hh)] for hh in heads]
                dob = [do_s[pl.ds(q0, TQ), _lanes(hh)] for hh in heads]
                wb = [w_ref[hh, ki] for hh in heads]
                da = [_dot(dob[hh], v_ref[pl.ds(k0, TQ), _lanes(hh)], NT) * wb[hh].astype(F32) for hh in heads]
                for hh in heads:
                    dv_s[pl.ds(k0, TQ), _lanes(hh)] += _dot(wb[hh], dob[hh], TN)
                prefix = [_dot(da[hh].astype(BF16), earlier, NN) + state[hh][0] for hh in heads]
                dzb = []
                for hh in heads:
                    sgz = sg_ref[hh, ki].astype(F32)
                    dz = da[hh] * (1.0 - sgz) - sgz * prefix[hh]
                    if diag:
                        dz = jnp.where(causal, dz, 0.0)
                    dzb.append((dz * scale).astype(BF16))
                dq = [state[hh][1] + _dot(dzb[hh], kn_s[pl.ds(k0, TQ), _lanes(hh)], NN) for hh in heads]
                for hh in heads:
                    dkn_s[pl.ds(k0, TQ), _lanes(hh)] += _dot(dzb[hh], qb[hh], TN)
                return tuple((state[hh][0] + jnp.sum(da[hh], axis=1, keepdims=True), dq[hh]) for hh in heads)

            zero = (jnp.zeros((TQ, 1), F32), jnp.zeros((TQ, HEAD_DIM), F32))
            state = lax.fori_loop(0, qi, lambda i, st: grads_both(i, st, False), (zero,) * HP)
            state = grads_both(qi, state, True)
            for hh in heads:
                dq_ref[:, _lanes(hh)] = norm_bwd(q_ref[pl.ds(q0, TQ), _lanes(hh)], qg_ref, state[hh][1], dqg_ref)

        q_block()

        @pl.when(qi == nq - 1)
        def _():
            for hh in heads:
                dk_ref[:, _lanes(hh)] = norm_bwd(k_ref[:, _lanes(hh)], kg_ref, dkn_s[:, _lanes(hh)], dkg_ref)
            dv_ref[...] = dv_s[...].astype(BF16)

    spec = _attn_specs(S, H, HP)
    vec = pl.BlockSpec((1, HEAD_DIM), lambda b, h, qi: (0, 0))
    blk = pl.BlockSpec((S, ATT_W), lambda b, h, qi: (b, h))
    rows = pl.BlockSpec((TQ, ATT_W), lambda b, h, qi: (b * nq + qi, h))
    kept = pl.BlockSpec((None, HP, None, nq, TQ, TQ), lambda b, h, qi: (b, h, qi, 0, 0, 0))
    big = jax.ShapeDtypeStruct((B * S, H * HEAD_DIM), BF16)
    small = jax.ShapeDtypeStruct((1, HEAD_DIM), F32)
    return pl.pallas_call(
        body, name=name,
        out_shape=(big, big, big, big, small, small),
        grid=(B, H // HP, nq),
        in_specs=[spec(0), spec(1), spec(2), spec(3), blk, blk, kept, kept, vec, vec],
        out_specs=(rows, blk, blk, blk, vec, vec),
        scratch_shapes=[pltpu.VMEM((S, ATT_W), BF16)] * 3 + [pltpu.VMEM((S, ATT_W), F32)] * 2,
        compiler_params=_params(("arbitrary", "arbitrary", "arbitrary")),
    )(proj, proj, proj, proj, o, dog, w_kept, sg_kept, q_g, k_g)


HALF = GROUPS_PER_BLOCK * STATE


def _cmul(ar, ai, br, bi):
    return ar * br - ai * bi, ar * bi + ai * br


def _cpow(ar, ai, n):
    rr = ri = None
    while n:
        if n & 1:
            rr, ri = (ar, ai) if rr is None else _cmul(rr, ri, ar, ai)
        n >>= 1
        if n:
            ar, ai = _cmul(ar, ai, ar, ai)
    return rr, ri


def _segment_carry(er, ei, lr, li, seg_len, segs_per_seq, reverse):
    Lr, Li = _cpow(lr, li, seg_len)
    pos = lax.broadcasted_iota(jnp.int32, er.shape, 0) % segs_per_seq
    outr = jnp.zeros_like(er)
    outi = jnp.zeros_like(ei)
    pr = pi = None
    for d in range(1, segs_per_seq):
        shift = (SUBLANES - d) if reverse else d
        sr = pltpu.roll(er, shift, 0)
        si = pltpu.roll(ei, shift, 0)
        ok = (pos + d < segs_per_seq) if reverse else (pos >= d)
        sr = jnp.where(ok, sr, 0.0)
        si = jnp.where(ok, si, 0.0)
        if pr is not None:
            sr, si = _cmul(sr, si, pr, pi)
        outr = outr + sr
        outi = outi + si
        pr, pi = (Lr, Li) if pr is None else _cmul(pr, pi, Lr, Li)
    return outr, outi


def _s5_sizes(T, B):
    assert SUBLANES % B == 0
    segs_per_seq = SUBLANES // B
    n_steps = T // SUBLANES
    cj = _tile(n_steps, 64)
    return segs_per_seq, n_steps, cj


def _s5_fwd(u_p, wb, wc, lam, dvec, B, name):
    T, C = u_p.shape
    nb = C // 128
    segs_per_seq, n_steps, cj = _s5_sizes(T, B)
    n_chunks = n_steps // cj
    rows = cj * SUBLANES

    def body(u_ref, wb_ref, wc_ref, lam_ref, d_ref, y_ref, hin_ref, bu_s, h_s):
        lr = jnp.broadcast_to(lam_ref[:, :HALF], (SUBLANES, HALF))
        li = jnp.broadcast_to(lam_ref[:, HALF:], (SUBLANES, HALF))

        def scan_chunk(c, hr, hi, store):
            r0 = pl.multiple_of(c * rows, rows)
            if not store:
                bu_s[pl.ds(r0, rows), :] = _dot(u_ref[pl.ds(r0, rows), :], wb_ref[...], NN)

            def step(j, carry):
                hr, hi = carry
                o = pl.multiple_of(j * SUBLANES, SUBLANES)
                at = pl.multiple_of(r0 + o, SUBLANES)
                nr = lr * hr - li * hi + bu_s[pl.ds(at, SUBLANES), :HALF]
                ni = lr * hi + li * hr + bu_s[pl.ds(at, SUBLANES), HALF:]
                if store:
                    h_s[pl.ds(o, SUBLANES), :HALF] = nr
                    h_s[pl.ds(o, SUBLANES), HALF:] = ni
                return nr, ni

            hr, hi = lax.fori_loop(0, cj, step, (hr, hi))
            if store:
                uv = u_ref[pl.ds(r0, rows), :].astype(F32)
                y_ref[pl.ds(r0, rows), :] = _dot(h_s[...].astype(BF16), wc_ref[...], NN) + d_ref[...] * uv
            return hr, hi

        zero = jnp.zeros((SUBLANES, HALF), F32)
        er, ei = lax.fori_loop(0, n_chunks, lambda c, h: scan_chunk(c, h[0], h[1], False), (zero, zero))
        h0r, h0i = _segment_carry(er, ei, lr, li, n_steps, segs_per_seq, False)
        hin_ref[:, :HALF] = h0r
        hin_ref[:, HALF:] = h0i
        lax.fori_loop(0, n_chunks, lambda c, h: scan_chunk(c, h[0], h[1], True), (h0r, h0i))

    return pl.pallas_call(
        body, name=name,
        out_shape=(jax.ShapeDtypeStruct((T, C), F32), jax.ShapeDtypeStruct((nb, SUBLANES, 2 * HALF), F32)),
        grid=(nb,),
        in_specs=[pl.BlockSpec((T, 128), lambda g: (0, g)),
                  pl.BlockSpec((None, 128, 2 * HALF), lambda g: (g, 0, 0)),
                  pl.BlockSpec((None, 2 * HALF, 128), lambda g: (g, 0, 0)),
                  pl.BlockSpec((None, 1, 2 * HALF), lambda g: (g, 0, 0)),
                  pl.BlockSpec((1, 128), lambda g: (0, g))],
        out_specs=(pl.BlockSpec((T, 128), lambda g: (0, g)),
                   pl.BlockSpec((None, SUBLANES, 2 * HALF), lambda g: (g, 0, 0))),
        scratch_shapes=[pltpu.VMEM((T, 2 * HALF), F32), pltpu.VMEM((rows, 2 * HALF), F32)],
        compiler_params=_params(("parallel",)),
    )(u_p, wb, wc, lam, dvec)


def _s5_bwd(u_p, dy_p, wb, wbt, wc, wct, lam, dvec, h_in, B, name):
    T, C = u_p.shape
    nb = C // 128
    segs_per_seq, n_steps, cj = _s5_sizes(T, B)
    n_chunks = n_steps // cj
    rows = cj * SUBLANES

    def body(u_ref, dy_ref, wb_ref, wbt_ref, wc_ref, wct_ref, lam_ref, d_ref, hin_ref,
             du_ref, dwb_ref, dwc_ref, dlam_ref, dd_ref, h_all, dh_all, x_s, g_s):
        lr = jnp.broadcast_to(lam_ref[:, :HALF], (SUBLANES, HALF))
        li = jnp.broadcast_to(lam_ref[:, HALF:], (SUBLANES, HALF))
        zero = jnp.zeros((SUBLANES, HALF), F32)

        h_all[pl.ds(0, SUBLANES), :] = hin_ref[...]

        def fwd_chunk(c, carry):
            r0 = pl.multiple_of(c * rows, rows)
            x_s[...] = _dot(u_ref[pl.ds(r0, rows), :], wb_ref[...], NN)

            def step(j, carry):
                hr, hi = carry
                o = pl.multiple_of(j * SUBLANES, SUBLANES)
                nr = lr * hr - li * hi + x_s[pl.ds(o, SUBLANES), :HALF]
                ni = lr * hi + li * hr + x_s[pl.ds(o, SUBLANES), HALF:]
                late = pl.multiple_of(r0 + o + SUBLANES, SUBLANES)
                h_all[pl.ds(late, SUBLANES), :HALF] = nr
                h_all[pl.ds(late, SUBLANES), HALF:] = ni
                return nr, ni

            return lax.fori_loop(0, cj, step, carry)

        lax.fori_loop(0, n_chunks, fwd_chunk, (hin_ref[:, :HALF], hin_ref[:, HALF:]))

        def bwd_chunk(i, carry, store):
            c = n_chunks - 1 - i
            r0 = pl.multiple_of(c * rows, rows)
            dyv = dy_ref[pl.ds(r0, rows), :]
            if not store:
                dh_all[pl.ds(r0, rows), :] = _dot(dyv, wct_ref[...], NN)

            def step(jj, carry):
                ar, ai, accr, acci = carry
                j = cj - 1 - jj
                o = pl.multiple_of(j * SUBLANES, SUBLANES)
                prev = pl.multiple_of(r0 + o, SUBLANES)
                nr = lr * ar + li * ai + dh_all[pl.ds(prev, SUBLANES), :HALF]
                ni = lr * ai - li * ar + dh_all[pl.ds(prev, SUBLANES), HALF:]
                if store:
                    g_s[pl.ds(o, SUBLANES), :HALF] = nr
                    g_s[pl.ds(o, SUBLANES), HALF:] = ni
                    pr = h_all[pl.ds(prev, SUBLANES), :HALF]
                    pi = h_all[pl.ds(prev, SUBLANES), HALF:]
                    accr = accr + nr * pr + ni * pi
                    acci = acci + ni * pr - nr * pi
                return nr, ni, accr, acci

            carry = lax.fori_loop(0, cj, step, carry)
            if store:
                gb = g_s[...].astype(BF16)
                uv = u_ref[pl.ds(r0, rows), :]
                dyf = dyv.astype(F32)
                du_ref[pl.ds(r0, rows), :] = (_dot(gb, wbt_ref[...], NN) + d_ref[...] * dyf).astype(BF16)
                dwb_ref[...] += _dot(uv, gb, TN)
                hb = h_all[pl.ds(pl.multiple_of(r0 + SUBLANES, SUBLANES), rows), :].astype(BF16)
                dwc_ref[...] += _dot(hb, dyv, TN)
                dd_ref[...] += jnp.sum(dyf * uv.astype(F32), axis=0, keepdims=True)
            return carry

        er, ei, _, _ = lax.fori_loop(0, n_chunks, lambda i, c: bwd_chunk(i, c, False), (zero, zero, zero, zero))
        a0r, a0i = _segment_carry(er, ei, lr, -li, n_steps, segs_per_seq, True)
        dwb_ref[...] = jnp.zeros_like(dwb_ref)
        dwc_ref[...] = jnp.zeros_like(dwc_ref)
        dd_ref[...] = jnp.zeros_like(dd_ref)
        _, _, accr, acci = lax.fori_loop(0, n_chunks, lambda i, c: bwd_chunk(i, c, True), (a0r, a0i, zero, zero))
        dlam_ref[:, :HALF] = jnp.sum(accr, axis=0, keepdims=True)
        dlam_ref[:, HALF:] = jnp.sum(acci, axis=0, keepdims=True)

    col = pl.BlockSpec((T, 128), lambda g: (0, g))
    vec = pl.BlockSpec((1, 128), lambda g: (0, g))

    def per_block(*shape):
        return pl.BlockSpec((None,) + shape, lambda g: (g, 0, 0))

    return pl.pallas_call(
        body, name=name,
        out_shape=(jax.ShapeDtypeStruct((T, C), BF16),
                   jax.ShapeDtypeStruct((nb, 128, 2 * HALF), F32),
                   jax.ShapeDtypeStruct((nb, 2 * HALF, 128), F32),
                   jax.ShapeDtypeStruct((nb, 1, 2 * HALF), F32),
                   jax.ShapeDtypeStruct((1, C), F32)),
        grid=(nb,),
        in_specs=[col, col, per_block(128, 2 * HALF), per_block(2 * HALF, 128), per_block(2 * HALF, 128),
                  per_block(128, 2 * HALF), per_block(1, 2 * HALF), vec, per_block(SUBLANES, 2 * HALF)],
        out_specs=(col, per_block(128, 2 * HALF), per_block(2 * HALF, 128), per_block(1, 2 * HALF), vec),
        scratch_shapes=[pltpu.VMEM((T + SUBLANES, 2 * HALF), F32), pltpu.VMEM((T, 2 * HALF), F32),
                        pltpu.VMEM((rows, 2 * HALF), F32), pltpu.VMEM((rows, 2 * HALF), F32)],
        compiler_params=_params(("parallel",)),
    )(u_p, dy_p, wb, wbt, wc, wct, lam, dvec, h_in)


def _discretize(a_re, a_im, log_dt, b_re, b_im):
    dt = jnp.exp(log_dt)[:, None]
    mag = jnp.exp(a_re * dt)
    lam_re = mag * jnp.cos(a_im * dt)
    lam_im = mag * jnp.sin(a_im * dt)
    den = a_re * a_re + a_im * a_im
    f_re = ((lam_re - 1.0) * a_re + lam_im * a_im) / den
    f_im = (lam_im * a_re - (lam_re - 1.0) * a_im) / den
    bb_re = f_re[..., None] * b_re - f_im[..., None] * b_im
    bb_im = f_re[..., None] * b_im + f_im[..., None] * b_re
    return lam_re, lam_im, bb_re, bb_im


def _block_diag_in(bb_re, bb_im):
    eye = jnp.eye(GROUPS_PER_BLOCK, dtype=F32)

    def one(bb):
        t = bb.reshape(-1, GROUPS_PER_BLOCK, STATE, GROUP)
        return jnp.einsum('gapi,ab->gaibp', t, eye).reshape(-1, 128, HALF)

    return jnp.concatenate([one(bb_re), one(bb_im)], axis=-1)


def _block_diag_in_grad(dwb):
    eye = jnp.eye(GROUPS_PER_BLOCK, dtype=F32)

    def one(d):
        t = d.reshape(-1, GROUPS_PER_BLOCK, GROUP, GROUPS_PER_BLOCK, STATE)
        return jnp.einsum('gaibp,ab->gapi', t, eye).reshape(-1, STATE, GROUP)

    return one(dwb[..., :HALF]), one(dwb[..., HALF:])


def _block_diag_out(c_re, c_im):
    eye = jnp.eye(GROUPS_PER_BLOCK, dtype=F32)

    def one(cc):
        t = cc.reshape(-1, GROUPS_PER_BLOCK, GROUP, STATE)
        return jnp.einsum('gaip,ab->gbpai', t, eye).reshape(-1, HALF, 128)

    return jnp.concatenate([one(c_re), -one(c_im)], axis=1)


def _block_diag_out_grad(dwc):
    eye = jnp.eye(GROUPS_PER_BLOCK, dtype=F32)

    def one(d):
        t = d.reshape(-1, GROUPS_PER_BLOCK, STATE, GROUPS_PER_BLOCK, GROUP)
        return jnp.einsum('gbpai,ab->gaip', t, eye).reshape(-1, GROUP, STATE)

    return one(dwc[:, :HALF]), -one(dwc[:, HALF:])


def _pack(parts):
    return jnp.concatenate([p.reshape(-1, PACK_W) for p in parts], axis=0)


def _unpack(buf, shapes):
    lead = buf.shape[:-2]
    out, r = [], 0
    for s in shapes:
        n = math.prod(s) // PACK_W
        out.append(buf[..., r:r + n, :].reshape(lead + tuple(s)))
        r += n
    return out


def _permute_rows(a, n_steps):
    T, C = a.shape
    return a.reshape(SUBLANES, n_steps, C).transpose(1, 0, 2).reshape(T, C)


def _unpermute_rows(a, n_steps):
    T, C = a.shape
    return a.reshape(n_steps, SUBLANES, C).transpose(1, 0, 2).reshape(T, C)


def kernel(x, norm_g, attn_w_in, attn_q_g, attn_k_g, attn_w_out, ssm_w_in, ssm_A_re, ssm_A_im, ssm_log_dt, ssm_B_re, ssm_B_im, ssm_C_re, ssm_C_im, ssm_D, ssm_glu_w, ssm_glu_b, ssm_w_out, loss_target, m_norm_g, m_attn_w_in, m_attn_q_g, m_attn_k_g, m_attn_w_out, m_ssm_w_in, m_ssm_A_re, m_ssm_A_im, m_ssm_log_dt, m_ssm_B_re, m_ssm_B_im, m_ssm_C_re, m_ssm_C_im, m_ssm_D, m_ssm_glu_w, m_ssm_glu_b, m_ssm_w_out, v_norm_g, v_attn_w_in, v_attn_q_g, v_attn_k_g, v_attn_w_out, v_ssm_w_in, v_ssm_A_re, v_ssm_A_im, v_ssm_log_dt, v_ssm_B_re, v_ssm_B_im, v_ssm_C_re, v_ssm_C_im, v_ssm_D, v_ssm_glu_w, v_ssm_glu_b, v_ssm_w_out):
    B, S, D = x.shape
    T = B * S
    H = D // HEAD_DIM
    G_loc = ssm_A_re.shape[1]
    G = G_loc * N_DEV
    n_steps = T // SUBLANES
    me, chip = _my_index(), _my_chip()
    xf = x.reshape(T, D)
    target = loss_target.reshape(T, D)

    small_shapes = [(G_loc, STATE), (G_loc, STATE), (G_loc, STATE, GROUP), (G_loc, STATE, GROUP),
                    (G_loc, GROUP, STATE), (G_loc, GROUP, STATE), (G_loc * GROUP,), (G_loc * GROUP,)]
    disc_in = (ssm_A_re[0], ssm_A_im[0], ssm_log_dt[0], ssm_B_re[0], ssm_B_im[0])
    (lam_re, lam_im, bb_re, bb_im), disc_vjp = jax.vjp(_discretize, *disc_in)
    small = _pack([lam_re, lam_im, bb_re, bb_im, ssm_C_re[0], ssm_C_im[0], ssm_D[0], ssm_glu_b[0]])
    small_all = _exchange(small, False, "gather_small")
    c_in0 = _place(attn_w_in[0], me, N_DEV, BF16, "cast_attn_w_in")
    near = _exchange_start(c_in0, None, "near", "gather_attn_w_in_near_start", deps=(small_all,))
    beside = (near[3],)
    small_all = small_all + near[3][0, 0]
    c_out0 = _place(attn_w_out[0], me, N_DEV, BF16, "cast_attn_w_out", deps=beside)
    c_in1 = _place(ssm_w_in[0], me, N_DEV, BF16, "cast_ssm_w_in", deps=beside)
    c_glu = _place(ssm_glu_w[0], me, N_DEV, BF16, "cast_ssm_glu_w", deps=beside)
    c_out1 = _place(ssm_w_out[0], me, N_DEV, BF16, "cast_ssm_w_out", deps=beside)

    lam_re_a, lam_im_a, bb_re_a, bb_im_a, c_re_a, c_im_a, d_a, glu_b_a = [
        t.reshape((G,) + t.shape[2:]) if t.ndim > 2 else t.reshape(-1)
        for t in _unpack(small_all, small_shapes)]
    wb = _block_diag_in(bb_re_a, bb_im_a)
    wc = _block_diag_out(c_re_a, c_im_a)
    wb_b, wc_b = wb.astype(BF16), wc.astype(BF16)
    wbt_b, wct_b = wb_b.transpose(0, 2, 1), wc_b.transpose(0, 2, 1)
    lam = jnp.concatenate([lam_re_a.reshape(-1, 1, HALF), lam_im_a.reshape(-1, 1, HALF)], axis=-1)
    d_row = d_a.reshape(1, D)
    glu_b_row = glu_b_a.reshape(1, D)
    g0, g1 = norm_g[0:1], norm_g[1:2]
    q_g, k_g = attn_q_g, attn_k_g

    h0 = _rmsnorm_fwd(xf, g0, "norm0", deps=beside)
    done = (h0, c_out0, c_in1, c_glu, c_out1, wb_b, wbt_b, wc_b, wct_b, lam, d_row, glu_b_row)
    far = _exchange_start(_exchange_wait(near, done, "near", "gather_attn_w_in_near_wait"), None, "far",
                          "gather_attn_w_in_far_start")
    w_in0 = _exchange_wait(far, far[3], "far", "gather_attn_w_in_far_wait")
    s_out0 = _exchange_start(c_out0, None, "gather", "gather_attn_w_out_start", deps=(w_in0,))
    s_in1 = _exchange_start(c_in1, None, "gather", "gather_ssm_w_in_start", deps=(w_in0,))
    proj0 = _mm_nn(h0, w_in0, BF16, "attn_in", deps=(s_out0[3], s_in1[3]), tk=FULL_K)
    og, o, w_kept, sg_kept = _attn_fwd(proj0, q_g, k_g, B, S, H, "attn_fwd")
    w_out0 = _exchange_wait(s_out0, og, "gather", "gather_attn_w_out_wait").reshape(1, D, D)
    s_glu = _exchange_start(c_glu, None, "gather", "gather_ssm_glu_w_start", deps=(w_out0,))
    s_out1 = _exchange_start(c_out1, None, "gather", "gather_ssm_w_out_start", deps=(w_out0,))
    x1 = _mm_nn(og, w_out0, F32, "attn_out", residual=xf, deps=(s_glu[3], s_out1[3]), tm=FUSED_TM, tk=FULL_K)

    h1 = _rmsnorm_fwd(x1, g1, "norm1")
    w_in1 = _exchange_wait(s_in1, h1, "gather", "gather_ssm_w_in_wait")
    proj1 = _mm_nn(h1, w_in1, BF16, "ssm_in", tk=FULL_K)
    u_p = _permute_rows(proj1[:, :D], n_steps)
    gate1 = proj1[:, D:]
    y_p, h_in = _s5_fwd(u_p, wb_b, wc_b, lam, d_row, B, "s5_fwd")
    y_ssm = _unpermute_rows(y_p, n_steps)
    (yg,) = _ew(lambda a: (_gelu(a)[0],), [y_ssm], [], [BF16], 0, "gelu")
    w_glu = _exchange_wait(s_glu, yg, "gather", "gather_ssm_glu_w_wait").reshape(1, D, D)

    def glu_gate(zz, y, gt):
        gt = gt.astype(F32)
        return zz, _gelu(y)[0] * _sigmoid(zz) * (gt * _sigmoid(gt))

    z, y3 = _mm_nn(yg, w_glu, F32, "glu_in", bias=glu_b_row, extras=[(y_ssm, 0), (proj1, D // _tile(D, 1024))],
                   epilogue=glu_gate, out_dtypes=[F32, BF16], tm=FUSED_TM, tk=FULL_K)
    w_out1 = _exchange_wait(s_out1, y3, "gather", "gather_ssm_w_out_wait").reshape(1, D, D)

    def loss_head(out_tile, tgt):
        err = out_tile - tgt
        dy = err * (1.0 / D)
        return dy, dy, jnp.sum(err * err, axis=0, keepdims=True)

    dout, dout_b, sq = _mm_nn(y3, w_out1, F32, "ssm_out", residual=x1, extras=[(target, 0)], epilogue=loss_head,
                              out_dtypes=[F32, BF16], tile_sums=True, tm=FUSED_TM, tk=FULL_K)
    loss_part = _total(sq, 0.5 / D, "loss")
    loss = lax.psum(loss_part[0, 0], ("x", "y", "c"))

    p_w_out1 = _mm_tn(y3, dout_b, 1, BF16, "ssm_out_dw").reshape(N_DEV, D // N_DEV, D)
    sc_out1 = _exchange_start(_place(p_w_out1, me, N_DEV, BF16, "place_ssm_w_out"), p_w_out1, "scatter", "scatter_ssm_w_out_start")
    dy3 = _mm_nt(dout_b, w_out1, F32, "ssm_out_dx", deps=(sc_out1[3],), tq=FULL_K)

    def glu_bwd(d3, y, zz, gt):
        gt = gt.astype(F32)
        sg = _sigmoid(gt)
        sz = _sigmoid(zz)
        ygv, _ = _gelu(y)
        dy2 = d3 * (gt * sg)
        dgate = d3 * (ygv * sz) * (sg * (1.0 + gt * (1.0 - sg)))
        dz = dy2 * ygv * (sz * (1.0 - sz))
        return dz, dgate, dy2 * sz, jnp.sum(dz, axis=0, keepdims=True)

    dz_b, dgate1, dyg_a, dglu_b = _ew(glu_bwd, [dy3, y_ssm, z, gate1], [], [BF16, BF16, F32], 1, "glu_gate_bwd")
    p_w_glu = _mm_tn(yg, dz_b, 1, BF16, "glu_in_dw").reshape(N_DEV, D // N_DEV, D)
    sc_glu = _exchange_start(_place(p_w_glu, me, N_DEV, BF16, "place_ssm_glu_w"), p_w_glu, "scatter", "scatter_ssm_glu_w_start")

    def gelu_bwd(db, da, y):
        _, t = _gelu(y)
        dg = 0.5 * (1.0 + t) + 0.5 * y * (1.0 - t * t) * (GELU_C * (1.0 + 3.0 * 0.044715 * (y * y)))
        return (da + db) * dg

    dy_ssm = _mm_nt(dz_b, w_glu, BF16, "glu_in_dx", deps=(sc_glu[3],), extras=[dyg_a, y_ssm], epilogue=gelu_bwd,
                    tm=FUSED_TM, tq=FULL_K)
    dy_p = _permute_rows(dy_ssm, n_steps)
    du_p, dwb, dwc, dlam, dd = _s5_bwd(u_p, dy_p, wb_b, wbt_b, wc_b, wct_b, lam, d_row, h_in, B, "s5_bwd")
    du = _unpermute_rows(du_p, n_steps)
    dproj1 = jnp.concatenate([du, dgate1], axis=1)
    p_w_in1 = _mm_tn(h1, dproj1, N_DEV, BF16, "ssm_in_dw")
    sc_in1 = _exchange_start(_place(p_w_in1, me, N_DEV, BF16, "place_ssm_w_in"), p_w_in1, "scatter", "scatter_ssm_w_in_start")
    dh1 = _mm_nt(dproj1, w_in1, F32, "ssm_in_dx", deps=(sc_in1[3],), tq=FULL_K)
    dx1, dx1_b, dg1 = _rmsnorm_bwd(x1, dh1, dout, g1, "norm1_bwd")

    p_w_out0 = _mm_tn(og, dx1_b, 1, BF16, "attn_out_dw").reshape(N_DEV, D // N_DEV, D)
    sc_out0 = _exchange_start(_place(p_w_out0, me, N_DEV, BF16, "place_attn_w_out"), p_w_out0, "scatter", "scatter_attn_w_out_start")
    dog = _mm_nt(dx1_b, w_out0, BF16, "attn_out_dx", deps=(sc_out0[3],), tq=FULL_K)
    dq, dk, dv, dgate0, dqg, dkg = _attn_bwd(proj0, o, dog, w_kept, sg_kept, q_g, k_g, B, S, H, "attn_bwd")
    dproj0 = jnp.concatenate([dq, dk, dv, dgate0], axis=1)
    p_w_in0 = _mm_tn(h0, dproj0, N_DEV, BF16, "attn_in_dw")
    pair = _exchange_start(lax.empty((N_DEV // 2,) + p_w_in0.shape[1:], BF16), p_w_in0, "pair", "scatter_attn_w_in_pair_start")

    def update(started, after, w, m, v, name, kind="scatter"):
        recv = _exchange_wait(started, after, kind, "scatter_" + name + "_wait")
        return _adamw(recv, w[0], m[0], v[0], "adamw_" + name)

    r_ssm_w_out = update(sc_out1, pair[3], ssm_w_out, m_ssm_w_out, v_ssm_w_out, "ssm_w_out")
    r_ssm_glu_w = update(sc_glu, r_ssm_w_out[0], ssm_glu_w, m_ssm_glu_w, v_ssm_glu_w, "ssm_glu_w")
    r_ssm_w_in = update(sc_in1, r_ssm_glu_w[0], ssm_w_in, m_ssm_w_in, v_ssm_w_in, "ssm_w_in")
    r_attn_w_out = update(sc_out0, r_ssm_w_in[0], attn_w_out, m_attn_w_out, v_attn_w_out, "attn_w_out")

    got, p_w_in0 = _exchange_wait(pair, r_attn_w_out[0], "pair", "scatter_attn_w_in_pair_wait", with_source=True)
    q_w_in0 = _pair_sum(p_w_in0, got, "scatter_attn_w_in_pair_sum")
    sc_in0 = _exchange_start(_place(q_w_in0, chip, N_DEV // 2, BF16, "place_attn_w_in"), q_w_in0, "chips", "scatter_attn_w_in_start")
    dh0 = _mm_nt(dproj0, w_in0, F32, "attn_in_dx", deps=(sc_in0[3],), tq=FULL_K)
    dx, _, dg0 = _rmsnorm_bwd(xf, dh0, dx1, g0, "norm0_bwd")

    dbb_re, dbb_im = _block_diag_in_grad(dwb)
    dc_re, dc_im = _block_diag_out_grad(dwc)
    dlam_re = dlam[:, 0, :HALF].reshape(G, STATE)
    dlam_im = dlam[:, 0, HALF:].reshape(G, STATE)
    by_owner = [t.reshape((N_DEV, -1)) for t in (dlam_re, dlam_im, dbb_re, dbb_im, dc_re, dc_im, dd, dglu_b)]
    small_parts = jnp.concatenate([t.reshape(N_DEV, -1, PACK_W) for t in by_owner], axis=1)
    small_sum = _sum_parts(_exchange(small_parts, True, "scatter_small"), "sum_small")
    s_lam_re, s_lam_im, s_bb_re, s_bb_im, s_c_re, s_c_im, s_d, s_glu_b = _unpack(small_sum, small_shapes)
    g_a_re, g_a_im, g_log_dt, g_b_re, g_b_im = disc_vjp((s_lam_re, s_lam_im, s_bb_re, s_bb_im))

    local_names = ["ssm_A_re", "ssm_A_im", "ssm_log_dt", "ssm_B_re", "ssm_B_im", "ssm_C_re", "ssm_C_im",
                   "ssm_D", "ssm_glu_b"]
    local_g = [g_a_re, g_a_im, g_log_dt, g_b_re, g_b_im, s_c_re, s_c_im, s_d, s_glu_b]
    local_w = [ssm_A_re, ssm_A_im, ssm_log_dt, ssm_B_re, ssm_B_im, ssm_C_re, ssm_C_im, ssm_D, ssm_glu_b]
    local_m = [m_ssm_A_re, m_ssm_A_im, m_ssm_log_dt, m_ssm_B_re, m_ssm_B_im, m_ssm_C_re, m_ssm_C_im,
               m_ssm_D, m_ssm_glu_b]
    local_v = [v_ssm_A_re, v_ssm_A_im, v_ssm_log_dt, v_ssm_B_re, v_ssm_B_im, v_ssm_C_re, v_ssm_C_im,
               v_ssm_D, v_ssm_glu_b]
    r_local = _adamw_small(local_g, local_w, local_m, local_v, None, "adamw_small")

    rep_g = [jnp.concatenate([dg0, dg1], axis=0), dqg, dkg]
    rep_w = [norm_g, attn_q_g, attn_k_g]
    rep_m = [m_norm_g, m_attn_q_g, m_attn_k_g]
    rep_v = [v_norm_g, v_attn_q_g, v_attn_k_g]
    r_rep = _adamw_small(rep_g, rep_w, rep_m, rep_v, "gather_rep", "adamw_rep")

    r_attn_w_in = update(sc_in0, r_rep[0][0], attn_w_in, m_attn_w_in, v_attn_w_in, "attn_w_in", "chips")

    res = {"attn_w_in": r_attn_w_in, "attn_w_out": r_attn_w_out, "ssm_w_in": r_ssm_w_in,
           "ssm_glu_w": r_ssm_glu_w, "ssm_w_out": r_ssm_w_out}
    ref_w = {"attn_w_in": attn_w_in, "attn_w_out": attn_w_out, "ssm_w_in": ssm_w_in,
             "ssm_glu_w": ssm_glu_w, "ssm_w_out": ssm_w_out}
    for name, r, w in zip(local_names, r_local, local_w):
        res[name], ref_w[name] = r, w
    for name, r, w in zip(["norm_g", "attn_q_g", "attn_k_g"], r_rep, rep_w):
        res[name], ref_w[name] = r, w
    order = ["norm_g", "attn_w_in", "attn_q_g", "attn_k_g", "attn_w_out", "ssm_w_in", "ssm_A_re", "ssm_A_im",
             "ssm_log_dt", "ssm_B_re", "ssm_B_im", "ssm_C_re", "ssm_C_im", "ssm_D", "ssm_glu_w", "ssm_glu_b",
             "ssm_w_out"]
    outs = [loss, dx.reshape(B, S, D)]
    for kind in range(4):
        outs += [res[n][kind].reshape(ref_w[n].shape) for n in order]
    return tuple(outs)


def _adamw_small(grads, ws, ms, vs, gather_name, name):
    sizes = [math.prod(w.shape) for w in ws]
    total = sum(sizes)
    rows = -(-total // (PACK_W * 8)) * 8
    if rows > 256:
        rows = -(-rows // 256) * 256

    def pack(ts, fill):
        flat = jnp.concatenate([t.reshape(-1).astype(F32) for t in ts])
        flat = jnp.concatenate([flat, jnp.full((rows * PACK_W - total,), fill, F32)])
        return flat.reshape(rows, PACK_W)

    g = pack(grads, 0.0)
    parts = _exchange(g, False, gather_name) if gather_name else g[None]
    res = _adamw(parts, pack(ws, 0.0), pack(ms, 0.0), pack(vs, 1.0), name)
    outs = []
    off = 0
    flats = [r.reshape(-1) for r in res]
    for n in sizes:
        outs.append(tuple(f[off:off + n] for f in flats))
        off += n
    return outs
```

```python
import functools
import math

import jax
import jax.numpy as jnp
from jax import lax
from jax.experimental import pallas as pl
from jax.experimental.pallas import tpu as pltpu

F32 = jnp.float32
BF16 = jnp.bfloat16

N_DEV = 8
HEAD_DIM = 128
GROUP = 16
STATE = 64
GROUPS_PER_BLOCK = 8
SUBLANES = 8
RMS_EPS = 1e-6
ADAM_LR, ADAM_B1, ADAM_B2, ADAM_EPS, ADAM_WD, ADAM_STEP = 0.001, 0.9, 0.999, 1e-08, 0.01, 10
VMEM_LIMIT = 56 * 1024 * 1024
GELU_C = math.sqrt(2.0 / math.pi)
PACK_W = 128


def _params(sem, **kw):
    return pltpu.CompilerParams(dimension_semantics=sem, vmem_limit_bytes=VMEM_LIMIT, **kw)


def _tile(n, t):
    t = min(n, t)
    assert n % t == 0, (n, t)
    return t


def _ones_where(cond):
    return jnp.where(cond, 1.0, 0.0).astype(BF16)


def _sigmoid(x):
    return 1.0 / (1.0 + jnp.exp(-x))


def _log_sigmoid(z):
    return jnp.minimum(z, 0.0) - jnp.log(1.0 + jnp.exp(-jnp.abs(z)))


def _dot(a, b, dims):
    return lax.dot_general(a, b, (dims, ((), ())), preferred_element_type=F32)


NN = ((1,), (0,))
NT = ((1,), (1,))
TN = ((0,), (0,))


def _exchange(x, scatter, name, deps=()):
    shape = x.shape[1:] if scatter else x.shape

    def body(*refs):
        x_ref = refs[0]
        out_ref, send_sems, recv_sems, local_sem = refs[1 + len(deps):]
        ix, iy, ic = lax.axis_index("x"), lax.axis_index("y"), lax.axis_index("c")
        me = 4 * ix + 2 * iy + ic

        def peer(k):
            kx, ky, kc = (k >> 2) & 1, (k >> 1) & 1, k & 1
            px, py, pc = ix ^ kx, iy ^ ky, ic ^ kc
            return (px, py, pc), 4 * px + 2 * py + pc

        mine = pltpu.make_async_copy(x_ref.at[me] if scatter else x_ref, out_ref.at[me], local_sem)
        mine.start()
        copies = []
        for k in range(1, N_DEV):
            pid, pidx = peer(k)
            cp = pltpu.make_async_remote_copy(
                src_ref=x_ref.at[pidx] if scatter else x_ref,
                dst_ref=out_ref.at[me],
                send_sem=send_sems.at[k - 1], recv_sem=recv_sems.at[k - 1],
                device_id=pid, device_id_type=pl.DeviceIdType.MESH)
            cp.start()
            copies.append(cp)
        for k in range(1, N_DEV):
            pid, pidx = peer(k)
            pltpu.make_async_remote_copy(
                src_ref=x_ref.at[pidx] if scatter else x_ref,
                dst_ref=out_ref.at[pidx],
                send_sem=send_sems.at[k - 1], recv_sem=recv_sems.at[k - 1],
                device_id=pid, device_id_type=pl.DeviceIdType.MESH).wait_recv()
        for cp in copies:
            cp.wait_send()
        mine.wait()

    return pl.pallas_call(
        body, name=name,
        out_shape=jax.ShapeDtypeStruct((N_DEV,) + tuple(shape), x.dtype),
        in_specs=[pl.BlockSpec(memory_space=pl.ANY)] * (1 + len(deps)),
        out_specs=pl.BlockSpec(memory_space=pl.ANY),
        scratch_shapes=[pltpu.SemaphoreType.DMA((N_DEV - 1,)), pltpu.SemaphoreType.DMA((N_DEV - 1,)),
                        pltpu.SemaphoreType.DMA],
    )(x, *deps)


_HBM = pl.BlockSpec(memory_space=pltpu.HBM)
_SEM = pl.BlockSpec(memory_space=pltpu.SEMAPHORE)
_ANY = pl.BlockSpec(memory_space=pl.ANY)
_EFFECT = pltpu.SideEffectType.DATAFLOW_SIDE_EFFECTING


def _peer(k):
    ix, iy, ic = lax.axis_index("x"), lax.axis_index("y"), lax.axis_index("c")
    px, py, pc = ix ^ ((k >> 2) & 1), iy ^ ((k >> 1) & 1), ic ^ (k & 1)
    return (px, py, pc), 4 * px + 2 * py + pc


def _my_index():
    return 4 * lax.axis_index("x") + 2 * lax.axis_index("y") + lax.axis_index("c")


def _my_chip():
    return 2 * lax.axis_index("x") + lax.axis_index("y")


def _place(src, idx, n_slots, dtype, name, deps=(), tr=256, tc=1024):
    R, C = src.shape[-2:]
    tr, tc = _tile(R, tr), _tile(C, tc)
    idx = idx.astype(jnp.int32).reshape(1)

    def body(*refs):
        refs[-1][...] = refs[1][...].astype(dtype)

    slot = pl.BlockSpec((None, tr, tc), lambda r, c, idx_ref: (idx_ref[0], r, c))
    src_spec = slot if src.ndim == 3 else pl.BlockSpec((tr, tc), lambda r, c, idx_ref: (r, c))
    land = lax.empty((n_slots, R, C), dtype)
    return pl.pallas_call(
        body, name=name,
        out_shape=jax.ShapeDtypeStruct(land.shape, dtype),
        grid_spec=pltpu.PrefetchScalarGridSpec(
            num_scalar_prefetch=1, grid=(R // tr, C // tc), in_specs=[src_spec, _ANY] + [_ANY] * len(deps),
            out_specs=slot),
        input_output_aliases={2: 0},
        compiler_params=_params(("parallel", "parallel")),
    )(idx, src, land, *deps)


def _pair_sum(x, got, name, tr=256, tc=1024):
    _, R, C = x.shape
    tr, tc = _tile(R, tr), _tile(C, tc)
    core = lax.axis_index("c").astype(jnp.int32).reshape(1)

    def body(core_ref, x_ref, got_ref, out_ref):
        out_ref[...] = (x_ref[...].astype(F32) + got_ref[...].astype(F32)).astype(out_ref.dtype)

    blk = pl.BlockSpec((None, tr, tc), lambda i, r, c, core_ref: (i, r, c))
    return pl.pallas_call(
        body, name=name,
        out_shape=jax.ShapeDtypeStruct(got.shape, got.dtype),
        grid_spec=pltpu.PrefetchScalarGridSpec(
            num_scalar_prefetch=1, grid=(N_DEV // 2, R // tr, C // tc),
            in_specs=[pl.BlockSpec((None, tr, tc), lambda i, r, c, core_ref: (2 * i + core_ref[0], r, c)), blk],
            out_specs=blk),
        compiler_params=_params(("parallel", "parallel", "parallel")),
    )(core, x, got)


_N_COPIES = {"gather": N_DEV - 1, "scatter": N_DEV - 1, "pair": N_DEV // 2, "chips": N_DEV // 2 - 1,
             "near": N_DEV // 2, "far": N_DEV // 2 - 1}


def _copies(kind, land_ref, x_ref):
    ix, iy, ic = lax.axis_index("x"), lax.axis_index("y"), lax.axis_index("c")
    me, chip = _my_index(), _my_chip()
    out = []
    if kind in ("gather", "scatter"):
        for k in range(1, N_DEV):
            pid, pidx = _peer(k)
            out.append((land_ref.at[me] if kind == "gather" else x_ref.at[pidx], land_ref.at[me], pid, land_ref.at[pidx]))
    elif kind == "pair":
        for i in range(N_DEV // 2):
            out.append((x_ref.at[2 * i + (1 - ic)], land_ref.at[i], (ix, iy, 1 - ic), land_ref.at[i]))
    elif kind in ("near", "far"):
        if kind == "near":
            out.append((land_ref.at[me], land_ref.at[me], (ix, iy, 1 - ic), land_ref.at[me + 1 - 2 * ic]))
        for k in range(1, N_DEV // 2):
            px, py = ix ^ (k >> 1), iy ^ (k & 1)
            theirs = 4 * px + 2 * py + ic
            if kind == "near":
                out.append((land_ref.at[me], land_ref.at[me], (px, py, ic), land_ref.at[theirs]))
            else:
                out.append((land_ref.at[theirs], land_ref.at[theirs], (ix, iy, 1 - ic),
                            land_ref.at[theirs + 1 - 2 * ic]))
    else:
        for k in range(1, N_DEV // 2):
            px, py = ix ^ (k >> 1), iy ^ (k & 1)
            out.append((x_ref.at[2 * px + py], land_ref.at[chip], (px, py, ic), land_ref.at[2 * px + py]))
    return out


def _exchange_start(land, x, kind, name, deps=()):
    bufs = [land] if x is None else [land, x]
    nb, n = len(bufs), _N_COPIES[kind]

    def body(*refs):
        send_sems, recv_sems = refs[nb + len(deps):nb + len(deps) + 2]
        token = refs[2 * nb + len(deps) + 2]
        for i, (src, dst, pid, _) in enumerate(_copies(kind, refs[0], refs[nb - 1])):
            pltpu.make_async_remote_copy(src_ref=src, dst_ref=dst, send_sem=send_sems.at[i], recv_sem=recv_sems.at[i],
                                         device_id=pid, device_id_type=pl.DeviceIdType.MESH).start()
        token[...] = jnp.zeros_like(token)

    out = pl.pallas_call(
        body, name=name,
        out_shape=(pltpu.SemaphoreType.DMA((n,)), pltpu.SemaphoreType.DMA((n,)))
        + tuple(pltpu.HBM(t.shape, t.dtype) for t in bufs) + (jax.ShapeDtypeStruct((8, 128), F32),),
        in_specs=(_HBM,) * nb + (_ANY,) * len(deps),
        out_specs=(_SEM, _SEM) + (_HBM,) * nb + (pl.BlockSpec(memory_space=pltpu.VMEM),),
        input_output_aliases={i: 2 + i for i in range(nb)},
        compiler_params=pltpu.CompilerParams(has_side_effects=_EFFECT),
    )(*[pltpu.with_memory_space_constraint(t, pltpu.HBM) for t in bufs], *deps)
    return out[0], out[1], out[2:2 + nb], out[2 + nb]


def _exchange_wait(started, after, kind, name, with_source=False):
    send_sems, recv_sems, bufs, _ = started
    nb = len(bufs)
    after = tuple(after) if isinstance(after, (tuple, list)) else (after,)

    def body(*refs):
        send_sems, recv_sems = refs[nb:nb + 2]
        for i, (src, _, pid, landed) in enumerate(_copies(kind, refs[0], refs[nb - 1])):
            cp = pltpu.make_async_remote_copy(src_ref=src, dst_ref=landed, send_sem=send_sems.at[i],
                                              recv_sem=recv_sems.at[i], device_id=pid,
                                              device_id_type=pl.DeviceIdType.MESH)
            cp.wait_send()
            cp.wait_recv()

    out = pl.pallas_call(
        body, name=name,
        out_shape=tuple(pltpu.HBM(t.shape, t.dtype) for t in bufs),
        in_specs=(_HBM,) * nb + (_SEM, _SEM) + (_ANY,) * len(after),
        out_specs=(_HBM,) * nb,
        input_output_aliases={i: i for i in range(nb)},
        compiler_params=pltpu.CompilerParams(has_side_effects=_EFFECT),
    )(*bufs, send_sems, recv_sems, *after)
    return tuple(out) if with_source else out[0]


FUSED_TM = 512
FULL_K = 4096


def _accumulate(acc, part, step, n_steps, finish):
    if n_steps == 1:
        finish(part)
        return

    @pl.when(step == 0)
    def _():
        acc[...] = part

    @pl.when((step > 0) & (step < n_steps - 1))
    def _():
        acc[...] += part

    @pl.when(step == n_steps - 1)
    def _():
        finish(acc[...] + part)


def _mm_nn(a, b, out_dtype, name, bias=None, residual=None, deps=(), extras=(), epilogue=None,
           out_dtypes=None, tile_sums=False, tm=1024, tn=1024, tk=2048):
    M, K = a.shape
    J, K2, Nj = b.shape
    assert K == K2
    tm, tn, tk = _tile(M, tm), _tile(Nj, tn), _tile(K, tk)
    nb, nk = Nj // tn, K // tk
    out_dtypes = [out_dtype] if out_dtypes is None else list(out_dtypes)
    n_out = len(out_dtypes) + bool(tile_sums)

    def body(*refs):
        a_ref, b_ref = refs[0], refs[1]
        i = 2
        bias_ref = res_ref = None
        if bias is not None:
            bias_ref = refs[i]; i += 1
        if residual is not None:
            res_ref = refs[i]; i += 1
        extra_refs = refs[i:i + len(extras)]
        i += len(extras) + len(deps)
        o_refs, acc = refs[i:i + n_out], refs[i + n_out]

        def finish(r):
            if bias_ref is not None:
                r = r + bias_ref[...]
            if res_ref is not None:
                r = r + res_ref[...]
            res = (r,) if epilogue is None else epilogue(r, *[e[...] for e in extra_refs])
            for o_ref, v in zip(o_refs, res[:len(out_dtypes)]):
                o_ref[...] = v.astype(o_ref.dtype)
            if tile_sums:
                total = jnp.sum(res[-1], axis=1, keepdims=True)
                first = (lax.broadcasted_iota(jnp.int32, (8, 128), 0) == 0) & (lax.broadcasted_iota(jnp.int32, (8, 128), 1) == 0)
                o_refs[-1][...] = jnp.where(first, total, 0.0)

        _accumulate(acc, _dot(a_ref[...], b_ref[...], NN), pl.program_id(3), nk, finish)

    tile = pl.BlockSpec((tm, tn), lambda j, m, n, k: (m, j * nb + n))
    in_specs = [pl.BlockSpec((tm, tk), lambda j, m, n, k: (m, k)),
                pl.BlockSpec((None, tk, tn), lambda j, m, n, k: (j, k, n))]
    args = [a, b]
    if bias is not None:
        in_specs.append(pl.BlockSpec((1, tn), lambda j, m, n, k: (0, j * nb + n)))
        args.append(bias)
    if residual is not None:
        in_specs.append(tile)
        args.append(residual)
    for arr, off in extras:
        in_specs.append(pl.BlockSpec((tm, tn), lambda j, m, n, k, off=off: (m, off + j * nb + n)))
        args.append(arr)
    in_specs += [_ANY] * len(deps)
    args += list(deps)
    out_shape = [jax.ShapeDtypeStruct((M, J * Nj), d) for d in out_dtypes]
    out_specs = [tile] * len(out_dtypes)
    if tile_sums:
        out_shape.append(jax.ShapeDtypeStruct((M // tm * 8, J * nb * 128), F32))
        out_specs.append(pl.BlockSpec((8, 128), lambda j, m, n, k: (m, j * nb + n)))
    out = pl.pallas_call(
        body, name=name,
        out_shape=tuple(out_shape),
        grid=(J, M // tm, nb, nk),
        in_specs=in_specs,
        out_specs=tuple(out_specs),
        scratch_shapes=[pltpu.VMEM((tm, tn), F32)],
        compiler_params=_params(("parallel", "parallel", "parallel", "arbitrary")),
    )(*args)
    return out[0] if n_out == 1 else out


def _mm_nt(a, b, out_dtype, name, deps=(), extras=(), epilogue=None, tm=1024, tp=1024, tq=2048):
    M, Q = a.shape
    J, P, Qj = b.shape
    assert Q == J * Qj
    tm, tp = _tile(M, tm), _tile(P, tp)
    jb = max(1, min(J, tq // Qj))
    assert J % jb == 0
    tq = _tile(Qj, tq)
    nq = Qj // tq

    def body(*refs):
        a_ref, b_ref = refs[:2]
        extra_refs = refs[2:2 + len(extras)]
        o_ref, acc = refs[2 + len(extras) + len(deps):]

        def finish(r):
            if epilogue is not None:
                r = epilogue(r, *[e[...] for e in extra_refs])
            o_ref[...] = r.astype(out_dtype)

        part = _dot(a_ref[:, 0:tq], b_ref[0], NT)
        for jj in range(1, jb):
            part = part + _dot(a_ref[:, jj * tq:(jj + 1) * tq], b_ref[jj], NT)
        _accumulate(acc, part, pl.program_id(2) * nq + pl.program_id(3), J // jb * nq, finish)

    tile = pl.BlockSpec((tm, tp), lambda m, p, j, q: (m, p))
    return pl.pallas_call(
        body, name=name,
        out_shape=jax.ShapeDtypeStruct((M, P), out_dtype),
        grid=(M // tm, P // tp, J // jb, nq),
        in_specs=[pl.BlockSpec((tm, jb * tq), lambda m, p, j, q: (m, j * nq + q)),
                  pl.BlockSpec((jb, tp, tq), lambda m, p, j, q: (j, p, q))] + [tile] * len(extras) + [_ANY] * len(deps),
        out_specs=tile,
        scratch_shapes=[pltpu.VMEM((tm, tp), F32)],
        compiler_params=_params(("parallel", "parallel", "arbitrary", "arbitrary")),
    )(a, b, *extras, *deps)


def _mm_tn(a, b, J, out_dtype, name, tm=1024, tn=1024, tr=FULL_K):
    R, M = a.shape
    R2, N = b.shape
    assert R == R2 and N % J == 0
    Nj = N // J
    tm, tn, tr = _tile(M, tm), _tile(Nj, tn), _tile(R, tr)
    nb, nr = Nj // tn, R // tr

    def body(a_ref, b_ref, o_ref, acc):
        def finish(r):
            o_ref[...] = r.astype(out_dtype)

        _accumulate(acc, _dot(a_ref[...], b_ref[...], TN), pl.program_id(3), nr, finish)

    return pl.pallas_call(
        body, name=name,
        out_shape=jax.ShapeDtypeStruct((J, M, Nj), out_dtype),
        grid=(J, M // tm, nb, nr),
        in_specs=[pl.BlockSpec((tr, tm), lambda j, m, n, r: (r, m)),
                  pl.BlockSpec((tr, tn), lambda j, m, n, r: (r, j * nb + n))],
        out_specs=pl.BlockSpec((None, tm, tn), lambda j, m, n, r: (j, m, n)),
        scratch_shapes=[pltpu.VMEM((tm, tn), F32)],
        compiler_params=_params(("parallel", "parallel", "parallel", "arbitrary")),
    )(a, b)


def _ew(fn, ins, vecs, out_dtypes, n_acc, name, tr=256, tc=1024):
    T, C = ins[0].shape
    tr, tc = _tile(T, tr), _tile(C, tc)
    n_in, n_vec, n_out = len(ins), len(vecs), len(out_dtypes)

    def body(*refs):
        in_refs = refs[:n_in + n_vec]
        out_refs = refs[n_in + n_vec:n_in + n_vec + n_out]
        acc_refs = refs[n_in + n_vec + n_out:]
        res = fn(*[r[...] for r in in_refs])
        for o_ref, v in zip(out_refs, res[:n_out]):
            o_ref[...] = v.astype(o_ref.dtype)
        if n_acc:
            r = pl.program_id(1)

            @pl.when(r == 0)
            def _():
                for a_ref in acc_refs:
                    a_ref[...] = jnp.zeros_like(a_ref)

            for a_ref, v in zip(acc_refs, res[n_out:]):
                a_ref[...] += v

    blk = pl.BlockSpec((tr, tc), lambda c, r: (r, c))
    vec = pl.BlockSpec((1, tc), lambda c, r: (0, c))
    out = pl.pallas_call(
        body, name=name,
        out_shape=tuple([jax.ShapeDtypeStruct((T, C), d) for d in out_dtypes]
                        + [jax.ShapeDtypeStruct((1, C), F32)] * n_acc),
        grid=(C // tc, T // tr),
        in_specs=[blk] * n_in + [vec] * n_vec,
        out_specs=tuple([blk] * n_out + [vec] * n_acc),
        compiler_params=_params(("parallel", "arbitrary")),
    )(*ins, *vecs)
    return out


def _rmsnorm_fwd(x, g, name, deps=(), tr=128):
    T, D = x.shape
    tr = _tile(T, tr)

    def body(*refs):
        x_ref, g_ref, h_ref = refs[0], refs[1], refs[2 + len(deps)]
        xv = x_ref[...]
        r = lax.rsqrt(jnp.mean(xv * xv, axis=-1, keepdims=True) + RMS_EPS)
        h_ref[...] = ((xv * r) * g_ref[...]).astype(BF16)

    return pl.pallas_call(
        body, name=name,
        out_shape=jax.ShapeDtypeStruct((T, D), BF16),
        grid=(T // tr,),
        in_specs=[pl.BlockSpec((tr, D), lambda i: (i, 0)), pl.BlockSpec((1, D), lambda i: (0, 0))] + [_ANY] * len(deps),
        out_specs=pl.BlockSpec((tr, D), lambda i: (i, 0)),
        compiler_params=_params(("parallel",)),
    )(x, g, *deps)


def _rmsnorm_bwd(x, dh, dres, g, name, tr=128):
    T, D = x.shape
    tr = _tile(T, tr)

    def body(x_ref, dh_ref, dres_ref, g_ref, dx_ref, dxb_ref, dg_ref):
        xv = x_ref[...]
        r = lax.rsqrt(jnp.mean(xv * xv, axis=-1, keepdims=True) + RMS_EPS)
        xn = xv * r
        dhv = dh_ref[...].astype(F32)
        dxn = dhv * g_ref[...]
        dx = dres_ref[...] + r * (dxn - xn * jnp.mean(dxn * xn, axis=-1, keepdims=True))
        dx_ref[...] = dx
        dxb_ref[...] = dx.astype(BF16)

        @pl.when(pl.program_id(0) == 0)
        def _():
            dg_ref[...] = jnp.zeros_like(dg_ref)

        dg_ref[...] += jnp.sum(dhv * xn, axis=0, keepdims=True)

    blk = pl.BlockSpec((tr, D), lambda i: (i, 0))
    vec = pl.BlockSpec((1, D), lambda i: (0, 0))
    return pl.pallas_call(
        body, name=name,
        out_shape=(jax.ShapeDtypeStruct((T, D), F32), jax.ShapeDtypeStruct((T, D), BF16),
                   jax.ShapeDtypeStruct((1, D), F32)),
        grid=(T // tr,),
        in_specs=[blk, blk, blk, vec],
        out_specs=(blk, blk, vec),
        compiler_params=_params(("arbitrary",)),
    )(x, dh, dres, g)


def _total(x, scale, name, deps=()):
    def body(*refs):
        refs[-1][...] = jnp.sum(jnp.sum(refs[0][...], axis=1, keepdims=True), axis=0, keepdims=True) * scale

    return pl.pallas_call(
        body, name=name, out_shape=jax.ShapeDtypeStruct((1, 1), F32),
        in_specs=[pl.BlockSpec(memory_space=pltpu.VMEM)] + [_ANY] * len(deps),
        compiler_params=pltpu.CompilerParams(vmem_limit_bytes=VMEM_LIMIT),
    )(x, *deps)


def _gelu(x):
    t = jnp.tanh(GELU_C * (x + 0.044715 * (x * x * x)))
    return x * (0.5 * (1.0 + t)), t


def _adamw(parts, w, m, v, name, tr=256, tc=1024):
    n, R, C = parts.shape
    tr, tc = _tile(R, tr), _tile(C, tc)
    c1 = 1.0 - ADAM_B1 ** ADAM_STEP
    c2 = 1.0 - ADAM_B2 ** ADAM_STEP

    def body(p_ref, w_ref, m_ref, v_ref, g_out, d_out, m_out, v_out):
        g = p_ref[0].astype(F32)
        for k in range(1, n):
            g = g + p_ref[k].astype(F32)
        mn = ADAM_B1 * m_ref[...] + (1.0 - ADAM_B1) * g
        vn = ADAM_B2 * v_ref[...] + (1.0 - ADAM_B2) * (g * g)
        m_hat = mn / c1
        v_hat = vn / c2
        g_out[...] = g
        d_out[...] = -ADAM_LR * (m_hat / (jnp.sqrt(v_hat) + ADAM_EPS) + ADAM_WD * w_ref[...])
        m_out[...] = mn
        v_out[...] = vn

    blk = pl.BlockSpec((tr, tc), lambda r, c: (r, c))
    return pl.pallas_call(
        body, name=name,
        out_shape=tuple([jax.ShapeDtypeStruct((R, C), F32)] * 4),
        grid=(R // tr, C // tc),
        in_specs=[pl.BlockSpec((n, tr, tc), lambda r, c: (0, r, c)), blk, blk, blk],
        out_specs=(blk, blk, blk, blk),
        compiler_params=_params(("parallel", "parallel")),
    )(parts, w, m, v)


def _sum_parts(parts, name):
    n, R, C = parts.shape

    def body(p_ref, o_ref):
        g = p_ref[0].astype(F32)
        for k in range(1, n):
            g = g + p_ref[k].astype(F32)
        o_ref[...] = g

    return pl.pallas_call(
        body, name=name,
        out_shape=jax.ShapeDtypeStruct((R, C), F32),
        compiler_params=pltpu.CompilerParams(vmem_limit_bytes=VMEM_LIMIT),
    )(parts)


def _head_norm(xv):
    xv = xv.astype(F32)
    r = lax.rsqrt(jnp.mean(xv * xv, axis=-1, keepdims=True) + RMS_EPS)
    return xv * r, r


FWD_HEADS = 4
BWD_HEADS = 2


def _attn_specs(S, H, HP):
    def spec(part):
        return pl.BlockSpec((S, HP * HEAD_DIM), lambda b, h, qi: (b, part * (H // HP) + h))
    return spec


def _lanes(hh):
    return slice(hh * HEAD_DIM, (hh + 1) * HEAD_DIM)


def _attn_fwd(proj, q_g, k_g, B, S, H, name):
    TQ = _tile(S, 256)
    nq = S // TQ
    scale = 1.0 / math.sqrt(HEAD_DIM)
    HP = FWD_HEADS
    ATT_W = HP * HEAD_DIM
    heads = range(HP)

    def body(q_ref, k_ref, v_ref, gate_ref, qg_ref, kg_ref, og_ref, o_ref, w_ref, sg_ref, qn_s, kn_s):
        @pl.when(pl.program_id(2) == 0)
        def _():
            for hh in heads:
                qn_s[:, _lanes(hh)] = (_head_norm(q_ref[:, _lanes(hh)])[0] * qg_ref[...]).astype(BF16)
                kn_s[:, _lanes(hh)] = (_head_norm(k_ref[:, _lanes(hh)])[0] * kg_ref[...]).astype(BF16)

        row = lax.broadcasted_iota(jnp.int32, (TQ, TQ), 0)
        col = lax.broadcasted_iota(jnp.int32, (TQ, TQ), 1)
        later = _ones_where(row > col)
        causal = col < row

        def q_block(qi):
            q0 = pl.multiple_of(qi * TQ, TQ)

            def both(ki, state, diag):
                k0 = pl.multiple_of(ki * TQ, TQ)
                z = [_dot(qn_s[pl.ds(q0, TQ), _lanes(hh)], kn_s[pl.ds(k0, TQ), _lanes(hh)], NT) * scale
                     for hh in heads]
                ls = [_log_sigmoid(zz) for zz in z]
                l1m = [a - zz for a, zz in zip(ls, z)]
                if diag:
                    l1m = [jnp.where(causal, a, 0.0) for a in l1m]
                suffix = [_dot(l1m[hh].astype(BF16), later, NN) + state[hh][0] for hh in heads]
                w = [jnp.exp(a + sfx) for a, sfx in zip(ls, suffix)]
                if diag:
                    w = [jnp.where(causal, a, 0.0) for a in w]
                wb = [a.astype(BF16) for a in w]
                for hh in heads:
                    w_ref[hh, ki] = wb[hh]
                    sg_ref[hh, ki] = jnp.exp(ls[hh]).astype(BF16)
                acc = [state[hh][1] + _dot(wb[hh], v_ref[pl.ds(k0, TQ), _lanes(hh)], NN) for hh in heads]
                return tuple((state[hh][0] + jnp.sum(l1m[hh], axis=1, keepdims=True), acc[hh]) for hh in heads)

            zero = (jnp.zeros((TQ, 1), F32), jnp.zeros((TQ, HEAD_DIM), F32))
            state = both(qi, (zero,) * HP, True)
            state = lax.fori_loop(0, qi, lambda i, st: both(qi - 1 - i, st, False), state)
            for hh in heads:
                acc = state[hh][1]
                o_ref[:, _lanes(hh)] = acc.astype(BF16)
                gate = gate_ref[pl.ds(q0, TQ), _lanes(hh)].astype(F32)
                og_ref[:, _lanes(hh)] = (acc * (gate * _sigmoid(gate))).astype(BF16)

        q_block(pl.program_id(2))

    spec = _attn_specs(S, H, HP)
    vec = pl.BlockSpec((1, HEAD_DIM), lambda b, h, qi: (0, 0))
    out = pl.BlockSpec((TQ, ATT_W), lambda b, h, qi: (b * nq + qi, h))
    kept = pl.BlockSpec((None, HP, None, nq, TQ, TQ), lambda b, h, qi: (b, h, qi, 0, 0, 0))
    kept_shape = jax.ShapeDtypeStruct((B, H, nq, nq, TQ, TQ), BF16)
    return pl.pallas_call(
        body, name=name,
        out_shape=(jax.ShapeDtypeStruct((B * S, H * HEAD_DIM), BF16),) * 2 + (kept_shape,) * 2,
        grid=(B, H // HP, nq),
        in_specs=[spec(0), spec(1), spec(2), spec(3), vec, vec],
        out_specs=(out, out, kept, kept),
        scratch_shapes=[pltpu.VMEM((S, ATT_W), BF16)] * 2,
        compiler_params=_params(("parallel", "parallel", "arbitrary")),
    )(proj, proj, proj, proj, q_g, k_g)


def _attn_bwd(proj, o, dog, w_kept, sg_kept, q_g, k_g, B, S, H, name):
    TQ = _tile(S, 256)
    nq = S // TQ
    scale = 1.0 / math.sqrt(HEAD_DIM)
    HP = BWD_HEADS
    ATT_W = HP * HEAD_DIM
    heads = range(HP)

    def body(q_ref, k_ref, v_ref, gate_ref, o_ref, dog_ref, w_ref, sg_ref, qg_ref, kg_ref,
             dq_ref, dk_ref, dv_ref, dgate_ref, dqg_ref, dkg_ref,
             qn_s, kn_s, do_s, dkn_s, dv_s):
        qi = pl.program_id(2)

        @pl.when((pl.program_id(0) == 0) & (pl.program_id(1) == 0) & (qi == 0))
        def _():
            dqg_ref[...] = jnp.zeros_like(dqg_ref)
            dkg_ref[...] = jnp.zeros_like(dkg_ref)

        @pl.when(qi == 0)
        def _():
            for hh in heads:
                qn_s[:, _lanes(hh)] = (_head_norm(q_ref[:, _lanes(hh)])[0] * qg_ref[...]).astype(BF16)
                kn_s[:, _lanes(hh)] = (_head_norm(k_ref[:, _lanes(hh)])[0] * kg_ref[...]).astype(BF16)
            gate = gate_ref[...].astype(F32)
            sg = _sigmoid(gate)
            dog_v = dog_ref[...].astype(F32)
            do_s[...] = (dog_v * (gate * sg)).astype(BF16)
            dgate_ref[...] = (dog_v * o_ref[...].astype(F32) * (sg * (1.0 + gate * (1.0 - sg)))).astype(BF16)
            dkn_s[...] = jnp.zeros_like(dkn_s)
            dv_s[...] = jnp.zeros_like(dv_s)

        row = lax.broadcasted_iota(jnp.int32, (TQ, TQ), 0)
        col = lax.broadcasted_iota(jnp.int32, (TQ, TQ), 1)
        earlier = _ones_where(row < col)
        causal = col < row

        def norm_bwd(xv, g_ref, dn, dg_ref):
            xh, r = _head_norm(xv)
            dg_ref[...] += jnp.sum(dn * xh, axis=0, keepdims=True)
            dxh = dn * g_ref[...]
            return (r * (dxh - xh * jnp.mean(dxh * xh, axis=-1, keepdims=True))).astype(BF16)

        def q_block():
            q0 = pl.multiple_of(qi * TQ, TQ)

            def grads_both(ki, state, diag):
                k0 = pl.multiple_of(ki * TQ, TQ)
                qb = [qn_s[pl.ds(q0, TQ), _lanes(hh)] for hh in heads]
                dob = [do_s[pl.ds(q0, TQ), _lanes(hh)] for hh in heads]
                wb = [w_ref[hh, ki] for hh in heads]
                da = [_dot(dob[hh], v_ref[pl.ds(k0, TQ), _lanes(hh)], NT) * wb[hh].astype(F32) for hh in heads]
                for hh in heads:
                    dv_s[pl.ds(k0, TQ), _lanes(hh)] += _dot(wb[hh], dob[hh], TN)
                prefix = [_dot(da[hh].astype(BF16), earlier, NN) + state[hh][0] for hh in heads]
                dzb = []
                for hh in heads:
                    sgz = sg_ref[hh, ki].astype(F32)
                    dz = da[hh] * (1.0 - sgz) - sgz * prefix[hh]
                    if diag:
                        dz = jnp.where(causal, dz, 0.0)
                    dzb.append((dz * scale).astype(BF16))
                dq = [state[hh][1] + _dot(dzb[hh], kn_s[pl.ds(k0, TQ), _lanes(hh)], NN) for hh in heads]
                for hh in heads:
                    dkn_s[pl.ds(k0, TQ), _lanes(hh)] += _dot(dzb[hh], qb[hh], TN)
                return tuple((state[hh][0] + jnp.sum(da[hh], axis=1, keepdims=True), dq[hh]) for hh in heads)

            zero = (jnp.zeros((TQ, 1), F32), jnp.zeros((TQ, HEAD_DIM), F32))
            state = lax.fori_loop(0, qi, lambda i, st: grads_both(i, st, False), (zero,) * HP)
            state = grads_both(qi, state, True)
            for hh in heads:
                dq_ref[:, _lanes(hh)] = norm_bwd(q_ref[pl.ds(q0, TQ), _lanes(hh)], qg_ref, state[hh][1], dqg_ref)

        q_block()

        @pl.when(qi == nq - 1)
        def _():
            for hh in heads:
                dk_ref[:, _lanes(hh)] = norm_bwd(k_ref[:, _lanes(hh)], kg_ref, dkn_s[:, _lanes(hh)], dkg_ref)
            dv_ref[...] = dv_s[...].astype(BF16)

    spec = _attn_specs(S, H, HP)
    vec = pl.BlockSpec((1, HEAD_DIM), lambda b, h, qi: (0, 0))
    blk = pl.BlockSpec((S, ATT_W), lambda b, h, qi: (b, h))
    rows = pl.BlockSpec((TQ, ATT_W), lambda b, h, qi: (b * nq + qi, h))
    kept = pl.BlockSpec((None, HP, None, nq, TQ, TQ), lambda b, h, qi: (b, h, qi, 0, 0, 0))
    big = jax.ShapeDtypeStruct((B * S, H * HEAD_DIM), BF16)
    small = jax.ShapeDtypeStruct((1, HEAD_DIM), F32)
    return pl.pallas_call(
        body, name=name,
        out_shape=(big, big, big, big, small, small),
        grid=(B, H // HP, nq),
        in_specs=[spec(0), spec(1), spec(2), spec(3), blk, blk, kept, kept, vec, vec],
        out_specs=(rows, blk, blk, blk, vec, vec),
        scratch_shapes=[pltpu.VMEM((S, ATT_W), BF16)] * 3 + [pltpu.VMEM((S, ATT_W), F32)] * 2,
        compiler_params=_params(("arbitrary", "arbitrary", "arbitrary")),
    )(proj, proj, proj, proj, o, dog, w_kept, sg_kept, q_g, k_g)


HALF = GROUPS_PER_BLOCK * STATE


def _cmul(ar, ai, br, bi):
    return ar * br - ai * bi, ar * bi + ai * br


def _cpow(ar, ai, n):
    rr = ri = None
    while n:
        if n & 1:
            rr, ri = (ar, ai) if rr is None else _cmul(rr, ri, ar, ai)
        n >>= 1
        if n:
            ar, ai = _cmul(ar, ai, ar, ai)
    return rr, ri


def _segment_carry(er, ei, lr, li, seg_len, segs_per_seq, reverse):
    Lr, Li = _cpow(lr, li, seg_len)
    pos = lax.broadcasted_iota(jnp.int32, er.shape, 0) % segs_per_seq
    outr = jnp.zeros_like(er)
    outi = jnp.zeros_like(ei)
    pr = pi = None
    for d in range(1, segs_per_seq):
        shift = (SUBLANES - d) if reverse else d
        sr = pltpu.roll(er, shift, 0)
        si = pltpu.roll(ei, shift, 0)
        ok = (pos + d < segs_per_seq) if reverse else (pos >= d)
        sr = jnp.where(ok, sr, 0.0)
        si = jnp.where(ok, si, 0.0)
        if pr is not None:
            sr, si = _cmul(sr, si, pr, pi)
        outr = outr + sr
        outi = outi + si
        pr, pi = (Lr, Li) if pr is None else _cmul(pr, pi, Lr, Li)
    return outr, outi


def _s5_sizes(T, B):
    assert SUBLANES % B == 0
    segs_per_seq = SUBLANES // B
    n_steps = T // SUBLANES
    cj = _tile(n_steps, 64)
    return segs_per_seq, n_steps, cj


def _s5_fwd(u_p, wb, wc, lam, dvec, B, name):
    T, C = u_p.shape
    nb = C // 128
    segs_per_seq, n_steps, cj = _s5_sizes(T, B)
    n_chunks = n_steps // cj
    rows = cj * SUBLANES

    def body(u_ref, wb_ref, wc_ref, lam_ref, d_ref, y_ref, hin_ref, bu_s, h_s):
        lr = jnp.broadcast_to(lam_ref[:, :HALF], (SUBLANES, HALF))
        li = jnp.broadcast_to(lam_ref[:, HALF:], (SUBLANES, HALF))

        def scan_chunk(c, hr, hi, store):
            r0 = pl.multiple_of(c * rows, rows)
            if not store:
                bu_s[pl.ds(r0, rows), :] = _dot(u_ref[pl.ds(r0, rows), :], wb_ref[...], NN)

            def step(j, carry):
                hr, hi = carry
                o = pl.multiple_of(j * SUBLANES, SUBLANES)
                at = pl.multiple_of(r0 + o, SUBLANES)
                nr = lr * hr - li * hi + bu_s[pl.ds(at, SUBLANES), :HALF]
                ni = lr * hi + li * hr + bu_s[pl.ds(at, SUBLANES), HALF:]
                if store:
                    h_s[pl.ds(o, SUBLANES), :HALF] = nr
                    h_s[pl.ds(o, SUBLANES), HALF:] = ni
                return nr, ni

            hr, hi = lax.fori_loop(0, cj, step, (hr, hi))
            if store:
                uv = u_ref[pl.ds(r0, rows), :].astype(F32)
                y_ref[pl.ds(r0, rows), :] = _dot(h_s[...].astype(BF16), wc_ref[...], NN) + d_ref[...] * uv
            return hr, hi

        zero = jnp.zeros((SUBLANES, HALF), F32)
        er, ei = lax.fori_loop(0, n_chunks, lambda c, h: scan_chunk(c, h[0], h[1], False), (zero, zero))
        h0r, h0i = _segment_carry(er, ei, lr, li, n_steps, segs_per_seq, False)
        hin_ref[:, :HALF] = h0r
        hin_ref[:, HALF:] = h0i
        lax.fori_loop(0, n_chunks, lambda c, h: scan_chunk(c, h[0], h[1], True), (h0r, h0i))

    return pl.pallas_call(
        body, name=name,
        out_shape=(jax.ShapeDtypeStruct((T, C), F32), jax.ShapeDtypeStruct((nb, SUBLANES, 2 * HALF), F32)),
        grid=(nb,),
        in_specs=[pl.BlockSpec((T, 128), lambda g: (0, g)),
                  pl.BlockSpec((None, 128, 2 * HALF), lambda g: (g, 0, 0)),
                  pl.BlockSpec((None, 2 * HALF, 128), lambda g: (g, 0, 0)),
                  pl.BlockSpec((None, 1, 2 * HALF), lambda g: (g, 0, 0)),
                  pl.BlockSpec((1, 128), lambda g: (0, g))],
        out_specs=(pl.BlockSpec((T, 128), lambda g: (0, g)),
                   pl.BlockSpec((None, SUBLANES, 2 * HALF), lambda g: (g, 0, 0))),
        scratch_shapes=[pltpu.VMEM((T, 2 * HALF), F32), pltpu.VMEM((rows, 2 * HALF), F32)],
        compiler_params=_params(("parallel",)),
    )(u_p, wb, wc, lam, dvec)


def _s5_bwd(u_p, dy_p, wb, wbt, wc, wct, lam, dvec, h_in, B, name):
    T, C = u_p.shape
    nb = C // 128
    segs_per_seq, n_steps, cj = _s5_sizes(T, B)
    n_chunks = n_steps // cj
    rows = cj * SUBLANES

    def body(u_ref, dy_ref, wb_ref, wbt_ref, wc_ref, wct_ref, lam_ref, d_ref, hin_ref,
             du_ref, dwb_ref, dwc_ref, dlam_ref, dd_ref, h_all, dh_all, x_s, g_s):
        lr = jnp.broadcast_to(lam_ref[:, :HALF], (SUBLANES, HALF))
        li = jnp.broadcast_to(lam_ref[:, HALF:], (SUBLANES, HALF))
        zero = jnp.zeros((SUBLANES, HALF), F32)

        h_all[pl.ds(0, SUBLANES), :] = hin_ref[...]

        def fwd_chunk(c, carry):
            r0 = pl.multiple_of(c * rows, rows)
            x_s[...] = _dot(u_ref[pl.ds(r0, rows), :], wb_ref[...], NN)

            def step(j, carry):
                hr, hi = carry
                o = pl.multiple_of(j * SUBLANES, SUBLANES)
                nr = lr * hr - li * hi + x_s[pl.ds(o, SUBLANES), :HALF]
                ni = lr * hi + li * hr + x_s[pl.ds(o, SUBLANES), HALF:]
                late = pl.multiple_of(r0 + o + SUBLANES, SUBLANES)
                h_all[pl.ds(late, SUBLANES), :HALF] = nr
                h_all[pl.ds(late, SUBLANES), HALF:] = ni
                return nr, ni

            return lax.fori_loop(0, cj, step, carry)

        lax.fori_loop(0, n_chunks, fwd_chunk, (hin_ref[:, :HALF], hin_ref[:, HALF:]))

        def bwd_chunk(i, carry, store):
            c = n_chunks - 1 - i
            r0 = pl.multiple_of(c * rows, rows)
            dyv = dy_ref[pl.ds(r0, rows), :]
            if not store:
                dh_all[pl.ds(r0, rows), :] = _dot(dyv, wct_ref[...], NN)

            def step(jj, carry):
                ar, ai, accr, acci = carry
                j = cj - 1 - jj
                o = pl.multiple_of(j * SUBLANES, SUBLANES)
                prev = pl.multiple_of(r0 + o, SUBLANES)
                nr = lr * ar + li * ai + dh_all[pl.ds(prev, SUBLANES), :HALF]
                ni = lr * ai - li * ar + dh_all[pl.ds(prev, SUBLANES), HALF:]
                if store:
                    g_s[pl.ds(o, SUBLANES), :HALF] = nr
                    g_s[pl.ds(o, SUBLANES), HALF:] = ni
                    pr = h_all[pl.ds(prev, SUBLANES), :HALF]
                    pi = h_all[pl.ds(prev, SUBLANES), HALF:]
                    accr = accr + nr * pr + ni * pi
                    acci = acci + ni * pr - nr * pi
                return nr, ni, accr, acci

            carry = lax.fori_loop(0, cj, step, carry)
            if store:
                gb = g_s[...].astype(BF16)
                uv = u_ref[pl.ds(r0, rows), :]
                dyf = dyv.astype(F32)
                du_ref[pl.ds(r0, rows), :] = (_dot(gb, wbt_ref[...], NN) + d_ref[...] * dyf).astype(BF16)
                dwb_ref[...] += _dot(uv, gb, TN)
                hb = h_all[pl.ds(pl.multiple_of(r0 + SUBLANES, SUBLANES), rows), :].astype(BF16)
                dwc_ref[...] += _dot(hb, dyv, TN)
                dd_ref[...] += jnp.sum(dyf * uv.astype(F32), axis=0, keepdims=True)
            return carry

        er, ei, _, _ = lax.fori_loop(0, n_chunks, lambda i, c: bwd_chunk(i, c, False), (zero, zero, zero, zero))
        a0r, a0i = _segment_carry(er, ei, lr, -li, n_steps, segs_per_seq, True)
        dwb_ref[...] = jnp.zeros_like(dwb_ref)
        dwc_ref[...] = jnp.zeros_like(dwc_ref)
        dd_ref[...] = jnp.zeros_like(dd_ref)
        _, _, accr, acci = lax.fori_loop(0, n_chunks, lambda i, c: bwd_chunk(i, c, True), (a0r, a0i, zero, zero))
        dlam_ref[:, :HALF] = jnp.sum(accr, axis=0, keepdims=True)
        dlam_ref[:, HALF:] = jnp.sum(acci, axis=0, keepdims=True)

    col = pl.BlockSpec((T, 128), lambda g: (0, g))
    vec = pl.BlockSpec((1, 128), lambda g: (0, g))

    def per_block(*shape):
        return pl.BlockSpec((None,) + shape, lambda g: (g, 0, 0))

    return pl.pallas_call(
        body, name=name,
        out_shape=(jax.ShapeDtypeStruct((T, C), BF16),
                   jax.ShapeDtypeStruct((nb, 128, 2 * HALF), F32),
                   jax.ShapeDtypeStruct((nb, 2 * HALF, 128), F32),
                   jax.ShapeDtypeStruct((nb, 1, 2 * HALF), F32),
                   jax.ShapeDtypeStruct((1, C), F32)),
        grid=(nb,),
        in_specs=[col, col, per_block(128, 2 * HALF), per_block(2 * HALF, 128), per_block(2 * HALF, 128),
                  per_block(128, 2 * HALF), per_block(1, 2 * HALF), vec, per_block(SUBLANES, 2 * HALF)],
        out_specs=(col, per_block(128, 2 * HALF), per_block(2 * HALF, 128), per_block(1, 2 * HALF), vec),
        scratch_shapes=[pltpu.VMEM((T + SUBLANES, 2 * HALF), F32), pltpu.VMEM((T, 2 * HALF), F32),
                        pltpu.VMEM((rows, 2 * HALF), F32), pltpu.VMEM((rows, 2 * HALF), F32)],
        compiler_params=_params(("parallel",)),
    )(u_p, dy_p, wb, wbt, wc, wct, lam, dvec, h_in)


def _discretize(a_re, a_im, log_dt, b_re, b_im):
    dt = jnp.exp(log_dt)[:, None]
    mag = jnp.exp(a_re * dt)
    lam_re = mag * jnp.cos(a_im * dt)
    lam_im = mag * jnp.sin(a_im * dt)
    den = a_re * a_re + a_im * a_im
    f_re = ((lam_re - 1.0) * a_re + lam_im * a_im) / den
    f_im = (lam_im * a_re - (lam_re - 1.0) * a_im) / den
    bb_re = f_re[..., None] * b_re - f_im[..., None] * b_im
    bb_im = f_re[..., None] * b_im + f_im[..., None] * b_re
    return lam_re, lam_im, bb_re, bb_im


def _block_diag_in(bb_re, bb_im):
    eye = jnp.eye(GROUPS_PER_BLOCK, dtype=F32)

    def one(bb):
        t = bb.reshape(-1, GROUPS_PER_BLOCK, STATE, GROUP)
        return jnp.einsum('gapi,ab->gaibp', t, eye).reshape(-1, 128, HALF)

    return jnp.concatenate([one(bb_re), one(bb_im)], axis=-1)


def _block_diag_in_grad(dwb):
    eye = jnp.eye(GROUPS_PER_BLOCK, dtype=F32)

    def one(d):
        t = d.reshape(-1, GROUPS_PER_BLOCK, GROUP, GROUPS_PER_BLOCK, STATE)
        return jnp.einsum('gaibp,ab->gapi', t, eye).reshape(-1, STATE, GROUP)

    return one(dwb[..., :HALF]), one(dwb[..., HALF:])


def _block_diag_out(c_re, c_im):
    eye = jnp.eye(GROUPS_PER_BLOCK, dtype=F32)

    def one(cc):
        t = cc.reshape(-1, GROUPS_PER_BLOCK, GROUP, STATE)
        return jnp.einsum('gaip,ab->gbpai', t, eye).reshape(-1, HALF, 128)

    return jnp.concatenate([one(c_re), -one(c_im)], axis=1)


def _block_diag_out_grad(dwc):
    eye = jnp.eye(GROUPS_PER_BLOCK, dtype=F32)

    def one(d):
        t = d.reshape(-1, GROUPS_PER_BLOCK, STATE, GROUPS_PER_BLOCK, GROUP)
        return jnp.einsum('gbpai,ab->gaip', t, eye).reshape(-1, GROUP, STATE)

    return one(dwc[:, :HALF]), -one(dwc[:, HALF:])


def _pack(parts):
    return jnp.concatenate([p.reshape(-1, PACK_W) for p in parts], axis=0)


def _unpack(buf, shapes):
    lead = buf.shape[:-2]
    out, r = [], 0
    for s in shapes:
        n = math.prod(s) // PACK_W
        out.append(buf[..., r:r + n, :].reshape(lead + tuple(s)))
        r += n
    return out


def _permute_rows(a, n_steps):
    T, C = a.shape
    return a.reshape(SUBLANES, n_steps, C).transpose(1, 0, 2).reshape(T, C)


def _unpermute_rows(a, n_steps):
    T, C = a.shape
    return a.reshape(n_steps, SUBLANES, C).transpose(1, 0, 2).reshape(T, C)


def kernel(x, norm_g, attn_w_in, attn_q_g, attn_k_g, attn_w_out, ssm_w_in, ssm_A_re, ssm_A_im, ssm_log_dt, ssm_B_re, ssm_B_im, ssm_C_re, ssm_C_im, ssm_D, ssm_glu_w, ssm_glu_b, ssm_w_out, loss_target, m_norm_g, m_attn_w_in, m_attn_q_g, m_attn_k_g, m_attn_w_out, m_ssm_w_in, m_ssm_A_re, m_ssm_A_im, m_ssm_log_dt, m_ssm_B_re, m_ssm_B_im, m_ssm_C_re, m_ssm_C_im, m_ssm_D, m_ssm_glu_w, m_ssm_glu_b, m_ssm_w_out, v_norm_g, v_attn_w_in, v_attn_q_g, v_attn_k_g, v_attn_w_out, v_ssm_w_in, v_ssm_A_re, v_ssm_A_im, v_ssm_log_dt, v_ssm_B_re, v_ssm_B_im, v_ssm_C_re, v_ssm_C_im, v_ssm_D, v_ssm_glu_w, v_ssm_glu_b, v_ssm_w_out):
    B, S, D = x.shape
    T = B * S
    H = D // HEAD_DIM
    G_loc = ssm_A_re.shape[1]
    G = G_loc * N_DEV
    n_steps = T // SUBLANES
    me, chip = _my_index(), _my_chip()
    xf = x.reshape(T, D)
    target = loss_target.reshape(T, D)

    small_shapes = [(G_loc, STATE), (G_loc, STATE), (G_loc, STATE, GROUP), (G_loc, STATE, GROUP),
                    (G_loc, GROUP, STATE), (G_loc, GROUP, STATE), (G_loc * GROUP,), (G_loc * GROUP,)]
    disc_in = (ssm_A_re[0], ssm_A_im[0], ssm_log_dt[0], ssm_B_re[0], ssm_B_im[0])
    (lam_re, lam_im, bb_re, bb_im), disc_vjp = jax.vjp(_discretize, *disc_in)
    small = _pack([lam_re, lam_im, bb_re, bb_im, ssm_C_re[0], ssm_C_im[0], ssm_D[0], ssm_glu_b[0]])
    small_all = _exchange(small, False, "gather_small")
    c_in0 = _place(attn_w_in[0], me, N_DEV, BF16, "cast_attn_w_in")
    near = _exchange_start(c_in0, None, "near", "gather_attn_w_in_near_start", deps=(small_all,))
    beside = (near[3],)
    small_all = small_all + near[3][0, 0]
    c_out0 = _place(attn_w_out[0], me, N_DEV, BF16, "cast_attn_w_out", deps=beside)
    c_in1 = _place(ssm_w_in[0], me, N_DEV, BF16, "cast_ssm_w_in", deps=beside)
    c_glu = _place(ssm_glu_w[0], me, N_DEV, BF16, "cast_ssm_glu_w", deps=beside)
    c_out1 = _place(ssm_w_out[0], me, N_DEV, BF16, "cast_ssm_w_out", deps=beside)

    lam_re_a, lam_im_a, bb_re_a, bb_im_a, c_re_a, c_im_a, d_a, glu_b_a = [
        t.reshape((G,) + t.shape[2:]) if t.ndim > 2 else t.reshape(-1)
        for t in _unpack(small_all, small_shapes)]
    wb = _block_diag_in(bb_re_a, bb_im_a)
    wc = _block_diag_out(c_re_a, c_im_a)
    wb_b, wc_b = wb.astype(BF16), wc.astype(BF16)
    wbt_b, wct_b = wb_b.transpose(0, 2, 1), wc_b.transpose(0, 2, 1)
    lam = jnp.concatenate([lam_re_a.reshape(-1, 1, HALF), lam_im_a.reshape(-1, 1, HALF)], axis=-1)
    d_row = d_a.reshape(1, D)
    glu_b_row = glu_b_a.reshape(1, D)
    g0, g1 = norm_g[0:1], norm_g[1:2]
    q_g, k_g = attn_q_g, attn_k_g

    h0 = _rmsnorm_fwd(xf, g0, "norm0", deps=beside)
    done = (h0, c_out0, c_in1, c_glu, c_out1, wb_b, wbt_b, wc_b, wct_b, lam, d_row, glu_b_row)
    far = _exchange_start(_exchange_wait(near, done, "near", "gather_attn_w_in_near_wait"), None, "far",
                          "gather_attn_w_in_far_start")
    w_in0 = _exchange_wait(far, far[3], "far", "gather_attn_w_in_far_wait")
    s_out0 = _exchange_start(c_out0, None, "gather", "gather_attn_w_out_start", deps=(w_in0,))
    s_in1 = _exchange_start(c_in1, None, "gather", "gather_ssm_w_in_start", deps=(w_in0,))
    proj0 = _mm_nn(h0, w_in0, BF16, "attn_in", deps=(s_out0[3], s_in1[3]), tk=FULL_K)
    og, o, w_kept, sg_kept = _attn_fwd(proj0, q_g, k_g, B, S, H, "attn_fwd")
    w_out0 = _exchange_wait(s_out0, og, "gather", "gather_attn_w_out_wait").reshape(1, D, D)
    s_glu = _exchange_start(c_glu, None, "gather", "gather_ssm_glu_w_start", deps=(w_out0,))
    s_out1 = _exchange_start(c_out1, None, "gather", "gather_ssm_w_out_start", deps=(w_out0,))
    x1 = _mm_nn(og, w_out0, F32, "attn_out", residual=xf, deps=(s_glu[3], s_out1[3]), tm=FUSED_TM, tk=FULL_K)

    h1 = _rmsnorm_fwd(x1, g1, "norm1")
    w_in1 = _exchange_wait(s_in1, h1, "gather", "gather_ssm_w_in_wait")
    proj1 = _mm_nn(h1, w_in1, BF16, "ssm_in", tk=FULL_K)
    u_p = _permute_rows(proj1[:, :D], n_steps)
    gate1 = proj1[:, D:]
    y_p, h_in = _s5_fwd(u_p, wb_b, wc_b, lam, d_row, B, "s5_fwd")
    y_ssm = _unpermute_rows(y_p, n_steps)
    (yg,) = _ew(lambda a: (_gelu(a)[0],), [y_ssm], [], [BF16], 0, "gelu")
    w_glu = _exchange_wait(s_glu, yg, "gather", "gather_ssm_glu_w_wait").reshape(1, D, D)

    def glu_gate(zz, y, gt):
        gt = gt.astype(F32)
        return zz, _gelu(y)[0] * _sigmoid(zz) * (gt * _sigmoid(gt))

    z, y3 = _mm_nn(yg, w_glu, F32, "glu_in", bias=glu_b_row, extras=[(y_ssm, 0), (proj1, D // _tile(D, 1024))],
                   epilogue=glu_gate, out_dtypes=[F32, BF16], tm=FUSED_TM, tk=FULL_K)
    w_out1 = _exchange_wait(s_out1, y3, "gather", "gather_ssm_w_out_wait").reshape(1, D, D)

    def loss_head(out_tile, tgt):
        err = out_tile - tgt
        dy = err * (1.0 / D)
        return dy, dy, jnp.sum(err * err, axis=0, keepdims=True)

    dout, dout_b, sq = _mm_nn(y3, w_out1, F32, "ssm_out", residual=x1, extras=[(target, 0)], epilogue=loss_head,
                              out_dtypes=[F32, BF16], tile_sums=True, tm=FUSED_TM, tk=FULL_K)

    p_w_out1 = _mm_tn(y3, dout_b, 1, BF16, "ssm_out_dw").reshape(N_DEV, D // N_DEV, D)
    sc_out1 = _exchange_start(_place(p_w_out1, me, N_DEV, BF16, "place_ssm_w_out"), p_w_out1, "scatter", "scatter_ssm_w_out_start")
    dy3 = _mm_nt(dout_b, w_out1, F32, "ssm_out_dx", deps=(sc_out1[3],), tq=FULL_K)

    def glu_bwd(d3, y, zz, gt):
        gt = gt.astype(F32)
        sg = _sigmoid(gt)
        sz = _sigmoid(zz)
        ygv, _ = _gelu(y)
        dy2 = d3 * (gt * sg)
        dgate = d3 * (ygv * sz) * (sg * (1.0 + gt * (1.0 - sg)))
        dz = dy2 * ygv * (sz * (1.0 - sz))
        return dz, dgate, dy2 * sz, jnp.sum(dz, axis=0, keepdims=True)

    dz_b, dgate1, dyg_a, dglu_b = _ew(glu_bwd, [dy3, y_ssm, z, gate1], [], [BF16, BF16, F32], 1, "glu_gate_bwd")
    p_w_glu = _mm_tn(yg, dz_b, 1, BF16, "glu_in_dw").reshape(N_DEV, D // N_DEV, D)
    sc_glu = _exchange_start(_place(p_w_glu, me, N_DEV, BF16, "place_ssm_glu_w"), p_w_glu, "scatter", "scatter_ssm_glu_w_start")

    def gelu_bwd(db, da, y):
        _, t = _gelu(y)
        dg = 0.5 * (1.0 + t) + 0.5 * y * (1.0 - t * t) * (GELU_C * (1.0 + 3.0 * 0.044715 * (y * y)))
        return (da + db) * dg

    dy_ssm = _mm_nt(dz_b, w_glu, BF16, "glu_in_dx", deps=(sc_glu[3],), extras=[dyg_a, y_ssm], epilogue=gelu_bwd,
                    tm=FUSED_TM, tq=FULL_K)
    dy_p = _permute_rows(dy_ssm, n_steps)
    du_p, dwb, dwc, dlam, dd = _s5_bwd(u_p, dy_p, wb_b, wbt_b, wc_b, wct_b, lam, d_row, h_in, B, "s5_bwd")
    du = _unpermute_rows(du_p, n_steps)
    dproj1 = jnp.concatenate([du, dgate1], axis=1)
    p_w_in1 = _mm_tn(h1, dproj1, N_DEV, BF16, "ssm_in_dw")
    sc_in1 = _exchange_start(_place(p_w_in1, me, N_DEV, BF16, "place_ssm_w_in"), p_w_in1, "scatter", "scatter_ssm_w_in_start")
    dh1 = _mm_nt(dproj1, w_in1, F32, "ssm_in_dx", deps=(sc_in1[3],), tq=FULL_K)
    dx1, dx1_b, dg1 = _rmsnorm_bwd(x1, dh1, dout, g1, "norm1_bwd")

    p_w_out0 = _mm_tn(og, dx1_b, 1, BF16, "attn_out_dw").reshape(N_DEV, D // N_DEV, D)
    sc_out0 = _exchange_start(_place(p_w_out0, me, N_DEV, BF16, "place_attn_w_out"), p_w_out0, "scatter", "scatter_attn_w_out_start")
    dog = _mm_nt(dx1_b, w_out0, BF16, "attn_out_dx", deps=(sc_out0[3],), tq=FULL_K)
    dq, dk, dv, dgate0, dqg, dkg = _attn_bwd(proj0, o, dog, w_kept, sg_kept, q_g, k_g, B, S, H, "attn_bwd")
    dproj0 = jnp.concatenate([dq, dk, dv, dgate0], axis=1)
    p_w_in0 = _mm_tn(h0, dproj0, N_DEV, BF16, "attn_in_dw")
    pair = _exchange_start(lax.empty((N_DEV // 2,) + p_w_in0.shape[1:], BF16), p_w_in0, "pair", "scatter_attn_w_in_pair_start")

    def update(started, after, w, m, v, name, kind="scatter"):
        recv = _exchange_wait(started, after, kind, "scatter_" + name + "_wait")
        return _adamw(recv, w[0], m[0], v[0], "adamw_" + name)

    r_ssm_w_out = update(sc_out1, pair[3], ssm_w_out, m_ssm_w_out, v_ssm_w_out, "ssm_w_out")
    r_ssm_glu_w = update(sc_glu, r_ssm_w_out[0], ssm_glu_w, m_ssm_glu_w, v_ssm_glu_w, "ssm_glu_w")
    r_ssm_w_in = update(sc_in1, r_ssm_glu_w[0], ssm_w_in, m_ssm_w_in, v_ssm_w_in, "ssm_w_in")
    r_attn_w_out = update(sc_out0, r_ssm_w_in[0], attn_w_out, m_attn_w_out, v_attn_w_out, "attn_w_out")

    got, p_w_in0 = _exchange_wait(pair, r_attn_w_out[0], "pair", "scatter_attn_w_in_pair_wait", with_source=True)
    q_w_in0 = _pair_sum(p_w_in0, got, "scatter_attn_w_in_pair_sum")
    sc_in0 = _exchange_start(_place(q_w_in0, chip, N_DEV // 2, BF16, "place_attn_w_in"), q_w_in0, "chips", "scatter_attn_w_in_start")
    dh0 = _mm_nt(dproj0, w_in0, F32, "attn_in_dx", deps=(sc_in0[3],), tq=FULL_K)
    dx, _, dg0 = _rmsnorm_bwd(xf, dh0, dx1, g0, "norm0_bwd")

    dbb_re, dbb_im = _block_diag_in_grad(dwb)
    dc_re, dc_im = _block_diag_out_grad(dwc)
    dlam_re = dlam[:, 0, :HALF].reshape(G, STATE)
    dlam_im = dlam[:, 0, HALF:].reshape(G, STATE)
    by_owner = [t.reshape((N_DEV, -1)) for t in (dlam_re, dlam_im, dbb_re, dbb_im, dc_re, dc_im, dd, dglu_b)]
    small_parts = jnp.concatenate([t.reshape(N_DEV, -1, PACK_W) for t in by_owner], axis=1)
    small_sum = _sum_parts(_exchange(small_parts, True, "scatter_small"), "sum_small")
    s_lam_re, s_lam_im, s_bb_re, s_bb_im, s_c_re, s_c_im, s_d, s_glu_b = _unpack(small_sum, small_shapes)
    g_a_re, g_a_im, g_log_dt, g_b_re, g_b_im = disc_vjp((s_lam_re, s_lam_im, s_bb_re, s_bb_im))

    local_names = ["ssm_A_re", "ssm_A_im", "ssm_log_dt", "ssm_B_re", "ssm_B_im", "ssm_C_re", "ssm_C_im",
                   "ssm_D", "ssm_glu_b"]
    local_g = [g_a_re, g_a_im, g_log_dt, g_b_re, g_b_im, s_c_re, s_c_im, s_d, s_glu_b]
    local_w = [ssm_A_re, ssm_A_im, ssm_log_dt, ssm_B_re, ssm_B_im, ssm_C_re, ssm_C_im, ssm_D, ssm_glu_b]
    local_m = [m_ssm_A_re, m_ssm_A_im, m_ssm_log_dt, m_ssm_B_re, m_ssm_B_im, m_ssm_C_re, m_ssm_C_im,
               m_ssm_D, m_ssm_glu_b]
    local_v = [v_ssm_A_re, v_ssm_A_im, v_ssm_log_dt, v_ssm_B_re, v_ssm_B_im, v_ssm_C_re, v_ssm_C_im,
               v_ssm_D, v_ssm_glu_b]
    r_local = _adamw_small(local_g, local_w, local_m, local_v, None, "adamw_small")

    rep_g = [jnp.concatenate([dg0, dg1], axis=0), dqg, dkg]
    rep_w = [norm_g, attn_q_g, attn_k_g]
    rep_m = [m_norm_g, m_attn_q_g, m_attn_k_g]
    rep_v = [v_norm_g, v_attn_q_g, v_attn_k_g]
    r_rep = _adamw_small(rep_g, rep_w, rep_m, rep_v, "gather_rep", "adamw_rep")

    r_attn_w_in = update(sc_in0, r_rep[0][0], attn_w_in, m_attn_w_in, v_attn_w_in, "attn_w_in", "chips")

    res = {"attn_w_in": r_attn_w_in, "attn_w_out": r_attn_w_out, "ssm_w_in": r_ssm_w_in,
           "ssm_glu_w": r_ssm_glu_w, "ssm_w_out": r_ssm_w_out}
    ref_w = {"attn_w_in": attn_w_in, "attn_w_out": attn_w_out, "ssm_w_in": ssm_w_in,
             "ssm_glu_w": ssm_glu_w, "ssm_w_out": ssm_w_out}
    for name, r, w in zip(local_names, r_local, local_w):
        res[name], ref_w[name] = r, w
    for name, r, w in zip(["norm_g", "attn_q_g", "attn_k_g"], r_rep, rep_w):
        res[name], ref_w[name] = r, w
    order = ["norm_g", "attn_w_in", "attn_q_g", "attn_k_g", "attn_w_out", "ssm_w_in", "ssm_A_re", "ssm_A_im",
             "ssm_log_dt", "ssm_B_re", "ssm_B_im", "ssm_C_re", "ssm_C_im", "ssm_D", "ssm_glu_w", "ssm_glu_b",
             "ssm_w_out"]
    loss_part = _total(sq, 0.5 / D, "loss", deps=(r_attn_w_in[0],))
    loss = lax.psum(loss_part[0, 0], ("x", "y", "c"))
    outs = [loss, dx.reshape(B, S, D)]
    for kind in range(4):
        outs += [res[n][kind].reshape(ref_w[n].shape) for n in order]
    return tuple(outs)


def _adamw_small(grads, ws, ms, vs, gather_name, name):
    sizes = [math.prod(w.shape) for w in ws]
    total = sum(sizes)
    rows = -(-total // (PACK_W * 8)) * 8
    if rows > 256:
        rows = -(-rows // 256) * 256

    def pack(ts, fill):
        flat = jnp.concatenate([t.reshape(-1).astype(F32) for t in ts])
        flat = jnp.concatenate([flat, jnp.full((rows * PACK_W - total,), fill, F32)])
        return flat.reshape(rows, PACK_W)

    g = pack(grads, 0.0)
    parts = _exchange(g, False, gather_name) if gather_name else g[None]
    res = _adamw(parts, pack(ws, 0.0), pack(ms, 0.0), pack(vs, 1.0), name)
    outs = []
    off = 0
    flats = [r.reshape(-1) for r in res]
    for n in sizes:
        outs.append(tuple(f[off:off + n] for f in flats))
        off += n
    return outs
```
